```python
import math
import jax, jax.numpy as jnp
from jax import lax
import numpy as np

D_MODEL = 1024
BATCH = 8
SEQ = 4096
DEPTH = 1

HEAD_DIM = 64
ROPE_DIMS = HEAD_DIM // 4
ROPE_THETA = 500000.0
NORM_EPS = 1e-6
NEG_INF = -1e30

NSA_HEADS = 8
NSA_KV_GROUPS = 2
NSA_Q_PER_KV = NSA_HEADS // NSA_KV_GROUPS
CMP_BLOCK = 32
CMP_STRIDE = 16
CMP_HIDDEN = 256
SLC_BLOCK = 64
SLC_TOPN = 16
FORCE_SCORE = 1e3
WIN = 512
NSA_QBLOCK = 64

DIL_PATTERNS = ((128, 1), (512, 4), (2048, 16))
DIL_GROUPS = len(DIL_PATTERNS)
DIL_HEADS_PER_GROUP = 4
DIL_WIDTH = DIL_HEADS_PER_GROUP * HEAD_DIM
DIL_BLOCK = 128

NSA_Q_COLS = NSA_HEADS * HEAD_DIM
NSA_KV_COLS = NSA_KV_GROUPS * HEAD_DIM
NSA_GATE_COLS = 3 * NSA_HEADS
DIL_COLS = DIL_GROUPS * 3 * DIL_WIDTH
MERGE_COLS = 2 * D_MODEL
IN_COLS = NSA_Q_COLS + 6 * NSA_KV_COLS + NSA_GATE_COLS + DIL_COLS + MERGE_COLS

PEER_HEADS = 8
PEER_NKEYS = 128
PEER_TOPK = 16
PEER_QDIM = 128
PEER_N_EXPERTS = PEER_NKEYS ** 2
PEER_CHUNK = 128

kernel_name = "hybrid_nsa_dilated_peer_block"


def rms_norm(z, g):
    zf = z.astype(jnp.float32)
    zf = zf * lax.rsqrt(jnp.mean(zf * zf, axis=-1, keepdims=True) + NORM_EPS)
    return (zf * g.astype(jnp.float32)).astype(z.dtype)


def rope(z, pos):
    half = ROPE_DIMS // 2
    inv = ROPE_THETA ** (-(jnp.arange(half, dtype=jnp.float32) * 2.0 / ROPE_DIMS))
    ang = pos.astype(jnp.float32)[:, None] * inv[None, :]
    cos = jnp.cos(ang)[None, :, None, :]
    sin = jnp.sin(ang)[None, :, None, :]
    zf = z.astype(jnp.float32)
    z1 = zf[..., :half]
    z2 = zf[..., half:ROPE_DIMS]
    out = jnp.concatenate([z1 * cos - z2 * sin, z1 * sin + z2 * cos, zf[..., ROPE_DIMS:]], axis=-1)
    return out.astype(z.dtype)


def compress_blocks(z, idx, pe, w1, w2):
    B, _, G, dh = z.shape
    blk = z[:, idx] + pe[None, None, :, None, :]
    n_cmp = idx.shape[0]
    flat = blk.transpose(0, 1, 3, 2, 4).reshape(B, n_cmp, G, CMP_BLOCK * dh)
    return jax.nn.gelu(flat @ w1) @ w2


def nsa_attention(q, k_c, v_c, k_s, v_s, k_w, v_w, gate_logits, q_norm, k_norm,
                  cmp_pe_k, cmp_w1_k, cmp_w2_k, cmp_pe_v, cmp_w1_v, cmp_w2_v):
    B, S, _ = q.shape
    G, HG, dh = NSA_KV_GROUPS, NSA_Q_PER_KV, HEAD_DIM
    scale = HEAD_DIM ** -0.5
    dtype = q.dtype
    pos = jnp.arange(S, dtype=jnp.int32)
    q = rope(rms_norm(q.reshape(B, S, NSA_HEADS, dh), q_norm), pos)
    shp = (B, S, G, dh)
    k_s = rope(rms_norm(k_s.reshape(shp), k_norm[1]), pos)
    k_w = rope(rms_norm(k_w.reshape(shp), k_norm[2]), pos)
    v_s = v_s.reshape(shp)
    v_w = v_w.reshape(shp)

    n_cmp = (S - CMP_BLOCK) // CMP_STRIDE + 1
    starts = jnp.arange(n_cmp, dtype=jnp.int32) * CMP_STRIDE
    idx = starts[:, None] + jnp.arange(CMP_BLOCK, dtype=jnp.int32)[None, :]
    cmp_end = starts + CMP_BLOCK - 1
    k_cmp = compress_blocks(k_c.reshape(shp), idx, cmp_pe_k, cmp_w1_k, cmp_w2_k)
    v_cmp = compress_blocks(v_c.reshape(shp), idx, cmp_pe_v, cmp_w1_v, cmp_w2_v)
    k_cmp = rope(rms_norm(k_cmp, k_norm[0]), cmp_end)

    n_slc = S // SLC_BLOCK
    n_sel = min(SLC_TOPN, n_slc)
    s0 = np.arange(n_cmp) * CMP_STRIDE
    s1 = s0 + CMP_BLOCK
    b0 = np.arange(n_slc) * SLC_BLOCK
    b1 = b0 + SLC_BLOCK
    ov = np.clip(np.minimum(s1[:, None], b1[None, :]) - np.maximum(s0[:, None], b0[None, :]), 0, None) / CMP_BLOCK
    overlap = jnp.asarray(ov, dtype=jnp.float32)

    ks_blk = k_s.reshape(B, n_slc, SLC_BLOCK, G, dh).transpose(0, 3, 1, 2, 4)
    vs_blk = v_s.reshape(B, n_slc, SLC_BLOCK, G, dh).transpose(0, 3, 1, 2, 4)
    kw_pad = jnp.pad(k_w, ((0, 0), (WIN, 0), (0, 0), (0, 0)))
    vw_pad = jnp.pad(v_w, ((0, 0), (WIN, 0), (0, 0), (0, 0)))
    gates = jax.nn.sigmoid(gate_logits.astype(jnp.float32)).reshape(B, S, NSA_HEADS, 3)
    bi = jnp.arange(B)[:, None, None, None]
    gi = jnp.arange(G)[None, :, None, None]
    blk_ids = jnp.arange(n_slc, dtype=jnp.int32)
    QB = NSA_QBLOCK

    def body(n):
        t0 = n * QB
        t = t0 + jnp.arange(QB, dtype=jnp.int32)
        qb = lax.dynamic_slice_in_dim(q, t0, QB, axis=1).reshape(B, QB, G, HG, dh)
        sc = jnp.einsum('bqgjd,bcgd->bgjqc', qb, k_cmp).astype(jnp.float32) * scale
        valid_c = cmp_end[None, :] <= t[:, None]
        pc = jax.nn.softmax(jnp.where(valid_c, sc, NEG_INF), axis=-1)
        pc = jnp.where(valid_c, pc, 0.0)
        o_c = jnp.einsum('bgjqc,bcgd->bqgjd', pc.astype(dtype), v_cmp)
        imp = jnp.einsum('bgjqc,cn->bgqn', pc, overlap)
        cur = t // SLC_BLOCK
        forced = (blk_ids[None, :] == 0) | (blk_ids[None, :] == cur[:, None]) | (blk_ids[None, :] == cur[:, None] - 1)
        causal_b = blk_ids[None, :] * SLC_BLOCK <= t[:, None]
        score = jnp.where(forced, FORCE_SCORE, jnp.where(causal_b, imp, -1.0))
        _, sel = lax.top_k(score, n_sel)
        kg = ks_blk[bi, gi, sel]
        vg = vs_blk[bi, gi, sel]
        ss = jnp.einsum('bqgjd,bgqnkd->bgjqnk', qb, kg).astype(jnp.float32) * scale
        kpos = sel[..., None] * SLC_BLOCK + jnp.arange(SLC_BLOCK, dtype=jnp.int32)
        valid_s = kpos <= t[None, None, :, None, None]
        ss = jnp.where(valid_s[:, :, None], ss, NEG_INF).reshape(B, G, HG, QB, n_sel * SLC_BLOCK)
        ps = jax.nn.softmax(ss, axis=-1).reshape(B, G, HG, QB, n_sel, SLC_BLOCK)
        o_s = jnp.einsum('bgjqnk,bgqnkd->bqgjd', ps.astype(dtype), vg)
        kwb = lax.dynamic_slice_in_dim(kw_pad, t0, QB + WIN, axis=1)
        vwb = lax.dynamic_slice_in_dim(vw_pad, t0, QB + WIN, axis=1)
        kp = t0 - WIN + jnp.arange(QB + WIN, dtype=jnp.int32)
        dist = t[:, None] - kp[None, :]
        valid_w = (dist >= 0) & (dist < WIN) & (kp[None, :] >= 0)
        sw = jnp.einsum('bqgjd,bkgd->bgjqk', qb, kwb).astype(jnp.float32) * scale
        pw = jax.nn.softmax(jnp.where(valid_w, sw, NEG_INF), axis=-1)
        o_w = jnp.einsum('bgjqk,bkgd->bqgjd', pw.astype(dtype), vwb)
        gb = lax.dynamic_slice_in_dim(gates, t0, QB, axis=1).reshape(B, QB, G, HG, 3)
        o = gb[..., 0:1] * o_c + gb[..., 1:2] * o_s + gb[..., 2:3] * o_w
        return o.reshape(B, QB, NSA_HEADS * dh).astype(dtype)

    out = lax.map(body, jnp.arange(S // QB, dtype=jnp.int32))
    return out.transpose(1, 0, 2, 3).reshape(B, S, NSA_HEADS * dh)


def dilated_group(q, k, v, window, dilation):
    B, S, H, dh = q.shape
    scale = HEAD_DIM ** -0.5
    n_back = window // dilation
    L = -(-S // dilation)
    L = -(-L // DIL_BLOCK) * DIL_BLOCK
    Sp = L * dilation
    nb = L // DIL_BLOCK
    pad = ((0, 0), (0, Sp - S), (0, 0), (0, 0))

    def to_blocks(z):
        z = jnp.pad(z, pad).reshape(B, L, dilation, H, dh).transpose(0, 2, 3, 1, 4)
        return z.reshape(B, dilation, H, nb, DIL_BLOCK, dh)

    def with_prev(z):
        prev = jnp.pad(z[:, :, :, :-1], ((0, 0), (0, 0), (0, 0), (1, 0), (0, 0), (0, 0)))
        return jnp.concatenate([prev, z], axis=4)

    qb = to_blocks(q)
    kk = with_prev(to_blocks(k))
    vv = with_prev(to_blocks(v))
    s = jnp.einsum('brhnqd,brhnkd->brhnqk', qb, kk).astype(jnp.float32) * scale
    blk = jnp.arange(nb, dtype=jnp.int32)[:, None]
    pq = blk * DIL_BLOCK + jnp.arange(DIL_BLOCK, dtype=jnp.int32)[None, :]
    pk = (blk - 1) * DIL_BLOCK + jnp.arange(2 * DIL_BLOCK, dtype=jnp.int32)[None, :]
    dist = pq[:, :, None] - pk[:, None, :]
    valid = (dist >= 0) & (dist <= n_back) & (pk[:, None, :] >= 0)
    s = jnp.where(valid, s, NEG_INF)
    m = jnp.max(s, axis=-1, keepdims=True)
    e = jnp.exp(s - m)
    den = jnp.sum(e, axis=-1)
    o = jnp.einsum('brhnqk,brhnkd->brhnqd', e, vv.astype(jnp.float32)) / den[..., None]
    lse = m[..., 0] + jnp.log(den)
    o = o.reshape(B, dilation, H, L, dh).transpose(0, 3, 1, 2, 4).reshape(B, Sp, H, dh)[:, :S]
    lse = lse.reshape(B, dilation, H, L).transpose(0, 3, 1, 2).reshape(B, Sp, H)[:, :S]
    return o, lse


def dilated_attention(d_cols, q_norm, k_norm):
    B, S, _ = d_cols.shape
    pos = jnp.arange(S, dtype=jnp.int32)
    d = d_cols.reshape(B, S, DIL_GROUPS, 3, DIL_HEADS_PER_GROUP, HEAD_DIM)
    outs, lses = [], []
    for g, (window, dilation) in enumerate(DIL_PATTERNS):
        qg = rope(rms_norm(d[:, :, g, 0], q_norm[g]), pos)
        kg = rope(rms_norm(d[:, :, g, 1], k_norm[g]), pos)
        o, lse = dilated_group(qg, kg, d[:, :, g, 2], window, dilation)
        outs.append(o)
        lses.append(lse)
    alpha = jax.nn.softmax(jnp.stack(lses, axis=0), axis=0)
    o = jnp.sum(alpha[..., None] * jnp.stack(outs, axis=0), axis=0)
    return o.reshape(B, S, DIL_WIDTH).astype(d_cols.dtype)


def peer(hn, w_q, subkeys, u, v):
    B, S, D = hn.shape
    T = B * S
    H, K = PEER_HEADS, PEER_TOPK
    xt = hn.reshape(T, D)
    q = (xt @ w_q).reshape(T, H, 2, PEER_QDIM // 2)
    s = jnp.einsum('thcd,cnd->thcn', q, subkeys).astype(jnp.float32)
    s1, i1 = lax.top_k(s[:, :, 0], K)
    s2, i2 = lax.top_k(s[:, :, 1], K)
    cand = (s1[..., :, None] + s2[..., None, :]).reshape(T, H, K * K)
    sc, ci = lax.top_k(cand, K)
    e1 = jnp.take_along_axis(i1, ci // K, axis=-1)
    e2 = jnp.take_along_axis(i2, ci % K, axis=-1)
    idx = e1 * PEER_NKEYS + e2
    g = jax.nn.softmax(sc, axis=-1)
    C = PEER_CHUNK
    nc = T // C

    def body(args):
        xc, ic, gc = args
        a = jnp.einsum('cd,chkd->chk', xc, u[ic])
        act = (jax.nn.gelu(a.astype(jnp.float32)) * gc).astype(xc.dtype)
        return jnp.einsum('chk,chkd->cd', act, v[ic])

    out = lax.map(body, (xt.reshape(nc, C, D), idx.reshape(nc, C, H, K), g.reshape(nc, C, H, K)))
    return out.reshape(B, S, D)


def setup_inputs(seed: int = 0) -> dict:
    key = jax.random.key(seed)
    ks = jax.random.split(key, 24)
    f32 = jnp.float32

    def nrm(k, shape, scale):
        return jax.random.normal(k, shape, f32) * scale

    def gain(k, shape):
        return 1.0 + 0.02 * jax.random.normal(k, shape, f32)

    flat = CMP_BLOCK * HEAD_DIM
    nsa_w = NSA_HEADS * HEAD_DIM
    return {
        "x": nrm(ks[0], (BATCH, SEQ, D_MODEL), 1.0),
        "norm1_g": gain(ks[1], (D_MODEL,)),
        "w_in": nrm(ks[2], (D_MODEL, IN_COLS), D_MODEL ** -0.5),
        "nsa_q_norm": gain(ks[3], (HEAD_DIM,)),
        "nsa_k_norm": gain(ks[4], (3, HEAD_DIM)),
        "cmp_pe_k": nrm(ks[5], (CMP_BLOCK, HEAD_DIM), 0.1),
        "cmp_w1_k": nrm(ks[6], (flat, CMP_HIDDEN), flat ** -0.5),
        "cmp_w2_k": nrm(ks[7], (CMP_HIDDEN, HEAD_DIM), CMP_HIDDEN ** -0.5),
        "cmp_pe_v": nrm(ks[8], (CMP_BLOCK, HEAD_DIM), 0.1),
        "cmp_w1_v": nrm(ks[9], (flat, CMP_HIDDEN), flat ** -0.5),
        "cmp_w2_v": nrm(ks[10], (CMP_HIDDEN, HEAD_DIM), CMP_HIDDEN ** -0.5),
        "dil_q_norm": gain(ks[11], (DIL_GROUPS, HEAD_DIM)),
        "dil_k_norm": gain(ks[12], (DIL_GROUPS, HEAD_DIM)),
        "w_up_nsa": nrm(ks[13], (nsa_w, D_MODEL), nsa_w ** -0.5),
        "w_up_dil": nrm(ks[14], (DIL_WIDTH, D_MODEL), DIL_WIDTH ** -0.5),
        "w_o": nrm(ks[15], (D_MODEL, D_MODEL), D_MODEL ** -0.5),
        "norm2_g": gain(ks[16], (D_MODEL,)),
        "peer_wq": nrm(ks[17], (D_MODEL, PEER_HEADS * PEER_QDIM), D_MODEL ** -0.5),
        "peer_subkeys": nrm(ks[18], (2, PEER_NKEYS, PEER_QDIM // 2), (PEER_QDIM // 2) ** -0.5),
        "peer_u": nrm(ks[19], (PEER_N_EXPERTS, D_MODEL), D_MODEL ** -0.5),
        "peer_v": nrm(ks[20], (PEER_N_EXPERTS, D_MODEL), PEER_HEADS ** -0.5),
    }


def reference(x, norm1_g, w_in, nsa_q_norm, nsa_k_norm, cmp_pe_k, cmp_w1_k, cmp_w2_k,
              cmp_pe_v, cmp_w1_v, cmp_w2_v, dil_q_norm, dil_k_norm, w_up_nsa, w_up_dil,
              w_o, norm2_g, peer_wq, peer_subkeys, peer_u, peer_v):
    B, S, D = x.shape
    sizes = [NSA_Q_COLS] + [NSA_KV_COLS] * 6 + [NSA_GATE_COLS, DIL_COLS, MERGE_COLS]
    splits = [int(c) for c in np.cumsum(sizes)[:-1]]
    for _ in range(DEPTH):
        h = rms_norm(x, norm1_g)
        z = jnp.einsum('bsd,dc->bsc', h, w_in)
        q_n, kc, vc, ksl, vsl, kwn, vwn, g_nsa, dil, mg = jnp.split(z, splits, axis=-1)
        y_nsa = nsa_attention(q_n, kc, vc, ksl, vsl, kwn, vwn, g_nsa, nsa_q_norm, nsa_k_norm,
                              cmp_pe_k, cmp_w1_k, cmp_w2_k, cmp_pe_v, cmp_w1_v, cmp_w2_v)
        y_dil = dilated_attention(dil, dil_q_norm, dil_k_norm)
        gm = jax.nn.sigmoid(mg.astype(jnp.float32)).reshape(B, S, 2, D)
        merged = gm[:, :, 0] * (y_nsa @ w_up_nsa) + gm[:, :, 1] * (y_dil @ w_up_dil)
        x = x + merged.astype(x.dtype) @ w_o
        x = x + peer(rms_norm(x, norm2_g), peer_wq, peer_subkeys, peer_u, peer_v)
    return x
```

```python
import functools

import numpy as np
import jax
import jax.numpy as jnp
from jax import lax
from jax.experimental import pallas as pl
from jax.experimental.pallas import tpu as pltpu

F32 = jnp.float32
BF16 = jnp.bfloat16
I32 = jnp.int32

D_MODEL = 1024
HEAD_DIM = 64
ROPE_DIMS = 16
ROPE_THETA = 500000.0
NORM_EPS = 1e-6
NEG_INF = -1e30
LANES = 128

NSA_HEADS = 8
CMP_BLOCK = 32
CMP_STRIDE = 16
CMP_HIDDEN = 256
SLC_BLOCK = 64
SLC_TOPN = 16
FORCE_SCORE = 1e3
WIN = 512
NSA_QB = 128
SEL_CHUNK = 512

DIL_PATTERNS = ((128, 1), (512, 4), (2048, 16))
DIL_BLOCK = 128
DIL_PREP_ROWS = 1024

PEER_HEADS = 8
PEER_NKEYS = 128
PEER_TOPK = 16
PEER_TOK = 16

COL_MG = 0
COL_Q = 2048
COL_KV = 2560
COL_DIL = 3328
COL_GATE = 5632
IN_COLS_PAD = 5760

_NT = (((1,), (1,)), ((), ()))


def _dot(a, b):
    return jnp.dot(a, b, preferred_element_type=F32)


def _dot_nt(a, b):
    return lax.dot_general(a, b, _NT, preferred_element_type=F32)


def _split_bf16(a):
    hi = a.astype(BF16)
    lo = (a - hi.astype(F32)).astype(BF16)
    return hi, lo


def _dot_hilo(a, b_bf16):
    hi, lo = _split_bf16(a)
    return _dot(hi, b_bf16) + _dot(lo, b_bf16)


def _head_norm_rope(zt, gain, rope_a, rope_b, blockdiag):
    ss = _dot_hilo(zt * zt, blockdiag)
    zn = zt * lax.rsqrt(ss * (1.0 / HEAD_DIM) + NORM_EPS) * gain
    d = lax.broadcasted_iota(I32, zn.shape, 1) & (HEAD_DIM - 1)
    half = ROPE_DIMS // 2
    partner = jnp.where(d < half, pltpu.roll(zn, LANES - half, 1), pltpu.roll(zn, half, 1))
    return zn * rope_a + partner * rope_b


def _half_masks():
    lane = lax.broadcasted_iota(I32, (1, LANES), 1)
    lo = (lane < HEAD_DIM).astype(BF16)
    return lo, (1 - lo).astype(BF16)


def _inproj_kernel(x_ref, g_ref, w_ref, o_ref, h_scr):
    @pl.when(pl.program_id(1) == 0)
    def _():
        xf = x_ref[...]
        ms = jnp.mean(xf * xf, axis=-1, keepdims=True)
        h_scr[...] = (xf * lax.rsqrt(ms + NORM_EPS) * g_ref[...]).astype(BF16)

    o_ref[...] = _dot(h_scr[...], w_ref[...])


def _in_proj(x2, g1, w_bf16):
    T = x2.shape[0]
    tm, tn = 512, 640
    return pl.pallas_call(
        _inproj_kernel,
        out_shape=jax.ShapeDtypeStruct((T, IN_COLS_PAD), F32),
        grid=(T // tm, IN_COLS_PAD // tn),
        in_specs=[
            pl.BlockSpec((tm, D_MODEL), lambda i, j: (i, 0)),
            pl.BlockSpec((1, D_MODEL), lambda i, j: (0, 0)),
            pl.BlockSpec((D_MODEL, tn), lambda i, j: (0, j)),
        ],
        out_specs=pl.BlockSpec((tm, tn), lambda i, j: (i, j)),
        scratch_shapes=[pltpu.VMEM((tm, D_MODEL), BF16)],
        compiler_params=pltpu.CompilerParams(dimension_semantics=("parallel", "arbitrary")),
        name="in_proj",
    )(x2, g1, w_bf16)


def _nsa_prep_kernel(zq_ref, zks_ref, zvs_ref, zkw_ref, zvw_ref, ra_ref, rb_ref, bd_ref, gq_ref, gks_ref, gkw_ref,
                     q_ref, ks_ref, vs_ref, kw_ref, vw_ref):
    ra, rb, bd = ra_ref[...], rb_ref[...], bd_ref[...]
    lane = lax.broadcasted_iota(I32, ra.shape, 1)
    scale = HEAD_DIM ** -0.5

    zq = zq_ref[0]
    tiles = [_head_norm_rope(zq[:, m * LANES:(m + 1) * LANES], gq_ref[...], ra, rb, bd) * scale for m in range(4)]
    q_ref[0] = jnp.concatenate(tiles, axis=-1).astype(BF16)

    def dup(t, out_ref):
        sw = pltpu.roll(t, HEAD_DIM, 1)
        out_ref[0, 0] = jnp.where(lane < HEAD_DIM, t, sw).astype(BF16)
        out_ref[0, 1] = jnp.where(lane < HEAD_DIM, sw, t).astype(BF16)

    dup(_head_norm_rope(zks_ref[0], gks_ref[...], ra, rb, bd), ks_ref)
    dup(_head_norm_rope(zkw_ref[0], gkw_ref[...], ra, rb, bd), kw_ref)
    dup(zvs_ref[0], vs_ref)
    dup(zvw_ref[0], vw_ref)


def _nsa_prep(z3, rope_a, rope_b, blockdiag, gq, gks, gkw):
    B, S, _ = z3.shape
    tm = 512
    kvb = COL_KV // LANES
    zcol = lambda c: pl.BlockSpec((1, tm, LANES), lambda b, i, c=c: (b, i, c))
    const = lambda shape: pl.BlockSpec(shape, lambda b, i: tuple(0 for _ in shape))
    kv_out = pl.BlockSpec((1, 2, tm, LANES), lambda b, i: (b, 0, i, 0))
    kv_shape = jax.ShapeDtypeStruct((B, 2, S, LANES), BF16)
    return pl.pallas_call(
        _nsa_prep_kernel,
        out_shape=(jax.ShapeDtypeStruct((B, S, 512), BF16), kv_shape, kv_shape, kv_shape, kv_shape),
        grid=(B, S // tm),
        in_specs=[
            pl.BlockSpec((1, tm, 512), lambda b, i: (b, i, COL_Q // 512)),
            zcol(kvb + 2), zcol(kvb + 3), zcol(kvb + 4), zcol(kvb + 5),
            pl.BlockSpec((tm, LANES), lambda b, i: (i, 0)),
            pl.BlockSpec((tm, LANES), lambda b, i: (i, 0)),
            const((LANES, LANES)), const((1, LANES)), const((1, LANES)), const((1, LANES)),
        ],
        out_specs=(pl.BlockSpec((1, tm, 512), lambda b, i: (b, i, 0)), kv_out, kv_out, kv_out, kv_out),
        compiler_params=pltpu.CompilerParams(dimension_semantics=("parallel", "parallel")),
        name="nsa_prep",
    )(z3, z3, z3, z3, z3, rope_a, rope_b, blockdiag, gq, gks, gkw)


def _compress_kernel(fk_ref, fv_ref, pek_ref, pev_ref, w1k_ref, w1v_ref, w2k_ref, w2v_ref, ra_ref, rb_ref, bd_ref, gk_ref,
                     k_ref, v_ref):
    def mlp(f_ref, pe_ref, w1_ref, w2_ref):
        f = (f_ref[...] + pe_ref[...]).astype(BF16)
        h = jax.nn.gelu(_dot(f, w1_ref[...]))
        return _dot(h.astype(BF16), w2_ref[...])

    kc = mlp(fk_ref, pek_ref, w1k_ref, w2k_ref)
    k_ref[...] = _head_norm_rope(kc, gk_ref[...], ra_ref[...], rb_ref[...], bd_ref[...]).astype(BF16)
    v_ref[...] = mlp(fv_ref, pev_ref, w1v_ref, w2v_ref).astype(BF16)


def _compress(flat_k, flat_v, pek, pev, w1k, w1v, w2k, w2v, rope_a, rope_b, blockdiag, gk):
    rows, width = flat_k.shape
    nblk = rope_a.shape[0]
    row = pl.BlockSpec((nblk, width), lambda i: (i, 0))
    const = lambda shape: pl.BlockSpec(shape, lambda i: tuple(0 for _ in shape))
    out = pl.BlockSpec((nblk, LANES), lambda i: (i, 0))
    shp = jax.ShapeDtypeStruct((rows, LANES), BF16)
    return pl.pallas_call(
        _compress_kernel,
        out_shape=(shp, shp),
        grid=(rows // nblk,),
        in_specs=[row, row, const((1, width)), const((1, width)), const((width, CMP_HIDDEN)), const((width, CMP_HIDDEN)),
                  const((CMP_HIDDEN, LANES)), const((CMP_HIDDEN, LANES)), const((nblk, LANES)), const((nblk, LANES)),
                  const((LANES, LANES)), const((1, LANES))],
        out_specs=(out, out),
        compiler_params=pltpu.CompilerParams(dimension_semantics=("parallel",)),
        name="compress",
    )(flat_k, flat_v, pek, pev, w1k, w1v, w2k, w2v, rope_a, rope_b, blockdiag, gk)


def _softmax_rows(s):
    m = jnp.max(s, axis=-1, keepdims=True)
    e = jnp.exp(s - m)
    return e / jnp.sum(e, axis=-1, keepdims=True)


def _nsa_kernel(q_ref, kc_ref, vc_ref, ks_ref, vs_ref, kw_ref, vw_ref, gl_ref, ov_ref, y_ref):
    n = pl.program_id(1)
    t0 = n * NSA_QB
    qt = q_ref[0]
    hm = _half_masks()
    lane = lax.broadcasted_iota(I32, (NSA_QB, LANES), 1)
    t1 = t0 + lax.broadcasted_iota(I32, (NSA_QB, 1), 0)
    t4 = t0 + (lax.broadcasted_iota(I32, (4 * NSA_QB, 1), 0) & (NSA_QB - 1))
    gates = jax.nn.sigmoid(gl_ref[0])

    blk = lax.broadcasted_iota(I32, (LANES, NSA_QB), 0)
    tq = t0 + lax.broadcasted_iota(I32, (LANES, NSA_QB), 1)
    cur = tq >> 6
    forced = (blk == 0) | (blk == cur) | (blk == cur - 1)
    causal_b = blk * SLC_BLOCK <= tq
    n_slc_blocks = ks_ref.shape[2] // SLC_BLOCK

    tiles_out = []
    for g in range(2):
        q4 = jnp.concatenate(
            [qt[:, (2 * g + jj // 2) * LANES:(2 * g + jj // 2 + 1) * LANES] * hm[jj % 2] for jj in range(4)], axis=0)

        sc = _dot_nt(q4, kc_ref[0, g])
        cend = lax.broadcasted_iota(I32, (1, sc.shape[1]), 1) * CMP_STRIDE + (CMP_BLOCK - 1)
        valid_c = cend <= t4
        pc = _softmax_rows(jnp.where(valid_c, sc, NEG_INF))
        pc = jnp.where(valid_c, pc, 0.0)
        o_c = _dot(pc.astype(BF16), vc_ref[0, g])
        psum = pc[0:NSA_QB] + pc[NSA_QB:2 * NSA_QB] + pc[2 * NSA_QB:3 * NSA_QB] + pc[3 * NSA_QB:]
        imp = _dot_hilo(psum, ov_ref[...])

        score = jnp.where(forced, FORCE_SCORE, jnp.where(causal_b, imp.T, -1.0))
        score = jnp.where(blk < n_slc_blocks, score, -2.0)
        rank = jnp.zeros(score.shape, F32)
        for i in range(n_slc_blocks):
            ri = score[i:i + 1, :]
            ahead = (ri > score) | ((ri == score) & (blk > i))
            rank = rank + jnp.where(ahead, 1.0, 0.0)
        sel = (rank < float(SLC_TOPN)).astype(F32).T.astype(BF16)

        def sel_body(c, carry):
            m_i, l_i, acc = carry
            k0 = pl.multiple_of(c * SEL_CHUNK, SEL_CHUNK)
            kch = ks_ref[0, g, pl.ds(k0, SEL_CHUNK), :]
            vch = vs_ref[0, g, pl.ds(k0, SEL_CHUNK), :]
            s = _dot_nt(q4, kch)
            kpos = k0 + lax.broadcasted_iota(I32, (1, SEL_CHUNK), 1)
            expand = (lax.broadcasted_iota(I32, (LANES, SEL_CHUNK), 0)
                      == ((k0 + lax.broadcasted_iota(I32, (LANES, SEL_CHUNK), 1)) >> 6)).astype(BF16)
            allow = jnp.where((_dot(sel, expand) > 0.5) & (kpos <= t1), 0.0, 1.0)
            blocked = jnp.concatenate([allow] * 4, axis=0) > 0.5
            s = jnp.where(blocked, NEG_INF, s)
            m_new = jnp.maximum(m_i, jnp.max(s, axis=-1, keepdims=True))
            alpha = jnp.exp(m_i - m_new)
            p = jnp.exp(s - m_new)
            l_new = alpha * l_i + jnp.sum(p, axis=-1, keepdims=True)
            acc_new = alpha * acc + _dot(p.astype(BF16), vch)
            return m_new, l_new, acc_new

        init = (jnp.full((4 * NSA_QB, 1), NEG_INF, F32), jnp.zeros((4 * NSA_QB, 1), F32),
                jnp.zeros((4 * NSA_QB, LANES), F32))
        n_chunks = (t0 + NSA_QB + SEL_CHUNK - 1) // SEL_CHUNK
        _, l_s, acc_s = lax.fori_loop(0, n_chunks, sel_body, init)
        o_s = acc_s / l_s

        wlen = WIN + NSA_QB
        ws = pl.multiple_of(jnp.maximum(t0 - WIN, 0), NSA_QB)
        sw = _dot_nt(q4, kw_ref[0, g, pl.ds(ws, wlen), :])
        dist = t4 - (ws + lax.broadcasted_iota(I32, (1, wlen), 1))
        pw = _softmax_rows(jnp.where((dist >= 0) & (dist < WIN), sw, NEG_INF))
        o_w = _dot(pw.astype(BF16), vw_ref[0, g, pl.ds(ws, wlen), :])

        heads = []
        for jj in range(4):
            h = 4 * g + jj
            rows = slice(jj * NSA_QB, (jj + 1) * NSA_QB)
            heads.append(gates[:, 3 * h:3 * h + 1] * o_c[rows] + gates[:, 3 * h + 1:3 * h + 2] * o_s[rows]
                         + gates[:, 3 * h + 2:3 * h + 3] * o_w[rows])
        tiles_out.append(jnp.where(lane < HEAD_DIM, heads[0], heads[1]))
        tiles_out.append(jnp.where(lane < HEAD_DIM, heads[2], heads[3]))

    y_ref[0] = jnp.concatenate(tiles_out, axis=-1).astype(BF16)


def _nsa_attn(qn, kcd, vcd, ksd, vsd, kwd, vwd, z3, overlap):
    B, S, _ = qn.shape
    ncp = kcd.shape[2]
    full = lambda rows: pl.BlockSpec((1, 2, rows, LANES), lambda b, n: (b, 0, 0, 0))
    return pl.pallas_call(
        _nsa_kernel,
        out_shape=jax.ShapeDtypeStruct((B, S, 512), BF16),
        grid=(B, S // NSA_QB),
        in_specs=[
            pl.BlockSpec((1, NSA_QB, 512), lambda b, n: (b, n, 0)),
            full(ncp), full(ncp), full(S), full(S), full(S), full(S),
            pl.BlockSpec((1, NSA_QB, LANES), lambda b, n: (b, n, COL_GATE // LANES)),
            pl.BlockSpec((ncp, LANES), lambda b, n: (0, 0)),
        ],
        out_specs=pl.BlockSpec((1, NSA_QB, 512), lambda b, n: (b, n, 0)),
        compiler_params=pltpu.CompilerParams(dimension_semantics=("parallel", "arbitrary"),
                                             vmem_limit_bytes=48 * 1024 * 1024),
        name="nsa_attn",
    )(qn, kcd, vcd, ksd, vsd, kwd, vwd, z3, overlap)


def _dil_prep_kernel(*refs):
    zs, (ra_ref, rb_ref, bd_ref, gq_ref, gk_ref), outs = refs[0:18], refs[18:23], refs[23:32]
    bd = bd_ref[...]
    scale = HEAD_DIM ** -0.5
    for g, (_, d) in enumerate(DIL_PATTERNS):
        n = DIL_PREP_ROWS // d
        for r in range(d):
            rows = pl.ds(r, n, stride=d) if d > 1 else pl.ds(0, n)
            ra, rb = ra_ref[rows, :], rb_ref[rows, :]
            for which in range(3):
                for m in range(2):
                    z = zs[2 * (3 * g + which) + m][0, rows, :]
                    if which == 0:
                        z = _head_norm_rope(z, gq_ref[g], ra, rb, bd) * scale
                    elif which == 1:
                        z = _head_norm_rope(z, gk_ref[g], ra, rb, bd)
                    outs[3 * g + which][0, r, :, m * LANES:(m + 1) * LANES] = z.astype(BF16)


def _dil_prep(z3, rope_a, rope_b, blockdiag, gq, gk):
    B, S, _ = z3.shape
    nsteps = S // DIL_PREP_ROWS
    c0 = COL_DIL // LANES
    in_specs = [pl.BlockSpec((1, DIL_PREP_ROWS, LANES), lambda b, c, k=k: (b, c, c0 + k)) for k in range(18)]
    in_specs += [
        pl.BlockSpec((DIL_PREP_ROWS, LANES), lambda b, c: (c, 0)),
        pl.BlockSpec((DIL_PREP_ROWS, LANES), lambda b, c: (c, 0)),
        pl.BlockSpec((LANES, LANES), lambda b, c: (0, 0)),
        pl.BlockSpec((3, 1, LANES), lambda b, c: (0, 0, 0)),
        pl.BlockSpec((3, 1, LANES), lambda b, c: (0, 0, 0)),
    ]
    out_shape, out_specs = [], []
    for _, d in DIL_PATTERNS:
        for _ in range(3):
            out_shape.append(jax.ShapeDtypeStruct((B, d, S // d, 256), BF16))
            out_specs.append(pl.BlockSpec((1, d, DIL_PREP_ROWS // d, 256), lambda b, c: (b, 0, c, 0)))
    return pl.pallas_call(
        _dil_prep_kernel,
        out_shape=tuple(out_shape),
        grid=(B, nsteps),
        in_specs=in_specs,
        out_specs=tuple(out_specs),
        compiler_params=pltpu.CompilerParams(dimension_semantics=("parallel", "parallel"),
                                             vmem_limit_bytes=48 * 1024 * 1024),
        name="dil_prep",
    )(*([z3] * 18), rope_a, rope_b, blockdiag, gq, gk)


def _dil_kernel(*refs, seq):
    q_refs, k_refs, v_refs, y_ref, o_scr, l_scr = refs[0:3], refs[3:6], refs[6:9], refs[9], refs[10], refs[11]
    hm = _half_masks()
    lane = lax.broadcasted_iota(I32, (DIL_BLOCK, LANES), 1)
    qi = lax.broadcasted_iota(I32, (2 * DIL_BLOCK, 2 * DIL_BLOCK), 0) & (DIL_BLOCK - 1)
    ki = lax.broadcasted_iota(I32, (2 * DIL_BLOCK, 2 * DIL_BLOCK), 1)
    causal = (ki - DIL_BLOCK) <= qi

    for g, (_, d) in enumerate(DIL_PATTERNS):
        nb = seq // d // DIL_BLOCK

        def body(u, carry, g=g, d=d, nb=nb):
            j = u % nb
            r = u // nb
            r0 = pl.multiple_of(u * DIL_BLOCK, DIL_BLOCK)
            p0 = pl.multiple_of(jnp.maximum(u - 1, 0) * DIL_BLOCK, DIL_BLOCK)
            q = q_refs[g][0, pl.ds(r0, DIL_BLOCK), :]
            kcat = jnp.concatenate([k_refs[g][0, pl.ds(p0, DIL_BLOCK), :], k_refs[g][0, pl.ds(r0, DIL_BLOCK), :]], axis=0)
            vcat = jnp.concatenate([v_refs[g][0, pl.ds(p0, DIL_BLOCK), :], v_refs[g][0, pl.ds(r0, DIL_BLOCK), :]], axis=0)
            q2 = jnp.concatenate([q * hm[0], q * hm[1]], axis=0)
            s = _dot_nt(q2, kcat)
            first_key = jnp.maximum(qi, jnp.where(j >= 1, 0, DIL_BLOCK))
            s = jnp.where(causal & (ki >= first_key), s, NEG_INF)
            m = jnp.max(s, axis=-1, keepdims=True)
            e = jnp.exp(s - m)
            den = jnp.sum(e, axis=-1, keepdims=True)
            o2 = _dot(e.astype(BF16), vcat) / den
            lse = m + jnp.log(den)
            o = jnp.where(lane < HEAD_DIM, o2[:DIL_BLOCK], o2[DIL_BLOCK:])
            lv = jnp.where(lane < HEAD_DIM, lse[:DIL_BLOCK], lse[DIL_BLOCK:])
            tok0 = j * (DIL_BLOCK * d) + r
            rows = pl.ds(tok0, DIL_BLOCK, stride=d) if d > 1 else pl.ds(pl.multiple_of(tok0, DIL_BLOCK), DIL_BLOCK)
            o_scr[g, rows, :] = o
            l_scr[g, rows, :] = lv
            return carry

        lax.fori_loop(0, seq // DIL_BLOCK, body, 0)

    def merge(c, carry):
        rows = pl.ds(pl.multiple_of(c * 512, 512), 512)
        ls = [l_scr[g, rows, :] for g in range(3)]
        mx = jnp.maximum(jnp.maximum(ls[0], ls[1]), ls[2])
        ws = [jnp.exp(l - mx) for l in ls]
        num = ws[0] * o_scr[0, rows, :] + ws[1] * o_scr[1, rows, :] + ws[2] * o_scr[2, rows, :]
        y_ref[0, rows, :] = (num / (ws[0] + ws[1] + ws[2])).astype(BF16)
        return carry

    lax.fori_loop(0, seq // 512, merge, 0)


def _dil_attn(dq, dk, dv):
    B, S, _ = dq[0].shape
    spec = pl.BlockSpec((1, S, LANES), lambda b, m: (b, 0, m))
    return pl.pallas_call(
        functools.partial(_dil_kernel, seq=S),
        out_shape=jax.ShapeDtypeStruct((B, S, 256), BF16),
        grid=(B, 2),
        in_specs=[spec] * 9,
        out_specs=spec,
        scratch_shapes=[pltpu.VMEM((3, S, LANES), F32), pltpu.VMEM((3, S, LANES), F32)],
        compiler_params=pltpu.CompilerParams(dimension_semantics=("parallel", "parallel"),
                                             vmem_limit_bytes=56 * 1024 * 1024),
        name="dil_attn",
    )(*dq, *dk, *dv)


def _merge_kernel(x_ref, yn_ref, yd_ref, mg0_ref, mg1_ref, wn_ref, wd_ref, wo_ref, g2_ref, wq_ref, x1_ref, hn_ref, pq_ref):
    u1 = _dot(yn_ref[...], wn_ref[...])
    u2 = _dot(yd_ref[...], wd_ref[...])
    merged = jax.nn.sigmoid(mg0_ref[...]) * u1 + jax.nn.sigmoid(mg1_ref[...]) * u2
    x1 = x_ref[...] + _dot(merged.astype(BF16), wo_ref[...])
    x1_ref[...] = x1
    ms = jnp.mean(x1 * x1, axis=-1, keepdims=True)
    hn = x1 * lax.rsqrt(ms + NORM_EPS) * g2_ref[...]
    hn_ref[...] = hn
    pq_ref[...] = _dot(hn.astype(BF16), wq_ref[...])


def _merge(x2, yn2, yd2, z2, wn, wd, wo, g2, wq):
    T = x2.shape[0]
    tm = 512
    row = lambda w, c=0: pl.BlockSpec((tm, w), lambda i, c=c: (i, c))
    const = lambda shape: pl.BlockSpec(shape, lambda i: (0, 0))
    shp = jax.ShapeDtypeStruct((T, D_MODEL), F32)
    return pl.pallas_call(
        _merge_kernel,
        out_shape=(shp, shp, shp),
        grid=(T // tm,),
        in_specs=[row(D_MODEL), row(512), row(256), row(D_MODEL, COL_MG // D_MODEL), row(D_MODEL, COL_MG // D_MODEL + 1),
                  const((512, D_MODEL)), const((256, D_MODEL)), const((D_MODEL, D_MODEL)), const((1, D_MODEL)),
                  const((D_MODEL, D_MODEL))],
        out_specs=(row(D_MODEL), row(D_MODEL), row(D_MODEL)),
        compiler_params=pltpu.CompilerParams(dimension_semantics=("parallel",), vmem_limit_bytes=48 * 1024 * 1024),
        name="merge",
    )(x2, yn2, yd2, z2, z2, wn, wd, wo, g2, wq)


def _top16(s):
    n = s.shape[0]
    pos_iota = lax.broadcasted_iota(I32, s.shape, 0)
    vals, poss = [], []
    for _ in range(PEER_TOPK):
        m = jnp.max(s, axis=0, keepdims=True)
        pos = jnp.min(jnp.where(s == m, pos_iota, n), axis=0, keepdims=True)
        vals.append(m)
        poss.append(pos)
        s = jnp.where(pos_iota == pos, -jnp.inf, s)
    return vals, poss


def _route_kernel(q_ref, sk_ref, idx_ref, gate_ref):
    qh, ql = _split_bf16(q_ref[...])
    vals, ids = [], []
    for c in range(2):
        kh, kl = _split_bf16(sk_ref[c])
        s = _dot_nt(kh, qh) + _dot_nt(kh, ql) + _dot_nt(kl, qh)
        v, p = _top16(s)
        vals.append(v)
        ids.append(p)
    cand = jnp.concatenate([vals[0][a] + jnp.concatenate(vals[1], axis=0) for a in range(PEER_TOPK)], axis=0)
    eid = jnp.concatenate([ids[0][a] * PEER_NKEYS + jnp.concatenate(ids[1], axis=0) for a in range(PEER_TOPK)], axis=0)
    v, p = _top16(cand)
    pos_iota = lax.broadcasted_iota(I32, eid.shape, 0)
    sel_ids = [jnp.sum(jnp.where(pos_iota == pk, eid, 0), axis=0, keepdims=True) for pk in p]
    sc = jnp.concatenate(v, axis=0)
    e = jnp.exp(sc - sc[0:1])
    gate_ref[...] = e / jnp.sum(e, axis=0, keepdims=True)
    idx_ref[...] = jnp.concatenate(sel_ids, axis=0)


def _peer_route(pq, sk_pad):
    T = pq.shape[0]
    tt = 256
    return pl.pallas_call(
        _route_kernel,
        out_shape=(jax.ShapeDtypeStruct((PEER_HEADS * PEER_TOPK, T), I32),
                   jax.ShapeDtypeStruct((PEER_HEADS * PEER_TOPK, T), F32)),
        grid=(T // tt, PEER_HEADS),
        in_specs=[pl.BlockSpec((tt, LANES), lambda i, h: (i, h)),
                  pl.BlockSpec((2, PEER_NKEYS, LANES), lambda i, h: (0, 0, 0))],
        out_specs=(pl.BlockSpec((PEER_TOPK, tt), lambda i, h: (h, i)),
                   pl.BlockSpec((PEER_TOPK, tt), lambda i, h: (h, i))),
        compiler_params=pltpu.CompilerParams(dimension_semantics=("parallel", "parallel")),
        name="peer_route",
    )(pq, sk_pad)


def _unpack_pair(w):
    lo = pltpu.bitcast(w << 16, F32)
    hi = pltpu.bitcast(w & jnp.uint32(0xFFFF0000), F32)
    return lo, hi


def _peer_kernel(idx_ref, hn_ref, gate_ref, x1_ref, u_hbm, v_hbm, o_ref, ubuf, vbuf, sem):
    nk = PEER_HEADS * PEER_TOPK
    nrows = PEER_TOK * nk
    half = D_MODEL // 2

    def row_copies(p):
        e = idx_ref[p]
        return (pltpu.make_async_copy(u_hbm.at[pl.ds(e, 1)], ubuf.at[pl.ds(p, 1)], sem.at[0]),
                pltpu.make_async_copy(v_hbm.at[pl.ds(e, 1)], vbuf.at[pl.ds(p, 1)], sem.at[1]))

    def issue(p, carry):
        cu, cv = row_copies(p)
        cu.start()
        cv.start()
        return carry

    def drain(p, carry):
        cu, cv = row_copies(p)
        cu.wait()
        cv.wait()
        return carry

    lax.fori_loop(0, nrows, issue, 0, unroll=8)
    lax.fori_loop(0, nrows, drain, 0, unroll=32)

    eye = lax.broadcasted_iota(I32, (nk, nk), 0) == lax.broadcasted_iota(I32, (nk, nk), 1)

    def token(c, carry):
        rows = pl.ds(pl.multiple_of(c * nk, nk), nk)
        x = hn_ref[pl.ds(c, 1), :]
        ulo, uhi = _unpack_pair(ubuf[rows, :])
        a = jnp.sum(ulo * x[:, :half] + uhi * x[:, half:], axis=-1, keepdims=True)
        gcol = jnp.sum(jnp.where(eye, gate_ref[pl.ds(c, 1), :], 0.0), axis=-1, keepdims=True)
        act = jax.nn.gelu(a) * gcol
        vlo, vhi = _unpack_pair(vbuf[rows, :])
        out = jnp.concatenate([jnp.sum(act * vlo, axis=0, keepdims=True), jnp.sum(act * vhi, axis=0, keepdims=True)],
                              axis=-1)
        o_ref[pl.ds(c, 1), :] = x1_ref[pl.ds(c, 1), :] + out
        return carry

    lax.fori_loop(0, PEER_TOK, token, 0)


def _peer_expert(idx_flat, hn, gates, x1, u_packed, v_packed):
    T = hn.shape[0]
    nk = PEER_HEADS * PEER_TOPK
    row = lambda w: pl.BlockSpec((PEER_TOK, w), lambda i: (i, 0))
    return pl.pallas_call(
        _peer_kernel,
        out_shape=jax.ShapeDtypeStruct((T, D_MODEL), F32),
        grid=(T // PEER_TOK,),
        in_specs=[pl.BlockSpec((PEER_TOK * nk,), lambda i: (i,), memory_space=pltpu.SMEM),
                  row(D_MODEL), row(nk), row(D_MODEL),
                  pl.BlockSpec(memory_space=pl.ANY), pl.BlockSpec(memory_space=pl.ANY)],
        out_specs=row(D_MODEL),
        scratch_shapes=[pltpu.VMEM((PEER_TOK * nk, D_MODEL // 2), jnp.uint32),
                        pltpu.VMEM((PEER_TOK * nk, D_MODEL // 2), jnp.uint32),
                        pltpu.SemaphoreType.DMA((2,))],
        compiler_params=pltpu.CompilerParams(dimension_semantics=("arbitrary",)),
        name="peer_expert",
    )(idx_flat, hn, gates, x1, u_packed, v_packed)


def _rope_tables(pos):
    half = ROPE_DIMS // 2
    inv = ROPE_THETA ** (-(jnp.arange(half, dtype=F32) * 2.0 / ROPE_DIMS))
    ang = pos.astype(F32)[:, None] * inv[None, :]
    cos, sin = jnp.cos(ang), jnp.sin(ang)
    n = pos.shape[0]
    a = jnp.concatenate([cos, cos, jnp.ones((n, HEAD_DIM - ROPE_DIMS), F32)], axis=-1)
    b = jnp.concatenate([-sin, sin, jnp.zeros((n, HEAD_DIM - ROPE_DIMS), F32)], axis=-1)
    return jnp.tile(a, (1, 2)), jnp.tile(b, (1, 2))


def _pack_bf16_pairs(w):
    half = w.shape[1] // 2
    bits = lax.bitcast_convert_type(w.astype(BF16), jnp.uint16).astype(jnp.uint32)
    return bits[:, :half] | (bits[:, half:] << 16)


def _tile2(v):
    return jnp.tile(v.reshape(1, HEAD_DIM), (1, 2))


def kernel(x, norm1_g, w_in, nsa_q_norm, nsa_k_norm, cmp_pe_k, cmp_w1_k, cmp_w2_k, cmp_pe_v, cmp_w1_v, cmp_w2_v,
           dil_q_norm, dil_k_norm, w_up_nsa, w_up_dil, w_o, norm2_g, peer_wq, peer_subkeys, peer_u, peer_v):
    B, S, D = x.shape
    T = B * S
    assert D == D_MODEL and S % (DIL_PATTERNS[-1][1] * DIL_BLOCK) == 0 and S >= WIN + NSA_QB and T % 512 == 0
    x2 = x.reshape(T, D)

    n_q, n_kv, n_gate, n_dil = 512, 768, 24, 2304
    o_gate = n_q + n_kv
    o_dil = o_gate + n_gate
    o_mg = o_dil + n_dil
    w_perm = jnp.concatenate([w_in[:, o_mg:], w_in[:, :o_gate], w_in[:, o_dil:o_mg], w_in[:, o_gate:o_dil],
                              jnp.zeros((D, IN_COLS_PAD - w_in.shape[1]), w_in.dtype)], axis=1).astype(BF16)
    z2 = _in_proj(x2, norm1_g.reshape(1, D), w_perm)
    z3 = z2.reshape(B, S, IN_COLS_PAD)

    blockdiag = jnp.asarray(np.kron(np.eye(2), np.ones((HEAD_DIM, HEAD_DIM))), BF16)
    rope_a, rope_b = _rope_tables(jnp.arange(S))

    qn, ksd, vsd, kwd, vwd = _nsa_prep(z3, rope_a, rope_b, blockdiag, _tile2(nsa_q_norm), _tile2(nsa_k_norm[1]),
                                       _tile2(nsa_k_norm[2]))

    n_cmp = (S - CMP_BLOCK) // CMP_STRIDE + 1
    ncp = S // CMP_STRIDE
    def flat_blocks(col):
        zc = z3[:, :, col:col + LANES].reshape(B, S, 2, HEAD_DIM).transpose(0, 2, 1, 3)
        r = zc.reshape(B, 2, ncp, CMP_STRIDE * HEAD_DIM)
        nxt = jnp.concatenate([r[:, :, 1:], jnp.zeros_like(r[:, :, :1])], axis=2)
        return jnp.concatenate([r, nxt], axis=-1).reshape(B * 2 * ncp, CMP_BLOCK * HEAD_DIM)
    cmp_a, cmp_b = _rope_tables(jnp.arange(ncp) * CMP_STRIDE + CMP_BLOCK - 1)
    dup2 = lambda w: jnp.concatenate([w, w], axis=1).astype(BF16)
    kcd, vcd = _compress(flat_blocks(COL_KV), flat_blocks(COL_KV + LANES),
                         cmp_pe_k.reshape(1, -1), cmp_pe_v.reshape(1, -1), cmp_w1_k.astype(BF16), cmp_w1_v.astype(BF16),
                         dup2(cmp_w2_k), dup2(cmp_w2_v), cmp_a, cmp_b, blockdiag, _tile2(nsa_k_norm[0]))
    kcd = kcd.reshape(B, 2, ncp, LANES)
    vcd = vcd.reshape(B, 2, ncp, LANES)

    n_slc = S // SLC_BLOCK
    s0 = np.arange(n_cmp) * CMP_STRIDE
    b0 = np.arange(n_slc) * SLC_BLOCK
    ov = np.clip(np.minimum(s0[:, None] + CMP_BLOCK, b0[None, :] + SLC_BLOCK) - np.maximum(s0[:, None], b0[None, :]),
                 0, None) / CMP_BLOCK
    ov_pad = np.zeros((ncp, LANES), np.float32)
    ov_pad[:n_cmp, :n_slc] = ov
    y_nsa = _nsa_attn(qn, kcd, vcd, ksd, vsd, kwd, vwd, z3, jnp.asarray(ov_pad, BF16))

    gq = jnp.tile(dil_q_norm.reshape(3, 1, HEAD_DIM), (1, 1, 2))
    gk = jnp.tile(dil_k_norm.reshape(3, 1, HEAD_DIM), (1, 1, 2))
    prep = _dil_prep(z3, rope_a, rope_b, blockdiag, gq, gk)
    flat = [p.reshape(B, S, 256) for p in prep]
    y_dil = _dil_attn(flat[0::3], flat[1::3], flat[2::3])

    x1, hn, pq = _merge(x2, y_nsa.reshape(T, 512), y_dil.reshape(T, 256), z2, w_up_nsa.astype(BF16),
                        w_up_dil.astype(BF16), w_o.astype(BF16), norm2_g.reshape(1, D), peer_wq.astype(BF16))

    sub = PEER_NKEYS // 2
    sk_pad = jnp.stack([jnp.pad(peer_subkeys[0], ((0, 0), (0, sub))), jnp.pad(peer_subkeys[1], ((0, 0), (sub, 0)))])
    idx_t, gate_t = _peer_route(pq, sk_pad)
    out = _peer_expert(idx_t.T.reshape(-1), hn, gate_t.T, x1, _pack_bf16_pairs(peer_u), _pack_bf16_pairs(peer_v))
    return out.reshape(B, S, D)
```

```python
import functools

import numpy as np
import jax
import jax.numpy as jnp
from jax import lax
from jax.experimental import pallas as pl
from jax.experimental.pallas import tpu as pltpu

F32 = jnp.float32
BF16 = jnp.bfloat16
I32 = jnp.int32

D_MODEL = 1024
HEAD_DIM = 64
ROPE_DIMS = 16
ROPE_THETA = 500000.0
NORM_EPS = 1e-6
NEG_INF = -1e30
LANES = 128

NSA_HEADS = 8
CMP_BLOCK = 32
CMP_STRIDE = 16
CMP_HIDDEN = 256
SLC_BLOCK = 64
SLC_TOPN = 16
FORCE_SCORE = 1e3
WIN = 512
NSA_QB = 128
SEL_CHUNK = 512

DIL_PATTERNS = ((128, 1), (512, 4), (2048, 16))
DIL_BLOCK = 128
DIL_PREP_ROWS = 1024

PEER_HEADS = 8
PEER_NKEYS = 128
PEER_TOPK = 16
PEER_TOK = 16

COL_MG = 0
COL_Q = 2048
COL_KV = 2560
COL_DIL = 3328
COL_GATE = 5632
IN_COLS_PAD = 5760

_NT = (((1,), (1,)), ((), ()))


def _dot(a, b):
    return jnp.dot(a, b, preferred_element_type=F32)


def _dot_nt(a, b):
    return lax.dot_general(a, b, _NT, preferred_element_type=F32)


def _split_bf16(a):
    hi = a.astype(BF16)
    lo = (a - hi.astype(F32)).astype(BF16)
    return hi, lo


def _dot_hilo(a, b_bf16):
    hi, lo = _split_bf16(a)
    return _dot(hi, b_bf16) + _dot(lo, b_bf16)


def _head_norm_rope(zt, gain, rope_a, rope_b, blockdiag):
    ss = _dot_hilo(zt * zt, blockdiag)
    zn = zt * lax.rsqrt(ss * (1.0 / HEAD_DIM) + NORM_EPS) * gain
    d = lax.broadcasted_iota(I32, zn.shape, 1) & (HEAD_DIM - 1)
    half = ROPE_DIMS // 2
    partner = jnp.where(d < half, pltpu.roll(zn, LANES - half, 1), pltpu.roll(zn, half, 1))
    return zn * rope_a + partner * rope_b


def _half_masks():
    lane = lax.broadcasted_iota(I32, (1, LANES), 1)
    lo = (lane < HEAD_DIM).astype(BF16)
    return lo, (1 - lo).astype(BF16)


def _inproj_kernel(x_ref, g_ref, w_ref, o_ref, h_scr):
    @pl.when(pl.program_id(1) == 0)
    def _():
        xf = x_ref[...]
        ms = jnp.mean(xf * xf, axis=-1, keepdims=True)
        h_scr[...] = (xf * lax.rsqrt(ms + NORM_EPS) * g_ref[...]).astype(BF16)

    o_ref[...] = _dot(h_scr[...], w_ref[...])


def _in_proj(x2, g1, w_bf16):
    T = x2.shape[0]
    tm, tn = 512, 640
    return pl.pallas_call(
        _inproj_kernel,
        out_shape=jax.ShapeDtypeStruct((T, IN_COLS_PAD), F32),
        grid=(T // tm, IN_COLS_PAD // tn),
        in_specs=[
            pl.BlockSpec((tm, D_MODEL), lambda i, j: (i, 0)),
            pl.BlockSpec((1, D_MODEL), lambda i, j: (0, 0)),
            pl.BlockSpec((D_MODEL, tn), lambda i, j: (0, j)),
        ],
        out_specs=pl.BlockSpec((tm, tn), lambda i, j: (i, j)),
        scratch_shapes=[pltpu.VMEM((tm, D_MODEL), BF16)],
        compiler_params=pltpu.CompilerParams(dimension_semantics=("parallel", "arbitrary")),
        name="in_proj",
    )(x2, g1, w_bf16)


def _nsa_prep_kernel(zq_ref, zks_ref, zvs_ref, zkw_ref, zvw_ref, ra_ref, rb_ref, bd_ref, gq_ref, gks_ref, gkw_ref,
                     q_ref, ks_ref, vs_ref, kw_ref, vw_ref):
    ra, rb, bd = ra_ref[...], rb_ref[...], bd_ref[...]
    lane = lax.broadcasted_iota(I32, ra.shape, 1)
    scale = HEAD_DIM ** -0.5

    zq = zq_ref[0]
    tiles = [_head_norm_rope(zq[:, m * LANES:(m + 1) * LANES], gq_ref[...], ra, rb, bd) * scale for m in range(4)]
    q_ref[0] = jnp.concatenate(tiles, axis=-1).astype(BF16)

    def dup(t, out_ref):
        sw = pltpu.roll(t, HEAD_DIM, 1)
        out_ref[0, 0] = jnp.where(lane < HEAD_DIM, t, sw).astype(BF16)
        out_ref[0, 1] = jnp.where(lane < HEAD_DIM, sw, t).astype(BF16)

    dup(_head_norm_rope(zks_ref[0], gks_ref[...], ra, rb, bd), ks_ref)
    dup(_head_norm_rope(zkw_ref[0], gkw_ref[...], ra, rb, bd), kw_ref)
    dup(zvs_ref[0], vs_ref)
    dup(zvw_ref[0], vw_ref)


def _nsa_prep(z3, rope_a, rope_b, blockdiag, gq, gks, gkw):
    B, S, _ = z3.shape
    tm = 512
    kvb = COL_KV // LANES
    zcol = lambda c: pl.BlockSpec((1, tm, LANES), lambda b, i, c=c: (b, i, c))
    const = lambda shape: pl.BlockSpec(shape, lambda b, i: tuple(0 for _ in shape))
    kv_out = pl.BlockSpec((1, 2, tm, LANES), lambda b, i: (b, 0, i, 0))
    kv_shape = jax.ShapeDtypeStruct((B, 2, S, LANES), BF16)
    return pl.pallas_call(
        _nsa_prep_kernel,
        out_shape=(jax.ShapeDtypeStruct((B, S, 512), BF16), kv_shape, kv_shape, kv_shape, kv_shape),
        grid=(B, S // tm),
        in_specs=[
            pl.BlockSpec((1, tm, 512), lambda b, i: (b, i, COL_Q // 512)),
            zcol(kvb + 2), zcol(kvb + 3), zcol(kvb + 4), zcol(kvb + 5),
            pl.BlockSpec((tm, LANES), lambda b, i: (i, 0)),
            pl.BlockSpec((tm, LANES), lambda b, i: (i, 0)),
            const((LANES, LANES)), const((1, LANES)), const((1, LANES)), const((1, LANES)),
        ],
        out_specs=(pl.BlockSpec((1, tm, 512), lambda b, i: (b, i, 0)), kv_out, kv_out, kv_out, kv_out),
        compiler_params=pltpu.CompilerParams(dimension_semantics=("parallel", "parallel")),
        name="nsa_prep",
    )(z3, z3, z3, z3, z3, rope_a, rope_b, blockdiag, gq, gks, gkw)


def _compress_kernel(fk_ref, fv_ref, pek_ref, pev_ref, w1k_ref, w1v_ref, w2k_ref, w2v_ref, ra_ref, rb_ref, bd_ref, gk_ref,
                     k_ref, v_ref):
    def mlp(f_ref, pe_ref, w1_ref, w2_ref):
        f = (f_ref[...] + pe_ref[...]).astype(BF16)
        h = jax.nn.gelu(_dot(f, w1_ref[...]))
        return _dot(h.astype(BF16), w2_ref[...])

    kc = mlp(fk_ref, pek_ref, w1k_ref, w2k_ref)
    k_ref[...] = _head_norm_rope(kc, gk_ref[...], ra_ref[...], rb_ref[...], bd_ref[...]).astype(BF16)
    v_ref[...] = mlp(fv_ref, pev_ref, w1v_ref, w2v_ref).astype(BF16)


def _compress(flat_k, flat_v, pek, pev, w1k, w1v, w2k, w2v, rope_a, rope_b, blockdiag, gk):
    rows, width = flat_k.shape
    nblk = rope_a.shape[0]
    row = pl.BlockSpec((nblk, width), lambda i: (i, 0))
    const = lambda shape: pl.BlockSpec(shape, lambda i: tuple(0 for _ in shape))
    out = pl.BlockSpec((nblk, LANES), lambda i: (i, 0))
    shp = jax.ShapeDtypeStruct((rows, LANES), BF16)
    return pl.pallas_call(
        _compress_kernel,
        out_shape=(shp, shp),
        grid=(rows // nblk,),
        in_specs=[row, row, const((1, width)), const((1, width)), const((width, CMP_HIDDEN)), const((width, CMP_HIDDEN)),
                  const((CMP_HIDDEN, LANES)), const((CMP_HIDDEN, LANES)), const((nblk, LANES)), const((nblk, LANES)),
                  const((LANES, LANES)), const((1, LANES))],
        out_specs=(out, out),
        compiler_params=pltpu.CompilerParams(dimension_semantics=("parallel",)),
        name="compress",
    )(flat_k, flat_v, pek, pev, w1k, w1v, w2k, w2v, rope_a, rope_b, blockdiag, gk)


def _softmax_rows(s):
    m = jnp.max(s, axis=-1, keepdims=True)
    e = jnp.exp(s - m)
    return e / jnp.sum(e, axis=-1, keepdims=True)


def _nsa_kernel(q_ref, kc_ref, vc_ref, ks_ref, vs_ref, kw_ref, vw_ref, gl_ref, ov_ref, y_ref):
    n = pl.program_id(1)
    t0 = n * NSA_QB
    qt = q_ref[0]
    hm = _half_masks()
    lane = lax.broadcasted_iota(I32, (NSA_QB, LANES), 1)
    t1 = t0 + lax.broadcasted_iota(I32, (NSA_QB, 1), 0)
    t4 = t0 + (lax.broadcasted_iota(I32, (4 * NSA_QB, 1), 0) & (NSA_QB - 1))
    gates = jax.nn.sigmoid(gl_ref[0])

    blk = lax.broadcasted_iota(I32, (LANES, NSA_QB), 0)
    tq = t0 + lax.broadcasted_iota(I32, (LANES, NSA_QB), 1)
    cur = tq >> 6
    forced = (blk == 0) | (blk == cur) | (blk == cur - 1)
    causal_b = blk * SLC_BLOCK <= tq
    n_slc_blocks = ks_ref.shape[2] // SLC_BLOCK

    tiles_out = []
    for g in range(2):
        q4 = jnp.concatenate(
            [qt[:, (2 * g + jj // 2) * LANES:(2 * g + jj // 2 + 1) * LANES] * hm[jj % 2] for jj in range(4)], axis=0)

        sc = _dot_nt(q4, kc_ref[0, g])
        cend = lax.broadcasted_iota(I32, (1, sc.shape[1]), 1) * CMP_STRIDE + (CMP_BLOCK - 1)
        valid_c = cend <= t4
        pc = _softmax_rows(jnp.where(valid_c, sc, NEG_INF))
        pc = jnp.where(valid_c, pc, 0.0)
        o_c = _dot(pc.astype(BF16), vc_ref[0, g])
        psum = pc[0:NSA_QB] + pc[NSA_QB:2 * NSA_QB] + pc[2 * NSA_QB:3 * NSA_QB] + pc[3 * NSA_QB:]
        imp = _dot_hilo(psum, ov_ref[...])

        score = jnp.where(forced, FORCE_SCORE, jnp.where(causal_b, imp.T, -1.0))
        score = jnp.where(blk < n_slc_blocks, score, -2.0)
        rank = jnp.zeros(score.shape, F32)
        for i in range(n_slc_blocks):
            ri = score[i:i + 1, :]
            ahead = (ri > score) | ((ri == score) & (blk > i))
            rank = rank + jnp.where(ahead, 1.0, 0.0)
        sel = (rank < float(SLC_TOPN)).astype(F32).T.astype(BF16)

        def sel_body(c, carry):
            m_i, l_i, acc = carry
            k0 = pl.multiple_of(c * SEL_CHUNK, SEL_CHUNK)
            kch = ks_ref[0, g, pl.ds(k0, SEL_CHUNK), :]
            vch = vs_ref[0, g, pl.ds(k0, SEL_CHUNK), :]
            s = _dot_nt(q4, kch)
            kpos = k0 + lax.broadcasted_iota(I32, (1, SEL_CHUNK), 1)
            expand = (lax.broadcasted_iota(I32, (LANES, SEL_CHUNK), 0)
                      == ((k0 + lax.broadcasted_iota(I32, (LANES, SEL_CHUNK), 1)) >> 6)).astype(BF16)
            allow = jnp.where((_dot(sel, expand) > 0.5) & (kpos <= t1), 0.0, 1.0)
            blocked = jnp.concatenate([allow] * 4, axis=0) > 0.5
            s = jnp.where(blocked, NEG_INF, s)
            m_new = jnp.maximum(m_i, jnp.max(s, axis=-1, keepdims=True))
            alpha = jnp.exp(m_i - m_new)
            p = jnp.exp(s - m_new)
            l_new = alpha * l_i + jnp.sum(p, axis=-1, keepdims=True)
            acc_new = alpha * acc + _dot(p.astype(BF16), vch)
            return m_new, l_new, acc_new

        init = (jnp.full((4 * NSA_QB, 1), NEG_INF, F32), jnp.zeros((4 * NSA_QB, 1), F32),
                jnp.zeros((4 * NSA_QB, LANES), F32))
        n_chunks = (t0 + NSA_QB + SEL_CHUNK - 1) // SEL_CHUNK
        _, l_s, acc_s = lax.fori_loop(0, n_chunks, sel_body, init)
        o_s = acc_s / l_s

        wlen = WIN + NSA_QB
        ws = pl.multiple_of(jnp.maximum(t0 - WIN, 0), NSA_QB)
        sw = _dot_nt(q4, kw_ref[0, g, pl.ds(ws, wlen), :])
        dist = t4 - (ws + lax.broadcasted_iota(I32, (1, wlen), 1))
        pw = _softmax_rows(jnp.where((dist >= 0) & (dist < WIN), sw, NEG_INF))
        o_w = _dot(pw.astype(BF16), vw_ref[0, g, pl.ds(ws, wlen), :])

        heads = []
        for jj in range(4):
            h = 4 * g + jj
            rows = slice(jj * NSA_QB, (jj + 1) * NSA_QB)
            heads.append(gates[:, 3 * h:3 * h + 1] * o_c[rows] + gates[:, 3 * h + 1:3 * h + 2] * o_s[rows]
                         + gates[:, 3 * h + 2:3 * h + 3] * o_w[rows])
        tiles_out.append(jnp.where(lane < HEAD_DIM, heads[0], heads[1]))
        tiles_out.append(jnp.where(lane < HEAD_DIM, heads[2], heads[3]))

    y_ref[0] = jnp.concatenate(tiles_out, axis=-1).astype(BF16)


def _nsa_attn(qn, kcd, vcd, ksd, vsd, kwd, vwd, z3, overlap):
    B, S, _ = qn.shape
    ncp = kcd.shape[2]
    full = lambda rows: pl.BlockSpec((1, 2, rows, LANES), lambda b, n: (b, 0, 0, 0))
    return pl.pallas_call(
        _nsa_kernel,
        out_shape=jax.ShapeDtypeStruct((B, S, 512), BF16),
        grid=(B, S // NSA_QB),
        in_specs=[
            pl.BlockSpec((1, NSA_QB, 512), lambda b, n: (b, n, 0)),
            full(ncp), full(ncp), full(S), full(S), full(S), full(S),
            pl.BlockSpec((1, NSA_QB, LANES), lambda b, n: (b, n, COL_GATE // LANES)),
            pl.BlockSpec((ncp, LANES), lambda b, n: (0, 0)),
        ],
        out_specs=pl.BlockSpec((1, NSA_QB, 512), lambda b, n: (b, n, 0)),
        compiler_params=pltpu.CompilerParams(dimension_semantics=("parallel", "arbitrary"),
                                             vmem_limit_bytes=48 * 1024 * 1024),
        name="nsa_attn",
    )(qn, kcd, vcd, ksd, vsd, kwd, vwd, z3, overlap)


def _dil_prep_kernel(*refs):
    zs, (ra_ref, rb_ref, bd_ref, gq_ref, gk_ref), outs = refs[0:18], refs[18:23], refs[23:32]
    bd = bd_ref[...]
    scale = HEAD_DIM ** -0.5
    for g, (_, d) in enumerate(DIL_PATTERNS):
        n = DIL_PREP_ROWS // d
        for r in range(d):
            rows = pl.ds(r, n, stride=d) if d > 1 else pl.ds(0, n)
            ra, rb = ra_ref[rows, :], rb_ref[rows, :]
            for which in range(3):
                for m in range(2):
                    z = zs[2 * (3 * g + which) + m][0, rows, :]
                    if which == 0:
                        z = _head_norm_rope(z, gq_ref[g], ra, rb, bd) * scale
                    elif which == 1:
                        z = _head_norm_rope(z, gk_ref[g], ra, rb, bd)
                    outs[3 * g + which][0, r, :, m * LANES:(m + 1) * LANES] = z.astype(BF16)


def _dil_prep(z3, rope_a, rope_b, blockdiag, gq, gk):
    B, S, _ = z3.shape
    nsteps = S // DIL_PREP_ROWS
    c0 = COL_DIL // LANES
    in_specs = [pl.BlockSpec((1, DIL_PREP_ROWS, LANES), lambda b, c, k=k: (b, c, c0 + k)) for k in range(18)]
    in_specs += [
        pl.BlockSpec((DIL_PREP_ROWS, LANES), lambda b, c: (c, 0)),
        pl.BlockSpec((DIL_PREP_ROWS, LANES), lambda b, c: (c, 0)),
        pl.BlockSpec((LANES, LANES), lambda b, c: (0, 0)),
        pl.BlockSpec((3, 1, LANES), lambda b, c: (0, 0, 0)),
        pl.BlockSpec((3, 1, LANES), lambda b, c: (0, 0, 0)),
    ]
    out_shape, out_specs = [], []
    for _, d in DIL_PATTERNS:
        for _ in range(3):
            out_shape.append(jax.ShapeDtypeStruct((B, d, S // d, 256), BF16))
            out_specs.append(pl.BlockSpec((1, d, DIL_PREP_ROWS // d, 256), lambda b, c: (b, 0, c, 0)))
    return pl.pallas_call(
        _dil_prep_kernel,
        out_shape=tuple(out_shape),
        grid=(B, nsteps),
        in_specs=in_specs,
        out_specs=tuple(out_specs),
        compiler_params=pltpu.CompilerParams(dimension_semantics=("parallel", "parallel"),
                                             vmem_limit_bytes=48 * 1024 * 1024),
        name="dil_prep",
    )(*([z3] * 18), rope_a, rope_b, blockdiag, gq, gk)


def _dil_kernel(*refs, seq):
    q_refs, k_refs, v_refs, y_ref, o_scr, l_scr = refs[0:3], refs[3:6], refs[6:9], refs[9], refs[10], refs[11]
    hm = _half_masks()
    lane = lax.broadcasted_iota(I32, (DIL_BLOCK, LANES), 1)
    qi = lax.broadcasted_iota(I32, (2 * DIL_BLOCK, 2 * DIL_BLOCK), 0) & (DIL_BLOCK - 1)
    ki = lax.broadcasted_iota(I32, (2 * DIL_BLOCK, 2 * DIL_BLOCK), 1)
    causal = (ki - DIL_BLOCK) <= qi

    for g, (_, d) in enumerate(DIL_PATTERNS):
        nb = seq // d // DIL_BLOCK

        def body(u, carry, g=g, d=d, nb=nb):
            j = u % nb
            r = u // nb
            r0 = pl.multiple_of(u * DIL_BLOCK, DIL_BLOCK)
            p0 = pl.multiple_of(jnp.maximum(u - 1, 0) * DIL_BLOCK, DIL_BLOCK)
            q = q_refs[g][0, pl.ds(r0, DIL_BLOCK), :]
            kcat = jnp.concatenate([k_refs[g][0, pl.ds(p0, DIL_BLOCK), :], k_refs[g][0, pl.ds(r0, DIL_BLOCK), :]], axis=0)
            vcat = jnp.concatenate([v_refs[g][0, pl.ds(p0, DIL_BLOCK), :], v_refs[g][0, pl.ds(r0, DIL_BLOCK), :]], axis=0)
            q2 = jnp.concatenate([q * hm[0], q * hm[1]], axis=0)
            s = _dot_nt(q2, kcat)
            first_key = jnp.maximum(qi, jnp.where(j >= 1, 0, DIL_BLOCK))
            s = jnp.where(causal & (ki >= first_key), s, NEG_INF)
            m = jnp.max(s, axis=-1, keepdims=True)
            e = jnp.exp(s - m)
            den = jnp.sum(e, axis=-1, keepdims=True)
            o2 = _dot(e.astype(BF16), vcat) / den
            lse = m + jnp.log(den)
            o = jnp.where(lane < HEAD_DIM, o2[:DIL_BLOCK], o2[DIL_BLOCK:])
            lv = jnp.where(lane < HEAD_DIM, lse[:DIL_BLOCK], lse[DIL_BLOCK:])
            tok0 = j * (DIL_BLOCK * d) + r
            rows = pl.ds(tok0, DIL_BLOCK, stride=d) if d > 1 else pl.ds(pl.multiple_of(tok0, DIL_BLOCK), DIL_BLOCK)
            o_scr[g, rows, :] = o
            l_scr[g, rows, :] = lv
            return carry

        lax.fori_loop(0, seq // DIL_BLOCK, body, 0)

    def merge(c, carry):
        rows = pl.ds(pl.multiple_of(c * 512, 512), 512)
        ls = [l_scr[g, rows, :] for g in range(3)]
        mx = jnp.maximum(jnp.maximum(ls[0], ls[1]), ls[2])
        ws = [jnp.exp(l - mx) for l in ls]
        num = ws[0] * o_scr[0, rows, :] + ws[1] * o_scr[1, rows, :] + ws[2] * o_scr[2, rows, :]
        y_ref[0, rows, :] = (num / (ws[0] + ws[1] + ws[2])).astype(BF16)
        return carry

    lax.fori_loop(0, seq // 512, merge, 0)


def _dil_attn(dq, dk, dv):
    B, S, _ = dq[0].shape
    spec = pl.BlockSpec((1, S, LANES), lambda b, m: (b, 0, m))
    return pl.pallas_call(
        functools.partial(_dil_kernel, seq=S),
        out_shape=jax.ShapeDtypeStruct((B, S, 256), BF16),
        grid=(B, 2),
        in_specs=[spec] * 9,
        out_specs=spec,
        scratch_shapes=[pltpu.VMEM((3, S, LANES), F32), pltpu.VMEM((3, S, LANES), F32)],
        compiler_params=pltpu.CompilerParams(dimension_semantics=("parallel", "parallel"),
                                             vmem_limit_bytes=56 * 1024 * 1024),
        name="dil_attn",
    )(*dq, *dk, *dv)


def _merge_kernel(x_ref, yn_ref, yd_ref, mg0_ref, mg1_ref, wn_ref, wd_ref, wo_ref, g2_ref, wq_ref, x1_ref, hn_ref, pq_ref):
    u1 = _dot(yn_ref[...], wn_ref[...])
    u2 = _dot(yd_ref[...], wd_ref[...])
    merged = jax.nn.sigmoid(mg0_ref[...]) * u1 + jax.nn.sigmoid(mg1_ref[...]) * u2
    x1 = x_ref[...] + _dot(merged.astype(BF16), wo_ref[...])
    x1_ref[...] = x1
    ms = jnp.mean(x1 * x1, axis=-1, keepdims=True)
    hn = x1 * lax.rsqrt(ms + NORM_EPS) * g2_ref[...]
    hn_ref[...] = hn
    pq_ref[...] = _dot(hn.astype(BF16), wq_ref[...])


def _merge(x2, yn2, yd2, z2, wn, wd, wo, g2, wq):
    T = x2.shape[0]
    tm = 512
    row = lambda w, c=0: pl.BlockSpec((tm, w), lambda i, c=c: (i, c))
    const = lambda shape: pl.BlockSpec(shape, lambda i: (0, 0))
    shp = jax.ShapeDtypeStruct((T, D_MODEL), F32)
    return pl.pallas_call(
        _merge_kernel,
        out_shape=(shp, shp, shp),
        grid=(T // tm,),
        in_specs=[row(D_MODEL), row(512), row(256), row(D_MODEL, COL_MG // D_MODEL), row(D_MODEL, COL_MG // D_MODEL + 1),
                  const((512, D_MODEL)), const((256, D_MODEL)), const((D_MODEL, D_MODEL)), const((1, D_MODEL)),
                  const((D_MODEL, D_MODEL))],
        out_specs=(row(D_MODEL), row(D_MODEL), row(D_MODEL)),
        compiler_params=pltpu.CompilerParams(dimension_semantics=("parallel",), vmem_limit_bytes=48 * 1024 * 1024),
        name="merge",
    )(x2, yn2, yd2, z2, z2, wn, wd, wo, g2, wq)


def _top16(s):
    n = s.shape[0]
    pos_iota = lax.broadcasted_iota(I32, s.shape, 0)
    vals, poss = [], []
    for _ in range(PEER_TOPK):
        m = jnp.max(s, axis=0, keepdims=True)
        pos = jnp.min(jnp.where(s == m, pos_iota, n), axis=0, keepdims=True)
        vals.append(m)
        poss.append(pos)
        s = jnp.where(pos_iota == pos, -jnp.inf, s)
    return vals, poss


def _route_kernel(q_ref, sk_ref, idx_ref, gate_ref):
    qh, ql = _split_bf16(q_ref[...])
    vals, ids = [], []
    for c in range(2):
        kh, kl = _split_bf16(sk_ref[c])
        s = _dot_nt(kh, qh) + _dot_nt(kh, ql) + _dot_nt(kl, qh)
        v, p = _top16(s)
        vals.append(v)
        ids.append(p)
    cand = jnp.concatenate([vals[0][a] + jnp.concatenate(vals[1], axis=0) for a in range(PEER_TOPK)], axis=0)
    eid = jnp.concatenate([ids[0][a] * PEER_NKEYS + jnp.concatenate(ids[1], axis=0) for a in range(PEER_TOPK)], axis=0)
    v, p = _top16(cand)
    pos_iota = lax.broadcasted_iota(I32, eid.shape, 0)
    sel_ids = [jnp.sum(jnp.where(pos_iota == pk, eid, 0), axis=0, keepdims=True) for pk in p]
    sc = jnp.concatenate(v, axis=0)
    e = jnp.exp(sc - sc[0:1])
    gate_ref[...] = e / jnp.sum(e, axis=0, keepdims=True)
    idx_ref[...] = jnp.concatenate(sel_ids, axis=0)


def _peer_route(pq, sk_pad):
    T = pq.shape[0]
    tt = 256
    return pl.pallas_call(
        _route_kernel,
        out_shape=(jax.ShapeDtypeStruct((PEER_HEADS * PEER_TOPK, T), I32),
                   jax.ShapeDtypeStruct((PEER_HEADS * PEER_TOPK, T), F32)),
        grid=(T // tt, PEER_HEADS),
        in_specs=[pl.BlockSpec((tt, LANES), lambda i, h: (i, h)),
                  pl.BlockSpec((2, PEER_NKEYS, LANES), lambda i, h: (0, 0, 0))],
        out_specs=(pl.BlockSpec((PEER_TOPK, tt), lambda i, h: (h, i)),
                   pl.BlockSpec((PEER_TOPK, tt), lambda i, h: (h, i))),
        compiler_params=pltpu.CompilerParams(dimension_semantics=("parallel", "parallel")),
        name="peer_route",
    )(pq, sk_pad)


PEER_NK = PEER_HEADS * PEER_TOPK
SUBLANES = 8


def _unpack_pair(w):
    lo = pltpu.bitcast(w << 16, F32)
    hi = pltpu.bitcast(w & jnp.uint32(0xFFFF0000), F32)
    return lo, hi


def _peer_kernel(idx_cur, idx_nxt, hn_ref, gate_ref, x1_ref, uv_hbm, o_ref, buf, sem):
    i = pl.program_id(0)
    last = pl.num_programs(0) - 1
    slot = i % 2
    half_rows = SUBLANES // 2

    def tile_copy(idx_ref, s, c, k):
        p = c * PEER_NK + k
        dst = buf.at[s, pl.ds(pl.multiple_of(p * SUBLANES, SUBLANES), SUBLANES), :]
        return pltpu.make_async_copy(uv_hbm.at[idx_ref[p]], dst, sem.at[s, c])

    def fetch_token(idx_ref, s, c):
        for k in range(PEER_NK):
            tile_copy(idx_ref, s, c, k).start(priority=k % 2)

    def wait_token(idx_ref, s, c):
        for k in range(PEER_NK):
            tile_copy(idx_ref, s, c, k).wait()

    @pl.when(i == 0)
    def _():
        lax.fori_loop(0, PEER_TOK, lambda c, carry: (fetch_token(idx_cur, 0, c), carry)[1], 0)

    eye = lax.broadcasted_iota(I32, (PEER_NK, PEER_NK), 0) == lax.broadcasted_iota(I32, (PEER_NK, PEER_NK), 1)

    def token(c, carry):
        wait_token(idx_cur, slot, c)
        fetch_token(idx_nxt, 1 - slot, c)
        base = c * (PEER_NK * SUBLANES)
        plane = lambda s: buf[slot, pl.ds(base + s, PEER_NK, stride=SUBLANES), :]
        xt = hn_ref[c]
        acc = jnp.zeros((PEER_NK, LANES), F32)
        for s in range(half_rows):
            lo, hi = _unpack_pair(plane(s))
            acc = acc + lo * xt[s:s + 1, :] + hi * xt[s + half_rows:s + half_rows + 1, :]
        a = jnp.sum(acc, axis=-1, keepdims=True)
        gcol = jnp.sum(jnp.where(eye, gate_ref[pl.ds(c, 1), :], 0.0), axis=-1, keepdims=True)
        act = jax.nn.gelu(a) * gcol
        lo_rows, hi_rows = [], []
        for s in range(half_rows, SUBLANES):
            lo, hi = _unpack_pair(plane(s))
            lo_rows.append(jnp.sum(act * lo, axis=0, keepdims=True))
            hi_rows.append(jnp.sum(act * hi, axis=0, keepdims=True))
        o_ref[c] = x1_ref[c] + jnp.concatenate(lo_rows + hi_rows, axis=0)
        return carry

    lax.fori_loop(0, PEER_TOK, token, 0)

    @pl.when(i == last)
    def _():
        lax.fori_loop(0, PEER_TOK, lambda c, carry: (wait_token(idx_nxt, 1 - slot, c), carry)[1], 0)


def _peer_expert(idx_flat, hn_t, gates, x1_t, uv_tiles):
    T = hn_t.shape[0]
    n = T // PEER_TOK
    tile = pl.BlockSpec((PEER_TOK, SUBLANES, LANES), lambda i: (i, 0, 0))
    return pl.pallas_call(
        _peer_kernel,
        out_shape=jax.ShapeDtypeStruct((T, SUBLANES, LANES), F32),
        grid=(n,),
        in_specs=[pl.BlockSpec((PEER_TOK * PEER_NK,), lambda i: (i,), memory_space=pltpu.SMEM),
                  pl.BlockSpec((PEER_TOK * PEER_NK,), lambda i: (jnp.minimum(i + 1, n - 1),), memory_space=pltpu.SMEM),
                  tile, pl.BlockSpec((PEER_TOK, PEER_NK), lambda i: (i, 0)), tile,
                  pl.BlockSpec(memory_space=pl.ANY)],
        out_specs=tile,
        scratch_shapes=[pltpu.VMEM((2, PEER_TOK * PEER_NK * SUBLANES, LANES), jnp.uint32),
                        pltpu.SemaphoreType.DMA((2, PEER_TOK))],
        compiler_params=pltpu.CompilerParams(dimension_semantics=("arbitrary",), vmem_limit_bytes=48 * 1024 * 1024),
        name="peer_expert",
    )(idx_flat, idx_flat, hn_t, gates, x1_t, uv_tiles)


def _rope_tables(pos):
    half = ROPE_DIMS // 2
    inv = ROPE_THETA ** (-(jnp.arange(half, dtype=F32) * 2.0 / ROPE_DIMS))
    ang = pos.astype(F32)[:, None] * inv[None, :]
    cos, sin = jnp.cos(ang), jnp.sin(ang)
    n = pos.shape[0]
    a = jnp.concatenate([cos, cos, jnp.ones((n, HEAD_DIM - ROPE_DIMS), F32)], axis=-1)
    b = jnp.concatenate([-sin, sin, jnp.zeros((n, HEAD_DIM - ROPE_DIMS), F32)], axis=-1)
    return jnp.tile(a, (1, 2)), jnp.tile(b, (1, 2))


def _pack_bf16_pairs(w):
    half = w.shape[1] // 2
    bits = lax.bitcast_convert_type(w.astype(BF16), jnp.uint16).astype(jnp.uint32)
    return bits[:, :half] | (bits[:, half:] << 16)


def _tile2(v):
    return jnp.tile(v.reshape(1, HEAD_DIM), (1, 2))


def kernel(x, norm1_g, w_in, nsa_q_norm, nsa_k_norm, cmp_pe_k, cmp_w1_k, cmp_w2_k, cmp_pe_v, cmp_w1_v, cmp_w2_v,
           dil_q_norm, dil_k_norm, w_up_nsa, w_up_dil, w_o, norm2_g, peer_wq, peer_subkeys, peer_u, peer_v):
    B, S, D = x.shape
    T = B * S
    assert D == D_MODEL and S % (DIL_PATTERNS[-1][1] * DIL_BLOCK) == 0 and S >= WIN + NSA_QB and T % 512 == 0
    x2 = x.reshape(T, D)

    n_q, n_kv, n_gate, n_dil = 512, 768, 24, 2304
    o_gate = n_q + n_kv
    o_dil = o_gate + n_gate
    o_mg = o_dil + n_dil
    w_perm = jnp.concatenate([w_in[:, o_mg:], w_in[:, :o_gate], w_in[:, o_dil:o_mg], w_in[:, o_gate:o_dil],
                              jnp.zeros((D, IN_COLS_PAD - w_in.shape[1]), w_in.dtype)], axis=1).astype(BF16)
    z2 = _in_proj(x2, norm1_g.reshape(1, D), w_perm)
    z3 = z2.reshape(B, S, IN_COLS_PAD)

    blockdiag = jnp.asarray(np.kron(np.eye(2), np.ones((HEAD_DIM, HEAD_DIM))), BF16)
    rope_a, rope_b = _rope_tables(jnp.arange(S))

    qn, ksd, vsd, kwd, vwd = _nsa_prep(z3, rope_a, rope_b, blockdiag, _tile2(nsa_q_norm), _tile2(nsa_k_norm[1]),
                                       _tile2(nsa_k_norm[2]))

    n_cmp = (S - CMP_BLOCK) // CMP_STRIDE + 1
    ncp = S // CMP_STRIDE
    def flat_blocks(col):
        zc = z3[:, :, col:col + LANES].reshape(B, S, 2, HEAD_DIM).transpose(0, 2, 1, 3)
        r = zc.reshape(B, 2, ncp, CMP_STRIDE * HEAD_DIM)
        nxt = jnp.concatenate([r[:, :, 1:], jnp.zeros_like(r[:, :, :1])], axis=2)
        return jnp.concatenate([r, nxt], axis=-1).reshape(B * 2 * ncp, CMP_BLOCK * HEAD_DIM)
    cmp_a, cmp_b = _rope_tables(jnp.arange(ncp) * CMP_STRIDE + CMP_BLOCK - 1)
    dup2 = lambda w: jnp.concatenate([w, w], axis=1).astype(BF16)
    kcd, vcd = _compress(flat_blocks(COL_KV), flat_blocks(COL_KV + LANES),
                         cmp_pe_k.reshape(1, -1), cmp_pe_v.reshape(1, -1), cmp_w1_k.astype(BF16), cmp_w1_v.astype(BF16),
                         dup2(cmp_w2_k), dup2(cmp_w2_v), cmp_a, cmp_b, blockdiag, _tile2(nsa_k_norm[0]))
    kcd = kcd.reshape(B, 2, ncp, LANES)
    vcd = vcd.reshape(B, 2, ncp, LANES)

    n_slc = S // SLC_BLOCK
    s0 = np.arange(n_cmp) * CMP_STRIDE
    b0 = np.arange(n_slc) * SLC_BLOCK
    ov = np.clip(np.minimum(s0[:, None] + CMP_BLOCK, b0[None, :] + SLC_BLOCK) - np.maximum(s0[:, None], b0[None, :]),
                 0, None) / CMP_BLOCK
    ov_pad = np.zeros((ncp, LANES), np.float32)
    ov_pad[:n_cmp, :n_slc] = ov
    y_nsa = _nsa_attn(qn, kcd, vcd, ksd, vsd, kwd, vwd, z3, jnp.asarray(ov_pad, BF16))

    gq = jnp.tile(dil_q_norm.reshape(3, 1, HEAD_DIM), (1, 1, 2))
    gk = jnp.tile(dil_k_norm.reshape(3, 1, HEAD_DIM), (1, 1, 2))
    prep = _dil_prep(z3, rope_a, rope_b, blockdiag, gq, gk)
    flat = [p.reshape(B, S, 256) for p in prep]
    y_dil = _dil_attn(flat[0::3], flat[1::3], flat[2::3])

    x1, hn, pq = _merge(x2, y_nsa.reshape(T, 512), y_dil.reshape(T, 256), z2, w_up_nsa.astype(BF16),
                        w_up_dil.astype(BF16), w_o.astype(BF16), norm2_g.reshape(1, D), peer_wq.astype(BF16))

    sub = PEER_NKEYS // 2
    sk_pad = jnp.stack([jnp.pad(peer_subkeys[0], ((0, 0), (0, sub))), jnp.pad(peer_subkeys[1], ((0, 0), (sub, 0)))])
    idx_t, gate_t = _peer_route(pq, sk_pad)
    uv_tiles = jnp.concatenate([_pack_bf16_pairs(peer_u), _pack_bf16_pairs(peer_v)], axis=1)
    out = _peer_expert(idx_t.T.reshape(-1), hn.reshape(T, SUBLANES, LANES), gate_t.T, x1.reshape(T, SUBLANES, LANES),
                       uv_tiles.reshape(-1, SUBLANES, LANES))
    return out.reshape(B, S, D)
```

```python
import functools

import numpy as np
import jax
import jax.numpy as jnp
from jax import lax
from jax.experimental import pallas as pl
from jax.experimental.pallas import tpu as pltpu

F32 = jnp.float32
BF16 = jnp.bfloat16
I32 = jnp.int32

D_MODEL = 1024
HEAD_DIM = 64
ROPE_DIMS = 16
ROPE_THETA = 500000.0
NORM_EPS = 1e-6
NEG_INF = -1e30
LANES = 128

NSA_HEADS = 8
CMP_BLOCK = 32
CMP_STRIDE = 16
CMP_HIDDEN = 256
SLC_BLOCK = 64
SLC_TOPN = 16
FORCE_SCORE = 1e3
WIN = 512
NSA_QB = 128
SEL_CHUNK = 512

DIL_PATTERNS = ((128, 1), (512, 4), (2048, 16))
DIL_BLOCK = 128
DIL_PREP_ROWS = 1024

PEER_HEADS = 8
PEER_NKEYS = 128
PEER_TOPK = 16
PEER_TOK = 16

COL_MG = 0
COL_Q = 2048
COL_KV = 2560
COL_DIL = 3328
COL_GATE = 5632
IN_COLS_PAD = 5760

_NT = (((1,), (1,)), ((), ()))


def _dot(a, b):
    return jnp.dot(a, b, preferred_element_type=F32)


def _dot_nt(a, b):
    return lax.dot_general(a, b, _NT, preferred_element_type=F32)


def _split_bf16(a):
    hi = a.astype(BF16)
    lo = (a - hi.astype(F32)).astype(BF16)
    return hi, lo


def _dot_hilo(a, b_bf16):
    hi, lo = _split_bf16(a)
    return _dot(hi, b_bf16) + _dot(lo, b_bf16)


def _head_norm_rope(zt, gain, rope_a, rope_b, blockdiag):
    ss = _dot_hilo(zt * zt, blockdiag)
    zn = zt * lax.rsqrt(ss * (1.0 / HEAD_DIM) + NORM_EPS) * gain
    d = lax.broadcasted_iota(I32, zn.shape, 1) & (HEAD_DIM - 1)
    half = ROPE_DIMS // 2
    partner = jnp.where(d < half, pltpu.roll(zn, LANES - half, 1), pltpu.roll(zn, half, 1))
    return zn * rope_a + partner * rope_b


def _half_masks():
    lane = lax.broadcasted_iota(I32, (1, LANES), 1)
    lo = (lane < HEAD_DIM).astype(BF16)
    return lo, (1 - lo).astype(BF16)


def _inproj_kernel(x_ref, g_ref, w_ref, o_ref, h_scr):
    @pl.when(pl.program_id(1) == 0)
    def _():
        xf = x_ref[...]
        ms = jnp.mean(xf * xf, axis=-1, keepdims=True)
        h_scr[...] = (xf * lax.rsqrt(ms + NORM_EPS) * g_ref[...]).astype(BF16)

    o_ref[...] = _dot(h_scr[...], w_ref[...])


def _in_proj(x2, g1, w_bf16):
    T = x2.shape[0]
    tm, tn = 512, 640
    return pl.pallas_call(
        _inproj_kernel,
        out_shape=jax.ShapeDtypeStruct((T, IN_COLS_PAD), F32),
        grid=(T // tm, IN_COLS_PAD // tn),
        in_specs=[
            pl.BlockSpec((tm, D_MODEL), lambda i, j: (i, 0)),
            pl.BlockSpec((1, D_MODEL), lambda i, j: (0, 0)),
            pl.BlockSpec((D_MODEL, tn), lambda i, j: (0, j)),
        ],
        out_specs=pl.BlockSpec((tm, tn), lambda i, j: (i, j)),
        scratch_shapes=[pltpu.VMEM((tm, D_MODEL), BF16)],
        compiler_params=pltpu.CompilerParams(dimension_semantics=("parallel", "arbitrary")),
        name="in_proj",
    )(x2, g1, w_bf16)


def _nsa_prep_kernel(zq_ref, zks_ref, zvs_ref, zkw_ref, zvw_ref, ra_ref, rb_ref, bd_ref, gq_ref, gks_ref, gkw_ref,
                     q_ref, ks_ref, vs_ref, kw_ref, vw_ref):
    ra, rb, bd = ra_ref[...], rb_ref[...], bd_ref[...]
    lane = lax.broadcasted_iota(I32, ra.shape, 1)
    scale = HEAD_DIM ** -0.5

    zq = zq_ref[0]
    tiles = [_head_norm_rope(zq[:, m * LANES:(m + 1) * LANES], gq_ref[...], ra, rb, bd) * scale for m in range(4)]
    q_ref[0] = jnp.concatenate(tiles, axis=-1).astype(BF16)

    def dup(t, out_ref):
        sw = pltpu.roll(t, HEAD_DIM, 1)
        out_ref[0, 0] = jnp.where(lane < HEAD_DIM, t, sw).astype(BF16)
        out_ref[0, 1] = jnp.where(lane < HEAD_DIM, sw, t).astype(BF16)

    dup(_head_norm_rope(zks_ref[0], gks_ref[...], ra, rb, bd), ks_ref)
    dup(_head_norm_rope(zkw_ref[0], gkw_ref[...], ra, rb, bd), kw_ref)
    dup(zvs_ref[0], vs_ref)
    dup(zvw_ref[0], vw_ref)


def _nsa_prep(z3, rope_a, rope_b, blockdiag, gq, gks, gkw):
    B, S, _ = z3.shape
    tm = 512
    kvb = COL_KV // LANES
    zcol = lambda c: pl.BlockSpec((1, tm, LANES), lambda b, i, c=c: (b, i, c))
    const = lambda shape: pl.BlockSpec(shape, lambda b, i: tuple(0 for _ in shape))
    kv_out = pl.BlockSpec((1, 2, tm, LANES), lambda b, i: (b, 0, i, 0))
    kv_shape = jax.ShapeDtypeStruct((B, 2, S, LANES), BF16)
    return pl.pallas_call(
        _nsa_prep_kernel,
        out_shape=(jax.ShapeDtypeStruct((B, S, 512), BF16), kv_shape, kv_shape, kv_shape, kv_shape),
        grid=(B, S // tm),
        in_specs=[
            pl.BlockSpec((1, tm, 512), lambda b, i: (b, i, COL_Q // 512)),
            zcol(kvb + 2), zcol(kvb + 3), zcol(kvb + 4), zcol(kvb + 5),
            pl.BlockSpec((tm, LANES), lambda b, i: (i, 0)),
            pl.BlockSpec((tm, LANES), lambda b, i: (i, 0)),
            const((LANES, LANES)), const((1, LANES)), const((1, LANES)), const((1, LANES)),
        ],
        out_specs=(pl.BlockSpec((1, tm, 512), lambda b, i: (b, i, 0)), kv_out, kv_out, kv_out, kv_out),
        compiler_params=pltpu.CompilerParams(dimension_semantics=("parallel", "parallel")),
        name="nsa_prep",
    )(z3, z3, z3, z3, z3, rope_a, rope_b, blockdiag, gq, gks, gkw)


def _compress_kernel(fk_ref, fv_ref, pek_ref, pev_ref, w1k_ref, w1v_ref, w2k_ref, w2v_ref, ra_ref, rb_ref, bd_ref, gk_ref,
                     k_ref, v_ref):
    def mlp(f_ref, pe_ref, w1_ref, w2_ref):
        f = (f_ref[...] + pe_ref[...]).astype(BF16)
        h = jax.nn.gelu(_dot(f, w1_ref[...]))
        return _dot(h.astype(BF16), w2_ref[...])

    kc = mlp(fk_ref, pek_ref, w1k_ref, w2k_ref)
    k_ref[...] = _head_norm_rope(kc, gk_ref[...], ra_ref[...], rb_ref[...], bd_ref[...]).astype(BF16)
    v_ref[...] = mlp(fv_ref, pev_ref, w1v_ref, w2v_ref).astype(BF16)


def _compress(flat_k, flat_v, pek, pev, w1k, w1v, w2k, w2v, rope_a, rope_b, blockdiag, gk):
    rows, width = flat_k.shape
    nblk = rope_a.shape[0]
    row = pl.BlockSpec((nblk, width), lambda i: (i, 0))
    const = lambda shape: pl.BlockSpec(shape, lambda i: tuple(0 for _ in shape))
    out = pl.BlockSpec((nblk, LANES), lambda i: (i, 0))
    shp = jax.ShapeDtypeStruct((rows, LANES), BF16)
    return pl.pallas_call(
        _compress_kernel,
        out_shape=(shp, shp),
        grid=(rows // nblk,),
        in_specs=[row, row, const((1, width)), const((1, width)), const((width, CMP_HIDDEN)), const((width, CMP_HIDDEN)),
                  const((CMP_HIDDEN, LANES)), const((CMP_HIDDEN, LANES)), const((nblk, LANES)), const((nblk, LANES)),
                  const((LANES, LANES)), const((1, LANES))],
        out_specs=(out, out),
        compiler_params=pltpu.CompilerParams(dimension_semantics=("parallel",)),
        name="compress",
    )(flat_k, flat_v, pek, pev, w1k, w1v, w2k, w2v, rope_a, rope_b, blockdiag, gk)


def _softmax_rows(s):
    m = jnp.max(s, axis=-1, keepdims=True)
    e = jnp.exp(s - m)
    return e / jnp.sum(e, axis=-1, keepdims=True)


def _nsa_kernel(q_ref, kc_ref, vc_ref, ks_ref, vs_ref, kw_ref, vw_ref, gl_ref, ov_ref, y_ref):
    n = pl.program_id(1)
    t0 = n * NSA_QB
    qt = q_ref[0]
    hm = _half_masks()
    lane = lax.broadcasted_iota(I32, (NSA_QB, LANES), 1)
    t1 = t0 + lax.broadcasted_iota(I32, (NSA_QB, 1), 0)
    t4 = t0 + (lax.broadcasted_iota(I32, (4 * NSA_QB, 1), 0) & (NSA_QB - 1))
    gates = jax.nn.sigmoid(gl_ref[0])

    blk = lax.broadcasted_iota(I32, (LANES, NSA_QB), 0)
    tq = t0 + lax.broadcasted_iota(I32, (LANES, NSA_QB), 1)
    cur = tq >> 6
    forced = (blk == 0) | (blk == cur) | (blk == cur - 1)
    causal_b = blk * SLC_BLOCK <= tq
    n_slc_blocks = ks_ref.shape[2] // SLC_BLOCK

    tiles_out = []
    for g in range(2):
        q4 = jnp.concatenate(
            [qt[:, (2 * g + jj // 2) * LANES:(2 * g + jj // 2 + 1) * LANES] * hm[jj % 2] for jj in range(4)], axis=0)

        sc = _dot_nt(q4, kc_ref[0, g])
        cend = lax.broadcasted_iota(I32, (1, sc.shape[1]), 1) * CMP_STRIDE + (CMP_BLOCK - 1)
        valid_c = cend <= t4
        pc = _softmax_rows(jnp.where(valid_c, sc, NEG_INF))
        pc = jnp.where(valid_c, pc, 0.0)
        o_c = _dot(pc.astype(BF16), vc_ref[0, g])
        psum = pc[0:NSA_QB] + pc[NSA_QB:2 * NSA_QB] + pc[2 * NSA_QB:3 * NSA_QB] + pc[3 * NSA_QB:]
        imp = _dot_hilo(psum, ov_ref[...])

        score = jnp.where(forced, FORCE_SCORE, jnp.where(causal_b, imp.T, -1.0))
        score = jnp.where(blk < n_slc_blocks, score, -2.0)
        rank = jnp.zeros(score.shape, F32)
        for i in range(n_slc_blocks):
            ri = score[i:i + 1, :]
            ahead = (ri > score) | ((ri == score) & (blk > i))
            rank = rank + jnp.where(ahead, 1.0, 0.0)
        sel = (rank < float(SLC_TOPN)).astype(F32).T.astype(BF16)

        def sel_body(c, carry):
            m_i, l_i, acc = carry
            k0 = pl.multiple_of(c * SEL_CHUNK, SEL_CHUNK)
            kch = ks_ref[0, g, pl.ds(k0, SEL_CHUNK), :]
            vch = vs_ref[0, g, pl.ds(k0, SEL_CHUNK), :]
            s = _dot_nt(q4, kch)
            kpos = k0 + lax.broadcasted_iota(I32, (1, SEL_CHUNK), 1)
            expand = (lax.broadcasted_iota(I32, (LANES, SEL_CHUNK), 0)
                      == ((k0 + lax.broadcasted_iota(I32, (LANES, SEL_CHUNK), 1)) >> 6)).astype(BF16)
            allow = jnp.where((_dot(sel, expand) > 0.5) & (kpos <= t1), 0.0, 1.0)
            blocked = jnp.concatenate([allow] * 4, axis=0) > 0.5
            s = jnp.where(blocked, NEG_INF, s)
            m_new = jnp.maximum(m_i, jnp.max(s, axis=-1, keepdims=True))
            alpha = jnp.exp(m_i - m_new)
            p = jnp.exp(s - m_new)
            l_new = alpha * l_i + jnp.sum(p, axis=-1, keepdims=True)
            acc_new = alpha * acc + _dot(p.astype(BF16), vch)
            return m_new, l_new, acc_new

        init = (jnp.full((4 * NSA_QB, 1), NEG_INF, F32), jnp.zeros((4 * NSA_QB, 1), F32),
                jnp.zeros((4 * NSA_QB, LANES), F32))
        n_chunks = (t0 + NSA_QB + SEL_CHUNK - 1) // SEL_CHUNK
        _, l_s, acc_s = lax.fori_loop(0, n_chunks, sel_body, init)
        o_s = acc_s / l_s

        wlen = WIN + NSA_QB
        ws = pl.multiple_of(jnp.maximum(t0 - WIN, 0), NSA_QB)
        sw = _dot_nt(q4, kw_ref[0, g, pl.ds(ws, wlen), :])
        dist = t4 - (ws + lax.broadcasted_iota(I32, (1, wlen), 1))
        pw = _softmax_rows(jnp.where((dist >= 0) & (dist < WIN), sw, NEG_INF))
        o_w = _dot(pw.astype(BF16), vw_ref[0, g, pl.ds(ws, wlen), :])

        heads = []
        for jj in range(4):
            h = 4 * g + jj
            rows = slice(jj * NSA_QB, (jj + 1) * NSA_QB)
            heads.append(gates[:, 3 * h:3 * h + 1] * o_c[rows] + gates[:, 3 * h + 1:3 * h + 2] * o_s[rows]
                         + gates[:, 3 * h + 2:3 * h + 3] * o_w[rows])
        tiles_out.append(jnp.where(lane < HEAD_DIM, heads[0], heads[1]))
        tiles_out.append(jnp.where(lane < HEAD_DIM, heads[2], heads[3]))

    y_ref[0] = jnp.concatenate(tiles_out, axis=-1).astype(BF16)


def _nsa_attn(qn, kcd, vcd, ksd, vsd, kwd, vwd, z3, overlap):
    B, S, _ = qn.shape
    ncp = kcd.shape[2]
    full = lambda rows: pl.BlockSpec((1, 2, rows, LANES), lambda b, n: (b, 0, 0, 0))
    return pl.pallas_call(
        _nsa_kernel,
        out_shape=jax.ShapeDtypeStruct((B, S, 512), BF16),
        grid=(B, S // NSA_QB),
        in_specs=[
            pl.BlockSpec((1, NSA_QB, 512), lambda b, n: (b, n, 0)),
            full(ncp), full(ncp), full(S), full(S), full(S), full(S),
            pl.BlockSpec((1, NSA_QB, LANES), lambda b, n: (b, n, COL_GATE // LANES)),
            pl.BlockSpec((ncp, LANES), lambda b, n: (0, 0)),
        ],
        out_specs=pl.BlockSpec((1, NSA_QB, 512), lambda b, n: (b, n, 0)),
        compiler_params=pltpu.CompilerParams(dimension_semantics=("parallel", "arbitrary"),
                                             vmem_limit_bytes=48 * 1024 * 1024),
        name="nsa_attn",
    )(qn, kcd, vcd, ksd, vsd, kwd, vwd, z3, overlap)


def _dil_prep_kernel(*refs):
    zs, (ra_ref, rb_ref, bd_ref, gq_ref, gk_ref), outs = refs[0:18], refs[18:23], refs[23:32]
    bd = bd_ref[...]
    scale = HEAD_DIM ** -0.5
    for g, (_, d) in enumerate(DIL_PATTERNS):
        n = DIL_PREP_ROWS // d
        for r in range(d):
            rows = pl.ds(r, n, stride=d) if d > 1 else pl.ds(0, n)
            ra, rb = ra_ref[rows, :], rb_ref[rows, :]
            for which in range(3):
                for m in range(2):
                    z = zs[2 * (3 * g + which) + m][0, rows, :]
                    if which == 0:
                        z = _head_norm_rope(z, gq_ref[g], ra, rb, bd) * scale
                    elif which == 1:
                        z = _head_norm_rope(z, gk_ref[g], ra, rb, bd)
                    outs[3 * g + which][0, r, :, m * LANES:(m + 1) * LANES] = z.astype(BF16)


def _dil_prep(z3, rope_a, rope_b, blockdiag, gq, gk):
    B, S, _ = z3.shape
    nsteps = S // DIL_PREP_ROWS
    c0 = COL_DIL // LANES
    in_specs = [pl.BlockSpec((1, DIL_PREP_ROWS, LANES), lambda b, c, k=k: (b, c, c0 + k)) for k in range(18)]
    in_specs += [
        pl.BlockSpec((DIL_PREP_ROWS, LANES), lambda b, c: (c, 0)),
        pl.BlockSpec((DIL_PREP_ROWS, LANES), lambda b, c: (c, 0)),
        pl.BlockSpec((LANES, LANES), lambda b, c: (0, 0)),
        pl.BlockSpec((3, 1, LANES), lambda b, c: (0, 0, 0)),
        pl.BlockSpec((3, 1, LANES), lambda b, c: (0, 0, 0)),
    ]
    out_shape, out_specs = [], []
    for _, d in DIL_PATTERNS:
        for _ in range(3):
            out_shape.append(jax.ShapeDtypeStruct((B, d, S // d, 256), BF16))
            out_specs.append(pl.BlockSpec((1, d, DIL_PREP_ROWS // d, 256), lambda b, c: (b, 0, c, 0)))
    return pl.pallas_call(
        _dil_prep_kernel,
        out_shape=tuple(out_shape),
        grid=(B, nsteps),
        in_specs=in_specs,
        out_specs=tuple(out_specs),
        compiler_params=pltpu.CompilerParams(dimension_semantics=("parallel", "parallel"),
                                             vmem_limit_bytes=48 * 1024 * 1024),
        name="dil_prep",
    )(*([z3] * 18), rope_a, rope_b, blockdiag, gq, gk)


def _dil_kernel(*refs, seq):
    q_refs, k_refs, v_refs, y_ref, o_scr, l_scr = refs[0:3], refs[3:6], refs[6:9], refs[9], refs[10], refs[11]
    hm = _half_masks()
    lane = lax.broadcasted_iota(I32, (DIL_BLOCK, LANES), 1)
    qi = lax.broadcasted_iota(I32, (2 * DIL_BLOCK, 2 * DIL_BLOCK), 0) & (DIL_BLOCK - 1)
    ki = lax.broadcasted_iota(I32, (2 * DIL_BLOCK, 2 * DIL_BLOCK), 1)
    causal = (ki - DIL_BLOCK) <= qi

    for g, (_, d) in enumerate(DIL_PATTERNS):
        nb = seq // d // DIL_BLOCK

        def body(u, carry, g=g, d=d, nb=nb):
            j = u % nb
            r = u // nb
            r0 = pl.multiple_of(u * DIL_BLOCK, DIL_BLOCK)
            p0 = pl.multiple_of(jnp.maximum(u - 1, 0) * DIL_BLOCK, DIL_BLOCK)
            q = q_refs[g][0, pl.ds(r0, DIL_BLOCK), :]
            kcat = jnp.concatenate([k_refs[g][0, pl.ds(p0, DIL_BLOCK), :], k_refs[g][0, pl.ds(r0, DIL_BLOCK), :]], axis=0)
            vcat = jnp.concatenate([v_refs[g][0, pl.ds(p0, DIL_BLOCK), :], v_refs[g][0, pl.ds(r0, DIL_BLOCK), :]], axis=0)
            q2 = jnp.concatenate([q * hm[0], q * hm[1]], axis=0)
            s = _dot_nt(q2, kcat)
            first_key = jnp.maximum(qi, jnp.where(j >= 1, 0, DIL_BLOCK))
            s = jnp.where(causal & (ki >= first_key), s, NEG_INF)
            m = jnp.max(s, axis=-1, keepdims=True)
            e = jnp.exp(s - m)
            den = jnp.sum(e, axis=-1, keepdims=True)
            o2 = _dot(e.astype(BF16), vcat) / den
            lse = m + jnp.log(den)
            o = jnp.where(lane < HEAD_DIM, o2[:DIL_BLOCK], o2[DIL_BLOCK:])
            lv = jnp.where(lane < HEAD_DIM, lse[:DIL_BLOCK], lse[DIL_BLOCK:])
            tok0 = j * (DIL_BLOCK * d) + r
            rows = pl.ds(tok0, DIL_BLOCK, stride=d) if d > 1 else pl.ds(pl.multiple_of(tok0, DIL_BLOCK), DIL_BLOCK)
            o_scr[g, rows, :] = o
            l_scr[g, rows, :] = lv
            return carry

        lax.fori_loop(0, seq // DIL_BLOCK, body, 0)

    def merge(c, carry):
        rows = pl.ds(pl.multiple_of(c * 512, 512), 512)
        ls = [l_scr[g, rows, :] for g in range(3)]
        mx = jnp.maximum(jnp.maximum(ls[0], ls[1]), ls[2])
        ws = [jnp.exp(l - mx) for l in ls]
        num = ws[0] * o_scr[0, rows, :] + ws[1] * o_scr[1, rows, :] + ws[2] * o_scr[2, rows, :]
        y_ref[0, rows, :] = (num / (ws[0] + ws[1] + ws[2])).astype(BF16)
        return carry

    lax.fori_loop(0, seq // 512, merge, 0)


def _dil_attn(dq, dk, dv):
    B, S, _ = dq[0].shape
    spec = pl.BlockSpec((1, S, LANES), lambda b, m: (b, 0, m))
    return pl.pallas_call(
        functools.partial(_dil_kernel, seq=S),
        out_shape=jax.ShapeDtypeStruct((B, S, 256), BF16),
        grid=(B, 2),
        in_specs=[spec] * 9,
        out_specs=spec,
        scratch_shapes=[pltpu.VMEM((3, S, LANES), F32), pltpu.VMEM((3, S, LANES), F32)],
        compiler_params=pltpu.CompilerParams(dimension_semantics=("parallel", "parallel"),
                                             vmem_limit_bytes=56 * 1024 * 1024),
        name="dil_attn",
    )(*dq, *dk, *dv)


def _merge_kernel(x_ref, yn_ref, yd_ref, mg0_ref, mg1_ref, wn_ref, wd_ref, wo_ref, g2_ref, wq_ref, x1_ref, hn_ref, pq_ref):
    u1 = _dot(yn_ref[...], wn_ref[...])
    u2 = _dot(yd_ref[...], wd_ref[...])
    merged = jax.nn.sigmoid(mg0_ref[...]) * u1 + jax.nn.sigmoid(mg1_ref[...]) * u2
    x1 = x_ref[...] + _dot(merged.astype(BF16), wo_ref[...])
    x1_ref[...] = x1
    ms = jnp.mean(x1 * x1, axis=-1, keepdims=True)
    hn = x1 * lax.rsqrt(ms + NORM_EPS) * g2_ref[...]
    hn_ref[...] = hn
    pq_ref[...] = _dot(hn.astype(BF16), wq_ref[...])


def _merge(x2, yn2, yd2, z2, wn, wd, wo, g2, wq):
    T = x2.shape[0]
    tm = 512
    row = lambda w, c=0: pl.BlockSpec((tm, w), lambda i, c=c: (i, c))
    const = lambda shape: pl.BlockSpec(shape, lambda i: (0, 0))
    shp = jax.ShapeDtypeStruct((T, D_MODEL), F32)
    return pl.pallas_call(
        _merge_kernel,
        out_shape=(shp, shp, shp),
        grid=(T // tm,),
        in_specs=[row(D_MODEL), row(512), row(256), row(D_MODEL, COL_MG // D_MODEL), row(D_MODEL, COL_MG // D_MODEL + 1),
                  const((512, D_MODEL)), const((256, D_MODEL)), const((D_MODEL, D_MODEL)), const((1, D_MODEL)),
                  const((D_MODEL, D_MODEL))],
        out_specs=(row(D_MODEL), row(D_MODEL), row(D_MODEL)),
        compiler_params=pltpu.CompilerParams(dimension_semantics=("parallel",), vmem_limit_bytes=48 * 1024 * 1024),
        name="merge",
    )(x2, yn2, yd2, z2, z2, wn, wd, wo, g2, wq)


def _top16(s):
    n = s.shape[0]
    pos_iota = lax.broadcasted_iota(I32, s.shape, 0)
    vals, poss = [], []
    for _ in range(PEER_TOPK):
        m = jnp.max(s, axis=0, keepdims=True)
        pos = jnp.min(jnp.where(s == m, pos_iota, n), axis=0, keepdims=True)
        vals.append(m)
        poss.append(pos)
        s = jnp.where(pos_iota == pos, -jnp.inf, s)
    return vals, poss


def _route_kernel(q_ref, sk_ref, idx_ref, gate_ref):
    qh, ql = _split_bf16(q_ref[...])
    vals, ids = [], []
    for c in range(2):
        kh, kl = _split_bf16(sk_ref[c])
        s = _dot_nt(kh, qh) + _dot_nt(kh, ql) + _dot_nt(kl, qh)
        v, p = _top16(s)
        vals.append(v)
        ids.append(p)
    cand = jnp.concatenate([vals[0][a] + jnp.concatenate(vals[1], axis=0) for a in range(PEER_TOPK)], axis=0)
    eid = jnp.concatenate([ids[0][a] * PEER_NKEYS + jnp.concatenate(ids[1], axis=0) for a in range(PEER_TOPK)], axis=0)
    v, p = _top16(cand)
    pos_iota = lax.broadcasted_iota(I32, eid.shape, 0)
    sel_ids = [jnp.sum(jnp.where(pos_iota == pk, eid, 0), axis=0, keepdims=True) for pk in p]
    sc = jnp.concatenate(v, axis=0)
    e = jnp.exp(sc - sc[0:1])
    gate_ref[...] = e / jnp.sum(e, axis=0, keepdims=True)
    idx_ref[...] = jnp.concatenate(sel_ids, axis=0)


def _peer_route(pq, sk_pad):
    T = pq.shape[0]
    tt = 256
    return pl.pallas_call(
        _route_kernel,
        out_shape=(jax.ShapeDtypeStruct((PEER_HEADS * PEER_TOPK, T), I32),
                   jax.ShapeDtypeStruct((PEER_HEADS * PEER_TOPK, T), F32)),
        grid=(T // tt, PEER_HEADS),
        in_specs=[pl.BlockSpec((tt, LANES), lambda i, h: (i, h)),
                  pl.BlockSpec((2, PEER_NKEYS, LANES), lambda i, h: (0, 0, 0))],
        out_specs=(pl.BlockSpec((PEER_TOPK, tt), lambda i, h: (h, i)),
                   pl.BlockSpec((PEER_TOPK, tt), lambda i, h: (h, i))),
        compiler_params=pltpu.CompilerParams(dimension_semantics=("parallel", "parallel")),
        name="peer_route",
    )(pq, sk_pad)


PEER_NK = PEER_HEADS * PEER_TOPK
SUBLANES = 8


def _unpack_pair(w):
    lo = pltpu.bitcast(w << 16, F32)
    hi = pltpu.bitcast(w & jnp.uint32(0xFFFF0000), F32)
    return lo, hi


def _peer_kernel(idx_cur, idx_nxt, hn_ref, gate_ref, x1_ref, uv_hbm, o_ref, buf, sem):
    i = pl.program_id(0)
    last = pl.num_programs(0) - 1
    slot = i % 2
    half_rows = SUBLANES // 2

    def tile_copy(idx_ref, s, c, k):
        p = c * PEER_NK + k
        dst = buf.at[s, pl.ds(pl.multiple_of(p * SUBLANES, SUBLANES), SUBLANES), :]
        return pltpu.make_async_copy(uv_hbm.at[idx_ref[p]], dst, sem.at[s, c])

    def fetch_token(idx_ref, s, c):
        for k in range(PEER_NK):
            tile_copy(idx_ref, s, c, k).start(priority=k % 2)

    def wait_token(idx_ref, s, c):
        for k in range(PEER_NK):
            tile_copy(idx_ref, s, c, k).wait()

    @pl.when(i == 0)
    def _():
        lax.fori_loop(0, PEER_TOK, lambda c, carry: (fetch_token(idx_cur, 0, c), carry)[1], 0)

    eye = lax.broadcasted_iota(I32, (PEER_NK, PEER_NK), 0) == lax.broadcasted_iota(I32, (PEER_NK, PEER_NK), 1)

    def token(c, carry):
        wait_token(idx_cur, slot, c)
        base = c * (PEER_NK * SUBLANES)
        plane = lambda s: buf[slot, pl.ds(base + s, PEER_NK, stride=SUBLANES), :]
        xt = hn_ref[c]
        gcol = jnp.sum(jnp.where(eye, gate_ref[pl.ds(c, 1), :], 0.0), axis=-1, keepdims=True)
        acc = jnp.zeros((PEER_NK, LANES), F32)
        lo_rows, hi_rows = [], []
        per_phase = PEER_NK // SUBLANES
        for s in range(SUBLANES):
            for k in range(s * per_phase, (s + 1) * per_phase):
                tile_copy(idx_nxt, 1 - slot, c, k).start(priority=k % 2)
            lo, hi = _unpack_pair(plane(s))
            if s < half_rows:
                acc = acc + lo * xt[s:s + 1, :] + hi * xt[s + half_rows:s + half_rows + 1, :]
                if s == half_rows - 1:
                    act = jax.nn.gelu(jnp.sum(acc, axis=-1, keepdims=True)) * gcol
            else:
                lo_rows.append(jnp.sum(act * lo, axis=0, keepdims=True))
                hi_rows.append(jnp.sum(act * hi, axis=0, keepdims=True))
        o_ref[c] = x1_ref[c] + jnp.concatenate(lo_rows + hi_rows, axis=0)
        return carry

    for c in range(PEER_TOK):
        token(c, 0)

    @pl.when(i == last)
    def _():
        lax.fori_loop(0, PEER_TOK, lambda c, carry: (wait_token(idx_nxt, 1 - slot, c), carry)[1], 0)


def _peer_expert(idx_flat, hn_t, gates, x1_t, uv_tiles):
    T = hn_t.shape[0]
    n = T // PEER_TOK
    tile = pl.BlockSpec((PEER_TOK, SUBLANES, LANES), lambda i: (i, 0, 0))
    return pl.pallas_call(
        _peer_kernel,
        out_shape=jax.ShapeDtypeStruct((T, SUBLANES, LANES), F32),
        grid=(n,),
        in_specs=[pl.BlockSpec((PEER_TOK * PEER_NK,), lambda i: (i,), memory_space=pltpu.SMEM),
                  pl.BlockSpec((PEER_TOK * PEER_NK,), lambda i: (jnp.minimum(i + 1, n - 1),), memory_space=pltpu.SMEM),
                  tile, pl.BlockSpec((PEER_TOK, PEER_NK), lambda i: (i, 0)), tile,
                  pl.BlockSpec(memory_space=pl.ANY)],
        out_specs=tile,
        scratch_shapes=[pltpu.VMEM((2, PEER_TOK * PEER_NK * SUBLANES, LANES), jnp.uint32),
                        pltpu.SemaphoreType.DMA((2, PEER_TOK))],
        compiler_params=pltpu.CompilerParams(dimension_semantics=("arbitrary",), vmem_limit_bytes=48 * 1024 * 1024),
        name="peer_expert",
    )(idx_flat, idx_flat, hn_t, gates, x1_t, uv_tiles)


def _rope_tables(pos):
    half = ROPE_DIMS // 2
    inv = ROPE_THETA ** (-(jnp.arange(half, dtype=F32) * 2.0 / ROPE_DIMS))
    ang = pos.astype(F32)[:, None] * inv[None, :]
    cos, sin = jnp.cos(ang), jnp.sin(ang)
    n = pos.shape[0]
    a = jnp.concatenate([cos, cos, jnp.ones((n, HEAD_DIM - ROPE_DIMS), F32)], axis=-1)
    b = jnp.concatenate([-sin, sin, jnp.zeros((n, HEAD_DIM - ROPE_DIMS), F32)], axis=-1)
    return jnp.tile(a, (1, 2)), jnp.tile(b, (1, 2))


def _pack_bf16_pairs(w):
    half = w.shape[1] // 2
    bits = lax.bitcast_convert_type(w.astype(BF16), jnp.uint16).astype(jnp.uint32)
    return bits[:, :half] | (bits[:, half:] << 16)


def _tile2(v):
    return jnp.tile(v.reshape(1, HEAD_DIM), (1, 2))


def kernel(x, norm1_g, w_in, nsa_q_norm, nsa_k_norm, cmp_pe_k, cmp_w1_k, cmp_w2_k, cmp_pe_v, cmp_w1_v, cmp_w2_v,
           dil_q_norm, dil_k_norm, w_up_nsa, w_up_dil, w_o, norm2_g, peer_wq, peer_subkeys, peer_u, peer_v):
    B, S, D = x.shape
    T = B * S
    assert D == D_MODEL and S % (DIL_PATTERNS[-1][1] * DIL_BLOCK) == 0 and S >= WIN + NSA_QB and T % 512 == 0
    x2 = x.reshape(T, D)

    n_q, n_kv, n_gate, n_dil = 512, 768, 24, 2304
    o_gate = n_q + n_kv
    o_dil = o_gate + n_gate
    o_mg = o_dil + n_dil
    w_perm = jnp.concatenate([w_in[:, o_mg:], w_in[:, :o_gate], w_in[:, o_dil:o_mg], w_in[:, o_gate:o_dil],
                              jnp.zeros((D, IN_COLS_PAD - w_in.shape[1]), w_in.dtype)], axis=1).astype(BF16)
    z2 = _in_proj(x2, norm1_g.reshape(1, D), w_perm)
    z3 = z2.reshape(B, S, IN_COLS_PAD)

    blockdiag = jnp.asarray(np.kron(np.eye(2), np.ones((HEAD_DIM, HEAD_DIM))), BF16)
    rope_a, rope_b = _rope_tables(jnp.arange(S))

    qn, ksd, vsd, kwd, vwd = _nsa_prep(z3, rope_a, rope_b, blockdiag, _tile2(nsa_q_norm), _tile2(nsa_k_norm[1]),
                                       _tile2(nsa_k_norm[2]))

    n_cmp = (S - CMP_BLOCK) // CMP_STRIDE + 1
    ncp = S // CMP_STRIDE
    def flat_blocks(col):
        zc = z3[:, :, col:col + LANES].reshape(B, S, 2, HEAD_DIM).transpose(0, 2, 1, 3)
        r = zc.reshape(B, 2, ncp, CMP_STRIDE * HEAD_DIM)
        nxt = jnp.concatenate([r[:, :, 1:], jnp.zeros_like(r[:, :, :1])], axis=2)
        return jnp.concatenate([r, nxt], axis=-1).reshape(B * 2 * ncp, CMP_BLOCK * HEAD_DIM)
    cmp_a, cmp_b = _rope_tables(jnp.arange(ncp) * CMP_STRIDE + CMP_BLOCK - 1)
    dup2 = lambda w: jnp.concatenate([w, w], axis=1).astype(BF16)
    kcd, vcd = _compress(flat_blocks(COL_KV), flat_blocks(COL_KV + LANES),
                         cmp_pe_k.reshape(1, -1), cmp_pe_v.reshape(1, -1), cmp_w1_k.astype(BF16), cmp_w1_v.astype(BF16),
                         dup2(cmp_w2_k), dup2(cmp_w2_v), cmp_a, cmp_b, blockdiag, _tile2(nsa_k_norm[0]))
    kcd = kcd.reshape(B, 2, ncp, LANES)
    vcd = vcd.reshape(B, 2, ncp, LANES)

    n_slc = S // SLC_BLOCK
    s0 = np.arange(n_cmp) * CMP_STRIDE
    b0 = np.arange(n_slc) * SLC_BLOCK
    ov = np.clip(np.minimum(s0[:, None] + CMP_BLOCK, b0[None, :] + SLC_BLOCK) - np.maximum(s0[:, None], b0[None, :]),
                 0, None) / CMP_BLOCK
    ov_pad = np.zeros((ncp, LANES), np.float32)
    ov_pad[:n_cmp, :n_slc] = ov
    y_nsa = _nsa_attn(qn, kcd, vcd, ksd, vsd, kwd, vwd, z3, jnp.asarray(ov_pad, BF16))

    gq = jnp.tile(dil_q_norm.reshape(3, 1, HEAD_DIM), (1, 1, 2))
    gk = jnp.tile(dil_k_norm.reshape(3, 1, HEAD_DIM), (1, 1, 2))
    prep = _dil_prep(z3, rope_a, rope_b, blockdiag, gq, gk)
    flat = [p.reshape(B, S, 256) for p in prep]
    y_dil = _dil_attn(flat[0::3], flat[1::3], flat[2::3])

    x1, hn, pq = _merge(x2, y_nsa.reshape(T, 512), y_dil.reshape(T, 256), z2, w_up_nsa.astype(BF16),
                        w_up_dil.astype(BF16), w_o.astype(BF16), norm2_g.reshape(1, D), peer_wq.astype(BF16))

    sub = PEER_NKEYS // 2
    sk_pad = jnp.stack([jnp.pad(peer_subkeys[0], ((0, 0), (0, sub))), jnp.pad(peer_subkeys[1], ((0, 0), (sub, 0)))])
    idx_t, gate_t = _peer_route(pq, sk_pad)
    uv_tiles = jnp.concatenate([_pack_bf16_pairs(peer_u), _pack_bf16_pairs(peer_v)], axis=1)
    out = _peer_expert(idx_t.T.reshape(-1), hn.reshape(T, SUBLANES, LANES), gate_t.T, x1.reshape(T, SUBLANES, LANES),
                       uv_tiles.reshape(-1, SUBLANES, LANES))
    return out.reshape(B, S, D)
```

```python
import functools

import numpy as np
import jax
import jax.numpy as jnp
from jax import lax
from jax.experimental import pallas as pl
from jax.experimental.pallas import tpu as pltpu

F32 = jnp.float32
BF16 = jnp.bfloat16
I32 = jnp.int32

D_MODEL = 1024
HEAD_DIM = 64
ROPE_DIMS = 16
ROPE_THETA = 500000.0
NORM_EPS = 1e-6
NEG_INF = -1e30
LANES = 128

NSA_HEADS = 8
CMP_BLOCK = 32
CMP_STRIDE = 16
CMP_HIDDEN = 256
SLC_BLOCK = 64
SLC_TOPN = 16
FORCE_SCORE = 1e3
WIN = 512
NSA_QB = 128
SEL_CHUNK = 512

DIL_PATTERNS = ((128, 1), (512, 4), (2048, 16))
DIL_BLOCK = 128
DIL_PREP_ROWS = 1024

PEER_HEADS = 8
PEER_NKEYS = 128
PEER_TOPK = 16
PEER_TOK = 16

COL_MG = 0
COL_Q = 2048
COL_KV = 2560
COL_DIL = 3328
COL_GATE = 5632
IN_COLS_PAD = 5760

_NT = (((1,), (1,)), ((), ()))


def _dot(a, b):
    return jnp.dot(a, b, preferred_element_type=F32)


def _dot_nt(a, b):
    return lax.dot_general(a, b, _NT, preferred_element_type=F32)


def _split_bf16(a):
    hi = a.astype(BF16)
    lo = (a - hi.astype(F32)).astype(BF16)
    return hi, lo


def _dot_hilo(a, b_bf16):
    hi, lo = _split_bf16(a)
    return _dot(hi, b_bf16) + _dot(lo, b_bf16)


def _head_norm_rope(zt, gain, rope_a, rope_b, blockdiag):
    ss = _dot_hilo(zt * zt, blockdiag)
    zn = zt * lax.rsqrt(ss * (1.0 / HEAD_DIM) + NORM_EPS) * gain
    d = lax.broadcasted_iota(I32, zn.shape, 1) & (HEAD_DIM - 1)
    half = ROPE_DIMS // 2
    partner = jnp.where(d < half, pltpu.roll(zn, LANES - half, 1), pltpu.roll(zn, half, 1))
    return zn * rope_a + partner * rope_b


def _half_masks():
    lane = lax.broadcasted_iota(I32, (1, LANES), 1)
    lo = (lane < HEAD_DIM).astype(BF16)
    return lo, (1 - lo).astype(BF16)


def _inproj_kernel(x_ref, g_ref, w_ref, o_ref, h_scr):
    @pl.when(pl.program_id(1) == 0)
    def _():
        xf = x_ref[...]
        ms = jnp.mean(xf * xf, axis=-1, keepdims=True)
        h_scr[...] = (xf * lax.rsqrt(ms + NORM_EPS) * g_ref[...]).astype(BF16)

    o_ref[...] = _dot(h_scr[...], w_ref[...])


def _in_proj(x2, g1, w_bf16):
    T = x2.shape[0]
    tm, tn = 512, 640
    return pl.pallas_call(
        _inproj_kernel,
        out_shape=jax.ShapeDtypeStruct((T, IN_COLS_PAD), F32),
        grid=(T // tm, IN_COLS_PAD // tn),
        in_specs=[
            pl.BlockSpec((tm, D_MODEL), lambda i, j: (i, 0)),
            pl.BlockSpec((1, D_MODEL), lambda i, j: (0, 0)),
            pl.BlockSpec((D_MODEL, tn), lambda i, j: (0, j)),
        ],
        out_specs=pl.BlockSpec((tm, tn), lambda i, j: (i, j)),
        scratch_shapes=[pltpu.VMEM((tm, D_MODEL), BF16)],
        compiler_params=pltpu.CompilerParams(dimension_semantics=("parallel", "arbitrary")),
        name="in_proj",
    )(x2, g1, w_bf16)


def _nsa_prep_kernel(zq_ref, zks_ref, zvs_ref, zkw_ref, zvw_ref, ra_ref, rb_ref, bd_ref, gq_ref, gks_ref, gkw_ref,
                     q_ref, ks_ref, vs_ref, kw_ref, vw_ref):
    ra, rb, bd = ra_ref[...], rb_ref[...], bd_ref[...]
    lane = lax.broadcasted_iota(I32, ra.shape, 1)
    scale = HEAD_DIM ** -0.5

    zq = zq_ref[0]
    tiles = [_head_norm_rope(zq[:, m * LANES:(m + 1) * LANES], gq_ref[...], ra, rb, bd) * scale for m in range(4)]
    q_ref[0] = jnp.concatenate(tiles, axis=-1).astype(BF16)

    def dup(t, out_ref):
        sw = pltpu.roll(t, HEAD_DIM, 1)
        out_ref[0, 0] = jnp.where(lane < HEAD_DIM, t, sw).astype(BF16)
        out_ref[0, 1] = jnp.where(lane < HEAD_DIM, sw, t).astype(BF16)

    dup(_head_norm_rope(zks_ref[0], gks_ref[...], ra, rb, bd), ks_ref)
    dup(_head_norm_rope(zkw_ref[0], gkw_ref[...], ra, rb, bd), kw_ref)
    dup(zvs_ref[0], vs_ref)
    dup(zvw_ref[0], vw_ref)


def _nsa_prep(z3, rope_a, rope_b, blockdiag, gq, gks, gkw):
    B, S, _ = z3.shape
    tm = 512
    kvb = COL_KV // LANES
    zcol = lambda c: pl.BlockSpec((1, tm, LANES), lambda b, i, c=c: (b, i, c))
    const = lambda shape: pl.BlockSpec(shape, lambda b, i: tuple(0 for _ in shape))
    kv_out = pl.BlockSpec((1, 2, tm, LANES), lambda b, i: (b, 0, i, 0))
    kv_shape = jax.ShapeDtypeStruct((B, 2, S, LANES), BF16)
    return pl.pallas_call(
        _nsa_prep_kernel,
        out_shape=(jax.ShapeDtypeStruct((B, S, 512), BF16), kv_shape, kv_shape, kv_shape, kv_shape),
        grid=(B, S // tm),
        in_specs=[
            pl.BlockSpec((1, tm, 512), lambda b, i: (b, i, COL_Q // 512)),
            zcol(kvb + 2), zcol(kvb + 3), zcol(kvb + 4), zcol(kvb + 5),
            pl.BlockSpec((tm, LANES), lambda b, i: (i, 0)),
            pl.BlockSpec((tm, LANES), lambda b, i: (i, 0)),
            const((LANES, LANES)), const((1, LANES)), const((1, LANES)), const((1, LANES)),
        ],
        out_specs=(pl.BlockSpec((1, tm, 512), lambda b, i: (b, i, 0)), kv_out, kv_out, kv_out, kv_out),
        compiler_params=pltpu.CompilerParams(dimension_semantics=("parallel", "parallel")),
        name="nsa_prep",
    )(z3, z3, z3, z3, z3, rope_a, rope_b, blockdiag, gq, gks, gkw)


def _compress_kernel(fk_ref, fv_ref, pek_ref, pev_ref, w1k_ref, w1v_ref, w2k_ref, w2v_ref, ra_ref, rb_ref, bd_ref, gk_ref,
                     k_ref, v_ref):
    def mlp(f_ref, pe_ref, w1_ref, w2_ref):
        f = (f_ref[...] + pe_ref[...]).astype(BF16)
        h = jax.nn.gelu(_dot(f, w1_ref[...]))
        return _dot(h.astype(BF16), w2_ref[...])

    kc = mlp(fk_ref, pek_ref, w1k_ref, w2k_ref)
    k_ref[...] = _head_norm_rope(kc, gk_ref[...], ra_ref[...], rb_ref[...], bd_ref[...]).astype(BF16)
    v_ref[...] = mlp(fv_ref, pev_ref, w1v_ref, w2v_ref).astype(BF16)


def _compress(flat_k, flat_v, pek, pev, w1k, w1v, w2k, w2v, rope_a, rope_b, blockdiag, gk):
    rows, width = flat_k.shape
    nblk = rope_a.shape[0]
    row = pl.BlockSpec((nblk, width), lambda i: (i, 0))
    const = lambda shape: pl.BlockSpec(shape, lambda i: tuple(0 for _ in shape))
    out = pl.BlockSpec((nblk, LANES), lambda i: (i, 0))
    shp = jax.ShapeDtypeStruct((rows, LANES), BF16)
    return pl.pallas_call(
        _compress_kernel,
        out_shape=(shp, shp),
        grid=(rows // nblk,),
        in_specs=[row, row, const((1, width)), const((1, width)), const((width, CMP_HIDDEN)), const((width, CMP_HIDDEN)),
                  const((CMP_HIDDEN, LANES)), const((CMP_HIDDEN, LANES)), const((nblk, LANES)), const((nblk, LANES)),
                  const((LANES, LANES)), const((1, LANES))],
        out_specs=(out, out),
        compiler_params=pltpu.CompilerParams(dimension_semantics=("parallel",)),
        name="compress",
    )(flat_k, flat_v, pek, pev, w1k, w1v, w2k, w2v, rope_a, rope_b, blockdiag, gk)


def _softmax_rows(s):
    m = jnp.max(s, axis=-1, keepdims=True)
    e = jnp.exp(s - m)
    return e / jnp.sum(e, axis=-1, keepdims=True)


def _nsa_kernel(q_ref, kc_ref, vc_ref, ks_ref, vs_ref, kw_ref, vw_ref, gl_ref, ov_ref, ex_ref, y_ref):
    n = pl.program_id(1)
    t0 = n * NSA_QB
    qt = q_ref[0]
    hm = _half_masks()
    lane = lax.broadcasted_iota(I32, (NSA_QB, LANES), 1)
    t1 = t0 + lax.broadcasted_iota(I32, (NSA_QB, 1), 0)
    t4 = t0 + (lax.broadcasted_iota(I32, (4 * NSA_QB, 1), 0) & (NSA_QB - 1))
    gates = jax.nn.sigmoid(gl_ref[0])

    n_slc_blocks = ks_ref.shape[2] // SLC_BLOCK
    blk = lax.broadcasted_iota(I32, (n_slc_blocks, NSA_QB), 0)
    tq = t0 + lax.broadcasted_iota(I32, (n_slc_blocks, NSA_QB), 1)
    cur = tq >> 6
    forced = (blk == 0) | (blk == cur) | (blk == cur - 1)
    causal_b = blk * SLC_BLOCK <= tq

    tiles_out = []
    for g in range(2):
        q4 = jnp.concatenate(
            [qt[:, (2 * g + jj // 2) * LANES:(2 * g + jj // 2 + 1) * LANES] * hm[jj % 2] for jj in range(4)], axis=0)

        sc = _dot_nt(q4, kc_ref[0, g])
        cend = lax.broadcasted_iota(I32, (1, sc.shape[1]), 1) * CMP_STRIDE + (CMP_BLOCK - 1)
        valid_c = cend <= t4
        pc = _softmax_rows(jnp.where(valid_c, sc, NEG_INF))
        pc = jnp.where(valid_c, pc, 0.0)
        o_c = _dot(pc.astype(BF16), vc_ref[0, g])
        psum = pc[0:NSA_QB] + pc[NSA_QB:2 * NSA_QB] + pc[2 * NSA_QB:3 * NSA_QB] + pc[3 * NSA_QB:]
        imp = _dot_hilo(psum, ov_ref[...])

        score = jnp.where(forced, FORCE_SCORE, jnp.where(causal_b, imp.T[:n_slc_blocks], -1.0))
        n_grp = n_slc_blocks // SUBLANES
        grp = [score[SUBLANES * r:SUBLANES * (r + 1)] for r in range(n_grp)]
        ranks = [jnp.zeros((SUBLANES, NSA_QB), F32) for _ in range(n_grp)]
        row = lax.broadcasted_iota(I32, (SUBLANES, NSA_QB), 0)
        for i in range(n_slc_blocks):
            ri = grp[i // SUBLANES][i % SUBLANES:i % SUBLANES + 1, :]
            for r in range(n_grp):
                if r > i // SUBLANES:
                    ahead = ri >= grp[r]
                elif r < i // SUBLANES:
                    ahead = ri > grp[r]
                else:
                    ahead = (ri > grp[r]) | ((ri == grp[r]) & (row > i % SUBLANES))
                ranks[r] = ranks[r] + jnp.where(ahead, 1.0, 0.0)
        sel64 = jnp.where(jnp.concatenate(ranks, axis=0) < float(SLC_TOPN), 1.0, 0.0)
        sel = jnp.concatenate([sel64, jnp.zeros((LANES - n_slc_blocks, NSA_QB), F32)], axis=0).T.astype(BF16)

        def sel_chunk(c, carry, last):
            m_i, l_i, acc = carry
            k0 = pl.multiple_of(c * SEL_CHUNK, SEL_CHUNK)
            kch = ks_ref[0, g, pl.ds(k0, SEL_CHUNK), :]
            vch = vs_ref[0, g, pl.ds(k0, SEL_CHUNK), :]
            picked = _dot(sel, ex_ref[:, pl.ds(k0, SEL_CHUNK)])
            if last:
                kpos = k0 + lax.broadcasted_iota(I32, (1, SEL_CHUNK), 1)
                picked = jnp.where(kpos <= t1, picked, 0.0)
            bias = (picked - 1.0) * (-NEG_INF)
            s = (_dot_nt(q4, kch).reshape(4, NSA_QB, SEL_CHUNK) + bias[None]).reshape(4 * NSA_QB, SEL_CHUNK)
            m_new = jnp.maximum(m_i, jnp.max(s, axis=-1, keepdims=True))
            alpha = jnp.exp(m_i - m_new)
            p = jnp.exp(s - m_new)
            l_new = alpha * l_i + jnp.sum(p, axis=-1, keepdims=True)
            acc_new = alpha * acc + _dot(p.astype(BF16), vch)
            return m_new, l_new, acc_new

        init = (jnp.full((4 * NSA_QB, 1), NEG_INF, F32), jnp.zeros((4 * NSA_QB, 1), F32),
                jnp.zeros((4 * NSA_QB, LANES), F32))
        n_full = t0 // SEL_CHUNK
        carry = lax.fori_loop(0, n_full, functools.partial(sel_chunk, last=False), init)
        _, l_s, acc_s = sel_chunk(n_full, carry, last=True)
        o_s = acc_s / l_s

        wlen = WIN + NSA_QB
        ws = pl.multiple_of(jnp.maximum(t0 - WIN, 0), NSA_QB)
        sw = _dot_nt(q4, kw_ref[0, g, pl.ds(ws, wlen), :])
        dist = t4 - (ws + lax.broadcasted_iota(I32, (1, wlen), 1))
        pw = _softmax_rows(jnp.where((dist >= 0) & (dist < WIN), sw, NEG_INF))
        o_w = _dot(pw.astype(BF16), vw_ref[0, g, pl.ds(ws, wlen), :])

        heads = []
        for jj in range(4):
            h = 4 * g + jj
            rows = slice(jj * NSA_QB, (jj + 1) * NSA_QB)
            heads.append(gates[:, 3 * h:3 * h + 1] * o_c[rows] + gates[:, 3 * h + 1:3 * h + 2] * o_s[rows]
                         + gates[:, 3 * h + 2:3 * h + 3] * o_w[rows])
        tiles_out.append(jnp.where(lane < HEAD_DIM, heads[0], heads[1]))
        tiles_out.append(jnp.where(lane < HEAD_DIM, heads[2], heads[3]))

    y_ref[0] = jnp.concatenate(tiles_out, axis=-1).astype(BF16)


def _nsa_attn(qn, kcd, vcd, ksd, vsd, kwd, vwd, z3, overlap):
    B, S, _ = qn.shape
    ncp = kcd.shape[2]
    assert S // SLC_BLOCK <= LANES
    expand = jnp.asarray(np.arange(LANES)[:, None] == (np.arange(S)[None, :] // SLC_BLOCK), BF16)
    full = lambda rows: pl.BlockSpec((1, 2, rows, LANES), lambda b, n: (b, 0, 0, 0))
    return pl.pallas_call(
        _nsa_kernel,
        out_shape=jax.ShapeDtypeStruct((B, S, 512), BF16),
        grid=(B, S // NSA_QB),
        in_specs=[
            pl.BlockSpec((1, NSA_QB, 512), lambda b, n: (b, n, 0)),
            full(ncp), full(ncp), full(S), full(S), full(S), full(S),
            pl.BlockSpec((1, NSA_QB, LANES), lambda b, n: (b, n, COL_GATE // LANES)),
            pl.BlockSpec((ncp, LANES), lambda b, n: (0, 0)),
            pl.BlockSpec((LANES, S), lambda b, n: (0, 0)),
        ],
        out_specs=pl.BlockSpec((1, NSA_QB, 512), lambda b, n: (b, n, 0)),
        compiler_params=pltpu.CompilerParams(dimension_semantics=("parallel", "arbitrary"),
                                             vmem_limit_bytes=48 * 1024 * 1024),
        name="nsa_attn",
    )(qn, kcd, vcd, ksd, vsd, kwd, vwd, z3, overlap, expand)


def _dil_prep_kernel(*refs):
    zs, (ra_ref, rb_ref, bd_ref, gq_ref, gk_ref), outs = refs[0:18], refs[18:23], refs[23:32]
    bd = bd_ref[...]
    scale = HEAD_DIM ** -0.5
    for g, (_, d) in enumerate(DIL_PATTERNS):
        n = DIL_PREP_ROWS // d
        for r in range(d):
            rows = pl.ds(r, n, stride=d) if d > 1 else pl.ds(0, n)
            ra, rb = ra_ref[rows, :], rb_ref[rows, :]
            for which in range(3):
                for m in range(2):
                    z = zs[2 * (3 * g + which) + m][0, rows, :]
                    if which == 0:
                        z = _head_norm_rope(z, gq_ref[g], ra, rb, bd) * scale
                    elif which == 1:
                        z = _head_norm_rope(z, gk_ref[g], ra, rb, bd)
                    outs[3 * g + which][0, r, :, m * LANES:(m + 1) * LANES] = z.astype(BF16)


def _dil_prep(z3, rope_a, rope_b, blockdiag, gq, gk):
    B, S, _ = z3.shape
    nsteps = S // DIL_PREP_ROWS
    c0 = COL_DIL // LANES
    in_specs = [pl.BlockSpec((1, DIL_PREP_ROWS, LANES), lambda b, c, k=k: (b, c, c0 + k)) for k in range(18)]
    in_specs += [
        pl.BlockSpec((DIL_PREP_ROWS, LANES), lambda b, c: (c, 0)),
        pl.BlockSpec((DIL_PREP_ROWS, LANES), lambda b, c: (c, 0)),
        pl.BlockSpec((LANES, LANES), lambda b, c: (0, 0)),
        pl.BlockSpec((3, 1, LANES), lambda b, c: (0, 0, 0)),
        pl.BlockSpec((3, 1, LANES), lambda b, c: (0, 0, 0)),
    ]
    out_shape, out_specs = [], []
    for _, d in DIL_PATTERNS:
        for _ in range(3):
            out_shape.append(jax.ShapeDtypeStruct((B, d, S // d, 256), BF16))
            out_specs.append(pl.BlockSpec((1, d, DIL_PREP_ROWS // d, 256), lambda b, c: (b, 0, c, 0)))
    return pl.pallas_call(
        _dil_prep_kernel,
        out_shape=tuple(out_shape),
        grid=(B, nsteps),
        in_specs=in_specs,
        out_specs=tuple(out_specs),
        compiler_params=pltpu.CompilerParams(dimension_semantics=("parallel", "parallel"),
                                             vmem_limit_bytes=48 * 1024 * 1024),
        name="dil_prep",
    )(*([z3] * 18), rope_a, rope_b, blockdiag, gq, gk)


def _dil_kernel(*refs, seq):
    q_refs, k_refs, v_refs, y_ref, o_scr, l_scr = refs[0:3], refs[3:6], refs[6:9], refs[9], refs[10], refs[11]
    hm = _half_masks()
    lane = lax.broadcasted_iota(I32, (DIL_BLOCK, LANES), 1)
    qi = lax.broadcasted_iota(I32, (2 * DIL_BLOCK, 2 * DIL_BLOCK), 0) & (DIL_BLOCK - 1)
    ki = lax.broadcasted_iota(I32, (2 * DIL_BLOCK, 2 * DIL_BLOCK), 1)
    causal = (ki - DIL_BLOCK) <= qi

    for g, (_, d) in enumerate(DIL_PATTERNS):
        nb = seq // d // DIL_BLOCK

        def body(u, carry, g=g, d=d, nb=nb):
            j = u % nb
            r = u // nb
            r0 = pl.multiple_of(u * DIL_BLOCK, DIL_BLOCK)
            p0 = pl.multiple_of(jnp.maximum(u - 1, 0) * DIL_BLOCK, DIL_BLOCK)
            q = q_refs[g][0, pl.ds(r0, DIL_BLOCK), :]
            kcat = jnp.concatenate([k_refs[g][0, pl.ds(p0, DIL_BLOCK), :], k_refs[g][0, pl.ds(r0, DIL_BLOCK), :]], axis=0)
            vcat = jnp.concatenate([v_refs[g][0, pl.ds(p0, DIL_BLOCK), :], v_refs[g][0, pl.ds(r0, DIL_BLOCK), :]], axis=0)
            q2 = jnp.concatenate([q * hm[0], q * hm[1]], axis=0)
            s = _dot_nt(q2, kcat)
            first_key = jnp.maximum(qi, jnp.where(j >= 1, 0, DIL_BLOCK))
            s = jnp.where(causal & (ki >= first_key), s, NEG_INF)
            m = jnp.max(s, axis=-1, keepdims=True)
            e = jnp.exp(s - m)
            den = jnp.sum(e, axis=-1, keepdims=True)
            o2 = _dot(e.astype(BF16), vcat) / den
            lse = m + jnp.log(den)
            o = jnp.where(lane < HEAD_DIM, o2[:DIL_BLOCK], o2[DIL_BLOCK:])
            lv = jnp.where(lane < HEAD_DIM, lse[:DIL_BLOCK], lse[DIL_BLOCK:])
            tok0 = j * (DIL_BLOCK * d) + r
            rows = pl.ds(tok0, DIL_BLOCK, stride=d) if d > 1 else pl.ds(pl.multiple_of(tok0, DIL_BLOCK), DIL_BLOCK)
            o_scr[g, rows, :] = o
            l_scr[g, rows, :] = lv
            return carry

        lax.fori_loop(0, seq // DIL_BLOCK, body, 0)

    def merge(c, carry):
        rows = pl.ds(pl.multiple_of(c * 512, 512), 512)
        ls = [l_scr[g, rows, :] for g in range(3)]
        mx = jnp.maximum(jnp.maximum(ls[0], ls[1]), ls[2])
        ws = [jnp.exp(l - mx) for l in ls]
        num = ws[0] * o_scr[0, rows, :] + ws[1] * o_scr[1, rows, :] + ws[2] * o_scr[2, rows, :]
        y_ref[0, rows, :] = (num / (ws[0] + ws[1] + ws[2])).astype(BF16)
        return carry

    lax.fori_loop(0, seq // 512, merge, 0)


def _dil_attn(dq, dk, dv):
    B, S, _ = dq[0].shape
    spec = pl.BlockSpec((1, S, LANES), lambda b, m: (b, 0, m))
    return pl.pallas_call(
        functools.partial(_dil_kernel, seq=S),
        out_shape=jax.ShapeDtypeStruct((B, S, 256), BF16),
        grid=(B, 2),
        in_specs=[spec] * 9,
        out_specs=spec,
        scratch_shapes=[pltpu.VMEM((3, S, LANES), F32), pltpu.VMEM((3, S, LANES), F32)],
        compiler_params=pltpu.CompilerParams(dimension_semantics=("parallel", "parallel"),
                                             vmem_limit_bytes=56 * 1024 * 1024),
        name="dil_attn",
    )(*dq, *dk, *dv)


def _merge_kernel(x_ref, yn_ref, yd_ref, mg0_ref, mg1_ref, wn_ref, wd_ref, wo_ref, g2_ref, wq_ref, x1_ref, hn_ref, pq_ref):
    u1 = _dot(yn_ref[...], wn_ref[...])
    u2 = _dot(yd_ref[...], wd_ref[...])
    merged = jax.nn.sigmoid(mg0_ref[...]) * u1 + jax.nn.sigmoid(mg1_ref[...]) * u2
    x1 = x_ref[...] + _dot(merged.astype(BF16), wo_ref[...])
    x1_ref[...] = x1
    ms = jnp.mean(x1 * x1, axis=-1, keepdims=True)
    hn = x1 * lax.rsqrt(ms + NORM_EPS) * g2_ref[...]
    hn_ref[...] = hn
    pq_ref[...] = _dot(hn.astype(BF16), wq_ref[...])


def _merge(x2, yn2, yd2, z2, wn, wd, wo, g2, wq):
    T = x2.shape[0]
    tm = 512
    row = lambda w, c=0: pl.BlockSpec((tm, w), lambda i, c=c: (i, c))
    const = lambda shape: pl.BlockSpec(shape, lambda i: (0, 0))
    shp = jax.ShapeDtypeStruct((T, D_MODEL), F32)
    return pl.pallas_call(
        _merge_kernel,
        out_shape=(shp, shp, shp),
        grid=(T // tm,),
        in_specs=[row(D_MODEL), row(512), row(256), row(D_MODEL, COL_MG // D_MODEL), row(D_MODEL, COL_MG // D_MODEL + 1),
                  const((512, D_MODEL)), const((256, D_MODEL)), const((D_MODEL, D_MODEL)), const((1, D_MODEL)),
                  const((D_MODEL, D_MODEL))],
        out_specs=(row(D_MODEL), row(D_MODEL), row(D_MODEL)),
        compiler_params=pltpu.CompilerParams(dimension_semantics=("parallel",), vmem_limit_bytes=48 * 1024 * 1024),
        name="merge",
    )(x2, yn2, yd2, z2, z2, wn, wd, wo, g2, wq)


def _top16(s, rank_id=None):
    if rank_id is None:
        rank_id = lax.broadcasted_iota(I32, (s.shape[0], 1), 0)
    big = jnp.iinfo(jnp.int32).max
    vals, ids = [], []
    for _ in range(PEER_TOPK):
        m = jnp.max(s, axis=0, keepdims=True)
        win = jnp.min(jnp.where(s == m, rank_id, big), axis=0, keepdims=True)
        vals.append(m)
        ids.append(win)
        s = jnp.where(rank_id == win, -jnp.inf, s)
    return vals, ids


def _route_kernel(q_ref, sk_ref, idx_ref, gate_ref):
    K = PEER_TOPK
    qh, ql = _split_bf16(q_ref[...])
    vals, ids = [], []
    for c in range(2):
        kh, kl = _split_bf16(sk_ref[c])
        s = _dot_nt(kh, qh) + _dot_nt(kh, ql) + _dot_nt(kl, qh)
        v, p = _top16(s)
        vals.append(v)
        ids.append(p)
    v0, p0 = jnp.concatenate(vals[0], axis=0), jnp.concatenate(ids[0], axis=0)
    v1, p1 = jnp.concatenate(vals[1], axis=0), jnp.concatenate(ids[1], axis=0)

    a8 = lax.broadcasted_iota(I32, (SUBLANES, 1), 0)
    pieces = [(v0 + vals[1][0], p0 * PEER_NKEYS + ids[1][0], lax.broadcasted_iota(I32, (K, 1), 0) * K)]
    for b in range(1, SUBLANES):
        keep = a8 < K // (b + 1)
        pieces.append((jnp.where(keep, v0[:SUBLANES] + vals[1][b], -jnp.inf), p0[:SUBLANES] * PEER_NKEYS + ids[1][b],
                       a8 * K + b))
    pieces.append((vals[0][0] + v1[SUBLANES:], ids[0][0] * PEER_NKEYS + p1[SUBLANES:], a8 + SUBLANES))
    cand = jnp.concatenate([p[0] for p in pieces], axis=0)
    eid = jnp.concatenate([p[1] for p in pieces], axis=0)
    flat = jnp.concatenate([p[2] for p in pieces], axis=0)
    v, win = _top16(cand, flat)
    sel_ids = [jnp.sum(jnp.where(flat == w, eid, 0), axis=0, keepdims=True) for w in win]
    sc = jnp.concatenate(v, axis=0)
    e = jnp.exp(sc - sc[0:1])
    gate_ref[...] = e / jnp.sum(e, axis=0, keepdims=True)
    idx_ref[...] = jnp.concatenate(sel_ids, axis=0)


def _peer_route(pq, sk_pad):
    T = pq.shape[0]
    tt = 256
    return pl.pallas_call(
        _route_kernel,
        out_shape=(jax.ShapeDtypeStruct((PEER_HEADS * PEER_TOPK, T), I32),
                   jax.ShapeDtypeStruct((PEER_HEADS * PEER_TOPK, T), F32)),
        grid=(T // tt, PEER_HEADS),
        in_specs=[pl.BlockSpec((tt, LANES), lambda i, h: (i, h)),
                  pl.BlockSpec((2, PEER_NKEYS, LANES), lambda i, h: (0, 0, 0))],
        out_specs=(pl.BlockSpec((PEER_TOPK, tt), lambda i, h: (h, i)),
                   pl.BlockSpec((PEER_TOPK, tt), lambda i, h: (h, i))),
        compiler_params=pltpu.CompilerParams(dimension_semantics=("parallel", "parallel")),
        name="peer_route",
    )(pq, sk_pad)


PEER_NK = PEER_HEADS * PEER_TOPK
SUBLANES = 8


def _unpack_pair(w):
    lo = pltpu.bitcast(w << 16, F32)
    hi = pltpu.bitcast(w & jnp.uint32(0xFFFF0000), F32)
    return lo, hi


def _peer_kernel(idx_cur, idx_nxt, hn_ref, gate_ref, x1_ref, uv_hbm, o_ref, buf, sem):
    i = pl.program_id(0)
    last = pl.num_programs(0) - 1
    slot = i % 2
    half_rows = SUBLANES // 2

    def tile_copy(idx_ref, s, c, k):
        p = c * PEER_NK + k
        dst = buf.at[s, pl.ds(pl.multiple_of(p * SUBLANES, SUBLANES), SUBLANES), :]
        return pltpu.make_async_copy(uv_hbm.at[idx_ref[p]], dst, sem.at[s, c])

    def fetch_token(idx_ref, s, c):
        for k in range(PEER_NK):
            tile_copy(idx_ref, s, c, k).start(priority=k % 2)

    def wait_token(idx_ref, s, c):
        for k in range(PEER_NK):
            tile_copy(idx_ref, s, c, k).wait()

    @pl.when(i == 0)
    def _():
        lax.fori_loop(0, PEER_TOK, lambda c, carry: (fetch_token(idx_cur, 0, c), carry)[1], 0)

    eye = lax.broadcasted_iota(I32, (PEER_NK, PEER_NK), 0) == lax.broadcasted_iota(I32, (PEER_NK, PEER_NK), 1)

    def token(c, carry):
        wait_token(idx_cur, slot, c)
        base = c * (PEER_NK * SUBLANES)
        plane = lambda s: buf[slot, pl.ds(base + s, PEER_NK, stride=SUBLANES), :]
        xt = hn_ref[c]
        gcol = jnp.sum(jnp.where(eye, gate_ref[pl.ds(c, 1), :], 0.0), axis=-1, keepdims=True)
        acc = jnp.zeros((PEER_NK, LANES), F32)
        lo_rows, hi_rows = [], []
        per_phase = PEER_NK // SUBLANES
        for s in range(SUBLANES):
            for k in range(s * per_phase, (s + 1) * per_phase):
                tile_copy(idx_nxt, 1 - slot, c, k).start(priority=k % 2)
            lo, hi = _unpack_pair(plane(s))
            if s < half_rows:
                acc = acc + lo * xt[s:s + 1, :] + hi * xt[s + half_rows:s + half_rows + 1, :]
                if s == half_rows - 1:
                    act = jax.nn.gelu(jnp.sum(acc, axis=-1, keepdims=True)) * gcol
            else:
                lo_rows.append(jnp.sum(act * lo, axis=0, keepdims=True))
                hi_rows.append(jnp.sum(act * hi, axis=0, keepdims=True))
        o_ref[c] = x1_ref[c] + jnp.concatenate(lo_rows + hi_rows, axis=0)
        return carry

    for c in range(PEER_TOK):
        token(c, 0)

    @pl.when(i == last)
    def _():
        lax.fori_loop(0, PEER_TOK, lambda c, carry: (wait_token(idx_nxt, 1 - slot, c), carry)[1], 0)


def _peer_expert(idx_flat, hn_t, gates, x1_t, uv_tiles):
    T = hn_t.shape[0]
    n = T // PEER_TOK
    tile = pl.BlockSpec((PEER_TOK, SUBLANES, LANES), lambda i: (i, 0, 0))
    return pl.pallas_call(
        _peer_kernel,
        out_shape=jax.ShapeDtypeStruct((T, SUBLANES, LANES), F32),
        grid=(n,),
        in_specs=[pl.BlockSpec((PEER_TOK * PEER_NK,), lambda i: (i,), memory_space=pltpu.SMEM),
                  pl.BlockSpec((PEER_TOK * PEER_NK,), lambda i: (jnp.minimum(i + 1, n - 1),), memory_space=pltpu.SMEM),
                  tile, pl.BlockSpec((PEER_TOK, PEER_NK), lambda i: (i, 0)), tile,
                  pl.BlockSpec(memory_space=pl.ANY)],
        out_specs=tile,
        scratch_shapes=[pltpu.VMEM((2, PEER_TOK * PEER_NK * SUBLANES, LANES), jnp.uint32),
                        pltpu.SemaphoreType.DMA((2, PEER_TOK))],
        compiler_params=pltpu.CompilerParams(dimension_semantics=("arbitrary",), vmem_limit_bytes=48 * 1024 * 1024),
        name="peer_expert",
    )(idx_flat, idx_flat, hn_t, gates, x1_t, uv_tiles)


def _rope_tables(pos):
    half = ROPE_DIMS // 2
    inv = ROPE_THETA ** (-(jnp.arange(half, dtype=F32) * 2.0 / ROPE_DIMS))
    ang = pos.astype(F32)[:, None] * inv[None, :]
    cos, sin = jnp.cos(ang), jnp.sin(ang)
    n = pos.shape[0]
    a = jnp.concatenate([cos, cos, jnp.ones((n, HEAD_DIM - ROPE_DIMS), F32)], axis=-1)
    b = jnp.concatenate([-sin, sin, jnp.zeros((n, HEAD_DIM - ROPE_DIMS), F32)], axis=-1)
    return jnp.tile(a, (1, 2)), jnp.tile(b, (1, 2))


def _pack_bf16_pairs(w):
    half = w.shape[1] // 2
    bits = lax.bitcast_convert_type(w.astype(BF16), jnp.uint16).astype(jnp.uint32)
    return bits[:, :half] | (bits[:, half:] << 16)


def _tile2(v):
    return jnp.tile(v.reshape(1, HEAD_DIM), (1, 2))


def kernel(x, norm1_g, w_in, nsa_q_norm, nsa_k_norm, cmp_pe_k, cmp_w1_k, cmp_w2_k, cmp_pe_v, cmp_w1_v, cmp_w2_v,
           dil_q_norm, dil_k_norm, w_up_nsa, w_up_dil, w_o, norm2_g, peer_wq, peer_subkeys, peer_u, peer_v):
    B, S, D = x.shape
    T = B * S
    assert D == D_MODEL and S % (DIL_PATTERNS[-1][1] * DIL_BLOCK) == 0 and S >= WIN + NSA_QB and T % 512 == 0
    x2 = x.reshape(T, D)

    n_q, n_kv, n_gate, n_dil = 512, 768, 24, 2304
    o_gate = n_q + n_kv
    o_dil = o_gate + n_gate
    o_mg = o_dil + n_dil
    w_perm = jnp.concatenate([w_in[:, o_mg:], w_in[:, :o_gate], w_in[:, o_dil:o_mg], w_in[:, o_gate:o_dil],
                              jnp.zeros((D, IN_COLS_PAD - w_in.shape[1]), w_in.dtype)], axis=1).astype(BF16)
    z2 = _in_proj(x2, norm1_g.reshape(1, D), w_perm)
    z3 = z2.reshape(B, S, IN_COLS_PAD)

    blockdiag = jnp.asarray(np.kron(np.eye(2), np.ones((HEAD_DIM, HEAD_DIM))), BF16)
    rope_a, rope_b = _rope_tables(jnp.arange(S))

    qn, ksd, vsd, kwd, vwd = _nsa_prep(z3, rope_a, rope_b, blockdiag, _tile2(nsa_q_norm), _tile2(nsa_k_norm[1]),
                                       _tile2(nsa_k_norm[2]))

    n_cmp = (S - CMP_BLOCK) // CMP_STRIDE + 1
    ncp = S // CMP_STRIDE
    def flat_blocks(col):
        zc = z3[:, :, col:col + LANES].reshape(B, S, 2, HEAD_DIM).transpose(0, 2, 1, 3)
        r = zc.reshape(B, 2, ncp, CMP_STRIDE * HEAD_DIM)
        nxt = jnp.concatenate([r[:, :, 1:], jnp.zeros_like(r[:, :, :1])], axis=2)
        return jnp.concatenate([r, nxt], axis=-1).reshape(B * 2 * ncp, CMP_BLOCK * HEAD_DIM)
    cmp_a, cmp_b = _rope_tables(jnp.arange(ncp) * CMP_STRIDE + CMP_BLOCK - 1)
    dup2 = lambda w: jnp.concatenate([w, w], axis=1).astype(BF16)
    kcd, vcd = _compress(flat_blocks(COL_KV), flat_blocks(COL_KV + LANES),
                         cmp_pe_k.reshape(1, -1), cmp_pe_v.reshape(1, -1), cmp_w1_k.astype(BF16), cmp_w1_v.astype(BF16),
                         dup2(cmp_w2_k), dup2(cmp_w2_v), cmp_a, cmp_b, blockdiag, _tile2(nsa_k_norm[0]))
    kcd = kcd.reshape(B, 2, ncp, LANES)
    vcd = vcd.reshape(B, 2, ncp, LANES)

    n_slc = S // SLC_BLOCK
    s0 = np.arange(n_cmp) * CMP_STRIDE
    b0 = np.arange(n_slc) * SLC_BLOCK
    ov = np.clip(np.minimum(s0[:, None] + CMP_BLOCK, b0[None, :] + SLC_BLOCK) - np.maximum(s0[:, None], b0[None, :]),
                 0, None) / CMP_BLOCK
    ov_pad = np.zeros((ncp, LANES), np.float32)
    ov_pad[:n_cmp, :n_slc] = ov
    y_nsa = _nsa_attn(qn, kcd, vcd, ksd, vsd, kwd, vwd, z3, jnp.asarray(ov_pad, BF16))

    gq = jnp.tile(dil_q_norm.reshape(3, 1, HEAD_DIM), (1, 1, 2))
    gk = jnp.tile(dil_k_norm.reshape(3, 1, HEAD_DIM), (1, 1, 2))
    prep = _dil_prep(z3, rope_a, rope_b, blockdiag, gq, gk)
    flat = [p.reshape(B, S, 256) for p in prep]
    y_dil = _dil_attn(flat[0::3], flat[1::3], flat[2::3])

    x1, hn, pq = _merge(x2, y_nsa.reshape(T, 512), y_dil.reshape(T, 256), z2, w_up_nsa.astype(BF16),
                        w_up_dil.astype(BF16), w_o.astype(BF16), norm2_g.reshape(1, D), peer_wq.astype(BF16))

    sub = PEER_NKEYS // 2
    sk_pad = jnp.stack([jnp.pad(peer_subkeys[0], ((0, 0), (0, sub))), jnp.pad(peer_subkeys[1], ((0, 0), (sub, 0)))])
    idx_t, gate_t = _peer_route(pq, sk_pad)
    uv_tiles = jnp.concatenate([_pack_bf16_pairs(peer_u), _pack_bf16_pairs(peer_v)], axis=1)
    out = _peer_expert(idx_t.T.reshape(-1), hn.reshape(T, SUBLANES, LANES), gate_t.T, x1.reshape(T, SUBLANES, LANES),
                       uv_tiles.reshape(-1, SUBLANES, LANES))
    return out.reshape(B, S, D)
```

```python
import functools

import numpy as np
import jax
import jax.numpy as jnp
from jax import lax
from jax.experimental import pallas as pl
from jax.experimental.pallas import tpu as pltpu

F32 = jnp.float32
BF16 = jnp.bfloat16
I32 = jnp.int32

D_MODEL = 1024
HEAD_DIM = 64
ROPE_DIMS = 16
ROPE_THETA = 500000.0
NORM_EPS = 1e-6
NEG_INF = -1e30
LANES = 128

NSA_HEADS = 8
CMP_BLOCK = 32
CMP_STRIDE = 16
CMP_HIDDEN = 256
SLC_BLOCK = 64
SLC_TOPN = 16
FORCE_SCORE = 1e3
WIN = 512
NSA_QB = 128
SEL_CHUNK = 512

DIL_PATTERNS = ((128, 1), (512, 4), (2048, 16))
DIL_BLOCK = 128
DIL_PREP_ROWS = 1024

PEER_HEADS = 8
PEER_NKEYS = 128
PEER_TOPK = 16
PEER_TOK = 16

COL_MG = 0
COL_Q = 2048
COL_KV = 2560
COL_DIL = 3328
COL_GATE = 5632
IN_COLS_PAD = 5760

_NT = (((1,), (1,)), ((), ()))


def _dot(a, b):
    return jnp.dot(a, b, preferred_element_type=F32)


def _dot_nt(a, b):
    return lax.dot_general(a, b, _NT, preferred_element_type=F32)


def _split_bf16(a):
    hi = a.astype(BF16)
    lo = (a - hi.astype(F32)).astype(BF16)
    return hi, lo


def _dot_hilo(a, b_bf16):
    hi, lo = _split_bf16(a)
    return _dot(hi, b_bf16) + _dot(lo, b_bf16)


def _head_norm_rope(zt, gain, rope_a, rope_b, blockdiag):
    ss = _dot_hilo(zt * zt, blockdiag)
    zn = zt * lax.rsqrt(ss * (1.0 / HEAD_DIM) + NORM_EPS) * gain
    d = lax.broadcasted_iota(I32, zn.shape, 1) & (HEAD_DIM - 1)
    half = ROPE_DIMS // 2
    partner = jnp.where(d < half, pltpu.roll(zn, LANES - half, 1), pltpu.roll(zn, half, 1))
    return zn * rope_a + partner * rope_b


def _half_masks():
    lane = lax.broadcasted_iota(I32, (1, LANES), 1)
    lo = (lane < HEAD_DIM).astype(BF16)
    return lo, (1 - lo).astype(BF16)


def _inproj_kernel(x_ref, g_ref, w_ref, o_ref, h_scr):
    @pl.when(pl.program_id(1) == 0)
    def _():
        xf = x_ref[...]
        ms = jnp.mean(xf * xf, axis=-1, keepdims=True)
        h_scr[...] = (xf * lax.rsqrt(ms + NORM_EPS) * g_ref[...]).astype(BF16)

    o_ref[...] = _dot(h_scr[...], w_ref[...])


def _in_proj(x2, g1, w_bf16):
    T = x2.shape[0]
    tm, tn = 512, 640
    return pl.pallas_call(
        _inproj_kernel,
        out_shape=jax.ShapeDtypeStruct((T, IN_COLS_PAD), F32),
        grid=(T // tm, IN_COLS_PAD // tn),
        in_specs=[
            pl.BlockSpec((tm, D_MODEL), lambda i, j: (i, 0)),
            pl.BlockSpec((1, D_MODEL), lambda i, j: (0, 0)),
            pl.BlockSpec((D_MODEL, tn), lambda i, j: (0, j)),
        ],
        out_specs=pl.BlockSpec((tm, tn), lambda i, j: (i, j)),
        scratch_shapes=[pltpu.VMEM((tm, D_MODEL), BF16)],
        compiler_params=pltpu.CompilerParams(dimension_semantics=("parallel", "arbitrary")),
        name="in_proj",
    )(x2, g1, w_bf16)


def _nsa_prep_kernel(zq_ref, zks_ref, zvs_ref, zkw_ref, zvw_ref, ra_ref, rb_ref, bd_ref, gq_ref, gks_ref, gkw_ref,
                     q_ref, ks_ref, vs_ref, kw_ref, vw_ref):
    ra, rb, bd = ra_ref[...], rb_ref[...], bd_ref[...]
    lane = lax.broadcasted_iota(I32, ra.shape, 1)
    scale = HEAD_DIM ** -0.5

    zq = zq_ref[0]
    tiles = [_head_norm_rope(zq[:, m * LANES:(m + 1) * LANES], gq_ref[...], ra, rb, bd) * scale for m in range(4)]
    q_ref[0] = jnp.concatenate(tiles, axis=-1).astype(BF16)

    def dup(t, out_ref):
        sw = pltpu.roll(t, HEAD_DIM, 1)
        out_ref[0, 0] = jnp.where(lane < HEAD_DIM, t, sw).astype(BF16)
        out_ref[0, 1] = jnp.where(lane < HEAD_DIM, sw, t).astype(BF16)

    dup(_head_norm_rope(zks_ref[0], gks_ref[...], ra, rb, bd), ks_ref)
    dup(_head_norm_rope(zkw_ref[0], gkw_ref[...], ra, rb, bd), kw_ref)
    dup(zvs_ref[0], vs_ref)
    dup(zvw_ref[0], vw_ref)


def _nsa_prep(z3, rope_a, rope_b, blockdiag, gq, gks, gkw):
    B, S, _ = z3.shape
    tm = 512
    kvb = COL_KV // LANES
    zcol = lambda c: pl.BlockSpec((1, tm, LANES), lambda b, i, c=c: (b, i, c))
    const = lambda shape: pl.BlockSpec(shape, lambda b, i: tuple(0 for _ in shape))
    kv_out = pl.BlockSpec((1, 2, tm, LANES), lambda b, i: (b, 0, i, 0))
    kv_shape = jax.ShapeDtypeStruct((B, 2, S, LANES), BF16)
    return pl.pallas_call(
        _nsa_prep_kernel,
        out_shape=(jax.ShapeDtypeStruct((B, S, 512), BF16), kv_shape, kv_shape, kv_shape, kv_shape),
        grid=(B, S // tm),
        in_specs=[
            pl.BlockSpec((1, tm, 512), lambda b, i: (b, i, COL_Q // 512)),
            zcol(kvb + 2), zcol(kvb + 3), zcol(kvb + 4), zcol(kvb + 5),
            pl.BlockSpec((tm, LANES), lambda b, i: (i, 0)),
            pl.BlockSpec((tm, LANES), lambda b, i: (i, 0)),
            const((LANES, LANES)), const((1, LANES)), const((1, LANES)), const((1, LANES)),
        ],
        out_specs=(pl.BlockSpec((1, tm, 512), lambda b, i: (b, i, 0)), kv_out, kv_out, kv_out, kv_out),
        compiler_params=pltpu.CompilerParams(dimension_semantics=("parallel", "parallel")),
        name="nsa_prep",
    )(z3, z3, z3, z3, z3, rope_a, rope_b, blockdiag, gq, gks, gkw)


def _compress_kernel(fk_ref, fv_ref, pek_ref, pev_ref, w1k_ref, w1v_ref, w2k_ref, w2v_ref, ra_ref, rb_ref, bd_ref, gk_ref,
                     k_ref, v_ref):
    def mlp(f_ref, pe_ref, w1_ref, w2_ref):
        f = (f_ref[...] + pe_ref[...]).astype(BF16)
        h = jax.nn.gelu(_dot(f, w1_ref[...]))
        return _dot(h.astype(BF16), w2_ref[...])

    kc = mlp(fk_ref, pek_ref, w1k_ref, w2k_ref)
    k_ref[...] = _head_norm_rope(kc, gk_ref[...], ra_ref[...], rb_ref[...], bd_ref[...]).astype(BF16)
    v_ref[...] = mlp(fv_ref, pev_ref, w1v_ref, w2v_ref).astype(BF16)


def _compress(flat_k, flat_v, pek, pev, w1k, w1v, w2k, w2v, rope_a, rope_b, blockdiag, gk):
    rows, width = flat_k.shape
    nblk = rope_a.shape[0]
    row = pl.BlockSpec((nblk, width), lambda i: (i, 0))
    const = lambda shape: pl.BlockSpec(shape, lambda i: tuple(0 for _ in shape))
    out = pl.BlockSpec((nblk, LANES), lambda i: (i, 0))
    shp = jax.ShapeDtypeStruct((rows, LANES), BF16)
    return pl.pallas_call(
        _compress_kernel,
        out_shape=(shp, shp),
        grid=(rows // nblk,),
        in_specs=[row, row, const((1, width)), const((1, width)), const((width, CMP_HIDDEN)), const((width, CMP_HIDDEN)),
                  const((CMP_HIDDEN, LANES)), const((CMP_HIDDEN, LANES)), const((nblk, LANES)), const((nblk, LANES)),
                  const((LANES, LANES)), const((1, LANES))],
        out_specs=(out, out),
        compiler_params=pltpu.CompilerParams(dimension_semantics=("parallel",)),
        name="compress",
    )(flat_k, flat_v, pek, pev, w1k, w1v, w2k, w2v, rope_a, rope_b, blockdiag, gk)


def _softmax_rows(s):
    m = jnp.max(s, axis=-1, keepdims=True)
    e = jnp.exp(s - m)
    return e / jnp.sum(e, axis=-1, keepdims=True)


def _nsa_kernel(q_ref, kc_ref, vc_ref, ks_ref, vs_ref, kw_ref, vw_ref, gl_ref, ov_ref, ex_ref, y_ref):
    n = pl.program_id(1)
    t0 = n * NSA_QB
    qt = q_ref[0]
    hm = _half_masks()
    lane = lax.broadcasted_iota(I32, (NSA_QB, LANES), 1)
    t1 = t0 + lax.broadcasted_iota(I32, (NSA_QB, 1), 0)
    t4 = t0 + (lax.broadcasted_iota(I32, (4 * NSA_QB, 1), 0) & (NSA_QB - 1))
    gates = jax.nn.sigmoid(gl_ref[0])

    n_slc_blocks = ks_ref.shape[2] // SLC_BLOCK
    blk = lax.broadcasted_iota(I32, (n_slc_blocks, NSA_QB), 0)
    tq = t0 + lax.broadcasted_iota(I32, (n_slc_blocks, NSA_QB), 1)
    cur = tq >> 6
    forced = (blk == 0) | (blk == cur) | (blk == cur - 1)
    causal_b = blk * SLC_BLOCK <= tq

    tiles_out = []
    for g in range(2):
        q4 = jnp.concatenate(
            [qt[:, (2 * g + jj // 2) * LANES:(2 * g + jj // 2 + 1) * LANES] * hm[jj % 2] for jj in range(4)], axis=0)

        sc = _dot_nt(q4, kc_ref[0, g])
        cend = lax.broadcasted_iota(I32, (1, sc.shape[1]), 1) * CMP_STRIDE + (CMP_BLOCK - 1)
        valid_c = cend <= t4
        pc = _softmax_rows(jnp.where(valid_c, sc, NEG_INF))
        pc = jnp.where(valid_c, pc, 0.0)
        o_c = _dot(pc.astype(BF16), vc_ref[0, g])
        psum = pc[0:NSA_QB] + pc[NSA_QB:2 * NSA_QB] + pc[2 * NSA_QB:3 * NSA_QB] + pc[3 * NSA_QB:]
        imp = _dot_hilo(psum, ov_ref[...])

        score = jnp.where(forced, FORCE_SCORE, jnp.where(causal_b, imp.T[:n_slc_blocks], -1.0))
        n_grp = n_slc_blocks // SUBLANES
        grp = [score[SUBLANES * r:SUBLANES * (r + 1)] for r in range(n_grp)]
        ranks = [jnp.zeros((SUBLANES, NSA_QB), F32) for _ in range(n_grp)]
        row = lax.broadcasted_iota(I32, (SUBLANES, NSA_QB), 0)
        for i in range(n_slc_blocks):
            ri = grp[i // SUBLANES][i % SUBLANES:i % SUBLANES + 1, :]
            for r in range(n_grp):
                if r > i // SUBLANES:
                    ahead = ri >= grp[r]
                elif r < i // SUBLANES:
                    ahead = ri > grp[r]
                else:
                    ahead = (ri > grp[r]) | ((ri == grp[r]) & (row > i % SUBLANES))
                ranks[r] = ranks[r] + jnp.where(ahead, 1.0, 0.0)
        sel64 = jnp.where(jnp.concatenate(ranks, axis=0) < float(SLC_TOPN), 1.0, 0.0)
        sel = jnp.concatenate([sel64, jnp.zeros((LANES - n_slc_blocks, NSA_QB), F32)], axis=0).T.astype(BF16)

        def sel_chunk(c, carry, last):
            m_i, l_i, acc = carry
            k0 = pl.multiple_of(c * SEL_CHUNK, SEL_CHUNK)
            kch = ks_ref[0, g, pl.ds(k0, SEL_CHUNK), :]
            vch = vs_ref[0, g, pl.ds(k0, SEL_CHUNK), :]
            picked = _dot(sel, ex_ref[:, pl.ds(k0, SEL_CHUNK)])
            if last:
                kpos = k0 + lax.broadcasted_iota(I32, (1, SEL_CHUNK), 1)
                picked = jnp.where(kpos <= t1, picked, 0.0)
            bias = (picked - 1.0) * (-NEG_INF)
            s = (_dot_nt(q4, kch).reshape(4, NSA_QB, SEL_CHUNK) + bias[None]).reshape(4 * NSA_QB, SEL_CHUNK)
            m_new = jnp.maximum(m_i, jnp.max(s, axis=-1, keepdims=True))
            alpha = jnp.exp(m_i - m_new)
            p = jnp.exp(s - m_new)
            l_new = alpha * l_i + jnp.sum(p, axis=-1, keepdims=True)
            acc_new = alpha * acc + _dot(p.astype(BF16), vch)
            return m_new, l_new, acc_new

        init = (jnp.full((4 * NSA_QB, 1), NEG_INF, F32), jnp.zeros((4 * NSA_QB, 1), F32),
                jnp.zeros((4 * NSA_QB, LANES), F32))
        n_full = t0 // SEL_CHUNK
        carry = lax.fori_loop(0, n_full, functools.partial(sel_chunk, last=False), init)
        _, l_s, acc_s = sel_chunk(n_full, carry, last=True)
        o_s = acc_s / l_s

        wlen = WIN + NSA_QB
        ws = pl.multiple_of(jnp.maximum(t0 - WIN, 0), NSA_QB)
        sw = _dot_nt(q4, kw_ref[0, g, pl.ds(ws, wlen), :])
        dist = t4 - (ws + lax.broadcasted_iota(I32, (1, wlen), 1))
        pw = _softmax_rows(jnp.where((dist >= 0) & (dist < WIN), sw, NEG_INF))
        o_w = _dot(pw.astype(BF16), vw_ref[0, g, pl.ds(ws, wlen), :])

        heads = []
        for jj in range(4):
            h = 4 * g + jj
            rows = slice(jj * NSA_QB, (jj + 1) * NSA_QB)
            heads.append(gates[:, 3 * h:3 * h + 1] * o_c[rows] + gates[:, 3 * h + 1:3 * h + 2] * o_s[rows]
                         + gates[:, 3 * h + 2:3 * h + 3] * o_w[rows])
        tiles_out.append(jnp.where(lane < HEAD_DIM, heads[0], heads[1]))
        tiles_out.append(jnp.where(lane < HEAD_DIM, heads[2], heads[3]))

    y_ref[0] = jnp.concatenate(tiles_out, axis=-1).astype(BF16)


def _nsa_attn(qn, kcd, vcd, ksd, vsd, kwd, vwd, z3, overlap):
    B, S, _ = qn.shape
    ncp = kcd.shape[2]
    assert S // SLC_BLOCK <= LANES
    expand = jnp.asarray(np.arange(LANES)[:, None] == (np.arange(S)[None, :] // SLC_BLOCK), BF16)
    full = lambda rows: pl.BlockSpec((1, 2, rows, LANES), lambda b, n: (b, 0, 0, 0))
    return pl.pallas_call(
        _nsa_kernel,
        out_shape=jax.ShapeDtypeStruct((B, S, 512), BF16),
        grid=(B, S // NSA_QB),
        in_specs=[
            pl.BlockSpec((1, NSA_QB, 512), lambda b, n: (b, n, 0)),
            full(ncp), full(ncp), full(S), full(S), full(S), full(S),
            pl.BlockSpec((1, NSA_QB, LANES), lambda b, n: (b, n, COL_GATE // LANES)),
            pl.BlockSpec((ncp, LANES), lambda b, n: (0, 0)),
            pl.BlockSpec((LANES, S), lambda b, n: (0, 0)),
        ],
        out_specs=pl.BlockSpec((1, NSA_QB, 512), lambda b, n: (b, n, 0)),
        compiler_params=pltpu.CompilerParams(dimension_semantics=("parallel", "arbitrary"),
                                             vmem_limit_bytes=48 * 1024 * 1024),
        name="nsa_attn",
    )(qn, kcd, vcd, ksd, vsd, kwd, vwd, z3, overlap, expand)


def _dil_prep_kernel(*refs):
    zs, (ra_ref, rb_ref, bd_ref, gq_ref, gk_ref), outs = refs[0:18], refs[18:23], refs[23:32]
    bd = bd_ref[...]
    scale = HEAD_DIM ** -0.5
    for g, (_, d) in enumerate(DIL_PATTERNS):
        n = DIL_PREP_ROWS // d
        for r in range(d):
            rows = pl.ds(r, n, stride=d) if d > 1 else pl.ds(0, n)
            ra, rb = ra_ref[rows, :], rb_ref[rows, :]
            for which in range(3):
                for m in range(2):
                    z = zs[2 * (3 * g + which) + m][0, rows, :]
                    if which == 0:
                        z = _head_norm_rope(z, gq_ref[g], ra, rb, bd) * scale
                    elif which == 1:
                        z = _head_norm_rope(z, gk_ref[g], ra, rb, bd)
                    outs[3 * g + which][0, r, :, m * LANES:(m + 1) * LANES] = z.astype(BF16)


def _dil_prep(z3, rope_a, rope_b, blockdiag, gq, gk):
    B, S, _ = z3.shape
    nsteps = S // DIL_PREP_ROWS
    c0 = COL_DIL // LANES
    in_specs = [pl.BlockSpec((1, DIL_PREP_ROWS, LANES), lambda b, c, k=k: (b, c, c0 + k)) for k in range(18)]
    in_specs += [
        pl.BlockSpec((DIL_PREP_ROWS, LANES), lambda b, c: (c, 0)),
        pl.BlockSpec((DIL_PREP_ROWS, LANES), lambda b, c: (c, 0)),
        pl.BlockSpec((LANES, LANES), lambda b, c: (0, 0)),
        pl.BlockSpec((3, 1, LANES), lambda b, c: (0, 0, 0)),
        pl.BlockSpec((3, 1, LANES), lambda b, c: (0, 0, 0)),
    ]
    out_shape, out_specs = [], []
    for _, d in DIL_PATTERNS:
        for _ in range(3):
            out_shape.append(jax.ShapeDtypeStruct((B, d, S // d, 256), BF16))
            out_specs.append(pl.BlockSpec((1, d, DIL_PREP_ROWS // d, 256), lambda b, c: (b, 0, c, 0)))
    return pl.pallas_call(
        _dil_prep_kernel,
        out_shape=tuple(out_shape),
        grid=(B, nsteps),
        in_specs=in_specs,
        out_specs=tuple(out_specs),
        compiler_params=pltpu.CompilerParams(dimension_semantics=("parallel", "parallel"),
                                             vmem_limit_bytes=48 * 1024 * 1024),
        name="dil_prep",
    )(*([z3] * 18), rope_a, rope_b, blockdiag, gq, gk)


def _dil_kernel(*refs, seq):
    q_refs, k_refs, v_refs, y_ref, o_scr, l_scr = refs[0:3], refs[3:6], refs[6:9], refs[9], refs[10], refs[11]
    hm = _half_masks()
    lane = lax.broadcasted_iota(I32, (DIL_BLOCK, LANES), 1)
    qi = lax.broadcasted_iota(I32, (2 * DIL_BLOCK, 2 * DIL_BLOCK), 0) & (DIL_BLOCK - 1)
    ki = lax.broadcasted_iota(I32, (2 * DIL_BLOCK, 2 * DIL_BLOCK), 1)
    causal = (ki - DIL_BLOCK) <= qi

    for g, (_, d) in enumerate(DIL_PATTERNS):
        nb = seq // d // DIL_BLOCK

        def body(u, carry, g=g, d=d, nb=nb):
            j = u % nb
            r = u // nb
            r0 = pl.multiple_of(u * DIL_BLOCK, DIL_BLOCK)
            p0 = pl.multiple_of(jnp.maximum(u - 1, 0) * DIL_BLOCK, DIL_BLOCK)
            q = q_refs[g][0, pl.ds(r0, DIL_BLOCK), :]
            kcat = jnp.concatenate([k_refs[g][0, pl.ds(p0, DIL_BLOCK), :], k_refs[g][0, pl.ds(r0, DIL_BLOCK), :]], axis=0)
            vcat = jnp.concatenate([v_refs[g][0, pl.ds(p0, DIL_BLOCK), :], v_refs[g][0, pl.ds(r0, DIL_BLOCK), :]], axis=0)
            q2 = jnp.concatenate([q * hm[0], q * hm[1]], axis=0)
            s = _dot_nt(q2, kcat)
            first_key = jnp.maximum(qi, jnp.where(j >= 1, 0, DIL_BLOCK))
            s = jnp.where(causal & (ki >= first_key), s, NEG_INF)
            m = jnp.max(s, axis=-1, keepdims=True)
            e = jnp.exp(s - m)
            den = jnp.sum(e, axis=-1, keepdims=True)
            o2 = _dot(e.astype(BF16), vcat) / den
            lse = m + jnp.log(den)
            o = jnp.where(lane < HEAD_DIM, o2[:DIL_BLOCK], o2[DIL_BLOCK:])
            lv = jnp.where(lane < HEAD_DIM, lse[:DIL_BLOCK], lse[DIL_BLOCK:])
            tok0 = j * (DIL_BLOCK * d) + r
            rows = pl.ds(tok0, DIL_BLOCK, stride=d) if d > 1 else pl.ds(pl.multiple_of(tok0, DIL_BLOCK), DIL_BLOCK)
            o_scr[g, rows, :] = o
            l_scr[g, rows, :] = lv
            return carry

        lax.fori_loop(0, seq // DIL_BLOCK, body, 0)

    def merge(c, carry):
        rows = pl.ds(pl.multiple_of(c * 512, 512), 512)
        ls = [l_scr[g, rows, :] for g in range(3)]
        mx = jnp.maximum(jnp.maximum(ls[0], ls[1]), ls[2])
        ws = [jnp.exp(l - mx) for l in ls]
        num = ws[0] * o_scr[0, rows, :] + ws[1] * o_scr[1, rows, :] + ws[2] * o_scr[2, rows, :]
        y_ref[0, rows, :] = (num / (ws[0] + ws[1] + ws[2])).astype(BF16)
        return carry

    lax.fori_loop(0, seq // 512, merge, 0)


def _dil_attn(dq, dk, dv):
    B, S, _ = dq[0].shape
    spec = pl.BlockSpec((1, S, LANES), lambda b, m: (b, 0, m))
    return pl.pallas_call(
        functools.partial(_dil_kernel, seq=S),
        out_shape=jax.ShapeDtypeStruct((B, S, 256), BF16),
        grid=(B, 2),
        in_specs=[spec] * 9,
        out_specs=spec,
        scratch_shapes=[pltpu.VMEM((3, S, LANES), F32), pltpu.VMEM((3, S, LANES), F32)],
        compiler_params=pltpu.CompilerParams(dimension_semantics=("parallel", "parallel"),
                                             vmem_limit_bytes=56 * 1024 * 1024),
        name="dil_attn",
    )(*dq, *dk, *dv)


def _merge_kernel(x_ref, yn_ref, yd_ref, mg0_ref, mg1_ref, wn_ref, wd_ref, wo_ref, g2_ref, wq_ref, x1_ref, hn_ref, pq_ref):
    u1 = _dot(yn_ref[...], wn_ref[...])
    u2 = _dot(yd_ref[...], wd_ref[...])
    merged = jax.nn.sigmoid(mg0_ref[...]) * u1 + jax.nn.sigmoid(mg1_ref[...]) * u2
    x1 = x_ref[...] + _dot(merged.astype(BF16), wo_ref[...])
    x1_ref[...] = x1
    ms = jnp.mean(x1 * x1, axis=-1, keepdims=True)
    hn = x1 * lax.rsqrt(ms + NORM_EPS) * g2_ref[...]
    hn_ref[...] = hn
    pq_ref[...] = _dot(hn.astype(BF16), wq_ref[...])


def _merge(x2, yn2, yd2, z2, wn, wd, wo, g2, wq):
    T = x2.shape[0]
    tm = 512
    row = lambda w, c=0: pl.BlockSpec((tm, w), lambda i, c=c: (i, c))
    const = lambda shape: pl.BlockSpec(shape, lambda i: (0, 0))
    shp = jax.ShapeDtypeStruct((T, D_MODEL), F32)
    return pl.pallas_call(
        _merge_kernel,
        out_shape=(shp, shp, shp),
        grid=(T // tm,),
        in_specs=[row(D_MODEL), row(512), row(256), row(D_MODEL, COL_MG // D_MODEL), row(D_MODEL, COL_MG // D_MODEL + 1),
                  const((512, D_MODEL)), const((256, D_MODEL)), const((D_MODEL, D_MODEL)), const((1, D_MODEL)),
                  const((D_MODEL, D_MODEL))],
        out_specs=(row(D_MODEL), row(D_MODEL), row(D_MODEL)),
        compiler_params=pltpu.CompilerParams(dimension_semantics=("parallel",), vmem_limit_bytes=48 * 1024 * 1024),
        name="merge",
    )(x2, yn2, yd2, z2, z2, wn, wd, wo, g2, wq)


def _top16(s, rank_id=None):
    if rank_id is None:
        rank_id = lax.broadcasted_iota(I32, (s.shape[0], 1), 0)
    big = jnp.iinfo(jnp.int32).max
    vals, ids = [], []
    for _ in range(PEER_TOPK):
        m = jnp.max(s, axis=0, keepdims=True)
        win = jnp.min(jnp.where(s == m, rank_id, big), axis=0, keepdims=True)
        vals.append(m)
        ids.append(win)
        s = jnp.where(rank_id == win, -jnp.inf, s)
    return vals, ids


def _route_kernel(q_ref, sk_ref, idx_ref, gate_ref):
    K = PEER_TOPK
    qh, ql = _split_bf16(q_ref[...])
    vals, ids = [], []
    for c in range(2):
        kh, kl = _split_bf16(sk_ref[c])
        s = _dot_nt(kh, qh) + _dot_nt(kh, ql) + _dot_nt(kl, qh)
        v, p = _top16(s)
        vals.append(v)
        ids.append(p)
    v0, p0 = jnp.concatenate(vals[0], axis=0), jnp.concatenate(ids[0], axis=0)
    v1, p1 = jnp.concatenate(vals[1], axis=0), jnp.concatenate(ids[1], axis=0)

    a8 = lax.broadcasted_iota(I32, (SUBLANES, 1), 0)
    pieces = [(v0 + vals[1][0], p0 * PEER_NKEYS + ids[1][0], lax.broadcasted_iota(I32, (K, 1), 0) * K)]
    for b in range(1, SUBLANES):
        keep = a8 < K // (b + 1)
        pieces.append((jnp.where(keep, v0[:SUBLANES] + vals[1][b], -jnp.inf), p0[:SUBLANES] * PEER_NKEYS + ids[1][b],
                       a8 * K + b))
    pieces.append((vals[0][0] + v1[SUBLANES:], ids[0][0] * PEER_NKEYS + p1[SUBLANES:], a8 + SUBLANES))
    cand = jnp.concatenate([p[0] for p in pieces], axis=0)
    eid = jnp.concatenate([p[1] for p in pieces], axis=0)
    flat = jnp.concatenate([p[2] for p in pieces], axis=0)
    v, win = _top16(cand, flat)
    sel_ids = [jnp.sum(jnp.where(flat == w, eid, 0), axis=0, keepdims=True) for w in win]
    sc = jnp.concatenate(v, axis=0)
    e = jnp.exp(sc - sc[0:1])
    gate_ref[...] = e / jnp.sum(e, axis=0, keepdims=True)
    idx_ref[...] = jnp.concatenate(sel_ids, axis=0)


def _peer_route(pq, sk_pad):
    T = pq.shape[0]
    tt = 256
    return pl.pallas_call(
        _route_kernel,
        out_shape=(jax.ShapeDtypeStruct((PEER_HEADS * PEER_TOPK, T), I32),
                   jax.ShapeDtypeStruct((PEER_HEADS * PEER_TOPK, T), F32)),
        grid=(T // tt, PEER_HEADS),
        in_specs=[pl.BlockSpec((tt, LANES), lambda i, h: (i, h)),
                  pl.BlockSpec((2, PEER_NKEYS, LANES), lambda i, h: (0, 0, 0))],
        out_specs=(pl.BlockSpec((PEER_TOPK, tt), lambda i, h: (h, i)),
                   pl.BlockSpec((PEER_TOPK, tt), lambda i, h: (h, i))),
        compiler_params=pltpu.CompilerParams(dimension_semantics=("parallel", "parallel")),
        name="peer_route",
    )(pq, sk_pad)


PEER_NK = PEER_HEADS * PEER_TOPK
SUBLANES = 8


def _unpack_pair(w):
    lo = pltpu.bitcast(w << 16, F32)
    hi = pltpu.bitcast(w & jnp.uint32(0xFFFF0000), F32)
    return lo, hi


def _peer_kernel(idx_cur, idx_nxt, hn_ref, gate_ref, x1_ref, uv_hbm, o_ref, buf_a, buf_b, sem):
    i = pl.program_id(0)
    last = pl.num_programs(0) - 1
    half_rows = SUBLANES // 2
    bufs = (buf_a, buf_b)

    def tile_copy(idx_ref, s, c, k):
        p = c * PEER_NK + k
        return pltpu.make_async_copy(uv_hbm.at[idx_ref[p]], bufs[s].at[pl.ds(p * SUBLANES, SUBLANES), :], sem.at[s, c])

    def fetch_token(idx_ref, s, c):
        for k in range(PEER_NK):
            tile_copy(idx_ref, s, c, k).start(priority=k % 2)

    def wait_token(idx_ref, s, c):
        for k in range(PEER_NK):
            tile_copy(idx_ref, s, c, k).wait()

    @pl.when(i == 0)
    def _():
        lax.fori_loop(0, PEER_TOK, lambda c, carry: (fetch_token(idx_cur, 0, c), carry)[1], 0)

    eye = lax.broadcasted_iota(I32, (PEER_NK, PEER_NK), 0) == lax.broadcasted_iota(I32, (PEER_NK, PEER_NK), 1)

    def token(slot, c):
        wait_token(idx_cur, slot, c)
        fetch_token(idx_nxt, 1 - slot, c)
        base = c * (PEER_NK * SUBLANES)
        plane = lambda s: bufs[slot][pl.ds(base + s, PEER_NK, stride=SUBLANES), :]
        xt = hn_ref[c]
        gcol = jnp.sum(jnp.where(eye, gate_ref[pl.ds(c, 1), :], 0.0), axis=-1, keepdims=True)
        acc = jnp.zeros((PEER_NK, LANES), F32)
        for s in range(half_rows):
            lo, hi = _unpack_pair(plane(s))
            acc = acc + lo * xt[s:s + 1, :] + hi * xt[s + half_rows:s + half_rows + 1, :]
        act = jax.nn.gelu(jnp.sum(acc, axis=-1, keepdims=True)) * gcol
        lo_rows, hi_rows = [], []
        for s in range(half_rows, SUBLANES):
            lo, hi = _unpack_pair(plane(s))
            lo_rows.append(jnp.sum(act * lo, axis=0, keepdims=True))
            hi_rows.append(jnp.sum(act * hi, axis=0, keepdims=True))
        o_ref[c] = x1_ref[c] + jnp.concatenate(lo_rows + hi_rows, axis=0)

    def step(slot):
        for c in range(PEER_TOK):
            token(slot, c)

        @pl.when(i == last)
        def _():
            lax.fori_loop(0, PEER_TOK, lambda c, carry: (wait_token(idx_nxt, 1 - slot, c), carry)[1], 0)

    pl.when(i % 2 == 0)(lambda: step(0))
    pl.when(i % 2 == 1)(lambda: step(1))


def _peer_expert(idx_flat, hn_t, gates, x1_t, uv_tiles):
    T = hn_t.shape[0]
    n = T // PEER_TOK
    tile = pl.BlockSpec((PEER_TOK, SUBLANES, LANES), lambda i: (i, 0, 0))
    fetch_buf = pltpu.VMEM((PEER_TOK * PEER_NK * SUBLANES, LANES), jnp.uint32)
    return pl.pallas_call(
        _peer_kernel,
        out_shape=jax.ShapeDtypeStruct((T, SUBLANES, LANES), F32),
        grid=(n,),
        in_specs=[pl.BlockSpec((PEER_TOK * PEER_NK,), lambda i: (i,), memory_space=pltpu.SMEM),
                  pl.BlockSpec((PEER_TOK * PEER_NK,), lambda i: (jnp.minimum(i + 1, n - 1),), memory_space=pltpu.SMEM),
                  tile, pl.BlockSpec((PEER_TOK, PEER_NK), lambda i: (i, 0)), tile,
                  pl.BlockSpec(memory_space=pl.ANY)],
        out_specs=tile,
        scratch_shapes=[fetch_buf, fetch_buf, pltpu.SemaphoreType.DMA((2, PEER_TOK))],
        compiler_params=pltpu.CompilerParams(dimension_semantics=("arbitrary",), vmem_limit_bytes=48 * 1024 * 1024),
        name="peer_expert",
    )(idx_flat, idx_flat, hn_t, gates, x1_t, uv_tiles)


def _rope_tables(pos):
    half = ROPE_DIMS // 2
    inv = ROPE_THETA ** (-(jnp.arange(half, dtype=F32) * 2.0 / ROPE_DIMS))
    ang = pos.astype(F32)[:, None] * inv[None, :]
    cos, sin = jnp.cos(ang), jnp.sin(ang)
    n = pos.shape[0]
    a = jnp.concatenate([cos, cos, jnp.ones((n, HEAD_DIM - ROPE_DIMS), F32)], axis=-1)
    b = jnp.concatenate([-sin, sin, jnp.zeros((n, HEAD_DIM - ROPE_DIMS), F32)], axis=-1)
    return jnp.tile(a, (1, 2)), jnp.tile(b, (1, 2))


def _pack_bf16_pairs(w):
    half = w.shape[1] // 2
    bits = lax.bitcast_convert_type(w.astype(BF16), jnp.uint16).astype(jnp.uint32)
    return bits[:, :half] | (bits[:, half:] << 16)


def _tile2(v):
    return jnp.tile(v.reshape(1, HEAD_DIM), (1, 2))


def kernel(x, norm1_g, w_in, nsa_q_norm, nsa_k_norm, cmp_pe_k, cmp_w1_k, cmp_w2_k, cmp_pe_v, cmp_w1_v, cmp_w2_v,
           dil_q_norm, dil_k_norm, w_up_nsa, w_up_dil, w_o, norm2_g, peer_wq, peer_subkeys, peer_u, peer_v):
    B, S, D = x.shape
    T = B * S
    assert D == D_MODEL and S % (DIL_PATTERNS[-1][1] * DIL_BLOCK) == 0 and S >= WIN + NSA_QB and T % 512 == 0
    x2 = x.reshape(T, D)

    n_q, n_kv, n_gate, n_dil = 512, 768, 24, 2304
    o_gate = n_q + n_kv
    o_dil = o_gate + n_gate
    o_mg = o_dil + n_dil
    w_perm = jnp.concatenate([w_in[:, o_mg:], w_in[:, :o_gate], w_in[:, o_dil:o_mg], w_in[:, o_gate:o_dil],
                              jnp.zeros((D, IN_COLS_PAD - w_in.shape[1]), w_in.dtype)], axis=1).astype(BF16)
    z2 = _in_proj(x2, norm1_g.reshape(1, D), w_perm)
    z3 = z2.reshape(B, S, IN_COLS_PAD)

    blockdiag = jnp.asarray(np.kron(np.eye(2), np.ones((HEAD_DIM, HEAD_DIM))), BF16)
    rope_a, rope_b = _rope_tables(jnp.arange(S))

    qn, ksd, vsd, kwd, vwd = _nsa_prep(z3, rope_a, rope_b, blockdiag, _tile2(nsa_q_norm), _tile2(nsa_k_norm[1]),
                                       _tile2(nsa_k_norm[2]))

    n_cmp = (S - CMP_BLOCK) // CMP_STRIDE + 1
    ncp = S // CMP_STRIDE
    def flat_blocks(col):
        zc = z3[:, :, col:col + LANES].reshape(B, S, 2, HEAD_DIM).transpose(0, 2, 1, 3)
        r = zc.reshape(B, 2, ncp, CMP_STRIDE * HEAD_DIM)
        nxt = jnp.concatenate([r[:, :, 1:], jnp.zeros_like(r[:, :, :1])], axis=2)
        return jnp.concatenate([r, nxt], axis=-1).reshape(B * 2 * ncp, CMP_BLOCK * HEAD_DIM)
    cmp_a, cmp_b = _rope_tables(jnp.arange(ncp) * CMP_STRIDE + CMP_BLOCK - 1)
    dup2 = lambda w: jnp.concatenate([w, w], axis=1).astype(BF16)
    kcd, vcd = _compress(flat_blocks(COL_KV), flat_blocks(COL_KV + LANES),
                         cmp_pe_k.reshape(1, -1), cmp_pe_v.reshape(1, -1), cmp_w1_k.astype(BF16), cmp_w1_v.astype(BF16),
                         dup2(cmp_w2_k), dup2(cmp_w2_v), cmp_a, cmp_b, blockdiag, _tile2(nsa_k_norm[0]))
    kcd = kcd.reshape(B, 2, ncp, LANES)
    vcd = vcd.reshape(B, 2, ncp, LANES)

    n_slc = S // SLC_BLOCK
    s0 = np.arange(n_cmp) * CMP_STRIDE
    b0 = np.arange(n_slc) * SLC_BLOCK
    ov = np.clip(np.minimum(s0[:, None] + CMP_BLOCK, b0[None, :] + SLC_BLOCK) - np.maximum(s0[:, None], b0[None, :]),
                 0, None) / CMP_BLOCK
    ov_pad = np.zeros((ncp, LANES), np.float32)
    ov_pad[:n_cmp, :n_slc] = ov
    y_nsa = _nsa_attn(qn, kcd, vcd, ksd, vsd, kwd, vwd, z3, jnp.asarray(ov_pad, BF16))

    gq = jnp.tile(dil_q_norm.reshape(3, 1, HEAD_DIM), (1, 1, 2))
    gk = jnp.tile(dil_k_norm.reshape(3, 1, HEAD_DIM), (1, 1, 2))
    prep = _dil_prep(z3, rope_a, rope_b, blockdiag, gq, gk)
    flat = [p.reshape(B, S, 256) for p in prep]
    y_dil = _dil_attn(flat[0::3], flat[1::3], flat[2::3])

    x1, hn, pq = _merge(x2, y_nsa.reshape(T, 512), y_dil.reshape(T, 256), z2, w_up_nsa.astype(BF16),
                        w_up_dil.astype(BF16), w_o.astype(BF16), norm2_g.reshape(1, D), peer_wq.astype(BF16))

    sub = PEER_NKEYS // 2
    sk_pad = jnp.stack([jnp.pad(peer_subkeys[0], ((0, 0), (0, sub))), jnp.pad(peer_subkeys[1], ((0, 0), (sub, 0)))])
    idx_t, gate_t = _peer_route(pq, sk_pad)
    uv_tiles = jnp.concatenate([_pack_bf16_pairs(peer_u), _pack_bf16_pairs(peer_v)], axis=1)
    out = _peer_expert(idx_t.T.reshape(-1), hn.reshape(T, SUBLANES, LANES), gate_t.T, x1.reshape(T, SUBLANES, LANES),
                       uv_tiles.reshape(-1, SUBLANES, LANES))
    return out.reshape(B, S, D)
```

```python
import functools

import numpy as np
import jax
import jax.numpy as jnp
from jax import lax
from jax.experimental import pallas as pl
from jax.experimental.pallas import tpu as pltpu
from jax.experimental.pallas import tpu_sc as plsc

F32 = jnp.float32
BF16 = jnp.bfloat16
I32 = jnp.int32

D_MODEL = 1024
HEAD_DIM = 64
ROPE_DIMS = 16
ROPE_THETA = 500000.0
NORM_EPS = 1e-6
NEG_INF = -1e30
LANES = 128

NSA_HEADS = 8
CMP_BLOCK = 32
CMP_STRIDE = 16
CMP_HIDDEN = 256
SLC_BLOCK = 64
SLC_TOPN = 16
FORCE_SCORE = 1e3
WIN = 512
NSA_QB = 128
SEL_CHUNK = 512

DIL_PATTERNS = ((128, 1), (512, 4), (2048, 16))
DIL_BLOCK = 128
DIL_PREP_ROWS = 1024

PEER_HEADS = 8
PEER_NKEYS = 128
PEER_TOPK = 16
PEER_TOK = 16

COL_MG = 0
COL_Q = 2048
COL_KV = 2560
COL_DIL = 3328
COL_GATE = 5632
IN_COLS_PAD = 5760

_NT = (((1,), (1,)), ((), ()))


def _dot(a, b):
    return jnp.dot(a, b, preferred_element_type=F32)


def _dot_nt(a, b):
    return lax.dot_general(a, b, _NT, preferred_element_type=F32)


def _split_bf16(a):
    hi = a.astype(BF16)
    lo = (a - hi.astype(F32)).astype(BF16)
    return hi, lo


def _dot_hilo(a, b_bf16):
    hi, lo = _split_bf16(a)
    return _dot(hi, b_bf16) + _dot(lo, b_bf16)


def _head_norm_rope(zt, gain, rope_a, rope_b, blockdiag):
    ss = _dot_hilo(zt * zt, blockdiag)
    zn = zt * lax.rsqrt(ss * (1.0 / HEAD_DIM) + NORM_EPS) * gain
    d = lax.broadcasted_iota(I32, zn.shape, 1) & (HEAD_DIM - 1)
    half = ROPE_DIMS // 2
    partner = jnp.where(d < half, pltpu.roll(zn, LANES - half, 1), pltpu.roll(zn, half, 1))
    return zn * rope_a + partner * rope_b


def _half_masks():
    lane = lax.broadcasted_iota(I32, (1, LANES), 1)
    lo = (lane < HEAD_DIM).astype(BF16)
    return lo, (1 - lo).astype(BF16)


def _inproj_kernel(x_ref, g_ref, w_ref, o_ref, h_scr):
    @pl.when(pl.program_id(1) == 0)
    def _():
        xf = x_ref[...]
        ms = jnp.mean(xf * xf, axis=-1, keepdims=True)
        h_scr[...] = (xf * lax.rsqrt(ms + NORM_EPS) * g_ref[...]).astype(BF16)

    o_ref[...] = _dot(h_scr[...], w_ref[...])


def _in_proj(x2, g1, w_bf16):
    T = x2.shape[0]
    tm, tn = 512, 640
    return pl.pallas_call(
        _inproj_kernel,
        out_shape=jax.ShapeDtypeStruct((T, IN_COLS_PAD), F32),
        grid=(T // tm, IN_COLS_PAD // tn),
        in_specs=[
            pl.BlockSpec((tm, D_MODEL), lambda i, j: (i, 0)),
            pl.BlockSpec((1, D_MODEL), lambda i, j: (0, 0)),
            pl.BlockSpec((D_MODEL, tn), lambda i, j: (0, j)),
        ],
        out_specs=pl.BlockSpec((tm, tn), lambda i, j: (i, j)),
        scratch_shapes=[pltpu.VMEM((tm, D_MODEL), BF16)],
        compiler_params=pltpu.CompilerParams(dimension_semantics=("parallel", "arbitrary")),
        name="in_proj",
    )(x2, g1, w_bf16)


def _nsa_prep_kernel(zq_ref, zks_ref, zvs_ref, zkw_ref, zvw_ref, ra_ref, rb_ref, bd_ref, gq_ref, gks_ref, gkw_ref,
                     q_ref, ks_ref, vs_ref, kw_ref, vw_ref):
    ra, rb, bd = ra_ref[...], rb_ref[...], bd_ref[...]
    lane = lax.broadcasted_iota(I32, ra.shape, 1)
    scale = HEAD_DIM ** -0.5

    zq = zq_ref[0]
    tiles = [_head_norm_rope(zq[:, m * LANES:(m + 1) * LANES], gq_ref[...], ra, rb, bd) * scale for m in range(4)]
    q_ref[0] = jnp.concatenate(tiles, axis=-1).astype(BF16)

    def dup(t, out_ref):
        sw = pltpu.roll(t, HEAD_DIM, 1)
        out_ref[0, 0] = jnp.where(lane < HEAD_DIM, t, sw).astype(BF16)
        out_ref[0, 1] = jnp.where(lane < HEAD_DIM, sw, t).astype(BF16)

    dup(_head_norm_rope(zks_ref[0], gks_ref[...], ra, rb, bd), ks_ref)
    dup(_head_norm_rope(zkw_ref[0], gkw_ref[...], ra, rb, bd), kw_ref)
    dup(zvs_ref[0], vs_ref)
    dup(zvw_ref[0], vw_ref)


def _nsa_prep(z3, rope_a, rope_b, blockdiag, gq, gks, gkw):
    B, S, _ = z3.shape
    tm = 512
    kvb = COL_KV // LANES
    zcol = lambda c: pl.BlockSpec((1, tm, LANES), lambda b, i, c=c: (b, i, c))
    const = lambda shape: pl.BlockSpec(shape, lambda b, i: tuple(0 for _ in shape))
    kv_out = pl.BlockSpec((1, 2, tm, LANES), lambda b, i: (b, 0, i, 0))
    kv_shape = jax.ShapeDtypeStruct((B, 2, S, LANES), BF16)
    return pl.pallas_call(
        _nsa_prep_kernel,
        out_shape=(jax.ShapeDtypeStruct((B, S, 512), BF16), kv_shape, kv_shape, kv_shape, kv_shape),
        grid=(B, S // tm),
        in_specs=[
            pl.BlockSpec((1, tm, 512), lambda b, i: (b, i, COL_Q // 512)),
            zcol(kvb + 2), zcol(kvb + 3), zcol(kvb + 4), zcol(kvb + 5),
            pl.BlockSpec((tm, LANES), lambda b, i: (i, 0)),
            pl.BlockSpec((tm, LANES), lambda b, i: (i, 0)),
            const((LANES, LANES)), const((1, LANES)), const((1, LANES)), const((1, LANES)),
        ],
        out_specs=(pl.BlockSpec((1, tm, 512), lambda b, i: (b, i, 0)), kv_out, kv_out, kv_out, kv_out),
        compiler_params=pltpu.CompilerParams(dimension_semantics=("parallel", "parallel")),
        name="nsa_prep",
    )(z3, z3, z3, z3, z3, rope_a, rope_b, blockdiag, gq, gks, gkw)


def _compress_kernel(fk_ref, fv_ref, pek_ref, pev_ref, w1k_ref, w1v_ref, w2k_ref, w2v_ref, ra_ref, rb_ref, bd_ref, gk_ref,
                     k_ref, v_ref):
    def mlp(f_ref, pe_ref, w1_ref, w2_ref):
        f = (f_ref[...] + pe_ref[...]).astype(BF16)
        h = jax.nn.gelu(_dot(f, w1_ref[...]))
        return _dot(h.astype(BF16), w2_ref[...])

    kc = mlp(fk_ref, pek_ref, w1k_ref, w2k_ref)
    k_ref[...] = _head_norm_rope(kc, gk_ref[...], ra_ref[...], rb_ref[...], bd_ref[...]).astype(BF16)
    v_ref[...] = mlp(fv_ref, pev_ref, w1v_ref, w2v_ref).astype(BF16)


def _compress(flat_k, flat_v, pek, pev, w1k, w1v, w2k, w2v, rope_a, rope_b, blockdiag, gk):
    rows, width = flat_k.shape
    nblk = rope_a.shape[0]
    row = pl.BlockSpec((nblk, width), lambda i: (i, 0))
    const = lambda shape: pl.BlockSpec(shape, lambda i: tuple(0 for _ in shape))
    out = pl.BlockSpec((nblk, LANES), lambda i: (i, 0))
    shp = jax.ShapeDtypeStruct((rows, LANES), BF16)
    return pl.pallas_call(
        _compress_kernel,
        out_shape=(shp, shp),
        grid=(rows // nblk,),
        in_specs=[row, row, const((1, width)), const((1, width)), const((width, CMP_HIDDEN)), const((width, CMP_HIDDEN)),
                  const((CMP_HIDDEN, LANES)), const((CMP_HIDDEN, LANES)), const((nblk, LANES)), const((nblk, LANES)),
                  const((LANES, LANES)), const((1, LANES))],
        out_specs=(out, out),
        compiler_params=pltpu.CompilerParams(dimension_semantics=("parallel",)),
        name="compress",
    )(flat_k, flat_v, pek, pev, w1k, w1v, w2k, w2v, rope_a, rope_b, blockdiag, gk)


def _softmax_rows(s):
    m = jnp.max(s, axis=-1, keepdims=True)
    e = jnp.exp(s - m)
    return e / jnp.sum(e, axis=-1, keepdims=True)


def _nsa_kernel(q_ref, kc_ref, vc_ref, ks_ref, vs_ref, kw_ref, vw_ref, gl_ref, ov_ref, ex_ref, y_ref):
    n = pl.program_id(1)
    t0 = n * NSA_QB
    qt = q_ref[0]
    hm = _half_masks()
    lane = lax.broadcasted_iota(I32, (NSA_QB, LANES), 1)
    t1 = t0 + lax.broadcasted_iota(I32, (NSA_QB, 1), 0)
    t4 = t0 + (lax.broadcasted_iota(I32, (4 * NSA_QB, 1), 0) & (NSA_QB - 1))
    gates = jax.nn.sigmoid(gl_ref[0])

    n_slc_blocks = ks_ref.shape[2] // SLC_BLOCK
    blk = lax.broadcasted_iota(I32, (n_slc_blocks, NSA_QB), 0)
    tq = t0 + lax.broadcasted_iota(I32, (n_slc_blocks, NSA_QB), 1)
    cur = tq >> 6
    forced = (blk == 0) | (blk == cur) | (blk == cur - 1)
    causal_b = blk * SLC_BLOCK <= tq

    tiles_out = []
    for g in range(2):
        q4 = jnp.concatenate(
            [qt[:, (2 * g + jj // 2) * LANES:(2 * g + jj // 2 + 1) * LANES] * hm[jj % 2] for jj in range(4)], axis=0)

        sc = _dot_nt(q4, kc_ref[0, g])
        cend = lax.broadcasted_iota(I32, (1, sc.shape[1]), 1) * CMP_STRIDE + (CMP_BLOCK - 1)
        valid_c = cend <= t4
        pc = _softmax_rows(jnp.where(valid_c, sc, NEG_INF))
        pc = jnp.where(valid_c, pc, 0.0)
        o_c = _dot(pc.astype(BF16), vc_ref[0, g])
        psum = pc[0:NSA_QB] + pc[NSA_QB:2 * NSA_QB] + pc[2 * NSA_QB:3 * NSA_QB] + pc[3 * NSA_QB:]
        imp = _dot_hilo(psum, ov_ref[...])

        score = jnp.where(forced, FORCE_SCORE, jnp.where(causal_b, imp.T[:n_slc_blocks], -1.0))
        n_grp = n_slc_blocks // SUBLANES
        grp = [score[SUBLANES * r:SUBLANES * (r + 1)] for r in range(n_grp)]
        ranks = [jnp.zeros((SUBLANES, NSA_QB), F32) for _ in range(n_grp)]
        row = lax.broadcasted_iota(I32, (SUBLANES, NSA_QB), 0)
        for i in range(n_slc_blocks):
            ri = grp[i // SUBLANES][i % SUBLANES:i % SUBLANES + 1, :]
            for r in range(n_grp):
                if r > i // SUBLANES:
                    ahead = ri >= grp[r]
                elif r < i // SUBLANES:
                    ahead = ri > grp[r]
                else:
                    ahead = (ri > grp[r]) | ((ri == grp[r]) & (row > i % SUBLANES))
                ranks[r] = ranks[r] + jnp.where(ahead, 1.0, 0.0)
        sel64 = jnp.where(jnp.concatenate(ranks, axis=0) < float(SLC_TOPN), 1.0, 0.0)
        sel = jnp.concatenate([sel64, jnp.zeros((LANES - n_slc_blocks, NSA_QB), F32)], axis=0).T.astype(BF16)

        def sel_chunk(c, carry, last):
            m_i, l_i, acc = carry
            k0 = pl.multiple_of(c * SEL_CHUNK, SEL_CHUNK)
            kch = ks_ref[0, g, pl.ds(k0, SEL_CHUNK), :]
            vch = vs_ref[0, g, pl.ds(k0, SEL_CHUNK), :]
            picked = _dot(sel, ex_ref[:, pl.ds(k0, SEL_CHUNK)])
            if last:
                kpos = k0 + lax.broadcasted_iota(I32, (1, SEL_CHUNK), 1)
                picked = jnp.where(kpos <= t1, picked, 0.0)
            bias = (picked - 1.0) * (-NEG_INF)
            s = (_dot_nt(q4, kch).reshape(4, NSA_QB, SEL_CHUNK) + bias[None]).reshape(4 * NSA_QB, SEL_CHUNK)
            m_new = jnp.maximum(m_i, jnp.max(s, axis=-1, keepdims=True))
            alpha = jnp.exp(m_i - m_new)
            p = jnp.exp(s - m_new)
            l_new = alpha * l_i + jnp.sum(p, axis=-1, keepdims=True)
            acc_new = alpha * acc + _dot(p.astype(BF16), vch)
            return m_new, l_new, acc_new

        init = (jnp.full((4 * NSA_QB, 1), NEG_INF, F32), jnp.zeros((4 * NSA_QB, 1), F32),
                jnp.zeros((4 * NSA_QB, LANES), F32))
        n_full = t0 // SEL_CHUNK
        carry = lax.fori_loop(0, n_full, functools.partial(sel_chunk, last=False), init)
        _, l_s, acc_s = sel_chunk(n_full, carry, last=True)
        o_s = acc_s / l_s

        wlen = WIN + NSA_QB
        ws = pl.multiple_of(jnp.maximum(t0 - WIN, 0), NSA_QB)
        sw = _dot_nt(q4, kw_ref[0, g, pl.ds(ws, wlen), :])
        dist = t4 - (ws + lax.broadcasted_iota(I32, (1, wlen), 1))
        pw = _softmax_rows(jnp.where((dist >= 0) & (dist < WIN), sw, NEG_INF))
        o_w = _dot(pw.astype(BF16), vw_ref[0, g, pl.ds(ws, wlen), :])

        heads = []
        for jj in range(4):
            h = 4 * g + jj
            rows = slice(jj * NSA_QB, (jj + 1) * NSA_QB)
            heads.append(gates[:, 3 * h:3 * h + 1] * o_c[rows] + gates[:, 3 * h + 1:3 * h + 2] * o_s[rows]
                         + gates[:, 3 * h + 2:3 * h + 3] * o_w[rows])
        tiles_out.append(jnp.where(lane < HEAD_DIM, heads[0], heads[1]))
        tiles_out.append(jnp.where(lane < HEAD_DIM, heads[2], heads[3]))

    y_ref[0] = jnp.concatenate(tiles_out, axis=-1).astype(BF16)


def _nsa_attn(qn, kcd, vcd, ksd, vsd, kwd, vwd, z3, overlap):
    B, S, _ = qn.shape
    ncp = kcd.shape[2]
    assert S // SLC_BLOCK <= LANES
    expand = jnp.asarray(np.arange(LANES)[:, None] == (np.arange(S)[None, :] // SLC_BLOCK), BF16)
    full = lambda rows: pl.BlockSpec((1, 2, rows, LANES), lambda b, n: (b, 0, 0, 0))
    return pl.pallas_call(
        _nsa_kernel,
        out_shape=jax.ShapeDtypeStruct((B, S, 512), BF16),
        grid=(B, S // NSA_QB),
        in_specs=[
            pl.BlockSpec((1, NSA_QB, 512), lambda b, n: (b, n, 0)),
            full(ncp), full(ncp), full(S), full(S), full(S), full(S),
            pl.BlockSpec((1, NSA_QB, LANES), lambda b, n: (b, n, COL_GATE // LANES)),
            pl.BlockSpec((ncp, LANES), lambda b, n: (0, 0)),
            pl.BlockSpec((LANES, S), lambda b, n: (0, 0)),
        ],
        out_specs=pl.BlockSpec((1, NSA_QB, 512), lambda b, n: (b, n, 0)),
        compiler_params=pltpu.CompilerParams(dimension_semantics=("parallel", "arbitrary"),
                                             vmem_limit_bytes=48 * 1024 * 1024),
        name="nsa_attn",
    )(qn, kcd, vcd, ksd, vsd, kwd, vwd, z3, overlap, expand)


def _dil_prep_kernel(*refs):
    zs, (ra_ref, rb_ref, bd_ref, gq_ref, gk_ref), outs = refs[0:18], refs[18:23], refs[23:32]
    bd = bd_ref[...]
    scale = HEAD_DIM ** -0.5
    for g, (_, d) in enumerate(DIL_PATTERNS):
        n = DIL_PREP_ROWS // d
        for r in range(d):
            rows = pl.ds(r, n, stride=d) if d > 1 else pl.ds(0, n)
            ra, rb = ra_ref[rows, :], rb_ref[rows, :]
            for which in range(3):
                for m in range(2):
                    z = zs[2 * (3 * g + which) + m][0, rows, :]
                    if which == 0:
                        z = _head_norm_rope(z, gq_ref[g], ra, rb, bd) * scale
                    elif which == 1:
                        z = _head_norm_rope(z, gk_ref[g], ra, rb, bd)
                    outs[3 * g + which][0, r, :, m * LANES:(m + 1) * LANES] = z.astype(BF16)


def _dil_prep(z3, rope_a, rope_b, blockdiag, gq, gk):
    B, S, _ = z3.shape
    nsteps = S // DIL_PREP_ROWS
    c0 = COL_DIL // LANES
    in_specs = [pl.BlockSpec((1, DIL_PREP_ROWS, LANES), lambda b, c, k=k: (b, c, c0 + k)) for k in range(18)]
    in_specs += [
        pl.BlockSpec((DIL_PREP_ROWS, LANES), lambda b, c: (c, 0)),
        pl.BlockSpec((DIL_PREP_ROWS, LANES), lambda b, c: (c, 0)),
        pl.BlockSpec((LANES, LANES), lambda b, c: (0, 0)),
        pl.BlockSpec((3, 1, LANES), lambda b, c: (0, 0, 0)),
        pl.BlockSpec((3, 1, LANES), lambda b, c: (0, 0, 0)),
    ]
    out_shape, out_specs = [], []
    for _, d in DIL_PATTERNS:
        for _ in range(3):
            out_shape.append(jax.ShapeDtypeStruct((B, d, S // d, 256), BF16))
            out_specs.append(pl.BlockSpec((1, d, DIL_PREP_ROWS // d, 256), lambda b, c: (b, 0, c, 0)))
    return pl.pallas_call(
        _dil_prep_kernel,
        out_shape=tuple(out_shape),
        grid=(B, nsteps),
        in_specs=in_specs,
        out_specs=tuple(out_specs),
        compiler_params=pltpu.CompilerParams(dimension_semantics=("parallel", "parallel"),
                                             vmem_limit_bytes=48 * 1024 * 1024),
        name="dil_prep",
    )(*([z3] * 18), rope_a, rope_b, blockdiag, gq, gk)


def _dil_kernel(*refs, seq):
    q_refs, k_refs, v_refs, y_ref, o_scr, l_scr = refs[0:3], refs[3:6], refs[6:9], refs[9], refs[10], refs[11]
    hm = _half_masks()
    lane = lax.broadcasted_iota(I32, (DIL_BLOCK, LANES), 1)
    qi = lax.broadcasted_iota(I32, (2 * DIL_BLOCK, 2 * DIL_BLOCK), 0) & (DIL_BLOCK - 1)
    ki = lax.broadcasted_iota(I32, (2 * DIL_BLOCK, 2 * DIL_BLOCK), 1)
    causal = (ki - DIL_BLOCK) <= qi

    for g, (_, d) in enumerate(DIL_PATTERNS):
        nb = seq // d // DIL_BLOCK

        def body(u, carry, g=g, d=d, nb=nb):
            j = u % nb
            r = u // nb
            r0 = pl.multiple_of(u * DIL_BLOCK, DIL_BLOCK)
            p0 = pl.multiple_of(jnp.maximum(u - 1, 0) * DIL_BLOCK, DIL_BLOCK)
            q = q_refs[g][0, pl.ds(r0, DIL_BLOCK), :]
            kcat = jnp.concatenate([k_refs[g][0, pl.ds(p0, DIL_BLOCK), :], k_refs[g][0, pl.ds(r0, DIL_BLOCK), :]], axis=0)
            vcat = jnp.concatenate([v_refs[g][0, pl.ds(p0, DIL_BLOCK), :], v_refs[g][0, pl.ds(r0, DIL_BLOCK), :]], axis=0)
            q2 = jnp.concatenate([q * hm[0], q * hm[1]], axis=0)
            s = _dot_nt(q2, kcat)
            first_key = jnp.maximum(qi, jnp.where(j >= 1, 0, DIL_BLOCK))
            s = jnp.where(causal & (ki >= first_key), s, NEG_INF)
            m = jnp.max(s, axis=-1, keepdims=True)
            e = jnp.exp(s - m)
            den = jnp.sum(e, axis=-1, keepdims=True)
            o2 = _dot(e.astype(BF16), vcat) / den
            lse = m + jnp.log(den)
            o = jnp.where(lane < HEAD_DIM, o2[:DIL_BLOCK], o2[DIL_BLOCK:])
            lv = jnp.where(lane < HEAD_DIM, lse[:DIL_BLOCK], lse[DIL_BLOCK:])
            tok0 = j * (DIL_BLOCK * d) + r
            rows = pl.ds(tok0, DIL_BLOCK, stride=d) if d > 1 else pl.ds(pl.multiple_of(tok0, DIL_BLOCK), DIL_BLOCK)
            o_scr[g, rows, :] = o
            l_scr[g, rows, :] = lv
            return carry

        lax.fori_loop(0, seq // DIL_BLOCK, body, 0)

    def merge(c, carry):
        rows = pl.ds(pl.multiple_of(c * 512, 512), 512)
        ls = [l_scr[g, rows, :] for g in range(3)]
        mx = jnp.maximum(jnp.maximum(ls[0], ls[1]), ls[2])
        ws = [jnp.exp(l - mx) for l in ls]
        num = ws[0] * o_scr[0, rows, :] + ws[1] * o_scr[1, rows, :] + ws[2] * o_scr[2, rows, :]
        y_ref[0, rows, :] = (num / (ws[0] + ws[1] + ws[2])).astype(BF16)
        return carry

    lax.fori_loop(0, seq // 512, merge, 0)


def _dil_attn(dq, dk, dv):
    B, S, _ = dq[0].shape
    spec = pl.BlockSpec((1, S, LANES), lambda b, m: (b, 0, m))
    return pl.pallas_call(
        functools.partial(_dil_kernel, seq=S),
        out_shape=jax.ShapeDtypeStruct((B, S, 256), BF16),
        grid=(B, 2),
        in_specs=[spec] * 9,
        out_specs=spec,
        scratch_shapes=[pltpu.VMEM((3, S, LANES), F32), pltpu.VMEM((3, S, LANES), F32)],
        compiler_params=pltpu.CompilerParams(dimension_semantics=("parallel", "parallel"),
                                             vmem_limit_bytes=56 * 1024 * 1024),
        name="dil_attn",
    )(*dq, *dk, *dv)


def _merge_kernel(x_ref, yn_ref, yd_ref, mg0_ref, mg1_ref, wn_ref, wd_ref, wo_ref, g2_ref, wq_ref, x1_ref, hn_ref, pq_ref):
    u1 = _dot(yn_ref[...], wn_ref[...])
    u2 = _dot(yd_ref[...], wd_ref[...])
    merged = jax.nn.sigmoid(mg0_ref[...]) * u1 + jax.nn.sigmoid(mg1_ref[...]) * u2
    x1 = x_ref[...] + _dot(merged.astype(BF16), wo_ref[...])
    x1_ref[...] = x1
    ms = jnp.mean(x1 * x1, axis=-1, keepdims=True)
    hn = x1 * lax.rsqrt(ms + NORM_EPS) * g2_ref[...]
    hn_ref[...] = hn
    pq_ref[...] = _dot(hn.astype(BF16), wq_ref[...])


def _merge(x2, yn2, yd2, z2, wn, wd, wo, g2, wq):
    T = x2.shape[0]
    tm = 512
    row = lambda w, c=0: pl.BlockSpec((tm, w), lambda i, c=c: (i, c))
    const = lambda shape: pl.BlockSpec(shape, lambda i: (0, 0))
    shp = jax.ShapeDtypeStruct((T, D_MODEL), F32)
    return pl.pallas_call(
        _merge_kernel,
        out_shape=(shp, shp, shp),
        grid=(T // tm,),
        in_specs=[row(D_MODEL), row(512), row(256), row(D_MODEL, COL_MG // D_MODEL), row(D_MODEL, COL_MG // D_MODEL + 1),
                  const((512, D_MODEL)), const((256, D_MODEL)), const((D_MODEL, D_MODEL)), const((1, D_MODEL)),
                  const((D_MODEL, D_MODEL))],
        out_specs=(row(D_MODEL), row(D_MODEL), row(D_MODEL)),
        compiler_params=pltpu.CompilerParams(dimension_semantics=("parallel",), vmem_limit_bytes=48 * 1024 * 1024),
        name="merge",
    )(x2, yn2, yd2, z2, z2, wn, wd, wo, g2, wq)


def _top16(s, rank_id=None):
    if rank_id is None:
        rank_id = lax.broadcasted_iota(I32, (s.shape[0], 1), 0)
    big = jnp.iinfo(jnp.int32).max
    vals, ids = [], []
    for _ in range(PEER_TOPK):
        m = jnp.max(s, axis=0, keepdims=True)
        win = jnp.min(jnp.where(s == m, rank_id, big), axis=0, keepdims=True)
        vals.append(m)
        ids.append(win)
        s = jnp.where(rank_id == win, -jnp.inf, s)
    return vals, ids


def _route_kernel(q_ref, sk_ref, idx_ref, gate_ref):
    K = PEER_TOPK
    qh, ql = _split_bf16(q_ref[...])
    vals, ids = [], []
    for c in range(2):
        kh, kl = _split_bf16(sk_ref[c])
        s = _dot_nt(kh, qh) + _dot_nt(kh, ql) + _dot_nt(kl, qh)
        v, p = _top16(s)
        vals.append(v)
        ids.append(p)
    v0, p0 = jnp.concatenate(vals[0], axis=0), jnp.concatenate(ids[0], axis=0)
    v1, p1 = jnp.concatenate(vals[1], axis=0), jnp.concatenate(ids[1], axis=0)

    a8 = lax.broadcasted_iota(I32, (SUBLANES, 1), 0)
    pieces = [(v0 + vals[1][0], p0 * PEER_NKEYS + ids[1][0], lax.broadcasted_iota(I32, (K, 1), 0) * K)]
    for b in range(1, SUBLANES):
        keep = a8 < K // (b + 1)
        pieces.append((jnp.where(keep, v0[:SUBLANES] + vals[1][b], -jnp.inf), p0[:SUBLANES] * PEER_NKEYS + ids[1][b],
                       a8 * K + b))
    pieces.append((vals[0][0] + v1[SUBLANES:], ids[0][0] * PEER_NKEYS + p1[SUBLANES:], a8 + SUBLANES))
    cand = jnp.concatenate([p[0] for p in pieces], axis=0)
    eid = jnp.concatenate([p[1] for p in pieces], axis=0)
    flat = jnp.concatenate([p[2] for p in pieces], axis=0)
    v, win = _top16(cand, flat)
    sel_ids = [jnp.sum(jnp.where(flat == w, eid, 0), axis=0, keepdims=True) for w in win]
    sc = jnp.concatenate(v, axis=0)
    e = jnp.exp(sc - sc[0:1])
    gate_ref[...] = e / jnp.sum(e, axis=0, keepdims=True)
    idx_ref[...] = jnp.concatenate(sel_ids, axis=0)


def _peer_route(pq, sk_pad):
    T = pq.shape[0]
    tt = 256
    return pl.pallas_call(
        _route_kernel,
        out_shape=(jax.ShapeDtypeStruct((PEER_HEADS * PEER_TOPK, T), I32),
                   jax.ShapeDtypeStruct((PEER_HEADS * PEER_TOPK, T), F32)),
        grid=(T // tt, PEER_HEADS),
        in_specs=[pl.BlockSpec((tt, LANES), lambda i, h: (i, h)),
                  pl.BlockSpec((2, PEER_NKEYS, LANES), lambda i, h: (0, 0, 0))],
        out_specs=(pl.BlockSpec((PEER_TOPK, tt), lambda i, h: (h, i)),
                   pl.BlockSpec((PEER_TOPK, tt), lambda i, h: (h, i))),
        compiler_params=pltpu.CompilerParams(dimension_semantics=("parallel", "parallel")),
        name="peer_route",
    )(pq, sk_pad)


PEER_NK = PEER_HEADS * PEER_TOPK
SUBLANES = 8


def _unpack_pair(w):
    lo = pltpu.bitcast(w << 16, F32)
    hi = pltpu.bitcast(w & jnp.uint32(0xFFFF0000), F32)
    return lo, hi


def _peer_kernel(idx_cur, idx_nxt, hn_ref, gate_ref, x1_ref, uv_hbm, o_ref, buf_a, buf_b, sem):
    i = pl.program_id(0)
    last = pl.num_programs(0) - 1
    half_rows = SUBLANES // 2
    bufs = (buf_a, buf_b)

    def tile_copy(idx_ref, s, c, k):
        p = c * PEER_NK + k
        return pltpu.make_async_copy(uv_hbm.at[idx_ref[p]], bufs[s].at[pl.ds(p * SUBLANES, SUBLANES), :], sem.at[s, c])

    def fetch_token(idx_ref, s, c):
        for k in range(PEER_NK):
            tile_copy(idx_ref, s, c, k).start(priority=k % 2)

    def wait_token(idx_ref, s, c):
        for k in range(PEER_NK):
            tile_copy(idx_ref, s, c, k).wait()

    @pl.when(i == 0)
    def _():
        lax.fori_loop(0, PEER_TOK, lambda c, carry: (fetch_token(idx_cur, 0, c), carry)[1], 0)

    eye = lax.broadcasted_iota(I32, (PEER_NK, PEER_NK), 0) == lax.broadcasted_iota(I32, (PEER_NK, PEER_NK), 1)

    def token(slot, c):
        wait_token(idx_cur, slot, c)
        fetch_token(idx_nxt, 1 - slot, c)
        base = c * (PEER_NK * SUBLANES)
        plane = lambda s: bufs[slot][pl.ds(base + s, PEER_NK, stride=SUBLANES), :]
        xt = hn_ref[c]
        gcol = jnp.sum(jnp.where(eye, gate_ref[pl.ds(c, 1), :], 0.0), axis=-1, keepdims=True)
        acc = jnp.zeros((PEER_NK, LANES), F32)
        for s in range(half_rows):
            lo, hi = _unpack_pair(plane(s))
            acc = acc + lo * xt[s:s + 1, :] + hi * xt[s + half_rows:s + half_rows + 1, :]
        act = jax.nn.gelu(jnp.sum(acc, axis=-1, keepdims=True)) * gcol
        lo_rows, hi_rows = [], []
        for s in range(half_rows, SUBLANES):
            lo, hi = _unpack_pair(plane(s))
            lo_rows.append(jnp.sum(act * lo, axis=0, keepdims=True))
            hi_rows.append(jnp.sum(act * hi, axis=0, keepdims=True))
        o_ref[c] = x1_ref[c] + jnp.concatenate(lo_rows + hi_rows, axis=0)

    def step(slot):
        for c in range(PEER_TOK):
            token(slot, c)

        @pl.when(i == last)
        def _():
            lax.fori_loop(0, PEER_TOK, lambda c, carry: (wait_token(idx_nxt, 1 - slot, c), carry)[1], 0)

    pl.when(i % 2 == 0)(lambda: step(0))
    pl.when(i % 2 == 1)(lambda: step(1))


def _peer_expert(idx_flat, hn_t, gates, x1_t, uv_tiles):
    T = hn_t.shape[0]
    n = T // PEER_TOK
    tile = pl.BlockSpec((PEER_TOK, SUBLANES, LANES), lambda i: (i, 0, 0))
    fetch_buf = pltpu.VMEM((PEER_TOK * PEER_NK * SUBLANES, LANES), jnp.uint32)
    return pl.pallas_call(
        _peer_kernel,
        out_shape=jax.ShapeDtypeStruct((T, SUBLANES, LANES), F32),
        grid=(n,),
        in_specs=[pl.BlockSpec((PEER_TOK * PEER_NK,), lambda i: (i,), memory_space=pltpu.SMEM),
                  pl.BlockSpec((PEER_TOK * PEER_NK,), lambda i: (jnp.minimum(i + 1, n - 1),), memory_space=pltpu.SMEM),
                  tile, pl.BlockSpec((PEER_TOK, PEER_NK), lambda i: (i, 0)), tile,
                  pl.BlockSpec(memory_space=pl.ANY)],
        out_specs=tile,
        scratch_shapes=[fetch_buf, fetch_buf, pltpu.SemaphoreType.DMA((2, PEER_TOK))],
        compiler_params=pltpu.CompilerParams(dimension_semantics=("arbitrary",), vmem_limit_bytes=48 * 1024 * 1024),
        name="peer_expert",
    )(idx_flat, idx_flat, hn_t, gates, x1_t, uv_tiles)


SC_LANES = 16
SC_WORKERS = 32
SC_GATHER = 32


def _sc_lane_bcast(vec, k):
    idx = jnp.full((SC_LANES, 1), k, I32)
    dn = lax.GatherDimensionNumbers(offset_dims=(), collapsed_slice_dims=(0,), start_index_map=(0,))
    return lax.gather(vec, idx, dn, slice_sizes=(1,), mode=lax.GatherScatterMode.PROMISE_IN_BOUNDS)


def _sc_unpack_pair(w):
    return plsc.bitcast(w << 16, F32), plsc.bitcast(w & jnp.uint32(0xFFFF0000), F32)


def _peer_expert_sc(idx, gates, hn, x1, uv_rows):
    n = hn.shape[0]
    tokens_per_worker = n // SC_WORKERS
    n_gather = PEER_NK // SC_GATHER
    half = D_MODEL // 2
    n_chunk = half // SC_LANES
    mesh = plsc.VectorSubcoreMesh(core_axis_name="c", subcore_axis_name="s")

    def body(idx_hbm, g_hbm, hn_hbm, x1_hbm, uv_hbm, out_hbm, idx_v, g_v, x_v, o_v, rows_v, sem):
        worker = lax.axis_index("s") * 2 + lax.axis_index("c")
        lane = lax.iota(I32, SC_LANES)

        def token(ti, carry):
            t = worker * tokens_per_worker + ti
            pltpu.sync_copy(idx_hbm.at[t], idx_v)
            pltpu.sync_copy(g_hbm.at[t], g_v)
            pltpu.sync_copy(hn_hbm.at[t], x_v)
            pltpu.sync_copy(x1_hbm.at[t], o_v)
            for q in range(n_gather):
                pltpu.async_copy(uv_hbm.at[idx_v.at[q]], rows_v, sem).wait()

                def dot_chunk(j, accs):
                    off = pl.multiple_of(j * SC_LANES, SC_LANES)
                    xlo = x_v[pl.ds(off, SC_LANES)]
                    xhi = x_v[pl.ds(half + off, SC_LANES)]
                    out = []
                    for k in range(SC_GATHER):
                        lo, hi = _sc_unpack_pair(rows_v[k, pl.ds(off, SC_LANES)])
                        out.append(accs[k] + lo * xlo + hi * xhi)
                    return tuple(out)

                accs = lax.fori_loop(0, n_chunk, dot_chunk, tuple(jnp.zeros((SC_LANES,), F32) for _ in range(SC_GATHER)))
                acts = []
                for h in range(SC_GATHER // SC_LANES):
                    a = jnp.zeros((SC_LANES,), F32)
                    for kk in range(SC_LANES):
                        a = jnp.where(lane == kk, jnp.sum(accs[h * SC_LANES + kk]), a)
                    y = 0.7978845608028654 * (a + 0.044715 * a * a * a)
                    th = 1.0 - 2.0 / (jnp.exp(2.0 * y) + 1.0)
                    acts.append(0.5 * a * (1.0 + th) * g_v[pl.ds(q * SC_GATHER + h * SC_LANES, SC_LANES)])
                act_b = [_sc_lane_bcast(acts[k // SC_LANES], k % SC_LANES) for k in range(SC_GATHER)]

                def mix_chunk(j, carry2):
                    off = pl.multiple_of(j * SC_LANES, SC_LANES)
                    al = jnp.zeros((SC_LANES,), F32)
                    ah = jnp.zeros((SC_LANES,), F32)
                    for k in range(SC_GATHER):
                        lo, hi = _sc_unpack_pair(rows_v[k, pl.ds(half + off, SC_LANES)])
                        al = al + act_b[k] * lo
                        ah = ah + act_b[k] * hi
                    o_v[pl.ds(off, SC_LANES)] = o_v[pl.ds(off, SC_LANES)] + al
                    o_v[pl.ds(half + off, SC_LANES)] = o_v[pl.ds(half + off, SC_LANES)] + ah
                    return carry2

                lax.fori_loop(0, n_chunk, mix_chunk, 0)
            pltpu.sync_copy(o_v, out_hbm.at[t])
            return carry

        lax.fori_loop(0, tokens_per_worker, token, 0)

    return pl.kernel(
        body, mesh=mesh,
        out_type=jax.ShapeDtypeStruct((n, D_MODEL), F32),
        scratch_types=[pltpu.VMEM((n_gather, SC_GATHER), I32), pltpu.VMEM((PEER_NK,), F32), pltpu.VMEM((D_MODEL,), F32),
                       pltpu.VMEM((D_MODEL,), F32), pltpu.VMEM((SC_GATHER, D_MODEL), jnp.uint32), pltpu.SemaphoreType.DMA],
        compiler_params=pltpu.CompilerParams(needs_layout_passes=False),
        name="peer_expert_sc",
    )(idx, gates, hn, x1, uv_rows)


def _rope_tables(pos):
    half = ROPE_DIMS // 2
    inv = ROPE_THETA ** (-(jnp.arange(half, dtype=F32) * 2.0 / ROPE_DIMS))
    ang = pos.astype(F32)[:, None] * inv[None, :]
    cos, sin = jnp.cos(ang), jnp.sin(ang)
    n = pos.shape[0]
    a = jnp.concatenate([cos, cos, jnp.ones((n, HEAD_DIM - ROPE_DIMS), F32)], axis=-1)
    b = jnp.concatenate([-sin, sin, jnp.zeros((n, HEAD_DIM - ROPE_DIMS), F32)], axis=-1)
    return jnp.tile(a, (1, 2)), jnp.tile(b, (1, 2))


def _pack_bf16_pairs(w):
    half = w.shape[1] // 2
    bits = lax.bitcast_convert_type(w.astype(BF16), jnp.uint16).astype(jnp.uint32)
    return bits[:, :half] | (bits[:, half:] << 16)


def _tile2(v):
    return jnp.tile(v.reshape(1, HEAD_DIM), (1, 2))


def kernel(x, norm1_g, w_in, nsa_q_norm, nsa_k_norm, cmp_pe_k, cmp_w1_k, cmp_w2_k, cmp_pe_v, cmp_w1_v, cmp_w2_v,
           dil_q_norm, dil_k_norm, w_up_nsa, w_up_dil, w_o, norm2_g, peer_wq, peer_subkeys, peer_u, peer_v):
    B, S, D = x.shape
    T = B * S
    assert D == D_MODEL and S % (DIL_PATTERNS[-1][1] * DIL_BLOCK) == 0 and S >= WIN + NSA_QB and T % 512 == 0
    x2 = x.reshape(T, D)

    n_q, n_kv, n_gate, n_dil = 512, 768, 24, 2304
    o_gate = n_q + n_kv
    o_dil = o_gate + n_gate
    o_mg = o_dil + n_dil
    w_perm = jnp.concatenate([w_in[:, o_mg:], w_in[:, :o_gate], w_in[:, o_dil:o_mg], w_in[:, o_gate:o_dil],
                              jnp.zeros((D, IN_COLS_PAD - w_in.shape[1]), w_in.dtype)], axis=1).astype(BF16)
    z2 = _in_proj(x2, norm1_g.reshape(1, D), w_perm)
    z3 = z2.reshape(B, S, IN_COLS_PAD)

    blockdiag = jnp.asarray(np.kron(np.eye(2), np.ones((HEAD_DIM, HEAD_DIM))), BF16)
    rope_a, rope_b = _rope_tables(jnp.arange(S))

    qn, ksd, vsd, kwd, vwd = _nsa_prep(z3, rope_a, rope_b, blockdiag, _tile2(nsa_q_norm), _tile2(nsa_k_norm[1]),
                                       _tile2(nsa_k_norm[2]))

    n_cmp = (S - CMP_BLOCK) // CMP_STRIDE + 1
    ncp = S // CMP_STRIDE
    def flat_blocks(col):
        zc = z3[:, :, col:col + LANES].reshape(B, S, 2, HEAD_DIM).transpose(0, 2, 1, 3)
        r = zc.reshape(B, 2, ncp, CMP_STRIDE * HEAD_DIM)
        nxt = jnp.concatenate([r[:, :, 1:], jnp.zeros_like(r[:, :, :1])], axis=2)
        return jnp.concatenate([r, nxt], axis=-1).reshape(B * 2 * ncp, CMP_BLOCK * HEAD_DIM)
    cmp_a, cmp_b = _rope_tables(jnp.arange(ncp) * CMP_STRIDE + CMP_BLOCK - 1)
    dup2 = lambda w: jnp.concatenate([w, w], axis=1).astype(BF16)
    kcd, vcd = _compress(flat_blocks(COL_KV), flat_blocks(COL_KV + LANES),
                         cmp_pe_k.reshape(1, -1), cmp_pe_v.reshape(1, -1), cmp_w1_k.astype(BF16), cmp_w1_v.astype(BF16),
                         dup2(cmp_w2_k), dup2(cmp_w2_v), cmp_a, cmp_b, blockdiag, _tile2(nsa_k_norm[0]))
    kcd = kcd.reshape(B, 2, ncp, LANES)
    vcd = vcd.reshape(B, 2, ncp, LANES)

    n_slc = S // SLC_BLOCK
    s0 = np.arange(n_cmp) * CMP_STRIDE
    b0 = np.arange(n_slc) * SLC_BLOCK
    ov = np.clip(np.minimum(s0[:, None] + CMP_BLOCK, b0[None, :] + SLC_BLOCK) - np.maximum(s0[:, None], b0[None, :]),
                 0, None) / CMP_BLOCK
    ov_pad = np.zeros((ncp, LANES), np.float32)
    ov_pad[:n_cmp, :n_slc] = ov
    y_nsa = _nsa_attn(qn, kcd, vcd, ksd, vsd, kwd, vwd, z3, jnp.asarray(ov_pad, BF16))

    gq = jnp.tile(dil_q_norm.reshape(3, 1, HEAD_DIM), (1, 1, 2))
    gk = jnp.tile(dil_k_norm.reshape(3, 1, HEAD_DIM), (1, 1, 2))
    prep = _dil_prep(z3, rope_a, rope_b, blockdiag, gq, gk)
    flat = [p.reshape(B, S, 256) for p in prep]
    y_dil = _dil_attn(flat[0::3], flat[1::3], flat[2::3])

    x1, hn, pq = _merge(x2, y_nsa.reshape(T, 512), y_dil.reshape(T, 256), z2, w_up_nsa.astype(BF16),
                        w_up_dil.astype(BF16), w_o.astype(BF16), norm2_g.reshape(1, D), peer_wq.astype(BF16))

    sub = PEER_NKEYS // 2
    sk_pad = jnp.stack([jnp.pad(peer_subkeys[0], ((0, 0), (0, sub))), jnp.pad(peer_subkeys[1], ((0, 0), (sub, 0)))])
    idx_t, gate_t = _peer_route(pq, sk_pad)
    uv_rows = jnp.concatenate([_pack_bf16_pairs(peer_u), _pack_bf16_pairs(peer_v)], axis=1)
    idx, gates = idx_t.T, gate_t.T
    t_tc = T - (T // 4) // SC_WORKERS * SC_WORKERS
    out_tc = _peer_expert(idx[:t_tc].reshape(-1), hn[:t_tc].reshape(t_tc, SUBLANES, LANES), gates[:t_tc],
                          x1[:t_tc].reshape(t_tc, SUBLANES, LANES), uv_rows.reshape(-1, SUBLANES, LANES))
    out_sc = _peer_expert_sc(idx[t_tc:].reshape(T - t_tc, PEER_NK // SC_GATHER, SC_GATHER), gates[t_tc:], hn[t_tc:],
                             x1[t_tc:], uv_rows)
    return jnp.concatenate([out_tc.reshape(t_tc, D), out_sc], axis=0).reshape(B, S, D)
```

```python
import functools

import numpy as np
import jax
import jax.numpy as jnp
from jax import lax
from jax.experimental import pallas as pl
from jax.experimental.pallas import tpu as pltpu
from jax.experimental.pallas import tpu_sc as plsc

F32 = jnp.float32
BF16 = jnp.bfloat16
I32 = jnp.int32

D_MODEL = 1024
HEAD_DIM = 64
ROPE_DIMS = 16
ROPE_THETA = 500000.0
NORM_EPS = 1e-6
NEG_INF = -1e30
LANES = 128

NSA_HEADS = 8
CMP_BLOCK = 32
CMP_STRIDE = 16
CMP_HIDDEN = 256
SLC_BLOCK = 64
SLC_TOPN = 16
FORCE_SCORE = 1e3
WIN = 512
NSA_QB = 128
SEL_CHUNK = 512

DIL_PATTERNS = ((128, 1), (512, 4), (2048, 16))
DIL_BLOCK = 128
DIL_PREP_ROWS = 1024

PEER_HEADS = 8
PEER_NKEYS = 128
PEER_TOPK = 16
PEER_TOK = 16

COL_MG = 0
COL_Q = 2048
COL_KV = 2560
COL_DIL = 3328
COL_GATE = 5632
IN_COLS_PAD = 5760

_NT = (((1,), (1,)), ((), ()))


def _dot(a, b):
    return jnp.dot(a, b, preferred_element_type=F32)


def _dot_nt(a, b):
    return lax.dot_general(a, b, _NT, preferred_element_type=F32)


def _split_bf16(a):
    hi = a.astype(BF16)
    lo = (a - hi.astype(F32)).astype(BF16)
    return hi, lo


def _dot_hilo(a, b_bf16):
    hi, lo = _split_bf16(a)
    return _dot(hi, b_bf16) + _dot(lo, b_bf16)


def _head_norm_rope(zt, gain, rope_a, rope_b, blockdiag):
    ss = _dot_hilo(zt * zt, blockdiag)
    zn = zt * lax.rsqrt(ss * (1.0 / HEAD_DIM) + NORM_EPS) * gain
    d = lax.broadcasted_iota(I32, zn.shape, 1) & (HEAD_DIM - 1)
    half = ROPE_DIMS // 2
    partner = jnp.where(d < half, pltpu.roll(zn, LANES - half, 1), pltpu.roll(zn, half, 1))
    return zn * rope_a + partner * rope_b


def _half_masks():
    lane = lax.broadcasted_iota(I32, (1, LANES), 1)
    lo = (lane < HEAD_DIM).astype(BF16)
    return lo, (1 - lo).astype(BF16)


def _inproj_kernel(x_ref, g_ref, w_ref, o_ref, h_scr):
    @pl.when(pl.program_id(1) == 0)
    def _():
        xf = x_ref[...]
        ms = jnp.mean(xf * xf, axis=-1, keepdims=True)
        h_scr[...] = (xf * lax.rsqrt(ms + NORM_EPS) * g_ref[...]).astype(BF16)

    o_ref[...] = _dot(h_scr[...], w_ref[...])


def _in_proj(x2, g1, w_bf16):
    T = x2.shape[0]
    tm, tn = 512, 640
    return pl.pallas_call(
        _inproj_kernel,
        out_shape=jax.ShapeDtypeStruct((T, IN_COLS_PAD), F32),
        grid=(T // tm, IN_COLS_PAD // tn),
        in_specs=[
            pl.BlockSpec((tm, D_MODEL), lambda i, j: (i, 0)),
            pl.BlockSpec((1, D_MODEL), lambda i, j: (0, 0)),
            pl.BlockSpec((D_MODEL, tn), lambda i, j: (0, j)),
        ],
        out_specs=pl.BlockSpec((tm, tn), lambda i, j: (i, j)),
        scratch_shapes=[pltpu.VMEM((tm, D_MODEL), BF16)],
        compiler_params=pltpu.CompilerParams(dimension_semantics=("parallel", "arbitrary")),
        name="in_proj",
    )(x2, g1, w_bf16)


def _nsa_prep_kernel(zq_ref, zks_ref, zvs_ref, zkw_ref, zvw_ref, ra_ref, rb_ref, bd_ref, gq_ref, gks_ref, gkw_ref,
                     q_ref, ks_ref, vs_ref, kw_ref, vw_ref):
    ra, rb, bd = ra_ref[...], rb_ref[...], bd_ref[...]
    lane = lax.broadcasted_iota(I32, ra.shape, 1)
    scale = HEAD_DIM ** -0.5

    zq = zq_ref[0]
    tiles = [_head_norm_rope(zq[:, m * LANES:(m + 1) * LANES], gq_ref[...], ra, rb, bd) * scale for m in range(4)]
    q_ref[0] = jnp.concatenate(tiles, axis=-1).astype(BF16)

    def dup(t, out_ref):
        sw = pltpu.roll(t, HEAD_DIM, 1)
        out_ref[0, 0] = jnp.where(lane < HEAD_DIM, t, sw).astype(BF16)
        out_ref[0, 1] = jnp.where(lane < HEAD_DIM, sw, t).astype(BF16)

    dup(_head_norm_rope(zks_ref[0], gks_ref[...], ra, rb, bd), ks_ref)
    dup(_head_norm_rope(zkw_ref[0], gkw_ref[...], ra, rb, bd), kw_ref)
    dup(zvs_ref[0], vs_ref)
    dup(zvw_ref[0], vw_ref)


def _nsa_prep(z3, rope_a, rope_b, blockdiag, gq, gks, gkw):
    B, S, _ = z3.shape
    tm = 512
    kvb = COL_KV // LANES
    zcol = lambda c: pl.BlockSpec((1, tm, LANES), lambda b, i, c=c: (b, i, c))
    const = lambda shape: pl.BlockSpec(shape, lambda b, i: tuple(0 for _ in shape))
    kv_out = pl.BlockSpec((1, 2, tm, LANES), lambda b, i: (b, 0, i, 0))
    kv_shape = jax.ShapeDtypeStruct((B, 2, S, LANES), BF16)
    return pl.pallas_call(
        _nsa_prep_kernel,
        out_shape=(jax.ShapeDtypeStruct((B, S, 512), BF16), kv_shape, kv_shape, kv_shape, kv_shape),
        grid=(B, S // tm),
        in_specs=[
            pl.BlockSpec((1, tm, 512), lambda b, i: (b, i, COL_Q // 512)),
            zcol(kvb + 2), zcol(kvb + 3), zcol(kvb + 4), zcol(kvb + 5),
            pl.BlockSpec((tm, LANES), lambda b, i: (i, 0)),
            pl.BlockSpec((tm, LANES), lambda b, i: (i, 0)),
            const((LANES, LANES)), const((1, LANES)), const((1, LANES)), const((1, LANES)),
        ],
        out_specs=(pl.BlockSpec((1, tm, 512), lambda b, i: (b, i, 0)), kv_out, kv_out, kv_out, kv_out),
        compiler_params=pltpu.CompilerParams(dimension_semantics=("parallel", "parallel")),
        name="nsa_prep",
    )(z3, z3, z3, z3, z3, rope_a, rope_b, blockdiag, gq, gks, gkw)


def _compress_kernel(fk_ref, fv_ref, pek_ref, pev_ref, w1k_ref, w1v_ref, w2k_ref, w2v_ref, ra_ref, rb_ref, bd_ref, gk_ref,
                     k_ref, v_ref):
    def mlp(f_ref, pe_ref, w1_ref, w2_ref):
        f = (f_ref[...] + pe_ref[...]).astype(BF16)
        h = jax.nn.gelu(_dot(f, w1_ref[...]))
        return _dot(h.astype(BF16), w2_ref[...])

    kc = mlp(fk_ref, pek_ref, w1k_ref, w2k_ref)
    k_ref[...] = _head_norm_rope(kc, gk_ref[...], ra_ref[...], rb_ref[...], bd_ref[...]).astype(BF16)
    v_ref[...] = mlp(fv_ref, pev_ref, w1v_ref, w2v_ref).astype(BF16)


def _compress(flat_k, flat_v, pek, pev, w1k, w1v, w2k, w2v, rope_a, rope_b, blockdiag, gk):
    rows, width = flat_k.shape
    nblk = rope_a.shape[0]
    row = pl.BlockSpec((nblk, width), lambda i: (i, 0))
    const = lambda shape: pl.BlockSpec(shape, lambda i: tuple(0 for _ in shape))
    out = pl.BlockSpec((nblk, LANES), lambda i: (i, 0))
    shp = jax.ShapeDtypeStruct((rows, LANES), BF16)
    return pl.pallas_call(
        _compress_kernel,
        out_shape=(shp, shp),
        grid=(rows // nblk,),
        in_specs=[row, row, const((1, width)), const((1, width)), const((width, CMP_HIDDEN)), const((width, CMP_HIDDEN)),
                  const((CMP_HIDDEN, LANES)), const((CMP_HIDDEN, LANES)), const((nblk, LANES)), const((nblk, LANES)),
                  const((LANES, LANES)), const((1, LANES))],
        out_specs=(out, out),
        compiler_params=pltpu.CompilerParams(dimension_semantics=("parallel",)),
        name="compress",
    )(flat_k, flat_v, pek, pev, w1k, w1v, w2k, w2v, rope_a, rope_b, blockdiag, gk)


def _softmax_rows(s):
    m = jnp.max(s, axis=-1, keepdims=True)
    e = jnp.exp(s - m)
    return e / jnp.sum(e, axis=-1, keepdims=True)


def _nsa_kernel(q_ref, kc_ref, vc_ref, ks_ref, vs_ref, kw_ref, vw_ref, gl_ref, ov_ref, ex_ref, y_ref):
    n = pl.program_id(1)
    t0 = n * NSA_QB
    qt = q_ref[0]
    hm = _half_masks()
    lane = lax.broadcasted_iota(I32, (NSA_QB, LANES), 1)
    t1 = t0 + lax.broadcasted_iota(I32, (NSA_QB, 1), 0)
    t4 = t0 + (lax.broadcasted_iota(I32, (4 * NSA_QB, 1), 0) & (NSA_QB - 1))
    gates = jax.nn.sigmoid(gl_ref[0])

    n_slc_blocks = ks_ref.shape[2] // SLC_BLOCK
    blk = lax.broadcasted_iota(I32, (n_slc_blocks, NSA_QB), 0)
    tq = t0 + lax.broadcasted_iota(I32, (n_slc_blocks, NSA_QB), 1)
    cur = tq >> 6
    forced = (blk == 0) | (blk == cur) | (blk == cur - 1)
    causal_b = blk * SLC_BLOCK <= tq

    tiles_out = []
    for g in range(2):
        q4 = jnp.concatenate(
            [qt[:, (2 * g + jj // 2) * LANES:(2 * g + jj // 2 + 1) * LANES] * hm[jj % 2] for jj in range(4)], axis=0)

        sc = _dot_nt(q4, kc_ref[0, g])
        cend = lax.broadcasted_iota(I32, (1, sc.shape[1]), 1) * CMP_STRIDE + (CMP_BLOCK - 1)
        valid_c = cend <= t4
        pc = _softmax_rows(jnp.where(valid_c, sc, NEG_INF))
        pc = jnp.where(valid_c, pc, 0.0)
        o_c = _dot(pc.astype(BF16), vc_ref[0, g])
        psum = pc[0:NSA_QB] + pc[NSA_QB:2 * NSA_QB] + pc[2 * NSA_QB:3 * NSA_QB] + pc[3 * NSA_QB:]
        imp = _dot_hilo(psum, ov_ref[...])

        score = jnp.where(forced, FORCE_SCORE, jnp.where(causal_b, imp.T[:n_slc_blocks], -1.0))
        n_grp = n_slc_blocks // SUBLANES
        grp = [score[SUBLANES * r:SUBLANES * (r + 1)] for r in range(n_grp)]
        ranks = [jnp.zeros((SUBLANES, NSA_QB), F32) for _ in range(n_grp)]
        row = lax.broadcasted_iota(I32, (SUBLANES, NSA_QB), 0)
        for i in range(n_slc_blocks):
            ri = grp[i // SUBLANES][i % SUBLANES:i % SUBLANES + 1, :]
            for r in range(n_grp):
                if r > i // SUBLANES:
                    ahead = ri >= grp[r]
                elif r < i // SUBLANES:
                    ahead = ri > grp[r]
                else:
                    ahead = (ri > grp[r]) | ((ri == grp[r]) & (row > i % SUBLANES))
                ranks[r] = ranks[r] + jnp.where(ahead, 1.0, 0.0)
        sel64 = jnp.where(jnp.concatenate(ranks, axis=0) < float(SLC_TOPN), 1.0, 0.0)
        sel = jnp.concatenate([sel64, jnp.zeros((LANES - n_slc_blocks, NSA_QB), F32)], axis=0).T.astype(BF16)

        def sel_chunk(c, carry, last):
            m_i, l_i, acc = carry
            k0 = pl.multiple_of(c * SEL_CHUNK, SEL_CHUNK)
            kch = ks_ref[0, g, pl.ds(k0, SEL_CHUNK), :]
            vch = vs_ref[0, g, pl.ds(k0, SEL_CHUNK), :]
            picked = _dot(sel, ex_ref[:, pl.ds(k0, SEL_CHUNK)])
            if last:
                kpos = k0 + lax.broadcasted_iota(I32, (1, SEL_CHUNK), 1)
                picked = jnp.where(kpos <= t1, picked, 0.0)
            bias = (picked - 1.0) * (-NEG_INF)
            s = (_dot_nt(q4, kch).reshape(4, NSA_QB, SEL_CHUNK) + bias[None]).reshape(4 * NSA_QB, SEL_CHUNK)
            m_new = jnp.maximum(m_i, jnp.max(s, axis=-1, keepdims=True))
            alpha = jnp.exp(m_i - m_new)
            p = jnp.exp(s - m_new)
            l_new = alpha * l_i + jnp.sum(p, axis=-1, keepdims=True)
            acc_new = alpha * acc + _dot(p.astype(BF16), vch)
            return m_new, l_new, acc_new

        init = (jnp.full((4 * NSA_QB, 1), NEG_INF, F32), jnp.zeros((4 * NSA_QB, 1), F32),
                jnp.zeros((4 * NSA_QB, LANES), F32))
        n_full = t0 // SEL_CHUNK
        carry = lax.fori_loop(0, n_full, functools.partial(sel_chunk, last=False), init)
        _, l_s, acc_s = sel_chunk(n_full, carry, last=True)
        o_s = acc_s / l_s

        wlen = WIN + NSA_QB
        ws = pl.multiple_of(jnp.maximum(t0 - WIN, 0), NSA_QB)
        sw = _dot_nt(q4, kw_ref[0, g, pl.ds(ws, wlen), :])
        dist = t4 - (ws + lax.broadcasted_iota(I32, (1, wlen), 1))
        pw = _softmax_rows(jnp.where((dist >= 0) & (dist < WIN), sw, NEG_INF))
        o_w = _dot(pw.astype(BF16), vw_ref[0, g, pl.ds(ws, wlen), :])

        heads = []
        for jj in range(4):
            h = 4 * g + jj
            rows = slice(jj * NSA_QB, (jj + 1) * NSA_QB)
            heads.append(gates[:, 3 * h:3 * h + 1] * o_c[rows] + gates[:, 3 * h + 1:3 * h + 2] * o_s[rows]
                         + gates[:, 3 * h + 2:3 * h + 3] * o_w[rows])
        tiles_out.append(jnp.where(lane < HEAD_DIM, heads[0], heads[1]))
        tiles_out.append(jnp.where(lane < HEAD_DIM, heads[2], heads[3]))

    y_ref[0] = jnp.concatenate(tiles_out, axis=-1).astype(BF16)


def _nsa_attn(qn, kcd, vcd, ksd, vsd, kwd, vwd, z3, overlap):
    B, S, _ = qn.shape
    ncp = kcd.shape[2]
    assert S // SLC_BLOCK <= LANES
    expand = jnp.asarray(np.arange(LANES)[:, None] == (np.arange(S)[None, :] // SLC_BLOCK), BF16)
    full = lambda rows: pl.BlockSpec((1, 2, rows, LANES), lambda b, n: (b, 0, 0, 0))
    return pl.pallas_call(
        _nsa_kernel,
        out_shape=jax.ShapeDtypeStruct((B, S, 512), BF16),
        grid=(B, S // NSA_QB),
        in_specs=[
            pl.BlockSpec((1, NSA_QB, 512), lambda b, n: (b, n, 0)),
            full(ncp), full(ncp), full(S), full(S), full(S), full(S),
            pl.BlockSpec((1, NSA_QB, LANES), lambda b, n: (b, n, COL_GATE // LANES)),
            pl.BlockSpec((ncp, LANES), lambda b, n: (0, 0)),
            pl.BlockSpec((LANES, S), lambda b, n: (0, 0)),
        ],
        out_specs=pl.BlockSpec((1, NSA_QB, 512), lambda b, n: (b, n, 0)),
        compiler_params=pltpu.CompilerParams(dimension_semantics=("parallel", "arbitrary"),
                                             vmem_limit_bytes=48 * 1024 * 1024),
        name="nsa_attn",
    )(qn, kcd, vcd, ksd, vsd, kwd, vwd, z3, overlap, expand)


def _dil_prep_kernel(*refs):
    zs, (ra_ref, rb_ref, bd_ref, gq_ref, gk_ref), outs = refs[0:18], refs[18:23], refs[23:32]
    bd = bd_ref[...]
    scale = HEAD_DIM ** -0.5
    for g, (_, d) in enumerate(DIL_PATTERNS):
        n = DIL_PREP_ROWS // d
        for r in range(d):
            rows = pl.ds(r, n, stride=d) if d > 1 else pl.ds(0, n)
            ra, rb = ra_ref[rows, :], rb_ref[rows, :]
            for which in range(3):
                for m in range(2):
                    z = zs[2 * (3 * g + which) + m][0, rows, :]
                    if which == 0:
                        z = _head_norm_rope(z, gq_ref[g], ra, rb, bd) * scale
                    elif which == 1:
                        z = _head_norm_rope(z, gk_ref[g], ra, rb, bd)
                    outs[3 * g + which][0, r, :, m * LANES:(m + 1) * LANES] = z.astype(BF16)


def _dil_prep(z3, rope_a, rope_b, blockdiag, gq, gk):
    B, S, _ = z3.shape
    nsteps = S // DIL_PREP_ROWS
    c0 = COL_DIL // LANES
    in_specs = [pl.BlockSpec((1, DIL_PREP_ROWS, LANES), lambda b, c, k=k: (b, c, c0 + k)) for k in range(18)]
    in_specs += [
        pl.BlockSpec((DIL_PREP_ROWS, LANES), lambda b, c: (c, 0)),
        pl.BlockSpec((DIL_PREP_ROWS, LANES), lambda b, c: (c, 0)),
        pl.BlockSpec((LANES, LANES), lambda b, c: (0, 0)),
        pl.BlockSpec((3, 1, LANES), lambda b, c: (0, 0, 0)),
        pl.BlockSpec((3, 1, LANES), lambda b, c: (0, 0, 0)),
    ]
    out_shape, out_specs = [], []
    for _, d in DIL_PATTERNS:
        for _ in range(3):
            out_shape.append(jax.ShapeDtypeStruct((B, d, S // d, 256), BF16))
            out_specs.append(pl.BlockSpec((1, d, DIL_PREP_ROWS // d, 256), lambda b, c: (b, 0, c, 0)))
    return pl.pallas_call(
        _dil_prep_kernel,
        out_shape=tuple(out_shape),
        grid=(B, nsteps),
        in_specs=in_specs,
        out_specs=tuple(out_specs),
        compiler_params=pltpu.CompilerParams(dimension_semantics=("parallel", "parallel"),
                                             vmem_limit_bytes=48 * 1024 * 1024),
        name="dil_prep",
    )(*([z3] * 18), rope_a, rope_b, blockdiag, gq, gk)


def _dil_kernel(*refs, seq):
    q_refs, k_refs, v_refs, y_ref, o_scr, l_scr = refs[0:3], refs[3:6], refs[6:9], refs[9], refs[10], refs[11]
    hm = _half_masks()
    lane = lax.broadcasted_iota(I32, (DIL_BLOCK, LANES), 1)
    qi = lax.broadcasted_iota(I32, (2 * DIL_BLOCK, 2 * DIL_BLOCK), 0) & (DIL_BLOCK - 1)
    ki = lax.broadcasted_iota(I32, (2 * DIL_BLOCK, 2 * DIL_BLOCK), 1)
    causal = (ki - DIL_BLOCK) <= qi

    for g, (_, d) in enumerate(DIL_PATTERNS):
        nb = seq // d // DIL_BLOCK

        def body(u, carry, g=g, d=d, nb=nb):
            j = u % nb
            r = u // nb
            r0 = pl.multiple_of(u * DIL_BLOCK, DIL_BLOCK)
            p0 = pl.multiple_of(jnp.maximum(u - 1, 0) * DIL_BLOCK, DIL_BLOCK)
            q = q_refs[g][0, pl.ds(r0, DIL_BLOCK), :]
            kcat = jnp.concatenate([k_refs[g][0, pl.ds(p0, DIL_BLOCK), :], k_refs[g][0, pl.ds(r0, DIL_BLOCK), :]], axis=0)
            vcat = jnp.concatenate([v_refs[g][0, pl.ds(p0, DIL_BLOCK), :], v_refs[g][0, pl.ds(r0, DIL_BLOCK), :]], axis=0)
            q2 = jnp.concatenate([q * hm[0], q * hm[1]], axis=0)
            s = _dot_nt(q2, kcat)
            first_key = jnp.maximum(qi, jnp.where(j >= 1, 0, DIL_BLOCK))
            s = jnp.where(causal & (ki >= first_key), s, NEG_INF)
            m = jnp.max(s, axis=-1, keepdims=True)
            e = jnp.exp(s - m)
            den = jnp.sum(e, axis=-1, keepdims=True)
            o2 = _dot(e.astype(BF16), vcat) / den
            lse = m + jnp.log(den)
            o = jnp.where(lane < HEAD_DIM, o2[:DIL_BLOCK], o2[DIL_BLOCK:])
            lv = jnp.where(lane < HEAD_DIM, lse[:DIL_BLOCK], lse[DIL_BLOCK:])
            tok0 = j * (DIL_BLOCK * d) + r
            rows = pl.ds(tok0, DIL_BLOCK, stride=d) if d > 1 else pl.ds(pl.multiple_of(tok0, DIL_BLOCK), DIL_BLOCK)
            o_scr[g, rows, :] = o
            l_scr[g, rows, :] = lv
            return carry

        lax.fori_loop(0, seq // DIL_BLOCK, body, 0)

    def merge(c, carry):
        rows = pl.ds(pl.multiple_of(c * 512, 512), 512)
        ls = [l_scr[g, rows, :] for g in range(3)]
        mx = jnp.maximum(jnp.maximum(ls[0], ls[1]), ls[2])
        ws = [jnp.exp(l - mx) for l in ls]
        num = ws[0] * o_scr[0, rows, :] + ws[1] * o_scr[1, rows, :] + ws[2] * o_scr[2, rows, :]
        y_ref[0, rows, :] = (num / (ws[0] + ws[1] + ws[2])).astype(BF16)
        return carry

    lax.fori_loop(0, seq // 512, merge, 0)


def _dil_attn(dq, dk, dv):
    B, S, _ = dq[0].shape
    spec = pl.BlockSpec((1, S, LANES), lambda b, m: (b, 0, m))
    return pl.pallas_call(
        functools.partial(_dil_kernel, seq=S),
        out_shape=jax.ShapeDtypeStruct((B, S, 256), BF16),
        grid=(B, 2),
        in_specs=[spec] * 9,
        out_specs=spec,
        scratch_shapes=[pltpu.VMEM((3, S, LANES), F32), pltpu.VMEM((3, S, LANES), F32)],
        compiler_params=pltpu.CompilerParams(dimension_semantics=("parallel", "parallel"),
                                             vmem_limit_bytes=56 * 1024 * 1024),
        name="dil_attn",
    )(*dq, *dk, *dv)


def _merge_kernel(x_ref, yn_ref, yd_ref, mg0_ref, mg1_ref, wn_ref, wd_ref, wo_ref, g2_ref, wq_ref, x1_ref, hn_ref, pq_ref):
    u1 = _dot(yn_ref[...], wn_ref[...])
    u2 = _dot(yd_ref[...], wd_ref[...])
    merged = jax.nn.sigmoid(mg0_ref[...]) * u1 + jax.nn.sigmoid(mg1_ref[...]) * u2
    x1 = x_ref[...] + _dot(merged.astype(BF16), wo_ref[...])
    x1_ref[...] = x1
    ms = jnp.mean(x1 * x1, axis=-1, keepdims=True)
    hn = x1 * lax.rsqrt(ms + NORM_EPS) * g2_ref[...]
    hn_ref[...] = hn
    pq_ref[...] = _dot(hn.astype(BF16), wq_ref[...])


def _merge(x2, yn2, yd2, z2, wn, wd, wo, g2, wq):
    T = x2.shape[0]
    tm = 512
    row = lambda w, c=0: pl.BlockSpec((tm, w), lambda i, c=c: (i, c))
    const = lambda shape: pl.BlockSpec(shape, lambda i: (0, 0))
    shp = jax.ShapeDtypeStruct((T, D_MODEL), F32)
    return pl.pallas_call(
        _merge_kernel,
        out_shape=(shp, shp, shp),
        grid=(T // tm,),
        in_specs=[row(D_MODEL), row(512), row(256), row(D_MODEL, COL_MG // D_MODEL), row(D_MODEL, COL_MG // D_MODEL + 1),
                  const((512, D_MODEL)), const((256, D_MODEL)), const((D_MODEL, D_MODEL)), const((1, D_MODEL)),
                  const((D_MODEL, D_MODEL))],
        out_specs=(row(D_MODEL), row(D_MODEL), row(D_MODEL)),
        compiler_params=pltpu.CompilerParams(dimension_semantics=("parallel",), vmem_limit_bytes=48 * 1024 * 1024),
        name="merge",
    )(x2, yn2, yd2, z2, z2, wn, wd, wo, g2, wq)


def _top16(s, rank_id=None):
    if rank_id is None:
        rank_id = lax.broadcasted_iota(I32, (s.shape[0], 1), 0)
    big = jnp.iinfo(jnp.int32).max
    vals, ids = [], []
    for _ in range(PEER_TOPK):
        m = jnp.max(s, axis=0, keepdims=True)
        win = jnp.min(jnp.where(s == m, rank_id, big), axis=0, keepdims=True)
        vals.append(m)
        ids.append(win)
        s = jnp.where(rank_id == win, -jnp.inf, s)
    return vals, ids


def _route_kernel(q_ref, sk_ref, idx_ref, gate_ref):
    K = PEER_TOPK
    qh, ql = _split_bf16(q_ref[...])
    vals, ids = [], []
    for c in range(2):
        kh, kl = _split_bf16(sk_ref[c])
        s = _dot_nt(kh, qh) + _dot_nt(kh, ql) + _dot_nt(kl, qh)
        v, p = _top16(s)
        vals.append(v)
        ids.append(p)
    v0, p0 = jnp.concatenate(vals[0], axis=0), jnp.concatenate(ids[0], axis=0)
    v1, p1 = jnp.concatenate(vals[1], axis=0), jnp.concatenate(ids[1], axis=0)

    a8 = lax.broadcasted_iota(I32, (SUBLANES, 1), 0)
    pieces = [(v0 + vals[1][0], p0 * PEER_NKEYS + ids[1][0], lax.broadcasted_iota(I32, (K, 1), 0) * K)]
    for b in range(1, SUBLANES):
        keep = a8 < K // (b + 1)
        pieces.append((jnp.where(keep, v0[:SUBLANES] + vals[1][b], -jnp.inf), p0[:SUBLANES] * PEER_NKEYS + ids[1][b],
                       a8 * K + b))
    pieces.append((vals[0][0] + v1[SUBLANES:], ids[0][0] * PEER_NKEYS + p1[SUBLANES:], a8 + SUBLANES))
    cand = jnp.concatenate([p[0] for p in pieces], axis=0)
    eid = jnp.concatenate([p[1] for p in pieces], axis=0)
    flat = jnp.concatenate([p[2] for p in pieces], axis=0)
    v, win = _top16(cand, flat)
    sel_ids = [jnp.sum(jnp.where(flat == w, eid, 0), axis=0, keepdims=True) for w in win]
    sc = jnp.concatenate(v, axis=0)
    e = jnp.exp(sc - sc[0:1])
    gate_ref[...] = e / jnp.sum(e, axis=0, keepdims=True)
    idx_ref[...] = jnp.concatenate(sel_ids, axis=0)


def _peer_route(pq, sk_pad):
    T = pq.shape[0]
    tt = 256
    return pl.pallas_call(
        _route_kernel,
        out_shape=(jax.ShapeDtypeStruct((PEER_HEADS * PEER_TOPK, T), I32),
                   jax.ShapeDtypeStruct((PEER_HEADS * PEER_TOPK, T), F32)),
        grid=(T // tt, PEER_HEADS),
        in_specs=[pl.BlockSpec((tt, LANES), lambda i, h: (i, h)),
                  pl.BlockSpec((2, PEER_NKEYS, LANES), lambda i, h: (0, 0, 0))],
        out_specs=(pl.BlockSpec((PEER_TOPK, tt), lambda i, h: (h, i)),
                   pl.BlockSpec((PEER_TOPK, tt), lambda i, h: (h, i))),
        compiler_params=pltpu.CompilerParams(dimension_semantics=("parallel", "parallel")),
        name="peer_route",
    )(pq, sk_pad)


PEER_NK = PEER_HEADS * PEER_TOPK
SUBLANES = 8


def _unpack_pair(w):
    lo = pltpu.bitcast(w << 16, F32)
    hi = pltpu.bitcast(w & jnp.uint32(0xFFFF0000), F32)
    return lo, hi


def _peer_kernel(idx_cur, idx_nxt, hn_ref, gate_ref, x1_ref, uv_hbm, o_ref, buf_a, buf_b, sem):
    i = pl.program_id(0)
    last = pl.num_programs(0) - 1
    half_rows = SUBLANES // 2
    bufs = (buf_a, buf_b)

    def tile_copy(idx_ref, s, c, k):
        p = c * PEER_NK + k
        return pltpu.make_async_copy(uv_hbm.at[idx_ref[p]], bufs[s].at[pl.ds(p * SUBLANES, SUBLANES), :], sem.at[s, c])

    def fetch_token(idx_ref, s, c):
        for k in range(PEER_NK):
            tile_copy(idx_ref, s, c, k).start(priority=k % 2)

    def wait_token(idx_ref, s, c):
        for k in range(PEER_NK):
            tile_copy(idx_ref, s, c, k).wait()

    @pl.when(i == 0)
    def _():
        lax.fori_loop(0, PEER_TOK, lambda c, carry: (fetch_token(idx_cur, 0, c), carry)[1], 0)

    eye = lax.broadcasted_iota(I32, (PEER_NK, PEER_NK), 0) == lax.broadcasted_iota(I32, (PEER_NK, PEER_NK), 1)

    def token(slot, c):
        wait_token(idx_cur, slot, c)
        fetch_token(idx_nxt, 1 - slot, c)
        base = c * (PEER_NK * SUBLANES)
        plane = lambda s: bufs[slot][pl.ds(base + s, PEER_NK, stride=SUBLANES), :]
        xt = hn_ref[c]
        gcol = jnp.sum(jnp.where(eye, gate_ref[pl.ds(c, 1), :], 0.0), axis=-1, keepdims=True)
        acc = jnp.zeros((PEER_NK, LANES), F32)
        for s in range(half_rows):
            lo, hi = _unpack_pair(plane(s))
            acc = acc + lo * xt[s:s + 1, :] + hi * xt[s + half_rows:s + half_rows + 1, :]
        act = jax.nn.gelu(jnp.sum(acc, axis=-1, keepdims=True)) * gcol
        lo_rows, hi_rows = [], []
        for s in range(half_rows, SUBLANES):
            lo, hi = _unpack_pair(plane(s))
            lo_rows.append(jnp.sum(act * lo, axis=0, keepdims=True))
            hi_rows.append(jnp.sum(act * hi, axis=0, keepdims=True))
        o_ref[c] = x1_ref[c] + jnp.concatenate(lo_rows + hi_rows, axis=0)

    def step(slot):
        for c in range(PEER_TOK):
            token(slot, c)

        @pl.when(i == last)
        def _():
            lax.fori_loop(0, PEER_TOK, lambda c, carry: (wait_token(idx_nxt, 1 - slot, c), carry)[1], 0)

    pl.when(i % 2 == 0)(lambda: step(0))
    pl.when(i % 2 == 1)(lambda: step(1))


def _peer_expert(idx_flat, hn_t, gates, x1_t, uv_tiles):
    T = hn_t.shape[0]
    n = T // PEER_TOK
    tile = pl.BlockSpec((PEER_TOK, SUBLANES, LANES), lambda i: (i, 0, 0))
    fetch_buf = pltpu.VMEM((PEER_TOK * PEER_NK * SUBLANES, LANES), jnp.uint32)
    return pl.pallas_call(
        _peer_kernel,
        out_shape=jax.ShapeDtypeStruct((T, SUBLANES, LANES), F32),
        grid=(n,),
        in_specs=[pl.BlockSpec((PEER_TOK * PEER_NK,), lambda i: (i,), memory_space=pltpu.SMEM),
                  pl.BlockSpec((PEER_TOK * PEER_NK,), lambda i: (jnp.minimum(i + 1, n - 1),), memory_space=pltpu.SMEM),
                  tile, pl.BlockSpec((PEER_TOK, PEER_NK), lambda i: (i, 0)), tile,
                  pl.BlockSpec(memory_space=pl.ANY)],
        out_specs=tile,
        scratch_shapes=[fetch_buf, fetch_buf, pltpu.SemaphoreType.DMA((2, PEER_TOK))],
        compiler_params=pltpu.CompilerParams(dimension_semantics=("arbitrary",), vmem_limit_bytes=48 * 1024 * 1024),
        name="peer_expert",
    )(idx_flat, idx_flat, hn_t, gates, x1_t, uv_tiles)


SC_LANES = 16
SC_WORKERS = 32
SC_GATHER = 32
SC_BLOCK = 8
SC_SHARE = (13, 32)


def _sc_lane_bcast(vec, k):
    idx = jnp.full((SC_LANES, 1), k, I32)
    dn = lax.GatherDimensionNumbers(offset_dims=(), collapsed_slice_dims=(0,), start_index_map=(0,))
    return lax.gather(vec, idx, dn, slice_sizes=(1,), mode=lax.GatherScatterMode.PROMISE_IN_BOUNDS)


def _sc_unpack_pair(w):
    return plsc.bitcast(w << 16, F32), plsc.bitcast(w & jnp.uint32(0xFFFF0000), F32)


def _peer_expert_sc(idx, gates, hn, x1, uv_rows):
    n = hn.shape[0]
    assert n % (SC_WORKERS * SC_BLOCK) == 0
    tokens_per_worker = n // SC_WORKERS
    n_gather = PEER_NK // SC_GATHER
    half = D_MODEL // 2
    n_chunk = half // SC_LANES
    mesh = plsc.VectorSubcoreMesh(core_axis_name="c", subcore_axis_name="s")

    def body(idx_hbm, g_hbm, hn_hbm, x1_hbm, uv_hbm, out_hbm, idx_v, g_v, x_v, o_v, rows0, rows1, sem0, sem1):
        worker = lax.axis_index("s") * 2 + lax.axis_index("c")
        lane = lax.iota(I32, SC_LANES)
        rows, sems = (rows0, rows1), (sem0, sem1)

        def gather(ti, q):
            return pltpu.make_async_copy(uv_hbm.at[idx_v.at[ti, q]], rows[q % 2], sems[q % 2])

        def block(bi, carry):
            t0 = pl.multiple_of(worker * tokens_per_worker + bi * SC_BLOCK, SC_BLOCK)
            pltpu.sync_copy(idx_hbm.at[pl.ds(t0, SC_BLOCK)], idx_v)
            pltpu.sync_copy(g_hbm.at[pl.ds(t0, SC_BLOCK)], g_v)
            pltpu.sync_copy(hn_hbm.at[pl.ds(t0, SC_BLOCK)], x_v)
            pltpu.sync_copy(x1_hbm.at[pl.ds(t0, SC_BLOCK)], o_v)
            gather(0, 0).start()

            def token(ti, carry1):
                for q in range(n_gather):
                    if q + 1 < n_gather:
                        gather(ti, q + 1).start()
                    else:
                        @pl.when(ti + 1 < SC_BLOCK)
                        def _():
                            gather(ti + 1, 0).start()
                    gather(ti, q).wait()
                    rows_v = rows[q % 2]

                    def dot_chunk(j, accs):
                        off = pl.multiple_of(j * SC_LANES, SC_LANES)
                        xlo = x_v[ti, pl.ds(off, SC_LANES)]
                        xhi = x_v[ti, pl.ds(half + off, SC_LANES)]
                        out = []
                        for k in range(SC_GATHER):
                            lo, hi = _sc_unpack_pair(rows_v[k, pl.ds(off, SC_LANES)])
                            out.append(accs[k] + lo * xlo + hi * xhi)
                        return tuple(out)

                    accs = lax.fori_loop(0, n_chunk, dot_chunk,
                                         tuple(jnp.zeros((SC_LANES,), F32) for _ in range(SC_GATHER)))
                    acts = []
                    for h in range(SC_GATHER // SC_LANES):
                        a = jnp.zeros((SC_LANES,), F32)
                        for kk in range(SC_LANES):
                            a = jnp.where(lane == kk, jnp.sum(accs[h * SC_LANES + kk]), a)
                        y = 0.7978845608028654 * (a + 0.044715 * a * a * a)
                        th = 1.0 - 2.0 / (jnp.exp(2.0 * y) + 1.0)
                        acts.append(0.5 * a * (1.0 + th) * g_v[ti, pl.ds(q * SC_GATHER + h * SC_LANES, SC_LANES)])
                    act_b = [_sc_lane_bcast(acts[k // SC_LANES], k % SC_LANES) for k in range(SC_GATHER)]

                    def mix_chunk(j, carry2):
                        off = pl.multiple_of(j * SC_LANES, SC_LANES)
                        al = jnp.zeros((SC_LANES,), F32)
                        ah = jnp.zeros((SC_LANES,), F32)
                        for k in range(SC_GATHER):
                            lo, hi = _sc_unpack_pair(rows_v[k, pl.ds(half + off, SC_LANES)])
                            al = al + act_b[k] * lo
                            ah = ah + act_b[k] * hi
                        o_v[ti, pl.ds(off, SC_LANES)] = o_v[ti, pl.ds(off, SC_LANES)] + al
                        o_v[ti, pl.ds(half + off, SC_LANES)] = o_v[ti, pl.ds(half + off, SC_LANES)] + ah
                        return carry2

                    lax.fori_loop(0, n_chunk, mix_chunk, 0)
                return carry1

            lax.fori_loop(0, SC_BLOCK, token, 0)
            pltpu.sync_copy(o_v, out_hbm.at[pl.ds(t0, SC_BLOCK)])
            return carry

        lax.fori_loop(0, tokens_per_worker // SC_BLOCK, block, 0)

    row_buf = pltpu.VMEM((SC_GATHER, D_MODEL), jnp.uint32)
    return pl.kernel(
        body, mesh=mesh,
        out_type=jax.ShapeDtypeStruct((n, D_MODEL), F32),
        scratch_types=[pltpu.VMEM((SC_BLOCK, n_gather, SC_GATHER), I32), pltpu.VMEM((SC_BLOCK, PEER_NK), F32),
                       pltpu.VMEM((SC_BLOCK, D_MODEL), F32), pltpu.VMEM((SC_BLOCK, D_MODEL), F32), row_buf, row_buf,
                       pltpu.SemaphoreType.DMA, pltpu.SemaphoreType.DMA],
        compiler_params=pltpu.CompilerParams(needs_layout_passes=False),
        name="peer_expert_sc",
    )(idx, gates, hn, x1, uv_rows)


def _rope_tables(pos):
    half = ROPE_DIMS // 2
    inv = ROPE_THETA ** (-(jnp.arange(half, dtype=F32) * 2.0 / ROPE_DIMS))
    ang = pos.astype(F32)[:, None] * inv[None, :]
    cos, sin = jnp.cos(ang), jnp.sin(ang)
    n = pos.shape[0]
    a = jnp.concatenate([cos, cos, jnp.ones((n, HEAD_DIM - ROPE_DIMS), F32)], axis=-1)
    b = jnp.concatenate([-sin, sin, jnp.zeros((n, HEAD_DIM - ROPE_DIMS), F32)], axis=-1)
    return jnp.tile(a, (1, 2)), jnp.tile(b, (1, 2))


def _pack_bf16_pairs(w):
    half = w.shape[1] // 2
    bits = lax.bitcast_convert_type(w.astype(BF16), jnp.uint16).astype(jnp.uint32)
    return bits[:, :half] | (bits[:, half:] << 16)


def _tile2(v):
    return jnp.tile(v.reshape(1, HEAD_DIM), (1, 2))


def kernel(x, norm1_g, w_in, nsa_q_norm, nsa_k_norm, cmp_pe_k, cmp_w1_k, cmp_w2_k, cmp_pe_v, cmp_w1_v, cmp_w2_v,
           dil_q_norm, dil_k_norm, w_up_nsa, w_up_dil, w_o, norm2_g, peer_wq, peer_subkeys, peer_u, peer_v):
    B, S, D = x.shape
    T = B * S
    assert D == D_MODEL and S % (DIL_PATTERNS[-1][1] * DIL_BLOCK) == 0 and S >= WIN + NSA_QB and T % 512 == 0
    x2 = x.reshape(T, D)

    n_q, n_kv, n_gate, n_dil = 512, 768, 24, 2304
    o_gate = n_q + n_kv
    o_dil = o_gate + n_gate
    o_mg = o_dil + n_dil
    w_perm = jnp.concatenate([w_in[:, o_mg:], w_in[:, :o_gate], w_in[:, o_dil:o_mg], w_in[:, o_gate:o_dil],
                              jnp.zeros((D, IN_COLS_PAD - w_in.shape[1]), w_in.dtype)], axis=1).astype(BF16)
    z2 = _in_proj(x2, norm1_g.reshape(1, D), w_perm)
    z3 = z2.reshape(B, S, IN_COLS_PAD)

    blockdiag = jnp.asarray(np.kron(np.eye(2), np.ones((HEAD_DIM, HEAD_DIM))), BF16)
    rope_a, rope_b = _rope_tables(jnp.arange(S))

    qn, ksd, vsd, kwd, vwd = _nsa_prep(z3, rope_a, rope_b, blockdiag, _tile2(nsa_q_norm), _tile2(nsa_k_norm[1]),
                                       _tile2(nsa_k_norm[2]))

    n_cmp = (S - CMP_BLOCK) // CMP_STRIDE + 1
    ncp = S // CMP_STRIDE
    def flat_blocks(col):
        zc = z3[:, :, col:col + LANES].reshape(B, S, 2, HEAD_DIM).transpose(0, 2, 1, 3)
        r = zc.reshape(B, 2, ncp, CMP_STRIDE * HEAD_DIM)
        nxt = jnp.concatenate([r[:, :, 1:], jnp.zeros_like(r[:, :, :1])], axis=2)
        return jnp.concatenate([r, nxt], axis=-1).reshape(B * 2 * ncp, CMP_BLOCK * HEAD_DIM)
    cmp_a, cmp_b = _rope_tables(jnp.arange(ncp) * CMP_STRIDE + CMP_BLOCK - 1)
    dup2 = lambda w: jnp.concatenate([w, w], axis=1).astype(BF16)
    kcd, vcd = _compress(flat_blocks(COL_KV), flat_blocks(COL_KV + LANES),
                         cmp_pe_k.reshape(1, -1), cmp_pe_v.reshape(1, -1), cmp_w1_k.astype(BF16), cmp_w1_v.astype(BF16),
                         dup2(cmp_w2_k), dup2(cmp_w2_v), cmp_a, cmp_b, blockdiag, _tile2(nsa_k_norm[0]))
    kcd = kcd.reshape(B, 2, ncp, LANES)
    vcd = vcd.reshape(B, 2, ncp, LANES)

    n_slc = S // SLC_BLOCK
    s0 = np.arange(n_cmp) * CMP_STRIDE
    b0 = np.arange(n_slc) * SLC_BLOCK
    ov = np.clip(np.minimum(s0[:, None] + CMP_BLOCK, b0[None, :] + SLC_BLOCK) - np.maximum(s0[:, None], b0[None, :]),
                 0, None) / CMP_BLOCK
    ov_pad = np.zeros((ncp, LANES), np.float32)
    ov_pad[:n_cmp, :n_slc] = ov
    y_nsa = _nsa_attn(qn, kcd, vcd, ksd, vsd, kwd, vwd, z3, jnp.asarray(ov_pad, BF16))

    gq = jnp.tile(dil_q_norm.reshape(3, 1, HEAD_DIM), (1, 1, 2))
    gk = jnp.tile(dil_k_norm.reshape(3, 1, HEAD_DIM), (1, 1, 2))
    prep = _dil_prep(z3, rope_a, rope_b, blockdiag, gq, gk)
    flat = [p.reshape(B, S, 256) for p in prep]
    y_dil = _dil_attn(flat[0::3], flat[1::3], flat[2::3])

    x1, hn, pq = _merge(x2, y_nsa.reshape(T, 512), y_dil.reshape(T, 256), z2, w_up_nsa.astype(BF16),
                        w_up_dil.astype(BF16), w_o.astype(BF16), norm2_g.reshape(1, D), peer_wq.astype(BF16))

    sub = PEER_NKEYS // 2
    sk_pad = jnp.stack([jnp.pad(peer_subkeys[0], ((0, 0), (0, sub))), jnp.pad(peer_subkeys[1], ((0, 0), (sub, 0)))])
    idx_t, gate_t = _peer_route(pq, sk_pad)
    uv_rows = jnp.concatenate([_pack_bf16_pairs(peer_u), _pack_bf16_pairs(peer_v)], axis=1)
    idx, gates = idx_t.T, gate_t.T
    sc_unit = SC_WORKERS * SC_BLOCK
    t_tc = T - (T * SC_SHARE[0] // SC_SHARE[1]) // sc_unit * sc_unit
    assert t_tc % PEER_TOK == 0
    out_tc = _peer_expert(idx[:t_tc].reshape(-1), hn[:t_tc].reshape(t_tc, SUBLANES, LANES), gates[:t_tc],
                          x1[:t_tc].reshape(t_tc, SUBLANES, LANES), uv_rows.reshape(-1, SUBLANES, LANES))
    out_sc = _peer_expert_sc(idx[t_tc:].reshape(T - t_tc, PEER_NK // SC_GATHER, SC_GATHER), gates[t_tc:], hn[t_tc:],
                             x1[t_tc:], uv_rows)
    return jnp.concatenate([out_tc.reshape(t_tc, D), out_sc], axis=0).reshape(B, S, D)
```

```python
import functools

import numpy as np
import jax
import jax.numpy as jnp
from jax import lax
from jax.experimental import pallas as pl
from jax.experimental.pallas import tpu as pltpu
from jax.experimental.pallas import tpu_sc as plsc

F32 = jnp.float32
BF16 = jnp.bfloat16
I32 = jnp.int32

D_MODEL = 1024
HEAD_DIM = 64
ROPE_DIMS = 16
ROPE_THETA = 500000.0
NORM_EPS = 1e-6
NEG_INF = -1e30
LANES = 128

NSA_HEADS = 8
CMP_BLOCK = 32
CMP_STRIDE = 16
CMP_HIDDEN = 256
SLC_BLOCK = 64
SLC_TOPN = 16
FORCE_SCORE = 1e3
WIN = 512
NSA_QB = 128
SEL_CHUNK = 512

DIL_PATTERNS = ((128, 1), (512, 4), (2048, 16))
DIL_BLOCK = 128
DIL_PREP_ROWS = 1024

PEER_HEADS = 8
PEER_NKEYS = 128
PEER_TOPK = 16
PEER_TOK = 16

COL_MG = 0
COL_Q = 2048
COL_KV = 2560
COL_DIL = 3328
COL_GATE = 5632
IN_COLS_PAD = 5760

_NT = (((1,), (1,)), ((), ()))


def _dot(a, b):
    return jnp.dot(a, b, preferred_element_type=F32)


def _dot_nt(a, b):
    return lax.dot_general(a, b, _NT, preferred_element_type=F32)


def _split_bf16(a):
    hi = a.astype(BF16)
    lo = (a - hi.astype(F32)).astype(BF16)
    return hi, lo


def _dot_hilo(a, b_bf16):
    hi, lo = _split_bf16(a)
    return _dot(hi, b_bf16) + _dot(lo, b_bf16)


def _head_norm_rope(zt, gain, rope_a, rope_b, blockdiag):
    ss = _dot_hilo(zt * zt, blockdiag)
    zn = zt * lax.rsqrt(ss * (1.0 / HEAD_DIM) + NORM_EPS) * gain
    d = lax.broadcasted_iota(I32, zn.shape, 1) & (HEAD_DIM - 1)
    half = ROPE_DIMS // 2
    partner = jnp.where(d < half, pltpu.roll(zn, LANES - half, 1), pltpu.roll(zn, half, 1))
    return zn * rope_a + partner * rope_b


def _half_masks():
    lane = lax.broadcasted_iota(I32, (1, LANES), 1)
    lo = (lane < HEAD_DIM).astype(BF16)
    return lo, (1 - lo).astype(BF16)


def _inproj_kernel(x_ref, g_ref, w_ref, o_ref, h_scr):
    @pl.when(pl.program_id(1) == 0)
    def _():
        xf = x_ref[...]
        ms = jnp.mean(xf * xf, axis=-1, keepdims=True)
        h_scr[...] = (xf * lax.rsqrt(ms + NORM_EPS) * g_ref[...]).astype(BF16)

    o_ref[...] = _dot(h_scr[...], w_ref[...])


def _in_proj(x2, g1, w_bf16):
    T = x2.shape[0]
    tm, tn = 1024, 640
    return pl.pallas_call(
        _inproj_kernel,
        out_shape=jax.ShapeDtypeStruct((T, IN_COLS_PAD), F32),
        grid=(T // tm, IN_COLS_PAD // tn),
        in_specs=[
            pl.BlockSpec((tm, D_MODEL), lambda i, j: (i, 0)),
            pl.BlockSpec((1, D_MODEL), lambda i, j: (0, 0)),
            pl.BlockSpec((D_MODEL, tn), lambda i, j: (0, j)),
        ],
        out_specs=pl.BlockSpec((tm, tn), lambda i, j: (i, j)),
        scratch_shapes=[pltpu.VMEM((tm, D_MODEL), BF16)],
        compiler_params=pltpu.CompilerParams(dimension_semantics=("parallel", "arbitrary")),
        name="in_proj",
    )(x2, g1, w_bf16)


def _nsa_prep_kernel(zq_ref, zks_ref, zvs_ref, zkw_ref, zvw_ref, ra_ref, rb_ref, bd_ref, gq_ref, gks_ref, gkw_ref,
                     q_ref, ks_ref, vs_ref, kw_ref, vw_ref):
    ra, rb, bd = ra_ref[...], rb_ref[...], bd_ref[...]
    lane = lax.broadcasted_iota(I32, ra.shape, 1)
    scale = HEAD_DIM ** -0.5

    zq = zq_ref[0]
    tiles = [_head_norm_rope(zq[:, m * LANES:(m + 1) * LANES], gq_ref[...], ra, rb, bd) * scale for m in range(4)]
    q_ref[0] = jnp.concatenate(tiles, axis=-1).astype(BF16)

    def dup(t, out_ref):
        sw = pltpu.roll(t, HEAD_DIM, 1)
        out_ref[0, 0] = jnp.where(lane < HEAD_DIM, t, sw).astype(BF16)
        out_ref[0, 1] = jnp.where(lane < HEAD_DIM, sw, t).astype(BF16)

    dup(_head_norm_rope(zks_ref[0], gks_ref[...], ra, rb, bd), ks_ref)
    dup(_head_norm_rope(zkw_ref[0], gkw_ref[...], ra, rb, bd), kw_ref)
    dup(zvs_ref[0], vs_ref)
    dup(zvw_ref[0], vw_ref)


def _nsa_prep(z3, rope_a, rope_b, blockdiag, gq, gks, gkw):
    B, S, _ = z3.shape
    tm = 512
    kvb = COL_KV // LANES
    zcol = lambda c: pl.BlockSpec((1, tm, LANES), lambda b, i, c=c: (b, i, c))
    const = lambda shape: pl.BlockSpec(shape, lambda b, i: tuple(0 for _ in shape))
    kv_out = pl.BlockSpec((1, 2, tm, LANES), lambda b, i: (b, 0, i, 0))
    kv_shape = jax.ShapeDtypeStruct((B, 2, S, LANES), BF16)
    return pl.pallas_call(
        _nsa_prep_kernel,
        out_shape=(jax.ShapeDtypeStruct((B, S, 512), BF16), kv_shape, kv_shape, kv_shape, kv_shape),
        grid=(B, S // tm),
        in_specs=[
            pl.BlockSpec((1, tm, 512), lambda b, i: (b, i, COL_Q // 512)),
            zcol(kvb + 2), zcol(kvb + 3), zcol(kvb + 4), zcol(kvb + 5),
            pl.BlockSpec((tm, LANES), lambda b, i: (i, 0)),
            pl.BlockSpec((tm, LANES), lambda b, i: (i, 0)),
            const((LANES, LANES)), const((1, LANES)), const((1, LANES)), const((1, LANES)),
        ],
        out_specs=(pl.BlockSpec((1, tm, 512), lambda b, i: (b, i, 0)), kv_out, kv_out, kv_out, kv_out),
        compiler_params=pltpu.CompilerParams(dimension_semantics=("parallel", "parallel")),
        name="nsa_prep",
    )(z3, z3, z3, z3, z3, rope_a, rope_b, blockdiag, gq, gks, gkw)


def _compress_kernel(fk_ref, fv_ref, pek_ref, pev_ref, w1k_ref, w1v_ref, w2k_ref, w2v_ref, ra_ref, rb_ref, bd_ref, gk_ref,
                     k_ref, v_ref):
    def mlp(f_ref, pe_ref, w1_ref, w2_ref):
        f = (f_ref[...] + pe_ref[...]).astype(BF16)
        h = jax.nn.gelu(_dot(f, w1_ref[...]))
        return _dot(h.astype(BF16), w2_ref[...])

    kc = mlp(fk_ref, pek_ref, w1k_ref, w2k_ref)
    k_ref[...] = _head_norm_rope(kc, gk_ref[...], ra_ref[...], rb_ref[...], bd_ref[...]).astype(BF16)
    v_ref[...] = mlp(fv_ref, pev_ref, w1v_ref, w2v_ref).astype(BF16)


def _compress(flat_k, flat_v, pek, pev, w1k, w1v, w2k, w2v, rope_a, rope_b, blockdiag, gk):
    rows, width = flat_k.shape
    nblk = rope_a.shape[0]
    row = pl.BlockSpec((nblk, width), lambda i: (i, 0))
    const = lambda shape: pl.BlockSpec(shape, lambda i: tuple(0 for _ in shape))
    out = pl.BlockSpec((nblk, LANES), lambda i: (i, 0))
    shp = jax.ShapeDtypeStruct((rows, LANES), BF16)
    return pl.pallas_call(
        _compress_kernel,
        out_shape=(shp, shp),
        grid=(rows // nblk,),
        in_specs=[row, row, const((1, width)), const((1, width)), const((width, CMP_HIDDEN)), const((width, CMP_HIDDEN)),
                  const((CMP_HIDDEN, LANES)), const((CMP_HIDDEN, LANES)), const((nblk, LANES)), const((nblk, LANES)),
                  const((LANES, LANES)), const((1, LANES))],
        out_specs=(out, out),
        compiler_params=pltpu.CompilerParams(dimension_semantics=("parallel",)),
        name="compress",
    )(flat_k, flat_v, pek, pev, w1k, w1v, w2k, w2v, rope_a, rope_b, blockdiag, gk)


def _softmax_rows(s):
    m = jnp.max(s, axis=-1, keepdims=True)
    e = jnp.exp(s - m)
    return e / jnp.sum(e, axis=-1, keepdims=True)


def _nsa_kernel(q_ref, kc_ref, vc_ref, ks_ref, vs_ref, kw_ref, vw_ref, gl_ref, ov_ref, ex_ref, y_ref):
    n = pl.program_id(1)
    t0 = n * NSA_QB
    qt = q_ref[0]
    hm = _half_masks()
    lane = lax.broadcasted_iota(I32, (NSA_QB, LANES), 1)
    t1 = t0 + lax.broadcasted_iota(I32, (NSA_QB, 1), 0)
    t4 = t0 + (lax.broadcasted_iota(I32, (4 * NSA_QB, 1), 0) & (NSA_QB - 1))
    gates = jax.nn.sigmoid(gl_ref[0])

    n_slc_blocks = ks_ref.shape[2] // SLC_BLOCK
    blk = lax.broadcasted_iota(I32, (n_slc_blocks, NSA_QB), 0)
    tq = t0 + lax.broadcasted_iota(I32, (n_slc_blocks, NSA_QB), 1)
    cur = tq >> 6
    forced = (blk == 0) | (blk == cur) | (blk == cur - 1)
    causal_b = blk * SLC_BLOCK <= tq

    tiles_out = []
    for g in range(2):
        q4 = jnp.concatenate(
            [qt[:, (2 * g + jj // 2) * LANES:(2 * g + jj // 2 + 1) * LANES] * hm[jj % 2] for jj in range(4)], axis=0)

        sc = _dot_nt(q4, kc_ref[0, g])
        cend = lax.broadcasted_iota(I32, (1, sc.shape[1]), 1) * CMP_STRIDE + (CMP_BLOCK - 1)
        valid_c = cend <= t4
        pc = _softmax_rows(jnp.where(valid_c, sc, NEG_INF))
        pc = jnp.where(valid_c, pc, 0.0)
        o_c = _dot(pc.astype(BF16), vc_ref[0, g])
        psum = pc[0:NSA_QB] + pc[NSA_QB:2 * NSA_QB] + pc[2 * NSA_QB:3 * NSA_QB] + pc[3 * NSA_QB:]
        imp = _dot_hilo(psum, ov_ref[...])

        score = jnp.where(forced, FORCE_SCORE, jnp.where(causal_b, imp.T[:n_slc_blocks], -1.0))
        n_grp = n_slc_blocks // SUBLANES
        grp = [score[SUBLANES * r:SUBLANES * (r + 1)] for r in range(n_grp)]
        ranks = [jnp.zeros((SUBLANES, NSA_QB), F32) for _ in range(n_grp)]
        row = lax.broadcasted_iota(I32, (SUBLANES, NSA_QB), 0)
        for i in range(n_slc_blocks):
            ri = grp[i // SUBLANES][i % SUBLANES:i % SUBLANES + 1, :]
            for r in range(n_grp):
                if r > i // SUBLANES:
                    ahead = ri >= grp[r]
                elif r < i // SUBLANES:
                    ahead = ri > grp[r]
                else:
                    ahead = (ri > grp[r]) | ((ri == grp[r]) & (row > i % SUBLANES))
                ranks[r] = ranks[r] + jnp.where(ahead, 1.0, 0.0)
        sel64 = jnp.where(jnp.concatenate(ranks, axis=0) < float(SLC_TOPN), 1.0, 0.0)
        sel = jnp.concatenate([sel64, jnp.zeros((LANES - n_slc_blocks, NSA_QB), F32)], axis=0).T.astype(BF16)

        def sel_chunk(c, carry, last):
            m_i, l_i, acc = carry
            k0 = pl.multiple_of(c * SEL_CHUNK, SEL_CHUNK)
            kch = ks_ref[0, g, pl.ds(k0, SEL_CHUNK), :]
            vch = vs_ref[0, g, pl.ds(k0, SEL_CHUNK), :]
            picked = _dot(sel, ex_ref[:, pl.ds(k0, SEL_CHUNK)])
            if last:
                kpos = k0 + lax.broadcasted_iota(I32, (1, SEL_CHUNK), 1)
                picked = jnp.where(kpos <= t1, picked, 0.0)
            bias = (picked - 1.0) * (-NEG_INF)
            s = (_dot_nt(q4, kch).reshape(4, NSA_QB, SEL_CHUNK) + bias[None]).reshape(4 * NSA_QB, SEL_CHUNK)
            m_new = jnp.maximum(m_i, jnp.max(s, axis=-1, keepdims=True))
            alpha = jnp.exp(m_i - m_new)
            p = jnp.exp(s - m_new)
            l_new = alpha * l_i + jnp.sum(p, axis=-1, keepdims=True)
            acc_new = alpha * acc + _dot(p.astype(BF16), vch)
            return m_new, l_new, acc_new

        init = (jnp.full((4 * NSA_QB, 1), NEG_INF, F32), jnp.zeros((4 * NSA_QB, 1), F32),
                jnp.zeros((4 * NSA_QB, LANES), F32))
        n_full = t0 // SEL_CHUNK
        carry = lax.fori_loop(0, n_full, functools.partial(sel_chunk, last=False), init)
        _, l_s, acc_s = sel_chunk(n_full, carry, last=True)
        o_s = acc_s / l_s

        wlen = WIN + NSA_QB
        ws = pl.multiple_of(jnp.maximum(t0 - WIN, 0), NSA_QB)
        sw = _dot_nt(q4, kw_ref[0, g, pl.ds(ws, wlen), :])
        dist = t4 - (ws + lax.broadcasted_iota(I32, (1, wlen), 1))
        pw = _softmax_rows(jnp.where((dist >= 0) & (dist < WIN), sw, NEG_INF))
        o_w = _dot(pw.astype(BF16), vw_ref[0, g, pl.ds(ws, wlen), :])

        heads = []
        for jj in range(4):
            h = 4 * g + jj
            rows = slice(jj * NSA_QB, (jj + 1) * NSA_QB)
            heads.append(gates[:, 3 * h:3 * h + 1] * o_c[rows] + gates[:, 3 * h + 1:3 * h + 2] * o_s[rows]
                         + gates[:, 3 * h + 2:3 * h + 3] * o_w[rows])
        tiles_out.append(jnp.where(lane < HEAD_DIM, heads[0], heads[1]))
        tiles_out.append(jnp.where(lane < HEAD_DIM, heads[2], heads[3]))

    y_ref[0] = jnp.concatenate(tiles_out, axis=-1).astype(BF16)


def _nsa_attn(qn, kcd, vcd, ksd, vsd, kwd, vwd, z3, overlap):
    B, S, _ = qn.shape
    ncp = kcd.shape[2]
    assert S // SLC_BLOCK <= LANES
    expand = jnp.asarray(np.arange(LANES)[:, None] == (np.arange(S)[None, :] // SLC_BLOCK), BF16)
    full = lambda rows: pl.BlockSpec((1, 2, rows, LANES), lambda b, n: (b, 0, 0, 0))
    return pl.pallas_call(
        _nsa_kernel,
        out_shape=jax.ShapeDtypeStruct((B, S, 512), BF16),
        grid=(B, S // NSA_QB),
        in_specs=[
            pl.BlockSpec((1, NSA_QB, 512), lambda b, n: (b, n, 0)),
            full(ncp), full(ncp), full(S), full(S), full(S), full(S),
            pl.BlockSpec((1, NSA_QB, LANES), lambda b, n: (b, n, COL_GATE // LANES)),
            pl.BlockSpec((ncp, LANES), lambda b, n: (0, 0)),
            pl.BlockSpec((LANES, S), lambda b, n: (0, 0)),
        ],
        out_specs=pl.BlockSpec((1, NSA_QB, 512), lambda b, n: (b, n, 0)),
        compiler_params=pltpu.CompilerParams(dimension_semantics=("parallel", "arbitrary"),
                                             vmem_limit_bytes=48 * 1024 * 1024),
        name="nsa_attn",
    )(qn, kcd, vcd, ksd, vsd, kwd, vwd, z3, overlap, expand)


def _dil_prep_kernel(*refs):
    zs, (ra_ref, rb_ref, bd_ref, gq_ref, gk_ref), outs = refs[0:18], refs[18:23], refs[23:32]
    bd = bd_ref[...]
    scale = HEAD_DIM ** -0.5
    for g, (_, d) in enumerate(DIL_PATTERNS):
        n = DIL_PREP_ROWS // d
        for r in range(d):
            rows = pl.ds(r, n, stride=d) if d > 1 else pl.ds(0, n)
            ra, rb = ra_ref[rows, :], rb_ref[rows, :]
            for which in range(3):
                for m in range(2):
                    z = zs[2 * (3 * g + which) + m][0, rows, :]
                    if which == 0:
                        z = _head_norm_rope(z, gq_ref[g], ra, rb, bd) * scale
                    elif which == 1:
                        z = _head_norm_rope(z, gk_ref[g], ra, rb, bd)
                    outs[3 * g + which][0, r, :, m * LANES:(m + 1) * LANES] = z.astype(BF16)


def _dil_prep(z3, rope_a, rope_b, blockdiag, gq, gk):
    B, S, _ = z3.shape
    nsteps = S // DIL_PREP_ROWS
    c0 = COL_DIL // LANES
    in_specs = [pl.BlockSpec((1, DIL_PREP_ROWS, LANES), lambda b, c, k=k: (b, c, c0 + k)) for k in range(18)]
    in_specs += [
        pl.BlockSpec((DIL_PREP_ROWS, LANES), lambda b, c: (c, 0)),
        pl.BlockSpec((DIL_PREP_ROWS, LANES), lambda b, c: (c, 0)),
        pl.BlockSpec((LANES, LANES), lambda b, c: (0, 0)),
        pl.BlockSpec((3, 1, LANES), lambda b, c: (0, 0, 0)),
        pl.BlockSpec((3, 1, LANES), lambda b, c: (0, 0, 0)),
    ]
    out_shape, out_specs = [], []
    for _, d in DIL_PATTERNS:
        for _ in range(3):
            out_shape.append(jax.ShapeDtypeStruct((B, d, S // d, 256), BF16))
            out_specs.append(pl.BlockSpec((1, d, DIL_PREP_ROWS // d, 256), lambda b, c: (b, 0, c, 0)))
    return pl.pallas_call(
        _dil_prep_kernel,
        out_shape=tuple(out_shape),
        grid=(B, nsteps),
        in_specs=in_specs,
        out_specs=tuple(out_specs),
        compiler_params=pltpu.CompilerParams(dimension_semantics=("parallel", "parallel"),
                                             vmem_limit_bytes=48 * 1024 * 1024),
        name="dil_prep",
    )(*([z3] * 18), rope_a, rope_b, blockdiag, gq, gk)


def _dil_kernel(*refs, seq):
    q_refs, k_refs, v_refs, y_ref, o_scr, l_scr = refs[0:3], refs[3:6], refs[6:9], refs[9], refs[10], refs[11]
    hm = _half_masks()
    lane = lax.broadcasted_iota(I32, (DIL_BLOCK, LANES), 1)
    qi = lax.broadcasted_iota(I32, (2 * DIL_BLOCK, 2 * DIL_BLOCK), 0) & (DIL_BLOCK - 1)
    ki = lax.broadcasted_iota(I32, (2 * DIL_BLOCK, 2 * DIL_BLOCK), 1)
    causal = (ki - DIL_BLOCK) <= qi

    for g, (_, d) in enumerate(DIL_PATTERNS):
        nb = seq // d // DIL_BLOCK

        def body(u, carry, g=g, d=d, nb=nb):
            j = u % nb
            r = u // nb
            r0 = pl.multiple_of(u * DIL_BLOCK, DIL_BLOCK)
            p0 = pl.multiple_of(jnp.maximum(u - 1, 0) * DIL_BLOCK, DIL_BLOCK)
            q = q_refs[g][0, pl.ds(r0, DIL_BLOCK), :]
            kcat = jnp.concatenate([k_refs[g][0, pl.ds(p0, DIL_BLOCK), :], k_refs[g][0, pl.ds(r0, DIL_BLOCK), :]], axis=0)
            vcat = jnp.concatenate([v_refs[g][0, pl.ds(p0, DIL_BLOCK), :], v_refs[g][0, pl.ds(r0, DIL_BLOCK), :]], axis=0)
            q2 = jnp.concatenate([q * hm[0], q * hm[1]], axis=0)
            s = _dot_nt(q2, kcat)
            first_key = jnp.maximum(qi, jnp.where(j >= 1, 0, DIL_BLOCK))
            s = jnp.where(causal & (ki >= first_key), s, NEG_INF)
            m = jnp.max(s, axis=-1, keepdims=True)
            e = jnp.exp(s - m)
            den = jnp.sum(e, axis=-1, keepdims=True)
            o2 = _dot(e.astype(BF16), vcat) / den
            lse = m + jnp.log(den)
            o = jnp.where(lane < HEAD_DIM, o2[:DIL_BLOCK], o2[DIL_BLOCK:])
            lv = jnp.where(lane < HEAD_DIM, lse[:DIL_BLOCK], lse[DIL_BLOCK:])
            tok0 = j * (DIL_BLOCK * d) + r
            rows = pl.ds(tok0, DIL_BLOCK, stride=d) if d > 1 else pl.ds(pl.multiple_of(tok0, DIL_BLOCK), DIL_BLOCK)
            o_scr[g, rows, :] = o
            l_scr[g, rows, :] = lv
            return carry

        lax.fori_loop(0, seq // DIL_BLOCK, body, 0)

    def merge(c, carry):
        rows = pl.ds(pl.multiple_of(c * 512, 512), 512)
        ls = [l_scr[g, rows, :] for g in range(3)]
        mx = jnp.maximum(jnp.maximum(ls[0], ls[1]), ls[2])
        ws = [jnp.exp(l - mx) for l in ls]
        num = ws[0] * o_scr[0, rows, :] + ws[1] * o_scr[1, rows, :] + ws[2] * o_scr[2, rows, :]
        y_ref[0, rows, :] = (num / (ws[0] + ws[1] + ws[2])).astype(BF16)
        return carry

    lax.fori_loop(0, seq // 512, merge, 0)


def _dil_attn(dq, dk, dv):
    B, S, _ = dq[0].shape
    spec = pl.BlockSpec((1, S, LANES), lambda b, m: (b, 0, m))
    return pl.pallas_call(
        functools.partial(_dil_kernel, seq=S),
        out_shape=jax.ShapeDtypeStruct((B, S, 256), BF16),
        grid=(B, 2),
        in_specs=[spec] * 9,
        out_specs=spec,
        scratch_shapes=[pltpu.VMEM((3, S, LANES), F32), pltpu.VMEM((3, S, LANES), F32)],
        compiler_params=pltpu.CompilerParams(dimension_semantics=("parallel", "parallel"),
                                             vmem_limit_bytes=56 * 1024 * 1024),
        name="dil_attn",
    )(*dq, *dk, *dv)


def _merge_kernel(x_ref, yn_ref, yd_ref, mg0_ref, mg1_ref, wn_ref, wd_ref, wo_ref, g2_ref, wq_ref, x1_ref, hn_ref, pq_ref):
    u1 = _dot(yn_ref[...], wn_ref[...])
    u2 = _dot(yd_ref[...], wd_ref[...])
    merged = jax.nn.sigmoid(mg0_ref[...]) * u1 + jax.nn.sigmoid(mg1_ref[...]) * u2
    x1 = x_ref[...] + _dot(merged.astype(BF16), wo_ref[...])
    x1_ref[...] = x1
    ms = jnp.mean(x1 * x1, axis=-1, keepdims=True)
    hn = x1 * lax.rsqrt(ms + NORM_EPS) * g2_ref[...]
    hn_ref[...] = hn
    pq_ref[...] = _dot(hn.astype(BF16), wq_ref[...])


def _merge(x2, yn2, yd2, z2, wn, wd, wo, g2, wq):
    T = x2.shape[0]
    tm = 512
    row = lambda w, c=0: pl.BlockSpec((tm, w), lambda i, c=c: (i, c))
    const = lambda shape: pl.BlockSpec(shape, lambda i: (0, 0))
    shp = jax.ShapeDtypeStruct((T, D_MODEL), F32)
    return pl.pallas_call(
        _merge_kernel,
        out_shape=(shp, shp, shp),
        grid=(T // tm,),
        in_specs=[row(D_MODEL), row(512), row(256), row(D_MODEL, COL_MG // D_MODEL), row(D_MODEL, COL_MG // D_MODEL + 1),
                  const((512, D_MODEL)), const((256, D_MODEL)), const((D_MODEL, D_MODEL)), const((1, D_MODEL)),
                  const((D_MODEL, D_MODEL))],
        out_specs=(row(D_MODEL), row(D_MODEL), row(D_MODEL)),
        compiler_params=pltpu.CompilerParams(dimension_semantics=("parallel",), vmem_limit_bytes=48 * 1024 * 1024),
        name="merge",
    )(x2, yn2, yd2, z2, z2, wn, wd, wo, g2, wq)


def _top16(s, rank_id=None):
    if rank_id is None:
        rank_id = lax.broadcasted_iota(I32, (s.shape[0], 1), 0)
    big = jnp.iinfo(jnp.int32).max
    vals, ids = [], []
    for _ in range(PEER_TOPK):
        m = jnp.max(s, axis=0, keepdims=True)
        win = jnp.min(jnp.where(s == m, rank_id, big), axis=0, keepdims=True)
        vals.append(m)
        ids.append(win)
        s = jnp.where(rank_id == win, -jnp.inf, s)
    return vals, ids


def _route_kernel(q_ref, sk_ref, idx_ref, gate_ref):
    K = PEER_TOPK
    qh, ql = _split_bf16(q_ref[...])
    vals, ids = [], []
    for c in range(2):
        kh, kl = _split_bf16(sk_ref[c])
        s = _dot_nt(kh, qh) + _dot_nt(kh, ql) + _dot_nt(kl, qh)
        v, p = _top16(s)
        vals.append(v)
        ids.append(p)
    v0, p0 = jnp.concatenate(vals[0], axis=0), jnp.concatenate(ids[0], axis=0)
    v1, p1 = jnp.concatenate(vals[1], axis=0), jnp.concatenate(ids[1], axis=0)

    a8 = lax.broadcasted_iota(I32, (SUBLANES, 1), 0)
    pieces = [(v0 + vals[1][0], p0 * PEER_NKEYS + ids[1][0], lax.broadcasted_iota(I32, (K, 1), 0) * K)]
    for b in range(1, SUBLANES):
        keep = a8 < K // (b + 1)
        pieces.append((jnp.where(keep, v0[:SUBLANES] + vals[1][b], -jnp.inf), p0[:SUBLANES] * PEER_NKEYS + ids[1][b],
                       a8 * K + b))
    pieces.append((vals[0][0] + v1[SUBLANES:], ids[0][0] * PEER_NKEYS + p1[SUBLANES:], a8 + SUBLANES))
    cand = jnp.concatenate([p[0] for p in pieces], axis=0)
    eid = jnp.concatenate([p[1] for p in pieces], axis=0)
    flat = jnp.concatenate([p[2] for p in pieces], axis=0)
    v, win = _top16(cand, flat)
    sel_ids = [jnp.sum(jnp.where(flat == w, eid, 0), axis=0, keepdims=True) for w in win]
    sc = jnp.concatenate(v, axis=0)
    e = jnp.exp(sc - sc[0:1])
    gate_ref[...] = e / jnp.sum(e, axis=0, keepdims=True)
    idx_ref[...] = jnp.concatenate(sel_ids, axis=0)


def _peer_route(pq, sk_pad):
    T = pq.shape[0]
    tt = 512
    return pl.pallas_call(
        _route_kernel,
        out_shape=(jax.ShapeDtypeStruct((PEER_HEADS * PEER_TOPK, T), I32),
                   jax.ShapeDtypeStruct((PEER_HEADS * PEER_TOPK, T), F32)),
        grid=(T // tt, PEER_HEADS),
        in_specs=[pl.BlockSpec((tt, LANES), lambda i, h: (i, h)),
                  pl.BlockSpec((2, PEER_NKEYS, LANES), lambda i, h: (0, 0, 0))],
        out_specs=(pl.BlockSpec((PEER_TOPK, tt), lambda i, h: (h, i)),
                   pl.BlockSpec((PEER_TOPK, tt), lambda i, h: (h, i))),
        compiler_params=pltpu.CompilerParams(dimension_semantics=("parallel", "parallel")),
        name="peer_route",
    )(pq, sk_pad)


PEER_NK = PEER_HEADS * PEER_TOPK
SUBLANES = 8


def _unpack_pair(w):
    lo = pltpu.bitcast(w << 16, F32)
    hi = pltpu.bitcast(w & jnp.uint32(0xFFFF0000), F32)
    return lo, hi


def _peer_kernel(idx_cur, idx_nxt, hn_ref, gate_ref, x1_ref, uv_hbm, o_ref, buf_a, buf_b, sem):
    i = pl.program_id(0)
    last = pl.num_programs(0) - 1
    half_rows = SUBLANES // 2
    bufs = (buf_a, buf_b)

    def tile_copy(idx_ref, s, c, k):
        p = c * PEER_NK + k
        return pltpu.make_async_copy(uv_hbm.at[idx_ref[p]], bufs[s].at[pl.ds(p * SUBLANES, SUBLANES), :], sem.at[s, c])

    def fetch_token(idx_ref, s, c):
        for k in range(PEER_NK):
            tile_copy(idx_ref, s, c, k).start(priority=k % 2)

    def wait_token(idx_ref, s, c):
        for k in range(PEER_NK):
            tile_copy(idx_ref, s, c, k).wait()

    @pl.when(i == 0)
    def _():
        lax.fori_loop(0, PEER_TOK, lambda c, carry: (fetch_token(idx_cur, 0, c), carry)[1], 0)

    eye = lax.broadcasted_iota(I32, (PEER_NK, PEER_NK), 0) == lax.broadcasted_iota(I32, (PEER_NK, PEER_NK), 1)

    def token(slot, c):
        wait_token(idx_cur, slot, c)
        fetch_token(idx_nxt, 1 - slot, c)
        base = c * (PEER_NK * SUBLANES)
        plane = lambda s: bufs[slot][pl.ds(base + s, PEER_NK, stride=SUBLANES), :]
        x = hn_ref[pl.ds(c, 1), :]
        xs = lambda s: x[:, s * LANES:(s + 1) * LANES]
        gcol = jnp.sum(jnp.where(eye, gate_ref[pl.ds(c, 1), :], 0.0), axis=-1, keepdims=True)
        acc = jnp.zeros((PEER_NK, LANES), F32)
        for s in range(half_rows):
            lo, hi = _unpack_pair(plane(s))
            acc = acc + lo * xs(s) + hi * xs(s + half_rows)
        act = jax.nn.gelu(jnp.sum(acc, axis=-1, keepdims=True)) * gcol
        lo_cols, hi_cols = [], []
        for s in range(half_rows, SUBLANES):
            lo, hi = _unpack_pair(plane(s))
            lo_cols.append(jnp.sum(act * lo, axis=0, keepdims=True))
            hi_cols.append(jnp.sum(act * hi, axis=0, keepdims=True))
        o_ref[pl.ds(c, 1), :] = x1_ref[pl.ds(c, 1), :] + jnp.concatenate(lo_cols + hi_cols, axis=-1)

    def step(slot):
        for c in range(PEER_TOK):
            token(slot, c)

        @pl.when(i == last)
        def _():
            lax.fori_loop(0, PEER_TOK, lambda c, carry: (wait_token(idx_nxt, 1 - slot, c), carry)[1], 0)

    pl.when(i % 2 == 0)(lambda: step(0))
    pl.when(i % 2 == 1)(lambda: step(1))


def _peer_expert(n_tokens, idx_flat, hn, gates, x1, uv_tiles):
    n = n_tokens // PEER_TOK
    row = pl.BlockSpec((PEER_TOK, D_MODEL), lambda i: (i, 0))
    fetch_buf = pltpu.VMEM((PEER_TOK * PEER_NK * SUBLANES, LANES), jnp.uint32)
    return pl.pallas_call(
        _peer_kernel,
        out_shape=jax.ShapeDtypeStruct((n_tokens, D_MODEL), F32),
        grid=(n,),
        in_specs=[pl.BlockSpec((PEER_TOK * PEER_NK,), lambda i: (i,), memory_space=pltpu.SMEM),
                  pl.BlockSpec((PEER_TOK * PEER_NK,), lambda i: (jnp.minimum(i + 1, n - 1),), memory_space=pltpu.SMEM),
                  row, pl.BlockSpec((PEER_TOK, PEER_NK), lambda i: (i, 0)), row,
                  pl.BlockSpec(memory_space=pl.ANY)],
        out_specs=row,
        scratch_shapes=[fetch_buf, fetch_buf, pltpu.SemaphoreType.DMA((2, PEER_TOK))],
        compiler_params=pltpu.CompilerParams(dimension_semantics=("arbitrary",), vmem_limit_bytes=48 * 1024 * 1024),
        name="peer_expert",
    )(idx_flat, idx_flat, hn, gates, x1, uv_tiles)


SC_LANES = 16
SC_WORKERS = 32
SC_GATHER = 32
SC_BLOCK = 8
SC_SHARE = (13, 32)


def _sc_lane_bcast(vec, k):
    idx = jnp.full((SC_LANES, 1), k, I32)
    dn = lax.GatherDimensionNumbers(offset_dims=(), collapsed_slice_dims=(0,), start_index_map=(0,))
    return lax.gather(vec, idx, dn, slice_sizes=(1,), mode=lax.GatherScatterMode.PROMISE_IN_BOUNDS)


def _sc_unpack_pair(w):
    return plsc.bitcast(w << 16, F32), plsc.bitcast(w & jnp.uint32(0xFFFF0000), F32)


def _peer_expert_sc(t_start, n, idx, gates, hn, x1, uv_rows):
    assert n % (SC_WORKERS * SC_BLOCK) == 0 and t_start % SC_BLOCK == 0
    tokens_per_worker = n // SC_WORKERS
    n_gather = PEER_NK // SC_GATHER
    half = D_MODEL // 2
    n_chunk = half // SC_LANES
    mesh = plsc.VectorSubcoreMesh(core_axis_name="c", subcore_axis_name="s")

    def body(idx_hbm, g_hbm, hn_hbm, x1_hbm, uv_hbm, out_hbm, idx_v, g_v, x_v, o_v, rows0, rows1, sem0, sem1):
        worker = lax.axis_index("s") * 2 + lax.axis_index("c")
        lane = lax.iota(I32, SC_LANES)
        rows, sems = (rows0, rows1), (sem0, sem1)

        def gather(ti, q):
            return pltpu.make_async_copy(uv_hbm.at[idx_v.at[ti, q]], rows[q % 2], sems[q % 2])

        def block(bi, carry):
            o0 = pl.multiple_of(worker * tokens_per_worker + bi * SC_BLOCK, SC_BLOCK)
            t0 = pl.multiple_of(t_start + o0, SC_BLOCK)
            pltpu.sync_copy(idx_hbm.at[pl.ds(t0, SC_BLOCK)], idx_v)
            pltpu.sync_copy(g_hbm.at[pl.ds(t0, SC_BLOCK)], g_v)
            pltpu.sync_copy(hn_hbm.at[pl.ds(t0, SC_BLOCK)], x_v)
            pltpu.sync_copy(x1_hbm.at[pl.ds(t0, SC_BLOCK)], o_v)
            gather(0, 0).start()

            def token(ti, carry1):
                for q in range(n_gather):
                    if q + 1 < n_gather:
                        gather(ti, q + 1).start()
                    else:
                        @pl.when(ti + 1 < SC_BLOCK)
                        def _():
                            gather(ti + 1, 0).start()
                    gather(ti, q).wait()
                    rows_v = rows[q % 2]

                    def dot_chunk(j, accs):
                        off = pl.multiple_of(j * SC_LANES, SC_LANES)
                        xlo = x_v[ti, pl.ds(off, SC_LANES)]
                        xhi = x_v[ti, pl.ds(half + off, SC_LANES)]
                        out = []
                        for k in range(SC_GATHER):
                            lo, hi = _sc_unpack_pair(rows_v[k, pl.ds(off, SC_LANES)])
                            out.append(accs[k] + lo * xlo + hi * xhi)
                        return tuple(out)

                    accs = lax.fori_loop(0, n_chunk, dot_chunk,
                                         tuple(jnp.zeros((SC_LANES,), F32) for _ in range(SC_GATHER)))
                    acts = []
                    for h in range(SC_GATHER // SC_LANES):
                        a = jnp.zeros((SC_LANES,), F32)
                        for kk in range(SC_LANES):
                            a = jnp.where(lane == kk, jnp.sum(accs[h * SC_LANES + kk]), a)
                        y = 0.7978845608028654 * (a + 0.044715 * a * a * a)
                        th = 1.0 - 2.0 / (jnp.exp(2.0 * y) + 1.0)
                        acts.append(0.5 * a * (1.0 + th) * g_v[ti, pl.ds(q * SC_GATHER + h * SC_LANES, SC_LANES)])
                    act_b = [_sc_lane_bcast(acts[k // SC_LANES], k % SC_LANES) for k in range(SC_GATHER)]

                    def mix_chunk(j, carry2):
                        off = pl.multiple_of(j * SC_LANES, SC_LANES)
                        al = jnp.zeros((SC_LANES,), F32)
                        ah = jnp.zeros((SC_LANES,), F32)
                        for k in range(SC_GATHER):
                            lo, hi = _sc_unpack_pair(rows_v[k, pl.ds(half + off, SC_LANES)])
                            al = al + act_b[k] * lo
                            ah = ah + act_b[k] * hi
                        o_v[ti, pl.ds(off, SC_LANES)] = o_v[ti, pl.ds(off, SC_LANES)] + al
                        o_v[ti, pl.ds(half + off, SC_LANES)] = o_v[ti, pl.ds(half + off, SC_LANES)] + ah
                        return carry2

                    lax.fori_loop(0, n_chunk, mix_chunk, 0)
                return carry1

            lax.fori_loop(0, SC_BLOCK, token, 0)
            pltpu.sync_copy(o_v, out_hbm.at[pl.ds(o0, SC_BLOCK)])
            return carry

        lax.fori_loop(0, tokens_per_worker // SC_BLOCK, block, 0)

    row_buf = pltpu.VMEM((SC_GATHER, D_MODEL), jnp.uint32)
    return pl.kernel(
        body, mesh=mesh,
        out_type=jax.ShapeDtypeStruct((n, D_MODEL), F32),
        scratch_types=[pltpu.VMEM((SC_BLOCK, n_gather, SC_GATHER), I32), pltpu.VMEM((SC_BLOCK, PEER_NK), F32),
                       pltpu.VMEM((SC_BLOCK, D_MODEL), F32), pltpu.VMEM((SC_BLOCK, D_MODEL), F32), row_buf, row_buf,
                       pltpu.SemaphoreType.DMA, pltpu.SemaphoreType.DMA],
        compiler_params=pltpu.CompilerParams(needs_layout_passes=False),
        name="peer_expert_sc",
    )(idx, gates, hn, x1, uv_rows)


def _rope_tables(pos):
    half = ROPE_DIMS // 2
    inv = ROPE_THETA ** (-(jnp.arange(half, dtype=F32) * 2.0 / ROPE_DIMS))
    ang = pos.astype(F32)[:, None] * inv[None, :]
    cos, sin = jnp.cos(ang), jnp.sin(ang)
    n = pos.shape[0]
    a = jnp.concatenate([cos, cos, jnp.ones((n, HEAD_DIM - ROPE_DIMS), F32)], axis=-1)
    b = jnp.concatenate([-sin, sin, jnp.zeros((n, HEAD_DIM - ROPE_DIMS), F32)], axis=-1)
    return jnp.tile(a, (1, 2)), jnp.tile(b, (1, 2))


def _pack_bf16_pairs(w):
    half = w.shape[1] // 2
    bits = lax.bitcast_convert_type(w.astype(BF16), jnp.uint16).astype(jnp.uint32)
    return bits[:, :half] | (bits[:, half:] << 16)


def _tile2(v):
    return jnp.tile(v.reshape(1, HEAD_DIM), (1, 2))


def kernel(x, norm1_g, w_in, nsa_q_norm, nsa_k_norm, cmp_pe_k, cmp_w1_k, cmp_w2_k, cmp_pe_v, cmp_w1_v, cmp_w2_v,
           dil_q_norm, dil_k_norm, w_up_nsa, w_up_dil, w_o, norm2_g, peer_wq, peer_subkeys, peer_u, peer_v):
    B, S, D = x.shape
    T = B * S
    assert D == D_MODEL and S % (DIL_PATTERNS[-1][1] * DIL_BLOCK) == 0 and S >= WIN + NSA_QB and T % 512 == 0
    x2 = x.reshape(T, D)

    n_q, n_kv, n_gate, n_dil = 512, 768, 24, 2304
    o_gate = n_q + n_kv
    o_dil = o_gate + n_gate
    o_mg = o_dil + n_dil
    w_perm = jnp.concatenate([w_in[:, o_mg:], w_in[:, :o_gate], w_in[:, o_dil:o_mg], w_in[:, o_gate:o_dil],
                              jnp.zeros((D, IN_COLS_PAD - w_in.shape[1]), w_in.dtype)], axis=1).astype(BF16)
    z2 = _in_proj(x2, norm1_g.reshape(1, D), w_perm)
    z3 = z2.reshape(B, S, IN_COLS_PAD)

    blockdiag = jnp.asarray(np.kron(np.eye(2), np.ones((HEAD_DIM, HEAD_DIM))), BF16)
    rope_a, rope_b = _rope_tables(jnp.arange(S))

    qn, ksd, vsd, kwd, vwd = _nsa_prep(z3, rope_a, rope_b, blockdiag, _tile2(nsa_q_norm), _tile2(nsa_k_norm[1]),
                                       _tile2(nsa_k_norm[2]))

    n_cmp = (S - CMP_BLOCK) // CMP_STRIDE + 1
    ncp = S // CMP_STRIDE
    def flat_blocks(col):
        zc = z3[:, :, col:col + LANES].reshape(B, S, 2, HEAD_DIM).transpose(0, 2, 1, 3)
        r = zc.reshape(B, 2, ncp, CMP_STRIDE * HEAD_DIM)
        nxt = jnp.concatenate([r[:, :, 1:], jnp.zeros_like(r[:, :, :1])], axis=2)
        return jnp.concatenate([r, nxt], axis=-1).reshape(B * 2 * ncp, CMP_BLOCK * HEAD_DIM)
    cmp_a, cmp_b = _rope_tables(jnp.arange(ncp) * CMP_STRIDE + CMP_BLOCK - 1)
    dup2 = lambda w: jnp.concatenate([w, w], axis=1).astype(BF16)
    kcd, vcd = _compress(flat_blocks(COL_KV), flat_blocks(COL_KV + LANES),
                         cmp_pe_k.reshape(1, -1), cmp_pe_v.reshape(1, -1), cmp_w1_k.astype(BF16), cmp_w1_v.astype(BF16),
                         dup2(cmp_w2_k), dup2(cmp_w2_v), cmp_a, cmp_b, blockdiag, _tile2(nsa_k_norm[0]))
    kcd = kcd.reshape(B, 2, ncp, LANES)
    vcd = vcd.reshape(B, 2, ncp, LANES)

    n_slc = S // SLC_BLOCK
    s0 = np.arange(n_cmp) * CMP_STRIDE
    b0 = np.arange(n_slc) * SLC_BLOCK
    ov = np.clip(np.minimum(s0[:, None] + CMP_BLOCK, b0[None, :] + SLC_BLOCK) - np.maximum(s0[:, None], b0[None, :]),
                 0, None) / CMP_BLOCK
    ov_pad = np.zeros((ncp, LANES), np.float32)
    ov_pad[:n_cmp, :n_slc] = ov
    y_nsa = _nsa_attn(qn, kcd, vcd, ksd, vsd, kwd, vwd, z3, jnp.asarray(ov_pad, BF16))

    gq = jnp.tile(dil_q_norm.reshape(3, 1, HEAD_DIM), (1, 1, 2))
    gk = jnp.tile(dil_k_norm.reshape(3, 1, HEAD_DIM), (1, 1, 2))
    prep = _dil_prep(z3, rope_a, rope_b, blockdiag, gq, gk)
    flat = [p.reshape(B, S, 256) for p in prep]
    y_dil = _dil_attn(flat[0::3], flat[1::3], flat[2::3])

    x1, hn, pq = _merge(x2, y_nsa.reshape(T, 512), y_dil.reshape(T, 256), z2, w_up_nsa.astype(BF16),
                        w_up_dil.astype(BF16), w_o.astype(BF16), norm2_g.reshape(1, D), peer_wq.astype(BF16))

    sub = PEER_NKEYS // 2
    sk_pad = jnp.stack([jnp.pad(peer_subkeys[0], ((0, 0), (0, sub))), jnp.pad(peer_subkeys[1], ((0, 0), (sub, 0)))])
    idx_t, gate_t = _peer_route(pq, sk_pad)
    uv_rows = jnp.concatenate([_pack_bf16_pairs(peer_u), _pack_bf16_pairs(peer_v)], axis=1)
    idx, gates = idx_t.T, gate_t.T
    sc_unit = SC_WORKERS * SC_BLOCK
    t_tc = T - (T * SC_SHARE[0] // SC_SHARE[1]) // sc_unit * sc_unit
    assert t_tc % PEER_TOK == 0
    out_tc = _peer_expert(t_tc, idx.reshape(-1), hn, gates, x1, uv_rows.reshape(-1, SUBLANES, LANES))
    out_sc = _peer_expert_sc(t_tc, T - t_tc, idx.reshape(T, PEER_NK // SC_GATHER, SC_GATHER), gates, hn, x1, uv_rows)
    return jnp.concatenate([out_tc, out_sc], axis=0).reshape(B, S, D)
```

```python
import functools

import numpy as np
import jax
import jax.numpy as jnp
from jax import lax
from jax.experimental import pallas as pl
from jax.experimental.pallas import tpu as pltpu
from jax.experimental.pallas import tpu_sc as plsc

F32 = jnp.float32
BF16 = jnp.bfloat16
I32 = jnp.int32

D_MODEL = 1024
HEAD_DIM = 64
ROPE_DIMS = 16
ROPE_THETA = 500000.0
NORM_EPS = 1e-6
NEG_INF = -1e30
LANES = 128

NSA_HEADS = 8
CMP_BLOCK = 32
CMP_STRIDE = 16
CMP_HIDDEN = 256
SLC_BLOCK = 64
SLC_TOPN = 16
FORCE_SCORE = 1e3
WIN = 512
NSA_QB = 128
SEL_CHUNK = 512

DIL_PATTERNS = ((128, 1), (512, 4), (2048, 16))
DIL_BLOCK = 128
DIL_PREP_ROWS = 1024

PEER_HEADS = 8
PEER_NKEYS = 128
PEER_TOPK = 16
PEER_TOK = 16

COL_MG = 0
COL_Q = 2048
COL_KV = 2560
COL_DIL = 3328
COL_GATE = 5632
IN_COLS_PAD = 5760

_NT = (((1,), (1,)), ((), ()))


def _dot(a, b):
    return jnp.dot(a, b, preferred_element_type=F32)


def _dot_nt(a, b):
    return lax.dot_general(a, b, _NT, preferred_element_type=F32)


def _split_bf16(a):
    hi = a.astype(BF16)
    lo = (a - hi.astype(F32)).astype(BF16)
    return hi, lo


def _dot_hilo(a, b_bf16):
    hi, lo = _split_bf16(a)
    return _dot(hi, b_bf16) + _dot(lo, b_bf16)


def _head_norm_rope(zt, gain, rope_a, rope_b, blockdiag):
    ss = _dot_hilo(zt * zt, blockdiag)
    zn = zt * lax.rsqrt(ss * (1.0 / HEAD_DIM) + NORM_EPS) * gain
    d = lax.broadcasted_iota(I32, zn.shape, 1) & (HEAD_DIM - 1)
    half = ROPE_DIMS // 2
    partner = jnp.where(d < half, pltpu.roll(zn, LANES - half, 1), pltpu.roll(zn, half, 1))
    return zn * rope_a + partner * rope_b


def _half_masks():
    lane = lax.broadcasted_iota(I32, (1, LANES), 1)
    lo = (lane < HEAD_DIM).astype(BF16)
    return lo, (1 - lo).astype(BF16)


def _inproj_kernel(x_ref, g_ref, w_ref, o_ref, h_scr):
    @pl.when(pl.program_id(1) == 0)
    def _():
        xf = x_ref[...]
        ms = jnp.mean(xf * xf, axis=-1, keepdims=True)
        h_scr[...] = (xf * lax.rsqrt(ms + NORM_EPS) * g_ref[...]).astype(BF16)

    o_ref[...] = _dot(h_scr[...], w_ref[...])


def _in_proj(x2, g1, w_bf16):
    T = x2.shape[0]
    tm, tn = 1024, 640
    return pl.pallas_call(
        _inproj_kernel,
        out_shape=jax.ShapeDtypeStruct((T, IN_COLS_PAD), F32),
        grid=(T // tm, IN_COLS_PAD // tn),
        in_specs=[
            pl.BlockSpec((tm, D_MODEL), lambda i, j: (i, 0)),
            pl.BlockSpec((1, D_MODEL), lambda i, j: (0, 0)),
            pl.BlockSpec((D_MODEL, tn), lambda i, j: (0, j)),
        ],
        out_specs=pl.BlockSpec((tm, tn), lambda i, j: (i, j)),
        scratch_shapes=[pltpu.VMEM((tm, D_MODEL), BF16)],
        compiler_params=pltpu.CompilerParams(dimension_semantics=("parallel", "arbitrary")),
        name="in_proj",
    )(x2, g1, w_bf16)


def _nsa_prep_kernel(zq_ref, zks_ref, zvs_ref, zkw_ref, zvw_ref, ra_ref, rb_ref, bd_ref, gq_ref, gks_ref, gkw_ref,
                     q_ref, ks_ref, vs_ref, kw_ref, vw_ref):
    ra, rb, bd = ra_ref[...], rb_ref[...], bd_ref[...]
    lane = lax.broadcasted_iota(I32, ra.shape, 1)
    scale = HEAD_DIM ** -0.5

    zq = zq_ref[0]
    tiles = [_head_norm_rope(zq[:, m * LANES:(m + 1) * LANES], gq_ref[...], ra, rb, bd) * scale for m in range(4)]
    q_ref[0] = jnp.concatenate(tiles, axis=-1).astype(BF16)

    def dup(t, out_ref):
        sw = pltpu.roll(t, HEAD_DIM, 1)
        out_ref[0, 0] = jnp.where(lane < HEAD_DIM, t, sw).astype(BF16)
        out_ref[0, 1] = jnp.where(lane < HEAD_DIM, sw, t).astype(BF16)

    dup(_head_norm_rope(zks_ref[0], gks_ref[...], ra, rb, bd), ks_ref)
    dup(_head_norm_rope(zkw_ref[0], gkw_ref[...], ra, rb, bd), kw_ref)
    dup(zvs_ref[0], vs_ref)
    dup(zvw_ref[0], vw_ref)


def _nsa_prep(z3, rope_a, rope_b, blockdiag, gq, gks, gkw):
    B, S, _ = z3.shape
    tm = 512
    kvb = COL_KV // LANES
    zcol = lambda c: pl.BlockSpec((1, tm, LANES), lambda b, i, c=c: (b, i, c))
    const = lambda shape: pl.BlockSpec(shape, lambda b, i: tuple(0 for _ in shape))
    kv_out = pl.BlockSpec((1, 2, tm, LANES), lambda b, i: (b, 0, i, 0))
    kv_shape = jax.ShapeDtypeStruct((B, 2, S, LANES), BF16)
    return pl.pallas_call(
        _nsa_prep_kernel,
        out_shape=(jax.ShapeDtypeStruct((B, S, 512), BF16), kv_shape, kv_shape, kv_shape, kv_shape),
        grid=(B, S // tm),
        in_specs=[
            pl.BlockSpec((1, tm, 512), lambda b, i: (b, i, COL_Q // 512)),
            zcol(kvb + 2), zcol(kvb + 3), zcol(kvb + 4), zcol(kvb + 5),
            pl.BlockSpec((tm, LANES), lambda b, i: (i, 0)),
            pl.BlockSpec((tm, LANES), lambda b, i: (i, 0)),
            const((LANES, LANES)), const((1, LANES)), const((1, LANES)), const((1, LANES)),
        ],
        out_specs=(pl.BlockSpec((1, tm, 512), lambda b, i: (b, i, 0)), kv_out, kv_out, kv_out, kv_out),
        compiler_params=pltpu.CompilerParams(dimension_semantics=("parallel", "parallel")),
        name="nsa_prep",
    )(z3, z3, z3, z3, z3, rope_a, rope_b, blockdiag, gq, gks, gkw)


def _compress_kernel(fk_ref, fv_ref, pek_ref, pev_ref, w1k_ref, w1v_ref, w2k_ref, w2v_ref, ra_ref, rb_ref, bd_ref, gk_ref,
                     k_ref, v_ref):
    def mlp(f_ref, pe_ref, w1_ref, w2_ref):
        f = (f_ref[...] + pe_ref[...]).astype(BF16)
        h = jax.nn.gelu(_dot(f, w1_ref[...]))
        return _dot(h.astype(BF16), w2_ref[...])

    kc = mlp(fk_ref, pek_ref, w1k_ref, w2k_ref)
    k_ref[...] = _head_norm_rope(kc, gk_ref[...], ra_ref[...], rb_ref[...], bd_ref[...]).astype(BF16)
    v_ref[...] = mlp(fv_ref, pev_ref, w1v_ref, w2v_ref).astype(BF16)


def _compress(flat_k, flat_v, pek, pev, w1k, w1v, w2k, w2v, rope_a, rope_b, blockdiag, gk):
    rows, width = flat_k.shape
    nblk = rope_a.shape[0]
    row = pl.BlockSpec((nblk, width), lambda i: (i, 0))
    const = lambda shape: pl.BlockSpec(shape, lambda i: tuple(0 for _ in shape))
    out = pl.BlockSpec((nblk, LANES), lambda i: (i, 0))
    shp = jax.ShapeDtypeStruct((rows, LANES), BF16)
    return pl.pallas_call(
        _compress_kernel,
        out_shape=(shp, shp),
        grid=(rows // nblk,),
        in_specs=[row, row, const((1, width)), const((1, width)), const((width, CMP_HIDDEN)), const((width, CMP_HIDDEN)),
                  const((CMP_HIDDEN, LANES)), const((CMP_HIDDEN, LANES)), const((nblk, LANES)), const((nblk, LANES)),
                  const((LANES, LANES)), const((1, LANES))],
        out_specs=(out, out),
        compiler_params=pltpu.CompilerParams(dimension_semantics=("parallel",)),
        name="compress",
    )(flat_k, flat_v, pek, pev, w1k, w1v, w2k, w2v, rope_a, rope_b, blockdiag, gk)


def _softmax_rows(s):
    m = jnp.max(s, axis=-1, keepdims=True)
    e = jnp.exp(s - m)
    return e / jnp.sum(e, axis=-1, keepdims=True)


def _nsa_kernel(q_ref, kc_ref, vc_ref, ks_ref, vs_ref, kw_ref, vw_ref, gl_ref, ov_ref, ex_ref, y_ref):
    n = pl.program_id(1)
    t0 = n * NSA_QB
    qt = q_ref[0]
    hm = _half_masks()
    lane = lax.broadcasted_iota(I32, (NSA_QB, LANES), 1)
    t1 = t0 + lax.broadcasted_iota(I32, (NSA_QB, 1), 0)
    t4 = t0 + (lax.broadcasted_iota(I32, (4 * NSA_QB, 1), 0) & (NSA_QB - 1))
    gates = jax.nn.sigmoid(gl_ref[0])

    n_slc_blocks = ks_ref.shape[2] // SLC_BLOCK
    blk = lax.broadcasted_iota(I32, (n_slc_blocks, NSA_QB), 0)
    tq = t0 + lax.broadcasted_iota(I32, (n_slc_blocks, NSA_QB), 1)
    cur = tq >> 6
    forced = (blk == 0) | (blk == cur) | (blk == cur - 1)
    causal_b = blk * SLC_BLOCK <= tq

    tiles_out = []
    for g in range(2):
        q4 = jnp.concatenate(
            [qt[:, (2 * g + jj // 2) * LANES:(2 * g + jj // 2 + 1) * LANES] * hm[jj % 2] for jj in range(4)], axis=0)

        sc = _dot_nt(q4, kc_ref[0, g])
        cend = lax.broadcasted_iota(I32, (1, sc.shape[1]), 1) * CMP_STRIDE + (CMP_BLOCK - 1)
        valid_c = cend <= t4
        pc = _softmax_rows(jnp.where(valid_c, sc, NEG_INF))
        pc = jnp.where(valid_c, pc, 0.0)
        o_c = _dot(pc.astype(BF16), vc_ref[0, g])
        psum = pc[0:NSA_QB] + pc[NSA_QB:2 * NSA_QB] + pc[2 * NSA_QB:3 * NSA_QB] + pc[3 * NSA_QB:]
        imp = _dot_hilo(psum, ov_ref[...])

        score = jnp.where(forced, FORCE_SCORE, jnp.where(causal_b, imp.T[:n_slc_blocks], -1.0))
        n_grp = n_slc_blocks // SUBLANES
        grp = [score[SUBLANES * r:SUBLANES * (r + 1)] for r in range(n_grp)]
        ranks = [jnp.zeros((SUBLANES, NSA_QB), F32) for _ in range(n_grp)]
        row = lax.broadcasted_iota(I32, (SUBLANES, NSA_QB), 0)
        for i in range(n_slc_blocks):
            ri = grp[i // SUBLANES][i % SUBLANES:i % SUBLANES + 1, :]
            for r in range(n_grp):
                if r > i // SUBLANES:
                    ahead = ri >= grp[r]
                elif r < i // SUBLANES:
                    ahead = ri > grp[r]
                else:
                    ahead = (ri > grp[r]) | ((ri == grp[r]) & (row > i % SUBLANES))
                ranks[r] = ranks[r] + jnp.where(ahead, 1.0, 0.0)
        sel64 = jnp.where(jnp.concatenate(ranks, axis=0) < float(SLC_TOPN), 1.0, 0.0)
        sel = jnp.concatenate([sel64, jnp.zeros((LANES - n_slc_blocks, NSA_QB), F32)], axis=0).T.astype(BF16)

        def sel_chunk(c, carry, last):
            m_i, l_i, acc = carry
            k0 = pl.multiple_of(c * SEL_CHUNK, SEL_CHUNK)
            kch = ks_ref[0, g, pl.ds(k0, SEL_CHUNK), :]
            vch = vs_ref[0, g, pl.ds(k0, SEL_CHUNK), :]
            picked = _dot(sel, ex_ref[:, pl.ds(k0, SEL_CHUNK)])
            if last:
                kpos = k0 + lax.broadcasted_iota(I32, (1, SEL_CHUNK), 1)
                picked = jnp.where(kpos <= t1, picked, 0.0)
            bias = (picked - 1.0) * (-NEG_INF)
            s = (_dot_nt(q4, kch).reshape(4, NSA_QB, SEL_CHUNK) + bias[None]).reshape(4 * NSA_QB, SEL_CHUNK)
            m_new = jnp.maximum(m_i, jnp.max(s, axis=-1, keepdims=True))
            alpha = jnp.exp(m_i - m_new)
            p = jnp.exp(s - m_new)
            l_new = alpha * l_i + jnp.sum(p, axis=-1, keepdims=True)
            acc_new = alpha * acc + _dot(p.astype(BF16), vch)
            return m_new, l_new, acc_new

        init = (jnp.full((4 * NSA_QB, 1), NEG_INF, F32), jnp.zeros((4 * NSA_QB, 1), F32),
                jnp.zeros((4 * NSA_QB, LANES), F32))
        n_full = t0 // SEL_CHUNK
        carry = lax.fori_loop(0, n_full, functools.partial(sel_chunk, last=False), init)
        _, l_s, acc_s = sel_chunk(n_full, carry, last=True)
        o_s = acc_s / l_s

        wlen = WIN + NSA_QB
        ws = pl.multiple_of(jnp.maximum(t0 - WIN, 0), NSA_QB)
        sw = _dot_nt(q4, kw_ref[0, g, pl.ds(ws, wlen), :])
        dist = t4 - (ws + lax.broadcasted_iota(I32, (1, wlen), 1))
        pw = _softmax_rows(jnp.where((dist >= 0) & (dist < WIN), sw, NEG_INF))
        o_w = _dot(pw.astype(BF16), vw_ref[0, g, pl.ds(ws, wlen), :])

        heads = []
        for jj in range(4):
            h = 4 * g + jj
            rows = slice(jj * NSA_QB, (jj + 1) * NSA_QB)
            heads.append(gates[:, 3 * h:3 * h + 1] * o_c[rows] + gates[:, 3 * h + 1:3 * h + 2] * o_s[rows]
                         + gates[:, 3 * h + 2:3 * h + 3] * o_w[rows])
        tiles_out.append(jnp.where(lane < HEAD_DIM, heads[0], heads[1]))
        tiles_out.append(jnp.where(lane < HEAD_DIM, heads[2], heads[3]))

    y_ref[0] = jnp.concatenate(tiles_out, axis=-1).astype(BF16)


def _nsa_attn(qn, kcd, vcd, ksd, vsd, kwd, vwd, z3, overlap):
    B, S, _ = qn.shape
    ncp = kcd.shape[2]
    assert S // SLC_BLOCK <= LANES
    expand = jnp.asarray(np.arange(LANES)[:, None] == (np.arange(S)[None, :] // SLC_BLOCK), BF16)
    full = lambda rows: pl.BlockSpec((1, 2, rows, LANES), lambda b, n: (b, 0, 0, 0))
    return pl.pallas_call(
        _nsa_kernel,
        out_shape=jax.ShapeDtypeStruct((B, S, 512), BF16),
        grid=(B, S // NSA_QB),
        in_specs=[
            pl.BlockSpec((1, NSA_QB, 512), lambda b, n: (b, n, 0)),
            full(ncp), full(ncp), full(S), full(S), full(S), full(S),
            pl.BlockSpec((1, NSA_QB, LANES), lambda b, n: (b, n, COL_GATE // LANES)),
            pl.BlockSpec((ncp, LANES), lambda b, n: (0, 0)),
            pl.BlockSpec((LANES, S), lambda b, n: (0, 0)),
        ],
        out_specs=pl.BlockSpec((1, NSA_QB, 512), lambda b, n: (b, n, 0)),
        compiler_params=pltpu.CompilerParams(dimension_semantics=("parallel", "arbitrary"),
                                             vmem_limit_bytes=48 * 1024 * 1024),
        name="nsa_attn",
    )(qn, kcd, vcd, ksd, vsd, kwd, vwd, z3, overlap, expand)


def _dil_prep_kernel(*refs):
    zs, (ra_ref, rb_ref, bd_ref, gq_ref, gk_ref), outs = refs[0:18], refs[18:23], refs[23:32]
    bd = bd_ref[...]
    scale = HEAD_DIM ** -0.5
    for g, (_, d) in enumerate(DIL_PATTERNS):
        n = DIL_PREP_ROWS // d
        for r in range(d):
            rows = pl.ds(r, n, stride=d) if d > 1 else pl.ds(0, n)
            ra, rb = ra_ref[rows, :], rb_ref[rows, :]
            for which in range(3):
                for m in range(2):
                    z = zs[2 * (3 * g + which) + m][0, rows, :]
                    if which == 0:
                        z = _head_norm_rope(z, gq_ref[g], ra, rb, bd) * scale
                    elif which == 1:
                        z = _head_norm_rope(z, gk_ref[g], ra, rb, bd)
                    outs[3 * g + which][0, r, :, m * LANES:(m + 1) * LANES] = z.astype(BF16)


def _dil_prep(z3, rope_a, rope_b, blockdiag, gq, gk):
    B, S, _ = z3.shape
    nsteps = S // DIL_PREP_ROWS
    c0 = COL_DIL // LANES
    in_specs = [pl.BlockSpec((1, DIL_PREP_ROWS, LANES), lambda b, c, k=k: (b, c, c0 + k)) for k in range(18)]
    in_specs += [
        pl.BlockSpec((DIL_PREP_ROWS, LANES), lambda b, c: (c, 0)),
        pl.BlockSpec((DIL_PREP_ROWS, LANES), lambda b, c: (c, 0)),
        pl.BlockSpec((LANES, LANES), lambda b, c: (0, 0)),
        pl.BlockSpec((3, 1, LANES), lambda b, c: (0, 0, 0)),
        pl.BlockSpec((3, 1, LANES), lambda b, c: (0, 0, 0)),
    ]
    out_shape, out_specs = [], []
    for _, d in DIL_PATTERNS:
        for _ in range(3):
            out_shape.append(jax.ShapeDtypeStruct((B, d, S // d, 256), BF16))
            out_specs.append(pl.BlockSpec((1, d, DIL_PREP_ROWS // d, 256), lambda b, c: (b, 0, c, 0)))
    return pl.pallas_call(
        _dil_prep_kernel,
        out_shape=tuple(out_shape),
        grid=(B, nsteps),
        in_specs=in_specs,
        out_specs=tuple(out_specs),
        compiler_params=pltpu.CompilerParams(dimension_semantics=("parallel", "parallel"),
                                             vmem_limit_bytes=48 * 1024 * 1024),
        name="dil_prep",
    )(*([z3] * 18), rope_a, rope_b, blockdiag, gq, gk)


def _dil_kernel(*refs, seq):
    q_refs, k_refs, v_refs, y_ref, o_scr, l_scr = refs[0:3], refs[3:6], refs[6:9], refs[9], refs[10], refs[11]
    hm = _half_masks()
    lane = lax.broadcasted_iota(I32, (DIL_BLOCK, LANES), 1)
    qi = lax.broadcasted_iota(I32, (2 * DIL_BLOCK, 2 * DIL_BLOCK), 0) & (DIL_BLOCK - 1)
    ki = lax.broadcasted_iota(I32, (2 * DIL_BLOCK, 2 * DIL_BLOCK), 1)
    causal = (ki - DIL_BLOCK) <= qi

    for g, (_, d) in enumerate(DIL_PATTERNS):
        nb = seq // d // DIL_BLOCK

        def body(u, carry, g=g, d=d, nb=nb):
            j = u % nb
            r = u // nb
            r0 = pl.multiple_of(u * DIL_BLOCK, DIL_BLOCK)
            p0 = pl.multiple_of(jnp.maximum(u - 1, 0) * DIL_BLOCK, DIL_BLOCK)
            q = q_refs[g][0, pl.ds(r0, DIL_BLOCK), :]
            kcat = jnp.concatenate([k_refs[g][0, pl.ds(p0, DIL_BLOCK), :], k_refs[g][0, pl.ds(r0, DIL_BLOCK), :]], axis=0)
            vcat = jnp.concatenate([v_refs[g][0, pl.ds(p0, DIL_BLOCK), :], v_refs[g][0, pl.ds(r0, DIL_BLOCK), :]], axis=0)
            q2 = jnp.concatenate([q * hm[0], q * hm[1]], axis=0)
            s = _dot_nt(q2, kcat)
            first_key = jnp.maximum(qi, jnp.where(j >= 1, 0, DIL_BLOCK))
            s = jnp.where(causal & (ki >= first_key), s, NEG_INF)
            m = jnp.max(s, axis=-1, keepdims=True)
            e = jnp.exp(s - m)
            den = jnp.sum(e, axis=-1, keepdims=True)
            o2 = _dot(e.astype(BF16), vcat) / den
            lse = m + jnp.log(den)
            o = jnp.where(lane < HEAD_DIM, o2[:DIL_BLOCK], o2[DIL_BLOCK:])
            lv = jnp.where(lane < HEAD_DIM, lse[:DIL_BLOCK], lse[DIL_BLOCK:])
            tok0 = j * (DIL_BLOCK * d) + r
            rows = pl.ds(tok0, DIL_BLOCK, stride=d) if d > 1 else pl.ds(pl.multiple_of(tok0, DIL_BLOCK), DIL_BLOCK)
            o_scr[g, rows, :] = o
            l_scr[g, rows, :] = lv
            return carry

        lax.fori_loop(0, seq // DIL_BLOCK, body, 0)

    def merge(c, carry):
        rows = pl.ds(pl.multiple_of(c * 512, 512), 512)
        ls = [l_scr[g, rows, :] for g in range(3)]
        mx = jnp.maximum(jnp.maximum(ls[0], ls[1]), ls[2])
        ws = [jnp.exp(l - mx) for l in ls]
        num = ws[0] * o_scr[0, rows, :] + ws[1] * o_scr[1, rows, :] + ws[2] * o_scr[2, rows, :]
        y_ref[0, rows, :] = (num / (ws[0] + ws[1] + ws[2])).astype(BF16)
        return carry

    lax.fori_loop(0, seq // 512, merge, 0)


def _dil_attn(dq, dk, dv):
    B, S, _ = dq[0].shape
    spec = pl.BlockSpec((1, S, LANES), lambda b, m: (b, 0, m))
    return pl.pallas_call(
        functools.partial(_dil_kernel, seq=S),
        out_shape=jax.ShapeDtypeStruct((B, S, 256), BF16),
        grid=(B, 2),
        in_specs=[spec] * 9,
        out_specs=spec,
        scratch_shapes=[pltpu.VMEM((3, S, LANES), F32), pltpu.VMEM((3, S, LANES), F32)],
        compiler_params=pltpu.CompilerParams(dimension_semantics=("parallel", "parallel"),
                                             vmem_limit_bytes=56 * 1024 * 1024),
        name="dil_attn",
    )(*dq, *dk, *dv)


def _merge_kernel(x_ref, yn_ref, yd_ref, mg0_ref, mg1_ref, wn_ref, wd_ref, wo_ref, g2_ref, wq_ref, x1_ref, hn_ref, pq_ref):
    u1 = _dot(yn_ref[...], wn_ref[...])
    u2 = _dot(yd_ref[...], wd_ref[...])
    merged = jax.nn.sigmoid(mg0_ref[...]) * u1 + jax.nn.sigmoid(mg1_ref[...]) * u2
    x1 = x_ref[...] + _dot(merged.astype(BF16), wo_ref[...])
    x1_ref[...] = x1
    ms = jnp.mean(x1 * x1, axis=-1, keepdims=True)
    hn = x1 * lax.rsqrt(ms + NORM_EPS) * g2_ref[...]
    hn_ref[...] = hn
    pq_ref[...] = _dot(hn.astype(BF16), wq_ref[...])


def _merge(x2, yn2, yd2, z2, wn, wd, wo, g2, wq):
    T = x2.shape[0]
    tm = 512
    row = lambda w, c=0: pl.BlockSpec((tm, w), lambda i, c=c: (i, c))
    const = lambda shape: pl.BlockSpec(shape, lambda i: (0, 0))
    shp = jax.ShapeDtypeStruct((T, D_MODEL), F32)
    return pl.pallas_call(
        _merge_kernel,
        out_shape=(shp, shp, shp),
        grid=(T // tm,),
        in_specs=[row(D_MODEL), row(512), row(256), row(D_MODEL, COL_MG // D_MODEL), row(D_MODEL, COL_MG // D_MODEL + 1),
                  const((512, D_MODEL)), const((256, D_MODEL)), const((D_MODEL, D_MODEL)), const((1, D_MODEL)),
                  const((D_MODEL, D_MODEL))],
        out_specs=(row(D_MODEL), row(D_MODEL), row(D_MODEL)),
        compiler_params=pltpu.CompilerParams(dimension_semantics=("parallel",), vmem_limit_bytes=48 * 1024 * 1024),
        name="merge",
    )(x2, yn2, yd2, z2, z2, wn, wd, wo, g2, wq)


def _top16(s, rank_id=None):
    if rank_id is None:
        rank_id = lax.broadcasted_iota(I32, (s.shape[0], 1), 0)
    big = jnp.iinfo(jnp.int32).max
    vals, ids = [], []
    for _ in range(PEER_TOPK):
        m = jnp.max(s, axis=0, keepdims=True)
        win = jnp.min(jnp.where(s == m, rank_id, big), axis=0, keepdims=True)
        vals.append(m)
        ids.append(win)
        s = jnp.where(rank_id == win, -jnp.inf, s)
    return vals, ids


def _route_kernel(q_ref, sk_ref, idx_ref, gate_ref):
    K = PEER_TOPK
    qh, ql = _split_bf16(q_ref[...])
    vals, ids = [], []
    for c in range(2):
        kh, kl = _split_bf16(sk_ref[c])
        s = _dot_nt(kh, qh) + _dot_nt(kh, ql) + _dot_nt(kl, qh)
        v, p = _top16(s)
        vals.append(v)
        ids.append(p)
    v0, p0 = jnp.concatenate(vals[0], axis=0), jnp.concatenate(ids[0], axis=0)
    v1, p1 = jnp.concatenate(vals[1], axis=0), jnp.concatenate(ids[1], axis=0)

    a8 = lax.broadcasted_iota(I32, (SUBLANES, 1), 0)
    pieces = [(v0 + vals[1][0], p0 * PEER_NKEYS + ids[1][0], lax.broadcasted_iota(I32, (K, 1), 0) * K)]
    for b in range(1, SUBLANES):
        keep = a8 < K // (b + 1)
        pieces.append((jnp.where(keep, v0[:SUBLANES] + vals[1][b], -jnp.inf), p0[:SUBLANES] * PEER_NKEYS + ids[1][b],
                       a8 * K + b))
    pieces.append((vals[0][0] + v1[SUBLANES:], ids[0][0] * PEER_NKEYS + p1[SUBLANES:], a8 + SUBLANES))
    cand = jnp.concatenate([p[0] for p in pieces], axis=0)
    eid = jnp.concatenate([p[1] for p in pieces], axis=0)
    flat = jnp.concatenate([p[2] for p in pieces], axis=0)
    v, win = _top16(cand, flat)
    sel_ids = [jnp.sum(jnp.where(flat == w, eid, 0), axis=0, keepdims=True) for w in win]
    sc = jnp.concatenate(v, axis=0)
    e = jnp.exp(sc - sc[0:1])
    gate_ref[...] = e / jnp.sum(e, axis=0, keepdims=True)
    idx_ref[...] = jnp.concatenate(sel_ids, axis=0)


def _peer_route(pq, sk_pad):
    T = pq.shape[0]
    tt = 512
    return pl.pallas_call(
        _route_kernel,
        out_shape=(jax.ShapeDtypeStruct((PEER_HEADS * PEER_TOPK, T), I32),
                   jax.ShapeDtypeStruct((PEER_HEADS * PEER_TOPK, T), F32)),
        grid=(T // tt, PEER_HEADS),
        in_specs=[pl.BlockSpec((tt, LANES), lambda i, h: (i, h)),
                  pl.BlockSpec((2, PEER_NKEYS, LANES), lambda i, h: (0, 0, 0))],
        out_specs=(pl.BlockSpec((PEER_TOPK, tt), lambda i, h: (h, i)),
                   pl.BlockSpec((PEER_TOPK, tt), lambda i, h: (h, i))),
        compiler_params=pltpu.CompilerParams(dimension_semantics=("parallel", "parallel")),
        name="peer_route",
    )(pq, sk_pad)


PEER_NK = PEER_HEADS * PEER_TOPK
SUBLANES = 8


def _unpack_pair(w):
    lo = pltpu.bitcast(w << 16, F32)
    hi = pltpu.bitcast(w & jnp.uint32(0xFFFF0000), F32)
    return lo, hi


def _peer_kernel(idx_cur, idx_nxt, hn_ref, gate_ref, x1_ref, uv_hbm, o_ref, buf_a, buf_b, sem):
    i = pl.program_id(0)
    last = pl.num_programs(0) - 1
    half_rows = SUBLANES // 2
    bufs = (buf_a, buf_b)

    def tile_copy(idx_ref, s, c, k):
        p = c * PEER_NK + k
        return pltpu.make_async_copy(uv_hbm.at[idx_ref[p]], bufs[s].at[pl.ds(p * SUBLANES, SUBLANES), :], sem.at[s, c])

    def fetch_token(idx_ref, s, c):
        for k in range(PEER_NK):
            tile_copy(idx_ref, s, c, k).start(priority=k % 2)

    def wait_token(idx_ref, s, c):
        for k in range(PEER_NK):
            tile_copy(idx_ref, s, c, k).wait()

    @pl.when(i == 0)
    def _():
        lax.fori_loop(0, PEER_TOK, lambda c, carry: (fetch_token(idx_cur, 0, c), carry)[1], 0)

    eye = lax.broadcasted_iota(I32, (PEER_NK, PEER_NK), 0) == lax.broadcasted_iota(I32, (PEER_NK, PEER_NK), 1)

    def token(slot, c):
        wait_token(idx_cur, slot, c)
        fetch_token(idx_nxt, 1 - slot, c)
        base = c * (PEER_NK * SUBLANES)
        plane = lambda s: bufs[slot][pl.ds(base + s, PEER_NK, stride=SUBLANES), :]
        x = hn_ref[pl.ds(c, 1), :]
        xs = lambda s: x[:, s * LANES:(s + 1) * LANES]
        gcol = jnp.sum(jnp.where(eye, gate_ref[pl.ds(c, 1), :], 0.0), axis=-1, keepdims=True)
        acc = jnp.zeros((PEER_NK, LANES), F32)
        for s in range(half_rows):
            lo, hi = _unpack_pair(plane(s))
            acc = acc + lo * xs(s) + hi * xs(s + half_rows)
        act = jax.nn.gelu(jnp.sum(acc, axis=-1, keepdims=True)) * gcol
        lo_cols, hi_cols = [], []
        for s in range(half_rows, SUBLANES):
            lo, hi = _unpack_pair(plane(s))
            lo_cols.append(jnp.sum(act * lo, axis=0, keepdims=True))
            hi_cols.append(jnp.sum(act * hi, axis=0, keepdims=True))
        o_ref[pl.ds(c, 1), :] = x1_ref[pl.ds(c, 1), :] + jnp.concatenate(lo_cols + hi_cols, axis=-1)

    def step(slot):
        for c in range(PEER_TOK):
            token(slot, c)

        @pl.when(i == last)
        def _():
            lax.fori_loop(0, PEER_TOK, lambda c, carry: (wait_token(idx_nxt, 1 - slot, c), carry)[1], 0)

    pl.when(i % 2 == 0)(lambda: step(0))
    pl.when(i % 2 == 1)(lambda: step(1))


def _peer_expert(n_tokens, idx_flat, hn, gates, x1, uv_tiles):
    n = n_tokens // PEER_TOK
    row = pl.BlockSpec((PEER_TOK, D_MODEL), lambda i: (i, 0))
    fetch_buf = pltpu.VMEM((PEER_TOK * PEER_NK * SUBLANES, LANES), jnp.uint32)
    return pl.pallas_call(
        _peer_kernel,
        out_shape=jax.ShapeDtypeStruct((n_tokens, D_MODEL), F32),
        grid=(n,),
        in_specs=[pl.BlockSpec((PEER_TOK * PEER_NK,), lambda i: (i,), memory_space=pltpu.SMEM),
                  pl.BlockSpec((PEER_TOK * PEER_NK,), lambda i: (jnp.minimum(i + 1, n - 1),), memory_space=pltpu.SMEM),
                  row, pl.BlockSpec((PEER_TOK, PEER_NK), lambda i: (i, 0)), row,
                  pl.BlockSpec(memory_space=pl.ANY)],
        out_specs=row,
        scratch_shapes=[fetch_buf, fetch_buf, pltpu.SemaphoreType.DMA((2, PEER_TOK))],
        compiler_params=pltpu.CompilerParams(dimension_semantics=("arbitrary",), vmem_limit_bytes=48 * 1024 * 1024),
        name="peer_expert",
    )(idx_flat, idx_flat, hn, gates, x1, uv_tiles)


SC_LANES = 16
SC_WORKERS = 32
SC_GATHER = 32
SC_BLOCK = 8
SC_SHARE = (13, 32)
SC_SHARE_EARLY = (22, 32)


def _sc_lane_bcast(vec, k):
    idx = jnp.full((SC_LANES, 1), k, I32)
    dn = lax.GatherDimensionNumbers(offset_dims=(), collapsed_slice_dims=(0,), start_index_map=(0,))
    return lax.gather(vec, idx, dn, slice_sizes=(1,), mode=lax.GatherScatterMode.PROMISE_IN_BOUNDS)


def _sc_unpack_pair(w):
    return plsc.bitcast(w << 16, F32), plsc.bitcast(w & jnp.uint32(0xFFFF0000), F32)


def _peer_expert_sc(t_start, n, idx, gates, hn, x1, uv_rows):
    assert n % (SC_WORKERS * SC_BLOCK) == 0 and t_start % SC_BLOCK == 0
    tokens_per_worker = n // SC_WORKERS
    n_gather = PEER_NK // SC_GATHER
    half = D_MODEL // 2
    n_chunk = half // SC_LANES
    mesh = plsc.VectorSubcoreMesh(core_axis_name="c", subcore_axis_name="s")

    def body(idx_hbm, g_hbm, hn_hbm, x1_hbm, uv_hbm, out_hbm, idx_v, g_v, x_v, o_v, rows0, rows1, sem0, sem1):
        worker = lax.axis_index("s") * 2 + lax.axis_index("c")
        lane = lax.iota(I32, SC_LANES)
        rows, sems = (rows0, rows1), (sem0, sem1)

        def gather(ti, q):
            return pltpu.make_async_copy(uv_hbm.at[idx_v.at[ti, q]], rows[q % 2], sems[q % 2])

        def block(bi, carry):
            o0 = pl.multiple_of(worker * tokens_per_worker + bi * SC_BLOCK, SC_BLOCK)
            t0 = pl.multiple_of(t_start + o0, SC_BLOCK)
            pltpu.sync_copy(idx_hbm.at[pl.ds(t0, SC_BLOCK)], idx_v)
            pltpu.sync_copy(g_hbm.at[pl.ds(t0, SC_BLOCK)], g_v)
            pltpu.sync_copy(hn_hbm.at[pl.ds(t0, SC_BLOCK)], x_v)
            pltpu.sync_copy(x1_hbm.at[pl.ds(t0, SC_BLOCK)], o_v)
            gather(0, 0).start()

            def token(ti, carry1):
                for q in range(n_gather):
                    if q + 1 < n_gather:
                        gather(ti, q + 1).start()
                    else:
                        @pl.when(ti + 1 < SC_BLOCK)
                        def _():
                            gather(ti + 1, 0).start()
                    gather(ti, q).wait()
                    rows_v = rows[q % 2]

                    def dot_chunk(j, accs):
                        off = pl.multiple_of(j * SC_LANES, SC_LANES)
                        xlo = x_v[ti, pl.ds(off, SC_LANES)]
                        xhi = x_v[ti, pl.ds(half + off, SC_LANES)]
                        out = []
                        for k in range(SC_GATHER):
                            lo, hi = _sc_unpack_pair(rows_v[k, pl.ds(off, SC_LANES)])
                            out.append(accs[k] + lo * xlo + hi * xhi)
                        return tuple(out)

                    accs = lax.fori_loop(0, n_chunk, dot_chunk,
                                         tuple(jnp.zeros((SC_LANES,), F32) for _ in range(SC_GATHER)))
                    acts = []
                    for h in range(SC_GATHER // SC_LANES):
                        a = jnp.zeros((SC_LANES,), F32)
                        for kk in range(SC_LANES):
                            a = jnp.where(lane == kk, jnp.sum(accs[h * SC_LANES + kk]), a)
                        y = 0.7978845608028654 * (a + 0.044715 * a * a * a)
                        th = 1.0 - 2.0 / (jnp.exp(2.0 * y) + 1.0)
                        acts.append(0.5 * a * (1.0 + th) * g_v[ti, pl.ds(q * SC_GATHER + h * SC_LANES, SC_LANES)])
                    act_b = [_sc_lane_bcast(acts[k // SC_LANES], k % SC_LANES) for k in range(SC_GATHER)]

                    def mix_chunk(j, carry2):
                        off = pl.multiple_of(j * SC_LANES, SC_LANES)
                        al = jnp.zeros((SC_LANES,), F32)
                        ah = jnp.zeros((SC_LANES,), F32)
                        for k in range(SC_GATHER):
                            lo, hi = _sc_unpack_pair(rows_v[k, pl.ds(half + off, SC_LANES)])
                            al = al + act_b[k] * lo
                            ah = ah + act_b[k] * hi
                        o_v[ti, pl.ds(off, SC_LANES)] = o_v[ti, pl.ds(off, SC_LANES)] + al
                        o_v[ti, pl.ds(half + off, SC_LANES)] = o_v[ti, pl.ds(half + off, SC_LANES)] + ah
                        return carry2

                    lax.fori_loop(0, n_chunk, mix_chunk, 0)
                return carry1

            lax.fori_loop(0, SC_BLOCK, token, 0)
            pltpu.sync_copy(o_v, out_hbm.at[pl.ds(o0, SC_BLOCK)])
            return carry

        lax.fori_loop(0, tokens_per_worker // SC_BLOCK, block, 0)

    row_buf = pltpu.VMEM((SC_GATHER, D_MODEL), jnp.uint32)
    return pl.kernel(
        body, mesh=mesh,
        out_type=jax.ShapeDtypeStruct((n, D_MODEL), F32),
        scratch_types=[pltpu.VMEM((SC_BLOCK, n_gather, SC_GATHER), I32), pltpu.VMEM((SC_BLOCK, PEER_NK), F32),
                       pltpu.VMEM((SC_BLOCK, D_MODEL), F32), pltpu.VMEM((SC_BLOCK, D_MODEL), F32), row_buf, row_buf,
                       pltpu.SemaphoreType.DMA, pltpu.SemaphoreType.DMA],
        compiler_params=pltpu.CompilerParams(needs_layout_passes=False),
        name="peer_expert_sc",
    )(idx, gates, hn, x1, uv_rows)


def _rope_tables(pos):
    half = ROPE_DIMS // 2
    inv = ROPE_THETA ** (-(jnp.arange(half, dtype=F32) * 2.0 / ROPE_DIMS))
    ang = pos.astype(F32)[:, None] * inv[None, :]
    cos, sin = jnp.cos(ang), jnp.sin(ang)
    n = pos.shape[0]
    a = jnp.concatenate([cos, cos, jnp.ones((n, HEAD_DIM - ROPE_DIMS), F32)], axis=-1)
    b = jnp.concatenate([-sin, sin, jnp.zeros((n, HEAD_DIM - ROPE_DIMS), F32)], axis=-1)
    return jnp.tile(a, (1, 2)), jnp.tile(b, (1, 2))


def _pack_bf16_pairs(w):
    half = w.shape[1] // 2
    bits = lax.bitcast_convert_type(w.astype(BF16), jnp.uint16).astype(jnp.uint32)
    return bits[:, :half] | (bits[:, half:] << 16)


def _tile2(v):
    return jnp.tile(v.reshape(1, HEAD_DIM), (1, 2))


def kernel(x, norm1_g, w_in, nsa_q_norm, nsa_k_norm, cmp_pe_k, cmp_w1_k, cmp_w2_k, cmp_pe_v, cmp_w1_v, cmp_w2_v,
           dil_q_norm, dil_k_norm, w_up_nsa, w_up_dil, w_o, norm2_g, peer_wq, peer_subkeys, peer_u, peer_v):
    B, S, D = x.shape
    assert D == D_MODEL and S % (DIL_PATTERNS[-1][1] * DIL_BLOCK) == 0 and S >= WIN + NSA_QB and (B * S) % 1024 == 0

    n_q, n_kv, n_gate, n_dil = 512, 768, 24, 2304
    o_gate = n_q + n_kv
    o_dil = o_gate + n_gate
    o_mg = o_dil + n_dil
    w_perm = jnp.concatenate([w_in[:, o_mg:], w_in[:, :o_gate], w_in[:, o_dil:o_mg], w_in[:, o_gate:o_dil],
                              jnp.zeros((D, IN_COLS_PAD - w_in.shape[1]), w_in.dtype)], axis=1).astype(BF16)
    blockdiag = jnp.asarray(np.kron(np.eye(2), np.ones((HEAD_DIM, HEAD_DIM))), BF16)
    rope_a, rope_b = _rope_tables(jnp.arange(S))
    sub = PEER_NKEYS // 2
    sk_pad = jnp.stack([jnp.pad(peer_subkeys[0], ((0, 0), (0, sub))), jnp.pad(peer_subkeys[1], ((0, 0), (sub, 0)))])
    uv_rows = jnp.concatenate([_pack_bf16_pairs(peer_u), _pack_bf16_pairs(peer_v)], axis=1)
    weights = dict(w_perm=w_perm, blockdiag=blockdiag, rope_a=rope_a, rope_b=rope_b, sk_pad=sk_pad, uv_rows=uv_rows,
                   wn=w_up_nsa.astype(BF16), wd=w_up_dil.astype(BF16), wo=w_o.astype(BF16), wq=peer_wq.astype(BF16))
    params = (norm1_g, nsa_q_norm, nsa_k_norm, cmp_pe_k, cmp_w1_k, cmp_w2_k, cmp_pe_v, cmp_w1_v, cmp_w2_v, dil_q_norm,
              dil_k_norm, norm2_g)

    if B % 2 == 0 and (B // 2 * S) % 1024 == 0:
        groups = ((x[:B // 2], SC_SHARE_EARLY), (x[B // 2:], SC_SHARE))
    else:
        groups = ((x, SC_SHARE),)
    outs = [_layer(xg, share, weights, *params) for xg, share in groups]
    return outs[0] if len(outs) == 1 else jnp.concatenate(outs, axis=0)


def _layer(x, sc_share, weights, norm1_g, nsa_q_norm, nsa_k_norm, cmp_pe_k, cmp_w1_k, cmp_w2_k, cmp_pe_v, cmp_w1_v, cmp_w2_v,
           dil_q_norm, dil_k_norm, norm2_g):
    B, S, D = x.shape
    T = B * S
    x2 = x.reshape(T, D)
    blockdiag, rope_a, rope_b = weights["blockdiag"], weights["rope_a"], weights["rope_b"]
    z2 = _in_proj(x2, norm1_g.reshape(1, D), weights["w_perm"])
    z3 = z2.reshape(B, S, IN_COLS_PAD)

    qn, ksd, vsd, kwd, vwd = _nsa_prep(z3, rope_a, rope_b, blockdiag, _tile2(nsa_q_norm), _tile2(nsa_k_norm[1]),
                                       _tile2(nsa_k_norm[2]))

    n_cmp = (S - CMP_BLOCK) // CMP_STRIDE + 1
    ncp = S // CMP_STRIDE
    def flat_blocks(col):
        zc = z3[:, :, col:col + LANES].reshape(B, S, 2, HEAD_DIM).transpose(0, 2, 1, 3)
        r = zc.reshape(B, 2, ncp, CMP_STRIDE * HEAD_DIM)
        nxt = jnp.concatenate([r[:, :, 1:], jnp.zeros_like(r[:, :, :1])], axis=2)
        return jnp.concatenate([r, nxt], axis=-1).reshape(B * 2 * ncp, CMP_BLOCK * HEAD_DIM)
    cmp_a, cmp_b = _rope_tables(jnp.arange(ncp) * CMP_STRIDE + CMP_BLOCK - 1)
    dup2 = lambda w: jnp.concatenate([w, w], axis=1).astype(BF16)
    kcd, vcd = _compress(flat_blocks(COL_KV), flat_blocks(COL_KV + LANES),
                         cmp_pe_k.reshape(1, -1), cmp_pe_v.reshape(1, -1), cmp_w1_k.astype(BF16), cmp_w1_v.astype(BF16),
                         dup2(cmp_w2_k), dup2(cmp_w2_v), cmp_a, cmp_b, blockdiag, _tile2(nsa_k_norm[0]))
    kcd = kcd.reshape(B, 2, ncp, LANES)
    vcd = vcd.reshape(B, 2, ncp, LANES)

    n_slc = S // SLC_BLOCK
    s0 = np.arange(n_cmp) * CMP_STRIDE
    b0 = np.arange(n_slc) * SLC_BLOCK
    ov = np.clip(np.minimum(s0[:, None] + CMP_BLOCK, b0[None, :] + SLC_BLOCK) - np.maximum(s0[:, None], b0[None, :]),
                 0, None) / CMP_BLOCK
    ov_pad = np.zeros((ncp, LANES), np.float32)
    ov_pad[:n_cmp, :n_slc] = ov
    y_nsa = _nsa_attn(qn, kcd, vcd, ksd, vsd, kwd, vwd, z3, jnp.asarray(ov_pad, BF16))

    gq = jnp.tile(dil_q_norm.reshape(3, 1, HEAD_DIM), (1, 1, 2))
    gk = jnp.tile(dil_k_norm.reshape(3, 1, HEAD_DIM), (1, 1, 2))
    prep = _dil_prep(z3, rope_a, rope_b, blockdiag, gq, gk)
    flat = [p.reshape(B, S, 256) for p in prep]
    y_dil = _dil_attn(flat[0::3], flat[1::3], flat[2::3])

    x1, hn, pq = _merge(x2, y_nsa.reshape(T, 512), y_dil.reshape(T, 256), z2, weights["wn"], weights["wd"], weights["wo"],
                        norm2_g.reshape(1, D), weights["wq"])

    idx_t, gate_t = _peer_route(pq, weights["sk_pad"])
    uv_rows = weights["uv_rows"]
    idx, gates = idx_t.T, gate_t.T
    sc_unit = SC_WORKERS * SC_BLOCK
    t_tc = T - (T * sc_share[0] // sc_share[1]) // sc_unit * sc_unit
    assert t_tc % PEER_TOK == 0 and t_tc >= 2 * PEER_TOK
    out_tc = _peer_expert(t_tc, idx.reshape(-1), hn, gates, x1, uv_rows.reshape(-1, SUBLANES, LANES))
    out_sc = _peer_expert_sc(t_tc, T - t_tc, idx.reshape(T, PEER_NK // SC_GATHER, SC_GATHER), gates, hn, x1, uv_rows)
    return jnp.concatenate([out_tc, out_sc], axis=0).reshape(B, S, D)
```

```python
import functools

import numpy as np
import jax
import jax.numpy as jnp
from jax import lax
from jax.experimental import pallas as pl
from jax.experimental.pallas import tpu as pltpu
from jax.experimental.pallas import tpu_sc as plsc

F32 = jnp.float32
BF16 = jnp.bfloat16
I32 = jnp.int32

D_MODEL = 1024
HEAD_DIM = 64
ROPE_DIMS = 16
ROPE_THETA = 500000.0
NORM_EPS = 1e-6
NEG_INF = -1e30
LANES = 128

NSA_HEADS = 8
CMP_BLOCK = 32
CMP_STRIDE = 16
CMP_HIDDEN = 256
SLC_BLOCK = 64
SLC_TOPN = 16
FORCE_SCORE = 1e3
WIN = 512
NSA_QB = 128
SEL_CHUNK = 512

DIL_PATTERNS = ((128, 1), (512, 4), (2048, 16))
DIL_BLOCK = 128
DIL_PREP_ROWS = 1024

PEER_HEADS = 8
PEER_NKEYS = 128
PEER_TOPK = 16
PEER_TOK = 16

COL_MG = 0
COL_Q = 2048
COL_KV = 2560
COL_DIL = 3328
COL_GATE = 5632
IN_COLS_PAD = 5760

_NT = (((1,), (1,)), ((), ()))


def _dot(a, b):
    return jnp.dot(a, b, preferred_element_type=F32)


def _dot_nt(a, b):
    return lax.dot_general(a, b, _NT, preferred_element_type=F32)


def _split_bf16(a):
    hi = a.astype(BF16)
    lo = (a - hi.astype(F32)).astype(BF16)
    return hi, lo


def _dot_hilo(a, b_bf16):
    hi, lo = _split_bf16(a)
    return _dot(hi, b_bf16) + _dot(lo, b_bf16)


def _head_norm_rope(zt, gain, rope_a, rope_b, blockdiag):
    ss = _dot_hilo(zt * zt, blockdiag)
    zn = zt * lax.rsqrt(ss * (1.0 / HEAD_DIM) + NORM_EPS) * gain
    d = lax.broadcasted_iota(I32, zn.shape, 1) & (HEAD_DIM - 1)
    half = ROPE_DIMS // 2
    partner = jnp.where(d < half, pltpu.roll(zn, LANES - half, 1), pltpu.roll(zn, half, 1))
    return zn * rope_a + partner * rope_b


def _half_masks():
    lane = lax.broadcasted_iota(I32, (1, LANES), 1)
    lo = (lane < HEAD_DIM).astype(BF16)
    return lo, (1 - lo).astype(BF16)


def _inproj_kernel(x_ref, g_ref, w_ref, o_ref, h_scr):
    @pl.when(pl.program_id(1) == 0)
    def _():
        xf = x_ref[...]
        ms = jnp.mean(xf * xf, axis=-1, keepdims=True)
        h_scr[...] = (xf * lax.rsqrt(ms + NORM_EPS) * g_ref[...]).astype(BF16)

    o_ref[...] = _dot(h_scr[...], w_ref[...])


def _in_proj(x2, g1, w_bf16):
    T = x2.shape[0]
    tm, tn = 1024, 640
    return pl.pallas_call(
        _inproj_kernel,
        out_shape=jax.ShapeDtypeStruct((T, IN_COLS_PAD), F32),
        grid=(T // tm, IN_COLS_PAD // tn),
        in_specs=[
            pl.BlockSpec((tm, D_MODEL), lambda i, j: (i, 0)),
            pl.BlockSpec((1, D_MODEL), lambda i, j: (0, 0)),
            pl.BlockSpec((D_MODEL, tn), lambda i, j: (0, j)),
        ],
        out_specs=pl.BlockSpec((tm, tn), lambda i, j: (i, j)),
        scratch_shapes=[pltpu.VMEM((tm, D_MODEL), BF16)],
        compiler_params=pltpu.CompilerParams(dimension_semantics=("parallel", "arbitrary")),
        name="in_proj",
    )(x2, g1, w_bf16)


def _nsa_prep_kernel(zq_ref, zks_ref, zvs_ref, zkw_ref, zvw_ref, ra_ref, rb_ref, bd_ref, gq_ref, gks_ref, gkw_ref,
                     q_ref, ks_ref, vs_ref, kw_ref, vw_ref):
    ra, rb, bd = ra_ref[...], rb_ref[...], bd_ref[...]
    lane = lax.broadcasted_iota(I32, ra.shape, 1)
    scale = HEAD_DIM ** -0.5

    zq = zq_ref[0]
    tiles = [_head_norm_rope(zq[:, m * LANES:(m + 1) * LANES], gq_ref[...], ra, rb, bd) * scale for m in range(4)]
    q_ref[0] = jnp.concatenate(tiles, axis=-1).astype(BF16)

    def dup(t, out_ref):
        sw = pltpu.roll(t, HEAD_DIM, 1)
        out_ref[0, 0] = jnp.where(lane < HEAD_DIM, t, sw).astype(BF16)
        out_ref[0, 1] = jnp.where(lane < HEAD_DIM, sw, t).astype(BF16)

    dup(_head_norm_rope(zks_ref[0], gks_ref[...], ra, rb, bd), ks_ref)
    dup(_head_norm_rope(zkw_ref[0], gkw_ref[...], ra, rb, bd), kw_ref)
    dup(zvs_ref[0], vs_ref)
    dup(zvw_ref[0], vw_ref)


def _nsa_prep(z3, rope_a, rope_b, blockdiag, gq, gks, gkw):
    B, S, _ = z3.shape
    tm = 512
    kvb = COL_KV // LANES
    zcol = lambda c: pl.BlockSpec((1, tm, LANES), lambda b, i, c=c: (b, i, c))
    const = lambda shape: pl.BlockSpec(shape, lambda b, i: tuple(0 for _ in shape))
    kv_out = pl.BlockSpec((1, 2, tm, LANES), lambda b, i: (b, 0, i, 0))
    kv_shape = jax.ShapeDtypeStruct((B, 2, S, LANES), BF16)
    return pl.pallas_call(
        _nsa_prep_kernel,
        out_shape=(jax.ShapeDtypeStruct((B, S, 512), BF16), kv_shape, kv_shape, kv_shape, kv_shape),
        grid=(B, S // tm),
        in_specs=[
            pl.BlockSpec((1, tm, 512), lambda b, i: (b, i, COL_Q // 512)),
            zcol(kvb + 2), zcol(kvb + 3), zcol(kvb + 4), zcol(kvb + 5),
            pl.BlockSpec((tm, LANES), lambda b, i: (i, 0)),
            pl.BlockSpec((tm, LANES), lambda b, i: (i, 0)),
            const((LANES, LANES)), const((1, LANES)), const((1, LANES)), const((1, LANES)),
        ],
        out_specs=(pl.BlockSpec((1, tm, 512), lambda b, i: (b, i, 0)), kv_out, kv_out, kv_out, kv_out),
        compiler_params=pltpu.CompilerParams(dimension_semantics=("parallel", "parallel")),
        name="nsa_prep",
    )(z3, z3, z3, z3, z3, rope_a, rope_b, blockdiag, gq, gks, gkw)


def _compress_kernel(fk_ref, fv_ref, pek_ref, pev_ref, w1k_ref, w1v_ref, w2k_ref, w2v_ref, ra_ref, rb_ref, bd_ref, gk_ref,
                     k_ref, v_ref):
    lane = lax.broadcasted_iota(I32, (fk_ref.shape[1], LANES), 1)

    def mlp(f_ref, pe_ref, w1_ref, w2_ref):
        f = (f_ref[0] + pe_ref[...]).astype(BF16)
        h = jax.nn.gelu(_dot(f, w1_ref[...]))
        return _dot(h.astype(BF16), w2_ref[...])

    def dup(t, out_ref):
        sw = pltpu.roll(t, HEAD_DIM, 1)
        out_ref[0, 0] = jnp.where(lane < HEAD_DIM, t, sw).astype(BF16)
        out_ref[0, 1] = jnp.where(lane < HEAD_DIM, sw, t).astype(BF16)

    kc = mlp(fk_ref, pek_ref, w1k_ref, w2k_ref)
    dup(_head_norm_rope(kc, gk_ref[...], ra_ref[...], rb_ref[...], bd_ref[...]), k_ref)
    dup(mlp(fv_ref, pev_ref, w1v_ref, w2v_ref), v_ref)


def _compress(flat_k, flat_v, pek, pev, w1k, w1v, w2k, w2v, rope_a, rope_b, blockdiag, gk):
    B, nblk, width = flat_k.shape
    row = pl.BlockSpec((1, nblk, width), lambda b: (b, 0, 0))
    const = lambda shape: pl.BlockSpec(shape, lambda b: tuple(0 for _ in shape))
    out = pl.BlockSpec((1, 2, nblk, LANES), lambda b: (b, 0, 0, 0))
    shp = jax.ShapeDtypeStruct((B, 2, nblk, LANES), BF16)
    return pl.pallas_call(
        _compress_kernel,
        out_shape=(shp, shp),
        grid=(B,),
        in_specs=[row, row, const((1, width)), const((1, width)), const((width, 2 * CMP_HIDDEN)),
                  const((width, 2 * CMP_HIDDEN)), const((2 * CMP_HIDDEN, LANES)), const((2 * CMP_HIDDEN, LANES)),
                  const((nblk, LANES)), const((nblk, LANES)), const((LANES, LANES)), const((1, LANES))],
        out_specs=(out, out),
        compiler_params=pltpu.CompilerParams(dimension_semantics=("parallel",), vmem_limit_bytes=48 * 1024 * 1024),
        name="compress",
    )(flat_k, flat_v, pek, pev, w1k, w1v, w2k, w2v, rope_a, rope_b, blockdiag, gk)


def _softmax_rows(s):
    m = jnp.max(s, axis=-1, keepdims=True)
    e = jnp.exp(s - m)
    return e / jnp.sum(e, axis=-1, keepdims=True)


def _nsa_kernel(q_ref, kc_ref, vc_ref, ks_ref, vs_ref, kw_ref, vw_ref, gl_ref, ov_ref, ex_ref, y_ref):
    n = pl.program_id(1)
    t0 = n * NSA_QB
    qt = q_ref[0]
    hm = _half_masks()
    lane = lax.broadcasted_iota(I32, (NSA_QB, LANES), 1)
    t1 = t0 + lax.broadcasted_iota(I32, (NSA_QB, 1), 0)
    t4 = t0 + (lax.broadcasted_iota(I32, (4 * NSA_QB, 1), 0) & (NSA_QB - 1))
    gates = jax.nn.sigmoid(gl_ref[0])

    n_slc_blocks = ks_ref.shape[2] // SLC_BLOCK
    blk = lax.broadcasted_iota(I32, (n_slc_blocks, NSA_QB), 0)
    tq = t0 + lax.broadcasted_iota(I32, (n_slc_blocks, NSA_QB), 1)
    cur = tq >> 6
    forced = (blk == 0) | (blk == cur) | (blk == cur - 1)
    causal_b = blk * SLC_BLOCK <= tq

    tiles_out = []
    for g in range(2):
        q4 = jnp.concatenate(
            [qt[:, (2 * g + jj // 2) * LANES:(2 * g + jj // 2 + 1) * LANES] * hm[jj % 2] for jj in range(4)], axis=0)

        sc = _dot_nt(q4, kc_ref[0, g])
        cend = lax.broadcasted_iota(I32, (1, sc.shape[1]), 1) * CMP_STRIDE + (CMP_BLOCK - 1)
        valid_c = cend <= t4
        pc = _softmax_rows(jnp.where(valid_c, sc, NEG_INF))
        pc = jnp.where(valid_c, pc, 0.0)
        o_c = _dot(pc.astype(BF16), vc_ref[0, g])
        psum = pc[0:NSA_QB] + pc[NSA_QB:2 * NSA_QB] + pc[2 * NSA_QB:3 * NSA_QB] + pc[3 * NSA_QB:]
        imp = _dot_hilo(psum, ov_ref[...])

        score = jnp.where(forced, FORCE_SCORE, jnp.where(causal_b, imp.T[:n_slc_blocks], -1.0))
        n_grp = n_slc_blocks // SUBLANES
        grp = [score[SUBLANES * r:SUBLANES * (r + 1)] for r in range(n_grp)]
        ranks = [jnp.zeros((SUBLANES, NSA_QB), F32) for _ in range(n_grp)]
        row = lax.broadcasted_iota(I32, (SUBLANES, NSA_QB), 0)
        for i in range(n_slc_blocks):
            ri = grp[i // SUBLANES][i % SUBLANES:i % SUBLANES + 1, :]
            for r in range(n_grp):
                if r > i // SUBLANES:
                    ahead = ri >= grp[r]
                elif r < i // SUBLANES:
                    ahead = ri > grp[r]
                else:
                    ahead = (ri > grp[r]) | ((ri == grp[r]) & (row > i % SUBLANES))
                ranks[r] = ranks[r] + jnp.where(ahead, 1.0, 0.0)
        sel64 = jnp.where(jnp.concatenate(ranks, axis=0) < float(SLC_TOPN), 1.0, 0.0)
        sel = jnp.concatenate([sel64, jnp.zeros((LANES - n_slc_blocks, NSA_QB), F32)], axis=0).T.astype(BF16)

        def sel_chunk(c, carry, last):
            m_i, l_i, acc = carry
            k0 = pl.multiple_of(c * SEL_CHUNK, SEL_CHUNK)
            kch = ks_ref[0, g, pl.ds(k0, SEL_CHUNK), :]
            vch = vs_ref[0, g, pl.ds(k0, SEL_CHUNK), :]
            picked = _dot(sel, ex_ref[:, pl.ds(k0, SEL_CHUNK)])
            if last:
                kpos = k0 + lax.broadcasted_iota(I32, (1, SEL_CHUNK), 1)
                picked = jnp.where(kpos <= t1, picked, 0.0)
            bias = (picked - 1.0) * (-NEG_INF)
            s = (_dot_nt(q4, kch).reshape(4, NSA_QB, SEL_CHUNK) + bias[None]).reshape(4 * NSA_QB, SEL_CHUNK)
            m_new = jnp.maximum(m_i, jnp.max(s, axis=-1, keepdims=True))
            alpha = jnp.exp(m_i - m_new)
            p = jnp.exp(s - m_new)
            l_new = alpha * l_i + jnp.sum(p, axis=-1, keepdims=True)
            acc_new = alpha * acc + _dot(p.astype(BF16), vch)
            return m_new, l_new, acc_new

        init = (jnp.full((4 * NSA_QB, 1), NEG_INF, F32), jnp.zeros((4 * NSA_QB, 1), F32),
                jnp.zeros((4 * NSA_QB, LANES), F32))
        n_full = t0 // SEL_CHUNK
        carry = lax.fori_loop(0, n_full, functools.partial(sel_chunk, last=False), init)
        _, l_s, acc_s = sel_chunk(n_full, carry, last=True)
        o_s = acc_s / l_s

        wlen = WIN + NSA_QB
        ws = pl.multiple_of(jnp.maximum(t0 - WIN, 0), NSA_QB)
        sw = _dot_nt(q4, kw_ref[0, g, pl.ds(ws, wlen), :])
        dist = t4 - (ws + lax.broadcasted_iota(I32, (1, wlen), 1))
        pw = _softmax_rows(jnp.where((dist >= 0) & (dist < WIN), sw, NEG_INF))
        o_w = _dot(pw.astype(BF16), vw_ref[0, g, pl.ds(ws, wlen), :])

        heads = []
        for jj in range(4):
            h = 4 * g + jj
            rows = slice(jj * NSA_QB, (jj + 1) * NSA_QB)
            heads.append(gates[:, 3 * h:3 * h + 1] * o_c[rows] + gates[:, 3 * h + 1:3 * h + 2] * o_s[rows]
                         + gates[:, 3 * h + 2:3 * h + 3] * o_w[rows])
        tiles_out.append(jnp.where(lane < HEAD_DIM, heads[0], heads[1]))
        tiles_out.append(jnp.where(lane < HEAD_DIM, heads[2], heads[3]))

    y_ref[0] = jnp.concatenate(tiles_out, axis=-1).astype(BF16)


def _nsa_attn(qn, kcd, vcd, ksd, vsd, kwd, vwd, z3, overlap):
    B, S, _ = qn.shape
    ncp = kcd.shape[2]
    assert S // SLC_BLOCK <= LANES
    expand = jnp.asarray(np.arange(LANES)[:, None] == (np.arange(S)[None, :] // SLC_BLOCK), BF16)
    full = lambda rows: pl.BlockSpec((1, 2, rows, LANES), lambda b, n: (b, 0, 0, 0))
    return pl.pallas_call(
        _nsa_kernel,
        out_shape=jax.ShapeDtypeStruct((B, S, 512), BF16),
        grid=(B, S // NSA_QB),
        in_specs=[
            pl.BlockSpec((1, NSA_QB, 512), lambda b, n: (b, n, 0)),
            full(ncp), full(ncp), full(S), full(S), full(S), full(S),
            pl.BlockSpec((1, NSA_QB, LANES), lambda b, n: (b, n, COL_GATE // LANES)),
            pl.BlockSpec((ncp, LANES), lambda b, n: (0, 0)),
            pl.BlockSpec((LANES, S), lambda b, n: (0, 0)),
        ],
        out_specs=pl.BlockSpec((1, NSA_QB, 512), lambda b, n: (b, n, 0)),
        compiler_params=pltpu.CompilerParams(dimension_semantics=("parallel", "arbitrary"),
                                             vmem_limit_bytes=48 * 1024 * 1024),
        name="nsa_attn",
    )(qn, kcd, vcd, ksd, vsd, kwd, vwd, z3, overlap, expand)


def _dil_prep_kernel(*refs):
    zs, (ra_ref, rb_ref, bd_ref, gq_ref, gk_ref), outs = refs[0:18], refs[18:23], refs[23:32]
    bd = bd_ref[...]
    scale = HEAD_DIM ** -0.5
    for g, (_, d) in enumerate(DIL_PATTERNS):
        n = DIL_PREP_ROWS // d
        for r in range(d):
            rows = pl.ds(r, n, stride=d) if d > 1 else pl.ds(0, n)
            ra, rb = ra_ref[rows, :], rb_ref[rows, :]
            for which in range(3):
                for m in range(2):
                    z = zs[2 * (3 * g + which) + m][0, rows, :]
                    if which == 0:
                        z = _head_norm_rope(z, gq_ref[g], ra, rb, bd) * scale
                    elif which == 1:
                        z = _head_norm_rope(z, gk_ref[g], ra, rb, bd)
                    outs[3 * g + which][0, r, :, m * LANES:(m + 1) * LANES] = z.astype(BF16)


def _dil_prep(z3, rope_a, rope_b, blockdiag, gq, gk):
    B, S, _ = z3.shape
    nsteps = S // DIL_PREP_ROWS
    c0 = COL_DIL // LANES
    in_specs = [pl.BlockSpec((1, DIL_PREP_ROWS, LANES), lambda b, c, k=k: (b, c, c0 + k)) for k in range(18)]
    in_specs += [
        pl.BlockSpec((DIL_PREP_ROWS, LANES), lambda b, c: (c, 0)),
        pl.BlockSpec((DIL_PREP_ROWS, LANES), lambda b, c: (c, 0)),
        pl.BlockSpec((LANES, LANES), lambda b, c: (0, 0)),
        pl.BlockSpec((3, 1, LANES), lambda b, c: (0, 0, 0)),
        pl.BlockSpec((3, 1, LANES), lambda b, c: (0, 0, 0)),
    ]
    out_shape, out_specs = [], []
    for _, d in DIL_PATTERNS:
        for _ in range(3):
            out_shape.append(jax.ShapeDtypeStruct((B, d, S // d, 256), BF16))
            out_specs.append(pl.BlockSpec((1, d, DIL_PREP_ROWS // d, 256), lambda b, c: (b, 0, c, 0)))
    return pl.pallas_call(
        _dil_prep_kernel,
        out_shape=tuple(out_shape),
        grid=(B, nsteps),
        in_specs=in_specs,
        out_specs=tuple(out_specs),
        compiler_params=pltpu.CompilerParams(dimension_semantics=("parallel", "parallel"),
                                             vmem_limit_bytes=48 * 1024 * 1024),
        name="dil_prep",
    )(*([z3] * 18), rope_a, rope_b, blockdiag, gq, gk)


def _dil_kernel(*refs, seq):
    q_refs, k_refs, v_refs, y_ref, o_scr, l_scr = refs[0:3], refs[3:6], refs[6:9], refs[9], refs[10], refs[11]
    hm = _half_masks()
    lane = lax.broadcasted_iota(I32, (DIL_BLOCK, LANES), 1)
    qi = lax.broadcasted_iota(I32, (2 * DIL_BLOCK, 2 * DIL_BLOCK), 0) & (DIL_BLOCK - 1)
    ki = lax.broadcasted_iota(I32, (2 * DIL_BLOCK, 2 * DIL_BLOCK), 1)
    causal = (ki - DIL_BLOCK) <= qi

    for g, (_, d) in enumerate(DIL_PATTERNS):
        nb = seq // d // DIL_BLOCK

        def body(u, carry, g=g, d=d, nb=nb):
            j = u % nb
            r = u // nb
            r0 = pl.multiple_of(u * DIL_BLOCK, DIL_BLOCK)
            p0 = pl.multiple_of(jnp.maximum(u - 1, 0) * DIL_BLOCK, DIL_BLOCK)
            q = q_refs[g][0, pl.ds(r0, DIL_BLOCK), :]
            kcat = jnp.concatenate([k_refs[g][0, pl.ds(p0, DIL_BLOCK), :], k_refs[g][0, pl.ds(r0, DIL_BLOCK), :]], axis=0)
            vcat = jnp.concatenate([v_refs[g][0, pl.ds(p0, DIL_BLOCK), :], v_refs[g][0, pl.ds(r0, DIL_BLOCK), :]], axis=0)
            q2 = jnp.concatenate([q * hm[0], q * hm[1]], axis=0)
            s = _dot_nt(q2, kcat)
            first_key = jnp.maximum(qi, jnp.where(j >= 1, 0, DIL_BLOCK))
            s = jnp.where(causal & (ki >= first_key), s, NEG_INF)
            m = jnp.max(s, axis=-1, keepdims=True)
            e = jnp.exp(s - m)
            den = jnp.sum(e, axis=-1, keepdims=True)
            o2 = _dot(e.astype(BF16), vcat) / den
            lse = m + jnp.log(den)
            o = jnp.where(lane < HEAD_DIM, o2[:DIL_BLOCK], o2[DIL_BLOCK:])
            lv = jnp.where(lane < HEAD_DIM, lse[:DIL_BLOCK], lse[DIL_BLOCK:])
            tok0 = j * (DIL_BLOCK * d) + r
            rows = pl.ds(tok0, DIL_BLOCK, stride=d) if d > 1 else pl.ds(pl.multiple_of(tok0, DIL_BLOCK), DIL_BLOCK)
            o_scr[g, rows, :] = o
            l_scr[g, rows, :] = lv
            return carry

        lax.fori_loop(0, seq // DIL_BLOCK, body, 0)

    def merge(c, carry):
        rows = pl.ds(pl.multiple_of(c * 512, 512), 512)
        ls = [l_scr[g, rows, :] for g in range(3)]
        mx = jnp.maximum(jnp.maximum(ls[0], ls[1]), ls[2])
        ws = [jnp.exp(l - mx) for l in ls]
        num = ws[0] * o_scr[0, rows, :] + ws[1] * o_scr[1, rows, :] + ws[2] * o_scr[2, rows, :]
        y_ref[0, rows, :] = (num / (ws[0] + ws[1] + ws[2])).astype(BF16)
        return carry

    lax.fori_loop(0, seq // 512, merge, 0)


def _dil_attn(dq, dk, dv):
    B, S, _ = dq[0].shape
    spec = pl.BlockSpec((1, S, LANES), lambda b, m: (b, 0, m))
    return pl.pallas_call(
        functools.partial(_dil_kernel, seq=S),
        out_shape=jax.ShapeDtypeStruct((B, S, 256), BF16),
        grid=(B, 2),
        in_specs=[spec] * 9,
        out_specs=spec,
        scratch_shapes=[pltpu.VMEM((3, S, LANES), F32), pltpu.VMEM((3, S, LANES), F32)],
        compiler_params=pltpu.CompilerParams(dimension_semantics=("parallel", "parallel"),
                                             vmem_limit_bytes=56 * 1024 * 1024),
        name="dil_attn",
    )(*dq, *dk, *dv)


def _merge_kernel(x_ref, yn_ref, yd_ref, mg0_ref, mg1_ref, wn_ref, wd_ref, wo_ref, g2_ref, wq_ref, x1_ref, hn_ref, pq_ref):
    u1 = _dot(yn_ref[...], wn_ref[...])
    u2 = _dot(yd_ref[...], wd_ref[...])
    merged = jax.nn.sigmoid(mg0_ref[...]) * u1 + jax.nn.sigmoid(mg1_ref[...]) * u2
    x1 = x_ref[...] + _dot(merged.astype(BF16), wo_ref[...])
    x1_ref[...] = x1
    ms = jnp.mean(x1 * x1, axis=-1, keepdims=True)
    hn = x1 * lax.rsqrt(ms + NORM_EPS) * g2_ref[...]
    hn_ref[...] = hn
    pq_ref[...] = _dot(hn.astype(BF16), wq_ref[...])


def _merge(x2, yn2, yd2, z2, wn, wd, wo, g2, wq):
    T = x2.shape[0]
    tm = 512
    row = lambda w, c=0: pl.BlockSpec((tm, w), lambda i, c=c: (i, c))
    const = lambda shape: pl.BlockSpec(shape, lambda i: (0, 0))
    shp = jax.ShapeDtypeStruct((T, D_MODEL), F32)
    return pl.pallas_call(
        _merge_kernel,
        out_shape=(shp, shp, shp),
        grid=(T // tm,),
        in_specs=[row(D_MODEL), row(512), row(256), row(D_MODEL, COL_MG // D_MODEL), row(D_MODEL, COL_MG // D_MODEL + 1),
                  const((512, D_MODEL)), const((256, D_MODEL)), const((D_MODEL, D_MODEL)), const((1, D_MODEL)),
                  const((D_MODEL, D_MODEL))],
        out_specs=(row(D_MODEL), row(D_MODEL), row(D_MODEL)),
        compiler_params=pltpu.CompilerParams(dimension_semantics=("parallel",), vmem_limit_bytes=48 * 1024 * 1024),
        name="merge",
    )(x2, yn2, yd2, z2, z2, wn, wd, wo, g2, wq)


def _top16(s, rank_id=None):
    if rank_id is None:
        rank_id = lax.broadcasted_iota(I32, (s.shape[0], 1), 0)
    big = jnp.iinfo(jnp.int32).max
    vals, ids = [], []
    for _ in range(PEER_TOPK):
        m = jnp.max(s, axis=0, keepdims=True)
        win = jnp.min(jnp.where(s == m, rank_id, big), axis=0, keepdims=True)
        vals.append(m)
        ids.append(win)
        s = jnp.where(rank_id == win, -jnp.inf, s)
    return vals, ids


def _route_kernel(q_ref, sk_ref, idx_ref, gate_ref):
    K = PEER_TOPK
    qh, ql = _split_bf16(q_ref[...])
    vals, ids = [], []
    for c in range(2):
        kh, kl = _split_bf16(sk_ref[c])
        s = _dot_nt(kh, qh) + _dot_nt(kh, ql) + _dot_nt(kl, qh)
        v, p = _top16(s)
        vals.append(v)
        ids.append(p)
    v0, p0 = jnp.concatenate(vals[0], axis=0), jnp.concatenate(ids[0], axis=0)
    v1, p1 = jnp.concatenate(vals[1], axis=0), jnp.concatenate(ids[1], axis=0)

    a8 = lax.broadcasted_iota(I32, (SUBLANES, 1), 0)
    pieces = [(v0 + vals[1][0], p0 * PEER_NKEYS + ids[1][0], lax.broadcasted_iota(I32, (K, 1), 0) * K)]
    for b in range(1, SUBLANES):
        keep = a8 < K // (b + 1)
        pieces.append((jnp.where(keep, v0[:SUBLANES] + vals[1][b], -jnp.inf), p0[:SUBLANES] * PEER_NKEYS + ids[1][b],
                       a8 * K + b))
    pieces.append((vals[0][0] + v1[SUBLANES:], ids[0][0] * PEER_NKEYS + p1[SUBLANES:], a8 + SUBLANES))
    cand = jnp.concatenate([p[0] for p in pieces], axis=0)
    eid = jnp.concatenate([p[1] for p in pieces], axis=0)
    flat = jnp.concatenate([p[2] for p in pieces], axis=0)
    v, win = _top16(cand, flat)
    sel_ids = [jnp.sum(jnp.where(flat == w, eid, 0), axis=0, keepdims=True) for w in win]
    sc = jnp.concatenate(v, axis=0)
    e = jnp.exp(sc - sc[0:1])
    gate_ref[...] = e / jnp.sum(e, axis=0, keepdims=True)
    idx_ref[...] = jnp.concatenate(sel_ids, axis=0)


def _peer_route(pq, sk_pad):
    T = pq.shape[0]
    tt = 512
    return pl.pallas_call(
        _route_kernel,
        out_shape=(jax.ShapeDtypeStruct((PEER_HEADS * PEER_TOPK, T), I32),
                   jax.ShapeDtypeStruct((PEER_HEADS * PEER_TOPK, T), F32)),
        grid=(T // tt, PEER_HEADS),
        in_specs=[pl.BlockSpec((tt, LANES), lambda i, h: (i, h)),
                  pl.BlockSpec((2, PEER_NKEYS, LANES), lambda i, h: (0, 0, 0))],
        out_specs=(pl.BlockSpec((PEER_TOPK, tt), lambda i, h: (h, i)),
                   pl.BlockSpec((PEER_TOPK, tt), lambda i, h: (h, i))),
        compiler_params=pltpu.CompilerParams(dimension_semantics=("parallel", "parallel")),
        name="peer_route",
    )(pq, sk_pad)


def _route_t_kernel(idx_ref, gate_ref, idx_o, gate_o):
    idx_o[...] = idx_ref[...].T
    gate_o[...] = gate_ref[...].T


def _route_transpose(idx_t, gate_t):
    nk, T = idx_t.shape
    tt = 512
    src = pl.BlockSpec((nk, tt), lambda i: (0, i))
    dst = pl.BlockSpec((tt, nk), lambda i: (i, 0))
    return pl.pallas_call(
        _route_t_kernel,
        out_shape=(jax.ShapeDtypeStruct((T, nk), I32), jax.ShapeDtypeStruct((T, nk), F32)),
        grid=(T // tt,),
        in_specs=[src, src],
        out_specs=(dst, dst),
        compiler_params=pltpu.CompilerParams(dimension_semantics=("parallel",)),
        name="route_transpose",
    )(idx_t, gate_t)


def _pack_words(w):
    half = w.shape[1] // 2
    bits = pltpu.bitcast(w.astype(BF16).astype(F32), jnp.uint32)
    return (bits[:, :half] >> 16) | (bits[:, half:] & jnp.uint32(0xFFFF0000))


def _pack_kernel(u_ref, v_ref, rows_ref, tiles_ref):
    n = u_ref.shape[0]
    words = jnp.concatenate([_pack_words(u_ref[...]), _pack_words(v_ref[...])], axis=-1)
    rows_ref[...] = words
    for s in range(SUBLANES):
        tiles_ref[pl.ds(s, n, stride=SUBLANES), :] = words[:, s * LANES:(s + 1) * LANES]


def _pack_tables(u, v):
    ne, d = u.shape
    tm = 256
    return pl.pallas_call(
        _pack_kernel,
        out_shape=(jax.ShapeDtypeStruct((ne, d), jnp.uint32), jax.ShapeDtypeStruct((ne * SUBLANES, LANES), jnp.uint32)),
        grid=(ne // tm,),
        in_specs=[pl.BlockSpec((tm, d), lambda i: (i, 0)), pl.BlockSpec((tm, d), lambda i: (i, 0))],
        out_specs=(pl.BlockSpec((tm, d), lambda i: (i, 0)), pl.BlockSpec((tm * SUBLANES, LANES), lambda i: (i, 0))),
        compiler_params=pltpu.CompilerParams(dimension_semantics=("parallel",)),
        name="pack_tables",
    )(u, v)


PEER_NK = PEER_HEADS * PEER_TOPK
SUBLANES = 8


def _unpack_pair(w):
    lo = pltpu.bitcast(w << 16, F32)
    hi = pltpu.bitcast(w & jnp.uint32(0xFFFF0000), F32)
    return lo, hi


def _peer_kernel(idx_cur, idx_nxt, hn_ref, gate_ref, x1_ref, uv_hbm, o_ref, buf_a, buf_b, sem):
    i = pl.program_id(0)
    last = pl.num_programs(0) - 1
    half_rows = SUBLANES // 2
    bufs = (buf_a, buf_b)

    def tile_copy(idx_ref, s, c, k):
        p = c * PEER_NK + k
        return pltpu.make_async_copy(uv_hbm.at[idx_ref[c, k]], bufs[s].at[pl.ds(p * SUBLANES, SUBLANES), :], sem.at[s, c])

    def fetch_token(idx_ref, s, c):
        for k in range(PEER_NK):
            tile_copy(idx_ref, s, c, k).start(priority=k % 2)

    def wait_token(idx_ref, s, c):
        for k in range(PEER_NK):
            tile_copy(idx_ref, s, c, k).wait()

    @pl.when(i == 0)
    def _():
        lax.fori_loop(0, PEER_TOK, lambda c, carry: (fetch_token(idx_cur, 0, c), carry)[1], 0)

    eye = lax.broadcasted_iota(I32, (PEER_NK, PEER_NK), 0) == lax.broadcasted_iota(I32, (PEER_NK, PEER_NK), 1)

    def token(slot, c):
        wait_token(idx_cur, slot, c)
        fetch_token(idx_nxt, 1 - slot, c)
        base = c * (PEER_NK * SUBLANES)
        plane = lambda s: bufs[slot][pl.ds(base + s, PEER_NK, stride=SUBLANES), :]
        x = hn_ref[pl.ds(c, 1), :]
        xs = lambda s: x[:, s * LANES:(s + 1) * LANES]
        gcol = jnp.sum(jnp.where(eye, gate_ref[pl.ds(c, 1), :], 0.0), axis=-1, keepdims=True)
        acc = jnp.zeros((PEER_NK, LANES), F32)
        for s in range(half_rows):
            lo, hi = _unpack_pair(plane(s))
            acc = acc + lo * xs(s) + hi * xs(s + half_rows)
        act = jax.nn.gelu(jnp.sum(acc, axis=-1, keepdims=True)) * gcol
        lo_cols, hi_cols = [], []
        for s in range(half_rows, SUBLANES):
            lo, hi = _unpack_pair(plane(s))
            lo_cols.append(jnp.sum(act * lo, axis=0, keepdims=True))
            hi_cols.append(jnp.sum(act * hi, axis=0, keepdims=True))
        o_ref[pl.ds(c, 1), :] = x1_ref[pl.ds(c, 1), :] + jnp.concatenate(lo_cols + hi_cols, axis=-1)

    def step(slot):
        for c in range(PEER_TOK):
            token(slot, c)

        @pl.when(i == last)
        def _():
            lax.fori_loop(0, PEER_TOK, lambda c, carry: (wait_token(idx_nxt, 1 - slot, c), carry)[1], 0)

    pl.when(i % 2 == 0)(lambda: step(0))
    pl.when(i % 2 == 1)(lambda: step(1))


def _peer_expert(n_tokens, idx_flat, hn, gates, x1, uv_tiles):
    n = n_tokens // PEER_TOK
    row = pl.BlockSpec((PEER_TOK, D_MODEL), lambda i: (i, 0))
    fetch_buf = pltpu.VMEM((PEER_TOK * PEER_NK * SUBLANES, LANES), jnp.uint32)
    return pl.pallas_call(
        _peer_kernel,
        out_shape=jax.ShapeDtypeStruct((n_tokens, D_MODEL), F32),
        grid=(n,),
        in_specs=[pl.BlockSpec((PEER_TOK, PEER_NK), lambda i: (i, 0), memory_space=pltpu.SMEM),
                  pl.BlockSpec((PEER_TOK, PEER_NK), lambda i: (jnp.minimum(i + 1, n - 1), 0), memory_space=pltpu.SMEM),
                  row, pl.BlockSpec((PEER_TOK, PEER_NK), lambda i: (i, 0)), row,
                  pl.BlockSpec(memory_space=pl.ANY)],
        out_specs=row,
        scratch_shapes=[fetch_buf, fetch_buf, pltpu.SemaphoreType.DMA((2, PEER_TOK))],
        compiler_params=pltpu.CompilerParams(dimension_semantics=("arbitrary",), vmem_limit_bytes=48 * 1024 * 1024),
        name="peer_expert",
    )(idx_flat, idx_flat, hn, gates, x1, uv_tiles)


SC_LANES = 16
SC_WORKERS = 32
SC_GATHER = 32
SC_BLOCK = 8
SC_SHARE = (13, 32)
SC_SHARE_EARLY = (22, 32)


def _sc_lane_bcast(vec, k):
    idx = jnp.full((SC_LANES, 1), k, I32)
    dn = lax.GatherDimensionNumbers(offset_dims=(), collapsed_slice_dims=(0,), start_index_map=(0,))
    return lax.gather(vec, idx, dn, slice_sizes=(1,), mode=lax.GatherScatterMode.PROMISE_IN_BOUNDS)


def _sc_unpack_pair(w):
    return plsc.bitcast(w << 16, F32), plsc.bitcast(w & jnp.uint32(0xFFFF0000), F32)


def _peer_expert_sc(t_start, n, idx, gates, hn, x1, uv_rows):
    assert n % (SC_WORKERS * SC_BLOCK) == 0 and t_start % SC_BLOCK == 0
    tokens_per_worker = n // SC_WORKERS
    n_gather = PEER_NK // SC_GATHER
    half = D_MODEL // 2
    n_chunk = half // SC_LANES
    mesh = plsc.VectorSubcoreMesh(core_axis_name="c", subcore_axis_name="s")

    def body(idx_hbm, g_hbm, hn_hbm, x1_hbm, uv_hbm, out_hbm, idx_v, g_v, x_v, o_v, rows0, rows1, sem0, sem1):
        worker = lax.axis_index("s") * 2 + lax.axis_index("c")
        lane = lax.iota(I32, SC_LANES)
        rows, sems = (rows0, rows1), (sem0, sem1)

        def gather(ti, q):
            picks = idx_v.at[ti, pl.ds(q * SC_GATHER, SC_GATHER)]
            return pltpu.make_async_copy(uv_hbm.at[picks], rows[q % 2], sems[q % 2])

        def block(bi, carry):
            o0 = pl.multiple_of(worker * tokens_per_worker + bi * SC_BLOCK, SC_BLOCK)
            t0 = pl.multiple_of(t_start + o0, SC_BLOCK)
            pltpu.sync_copy(idx_hbm.at[pl.ds(t0, SC_BLOCK)], idx_v)
            pltpu.sync_copy(g_hbm.at[pl.ds(t0, SC_BLOCK)], g_v)
            pltpu.sync_copy(hn_hbm.at[pl.ds(t0, SC_BLOCK)], x_v)
            pltpu.sync_copy(x1_hbm.at[pl.ds(t0, SC_BLOCK)], o_v)
            gather(0, 0).start()

            def token(ti, carry1):
                for q in range(n_gather):
                    if q + 1 < n_gather:
                        gather(ti, q + 1).start()
                    else:
                        @pl.when(ti + 1 < SC_BLOCK)
                        def _():
                            gather(ti + 1, 0).start()
                    gather(ti, q).wait()
                    rows_v = rows[q % 2]

                    def dot_chunk(j, accs):
                        off = pl.multiple_of(j * SC_LANES, SC_LANES)
                        xlo = x_v[ti, pl.ds(off, SC_LANES)]
                        xhi = x_v[ti, pl.ds(half + off, SC_LANES)]
                        out = []
                        for k in range(SC_GATHER):
                            lo, hi = _sc_unpack_pair(rows_v[k, pl.ds(off, SC_LANES)])
                            out.append(accs[k] + lo * xlo + hi * xhi)
                        return tuple(out)

                    accs = lax.fori_loop(0, n_chunk, dot_chunk,
                                         tuple(jnp.zeros((SC_LANES,), F32) for _ in range(SC_GATHER)))
                    acts = []
                    for h in range(SC_GATHER // SC_LANES):
                        a = jnp.zeros((SC_LANES,), F32)
                        for kk in range(SC_LANES):
                            a = jnp.where(lane == kk, jnp.sum(accs[h * SC_LANES + kk]), a)
                        y = 0.7978845608028654 * (a + 0.044715 * a * a * a)
                        th = 1.0 - 2.0 / (jnp.exp(2.0 * y) + 1.0)
                        acts.append(0.5 * a * (1.0 + th) * g_v[ti, pl.ds(q * SC_GATHER + h * SC_LANES, SC_LANES)])
                    act_b = [_sc_lane_bcast(acts[k // SC_LANES], k % SC_LANES) for k in range(SC_GATHER)]

                    def mix_chunk(j, carry2):
                        off = pl.multiple_of(j * SC_LANES, SC_LANES)
                        al = jnp.zeros((SC_LANES,), F32)
                        ah = jnp.zeros((SC_LANES,), F32)
                        for k in range(SC_GATHER):
                            lo, hi = _sc_unpack_pair(rows_v[k, pl.ds(half + off, SC_LANES)])
                            al = al + act_b[k] * lo
                            ah = ah + act_b[k] * hi
                        o_v[ti, pl.ds(off, SC_LANES)] = o_v[ti, pl.ds(off, SC_LANES)] + al
                        o_v[ti, pl.ds(half + off, SC_LANES)] = o_v[ti, pl.ds(half + off, SC_LANES)] + ah
                        return carry2

                    lax.fori_loop(0, n_chunk, mix_chunk, 0)
                return carry1

            lax.fori_loop(0, SC_BLOCK, token, 0)
            pltpu.sync_copy(o_v, out_hbm.at[pl.ds(o0, SC_BLOCK)])
            return carry

        lax.fori_loop(0, tokens_per_worker // SC_BLOCK, block, 0)

    row_buf = pltpu.VMEM((SC_GATHER, D_MODEL), jnp.uint32)
    return pl.kernel(
        body, mesh=mesh,
        out_type=jax.ShapeDtypeStruct((n, D_MODEL), F32),
        scratch_types=[pltpu.VMEM((SC_BLOCK, PEER_NK), I32), pltpu.VMEM((SC_BLOCK, PEER_NK), F32),
                       pltpu.VMEM((SC_BLOCK, D_MODEL), F32), pltpu.VMEM((SC_BLOCK, D_MODEL), F32), row_buf, row_buf,
                       pltpu.SemaphoreType.DMA, pltpu.SemaphoreType.DMA],
        compiler_params=pltpu.CompilerParams(needs_layout_passes=False),
        name="peer_expert_sc",
    )(idx, gates, hn, x1, uv_rows)


def _rope_tables(pos):
    half = ROPE_DIMS // 2
    inv = ROPE_THETA ** (-(jnp.arange(half, dtype=F32) * 2.0 / ROPE_DIMS))
    ang = pos.astype(F32)[:, None] * inv[None, :]
    cos, sin = jnp.cos(ang), jnp.sin(ang)
    n = pos.shape[0]
    a = jnp.concatenate([cos, cos, jnp.ones((n, HEAD_DIM - ROPE_DIMS), F32)], axis=-1)
    b = jnp.concatenate([-sin, sin, jnp.zeros((n, HEAD_DIM - ROPE_DIMS), F32)], axis=-1)
    return jnp.tile(a, (1, 2)), jnp.tile(b, (1, 2))


def _tile2(v):
    return jnp.tile(v.reshape(1, HEAD_DIM), (1, 2))


def kernel(x, norm1_g, w_in, nsa_q_norm, nsa_k_norm, cmp_pe_k, cmp_w1_k, cmp_w2_k, cmp_pe_v, cmp_w1_v, cmp_w2_v,
           dil_q_norm, dil_k_norm, w_up_nsa, w_up_dil, w_o, norm2_g, peer_wq, peer_subkeys, peer_u, peer_v):
    B, S, D = x.shape
    assert D == D_MODEL and S % (DIL_PATTERNS[-1][1] * DIL_BLOCK) == 0 and S >= WIN + NSA_QB and (B * S) % 1024 == 0

    n_q, n_kv, n_gate, n_dil = 512, 768, 24, 2304
    o_gate = n_q + n_kv
    o_dil = o_gate + n_gate
    o_mg = o_dil + n_dil
    w_perm = jnp.concatenate([w_in[:, o_mg:], w_in[:, :o_gate], w_in[:, o_dil:o_mg], w_in[:, o_gate:o_dil],
                              jnp.zeros((D, IN_COLS_PAD - w_in.shape[1]), w_in.dtype)], axis=1).astype(BF16)
    blockdiag = jnp.asarray(np.kron(np.eye(2), np.ones((HEAD_DIM, HEAD_DIM))), BF16)
    rope_a, rope_b = _rope_tables(jnp.arange(S))
    sub = PEER_NKEYS // 2
    sk_pad = jnp.stack([jnp.pad(peer_subkeys[0], ((0, 0), (0, sub))), jnp.pad(peer_subkeys[1], ((0, 0), (sub, 0)))])
    uv_rows, uv_tiles = _pack_tables(peer_u, peer_v)
    eye2 = jnp.eye(2, dtype=F32)
    w1_grp = lambda w: jnp.einsum('ldh,gq->lgdqh', w.reshape(CMP_BLOCK, HEAD_DIM, CMP_HIDDEN), eye2).reshape(
        CMP_BLOCK * 2 * HEAD_DIM, 2 * CMP_HIDDEN).astype(BF16)
    w2_grp = lambda w: jnp.einsum('hd,gq->ghqd', w, eye2).reshape(2 * CMP_HIDDEN, 2 * HEAD_DIM).astype(BF16)
    pe_grp = lambda pe: jnp.broadcast_to(pe[:, None, :], (CMP_BLOCK, 2, HEAD_DIM)).reshape(1, -1)
    cmp_a, cmp_b = _rope_tables(jnp.arange(S // CMP_STRIDE) * CMP_STRIDE + CMP_BLOCK - 1)
    weights = dict(w_perm=w_perm, blockdiag=blockdiag, rope_a=rope_a, rope_b=rope_b, sk_pad=sk_pad, uv_rows=uv_rows,
                   uv_tiles=uv_tiles.reshape(-1, SUBLANES, LANES),
                   wn=w_up_nsa.astype(BF16), wd=w_up_dil.astype(BF16), wo=w_o.astype(BF16), wq=peer_wq.astype(BF16),
                   cmp=(pe_grp(cmp_pe_k), pe_grp(cmp_pe_v), w1_grp(cmp_w1_k), w1_grp(cmp_w1_v), w2_grp(cmp_w2_k),
                        w2_grp(cmp_w2_v)), cmp_a=cmp_a, cmp_b=cmp_b)
    params = (norm1_g, nsa_q_norm, nsa_k_norm, dil_q_norm, dil_k_norm, norm2_g)

    if B % 2 == 0 and (B // 2 * S) % 1024 == 0:
        groups = ((x[:B // 2], SC_SHARE_EARLY), (x[B // 2:], SC_SHARE))
    else:
        groups = ((x, SC_SHARE),)
    outs = [_layer(xg, share, weights, *params) for xg, share in groups]
    return outs[0] if len(outs) == 1 else jnp.concatenate(outs, axis=0)


def _layer(x, sc_share, weights, norm1_g, nsa_q_norm, nsa_k_norm, dil_q_norm, dil_k_norm, norm2_g):
    B, S, D = x.shape
    T = B * S
    x2 = x.reshape(T, D)
    blockdiag, rope_a, rope_b = weights["blockdiag"], weights["rope_a"], weights["rope_b"]
    z2 = _in_proj(x2, norm1_g.reshape(1, D), weights["w_perm"])
    z3 = z2.reshape(B, S, IN_COLS_PAD)

    qn, ksd, vsd, kwd, vwd = _nsa_prep(z3, rope_a, rope_b, blockdiag, _tile2(nsa_q_norm), _tile2(nsa_k_norm[1]),
                                       _tile2(nsa_k_norm[2]))

    n_cmp = (S - CMP_BLOCK) // CMP_STRIDE + 1
    ncp = S // CMP_STRIDE
    def flat_blocks(col):
        r = z3[:, :, col:col + LANES].reshape(B, ncp, CMP_STRIDE * LANES)
        nxt = jnp.concatenate([r[:, 1:], jnp.zeros_like(r[:, :1])], axis=1)
        return jnp.concatenate([r, nxt], axis=-1)
    kcd, vcd = _compress(flat_blocks(COL_KV), flat_blocks(COL_KV + LANES), *weights["cmp"], weights["cmp_a"],
                         weights["cmp_b"], blockdiag, _tile2(nsa_k_norm[0]))

    n_slc = S // SLC_BLOCK
    s0 = np.arange(n_cmp) * CMP_STRIDE
    b0 = np.arange(n_slc) * SLC_BLOCK
    ov = np.clip(np.minimum(s0[:, None] + CMP_BLOCK, b0[None, :] + SLC_BLOCK) - np.maximum(s0[:, None], b0[None, :]),
                 0, None) / CMP_BLOCK
    ov_pad = np.zeros((ncp, LANES), np.float32)
    ov_pad[:n_cmp, :n_slc] = ov
    y_nsa = _nsa_attn(qn, kcd, vcd, ksd, vsd, kwd, vwd, z3, jnp.asarray(ov_pad, BF16))

    gq = jnp.tile(dil_q_norm.reshape(3, 1, HEAD_DIM), (1, 1, 2))
    gk = jnp.tile(dil_k_norm.reshape(3, 1, HEAD_DIM), (1, 1, 2))
    prep = _dil_prep(z3, rope_a, rope_b, blockdiag, gq, gk)
    flat = [p.reshape(B, S, 256) for p in prep]
    y_dil = _dil_attn(flat[0::3], flat[1::3], flat[2::3])

    x1, hn, pq = _merge(x2, y_nsa.reshape(T, 512), y_dil.reshape(T, 256), z2, weights["wn"], weights["wd"], weights["wo"],
                        norm2_g.reshape(1, D), weights["wq"])

    idx, gates = _route_transpose(*_peer_route(pq, weights["sk_pad"]))
    sc_unit = SC_WORKERS * SC_BLOCK
    t_tc = T - (T * sc_share[0] // sc_share[1]) // sc_unit * sc_unit
    assert t_tc % PEER_TOK == 0 and t_tc >= 2 * PEER_TOK
    out_tc = _peer_expert(t_tc, idx, hn, gates, x1, weights["uv_tiles"])
    out_sc = _peer_expert_sc(t_tc, T - t_tc, idx, gates, hn, x1, weights["uv_rows"])
    return jnp.concatenate([out_tc, out_sc], axis=0).reshape(B, S, D)
```

```python
import functools

import numpy as np
import jax
import jax.numpy as jnp
from jax import lax
from jax.experimental import pallas as pl
from jax.experimental.pallas import tpu as pltpu
from jax.experimental.pallas import tpu_sc as plsc

F32 = jnp.float32
BF16 = jnp.bfloat16
I32 = jnp.int32

D_MODEL = 1024
HEAD_DIM = 64
ROPE_DIMS = 16
ROPE_THETA = 500000.0
NORM_EPS = 1e-6
NEG_INF = -1e30
LANES = 128

NSA_HEADS = 8
CMP_BLOCK = 32
CMP_STRIDE = 16
CMP_HIDDEN = 256
SLC_BLOCK = 64
SLC_TOPN = 16
FORCE_SCORE = 1e3
WIN = 512
NSA_QB = 128
SEL_CHUNK = 512

DIL_PATTERNS = ((128, 1), (512, 4), (2048, 16))
DIL_BLOCK = 128
DIL_PREP_ROWS = 1024

PEER_HEADS = 8
PEER_NKEYS = 128
PEER_TOPK = 16
PEER_TOK = 16

COL_MG = 0
COL_Q = 2048
COL_KV = 2560
COL_DIL = 3328
COL_GATE = 5632
IN_COLS_PAD = 5760

_NT = (((1,), (1,)), ((), ()))


def _dot(a, b):
    return jnp.dot(a, b, preferred_element_type=F32)


def _dot_nt(a, b):
    return lax.dot_general(a, b, _NT, preferred_element_type=F32)


def _split_bf16(a):
    hi = a.astype(BF16)
    lo = (a - hi.astype(F32)).astype(BF16)
    return hi, lo


def _dot_hilo(a, b_bf16):
    hi, lo = _split_bf16(a)
    return _dot(hi, b_bf16) + _dot(lo, b_bf16)


def _head_norm_rope(zt, gain, rope_a, rope_b, blockdiag):
    ss = _dot_hilo(zt * zt, blockdiag)
    zn = zt * lax.rsqrt(ss * (1.0 / HEAD_DIM) + NORM_EPS) * gain
    d = lax.broadcasted_iota(I32, zn.shape, 1) & (HEAD_DIM - 1)
    half = ROPE_DIMS // 2
    partner = jnp.where(d < half, pltpu.roll(zn, LANES - half, 1), pltpu.roll(zn, half, 1))
    return zn * rope_a + partner * rope_b


def _half_masks():
    lane = lax.broadcasted_iota(I32, (1, LANES), 1)
    lo = (lane < HEAD_DIM).astype(BF16)
    return lo, (1 - lo).astype(BF16)


def _inproj_kernel(x_ref, g_ref, w_ref, o_ref, h_scr):
    @pl.when(pl.program_id(1) == 0)
    def _():
        xf = x_ref[...]
        ms = jnp.mean(xf * xf, axis=-1, keepdims=True)
        h_scr[...] = (xf * lax.rsqrt(ms + NORM_EPS) * g_ref[...]).astype(BF16)

    o_ref[...] = _dot(h_scr[...], w_ref[...])


def _in_proj(x2, g1, w_bf16):
    T = x2.shape[0]
    tm, tn = 1024, 640
    return pl.pallas_call(
        _inproj_kernel,
        out_shape=jax.ShapeDtypeStruct((T, IN_COLS_PAD), F32),
        grid=(T // tm, IN_COLS_PAD // tn),
        in_specs=[
            pl.BlockSpec((tm, D_MODEL), lambda i, j: (i, 0)),
            pl.BlockSpec((1, D_MODEL), lambda i, j: (0, 0)),
            pl.BlockSpec((D_MODEL, tn), lambda i, j: (0, j)),
        ],
        out_specs=pl.BlockSpec((tm, tn), lambda i, j: (i, j)),
        scratch_shapes=[pltpu.VMEM((tm, D_MODEL), BF16)],
        compiler_params=pltpu.CompilerParams(dimension_semantics=("parallel", "arbitrary")),
        name="in_proj",
    )(x2, g1, w_bf16)


def _nsa_prep_kernel(zq_ref, zks_ref, zvs_ref, zkw_ref, zvw_ref, ra_ref, rb_ref, bd_ref, gq_ref, gks_ref, gkw_ref,
                     q_ref, ks_ref, vs_ref, kw_ref, vw_ref):
    ra, rb, bd = ra_ref[...], rb_ref[...], bd_ref[...]
    lane = lax.broadcasted_iota(I32, ra.shape, 1)
    scale = HEAD_DIM ** -0.5

    zq = zq_ref[0]
    tiles = [_head_norm_rope(zq[:, m * LANES:(m + 1) * LANES], gq_ref[...], ra, rb, bd) * scale for m in range(4)]
    q_ref[0] = jnp.concatenate(tiles, axis=-1).astype(BF16)

    def dup(t, out_ref):
        sw = pltpu.roll(t, HEAD_DIM, 1)
        out_ref[0, 0] = jnp.where(lane < HEAD_DIM, t, sw).astype(BF16)
        out_ref[0, 1] = jnp.where(lane < HEAD_DIM, sw, t).astype(BF16)

    dup(_head_norm_rope(zks_ref[0], gks_ref[...], ra, rb, bd), ks_ref)
    dup(_head_norm_rope(zkw_ref[0], gkw_ref[...], ra, rb, bd), kw_ref)
    dup(zvs_ref[0], vs_ref)
    dup(zvw_ref[0], vw_ref)


def _nsa_prep(z3, rope_a, rope_b, blockdiag, gq, gks, gkw):
    B, S, _ = z3.shape
    tm = 512
    kvb = COL_KV // LANES
    zcol = lambda c: pl.BlockSpec((1, tm, LANES), lambda b, i, c=c: (b, i, c))
    const = lambda shape: pl.BlockSpec(shape, lambda b, i: tuple(0 for _ in shape))
    kv_out = pl.BlockSpec((1, 2, tm, LANES), lambda b, i: (b, 0, i, 0))
    kv_shape = jax.ShapeDtypeStruct((B, 2, S, LANES), BF16)
    return pl.pallas_call(
        _nsa_prep_kernel,
        out_shape=(jax.ShapeDtypeStruct((B, S, 512), BF16), kv_shape, kv_shape, kv_shape, kv_shape),
        grid=(B, S // tm),
        in_specs=[
            pl.BlockSpec((1, tm, 512), lambda b, i: (b, i, COL_Q // 512)),
            zcol(kvb + 2), zcol(kvb + 3), zcol(kvb + 4), zcol(kvb + 5),
            pl.BlockSpec((tm, LANES), lambda b, i: (i, 0)),
            pl.BlockSpec((tm, LANES), lambda b, i: (i, 0)),
            const((LANES, LANES)), const((1, LANES)), const((1, LANES)), const((1, LANES)),
        ],
        out_specs=(pl.BlockSpec((1, tm, 512), lambda b, i: (b, i, 0)), kv_out, kv_out, kv_out, kv_out),
        compiler_params=pltpu.CompilerParams(dimension_semantics=("parallel", "parallel")),
        name="nsa_prep",
    )(z3, z3, z3, z3, z3, rope_a, rope_b, blockdiag, gq, gks, gkw)


def _compress_kernel(fk_ref, fv_ref, pek_ref, pev_ref, w1k_ref, w1v_ref, w2k_ref, w2v_ref, ra_ref, rb_ref, bd_ref, gk_ref,
                     k_ref, v_ref):
    lane = lax.broadcasted_iota(I32, (fk_ref.shape[1], LANES), 1)

    def mlp(f_ref, pe_ref, w1_ref, w2_ref):
        f = (f_ref[0] + pe_ref[...]).astype(BF16)
        h = jax.nn.gelu(_dot(f, w1_ref[...]))
        return _dot(h.astype(BF16), w2_ref[...])

    def dup(t, out_ref):
        sw = pltpu.roll(t, HEAD_DIM, 1)
        out_ref[0, 0] = jnp.where(lane < HEAD_DIM, t, sw).astype(BF16)
        out_ref[0, 1] = jnp.where(lane < HEAD_DIM, sw, t).astype(BF16)

    kc = mlp(fk_ref, pek_ref, w1k_ref, w2k_ref)
    dup(_head_norm_rope(kc, gk_ref[...], ra_ref[...], rb_ref[...], bd_ref[...]), k_ref)
    dup(mlp(fv_ref, pev_ref, w1v_ref, w2v_ref), v_ref)


def _compress(flat_k, flat_v, pek, pev, w1k, w1v, w2k, w2v, rope_a, rope_b, blockdiag, gk):
    B, nblk, width = flat_k.shape
    row = pl.BlockSpec((1, nblk, width), lambda b: (b, 0, 0))
    const = lambda shape: pl.BlockSpec(shape, lambda b: tuple(0 for _ in shape))
    out = pl.BlockSpec((1, 2, nblk, LANES), lambda b: (b, 0, 0, 0))
    shp = jax.ShapeDtypeStruct((B, 2, nblk, LANES), BF16)
    return pl.pallas_call(
        _compress_kernel,
        out_shape=(shp, shp),
        grid=(B,),
        in_specs=[row, row, const((1, width)), const((1, width)), const((width, 2 * CMP_HIDDEN)),
                  const((width, 2 * CMP_HIDDEN)), const((2 * CMP_HIDDEN, LANES)), const((2 * CMP_HIDDEN, LANES)),
                  const((nblk, LANES)), const((nblk, LANES)), const((LANES, LANES)), const((1, LANES))],
        out_specs=(out, out),
        compiler_params=pltpu.CompilerParams(dimension_semantics=("parallel",), vmem_limit_bytes=48 * 1024 * 1024),
        name="compress",
    )(flat_k, flat_v, pek, pev, w1k, w1v, w2k, w2v, rope_a, rope_b, blockdiag, gk)


def _softmax_rows(s):
    m = jnp.max(s, axis=-1, keepdims=True)
    e = jnp.exp(s - m)
    return e / jnp.sum(e, axis=-1, keepdims=True)


def _nsa_kernel(q_ref, kc_ref, vc_ref, ks_ref, vs_ref, kw_ref, vw_ref, gl_ref, ov_ref, ex_ref, y_ref):
    n = pl.program_id(1)
    t0 = n * NSA_QB
    qt = q_ref[0]
    hm = _half_masks()
    lane = lax.broadcasted_iota(I32, (NSA_QB, LANES), 1)
    t1 = t0 + lax.broadcasted_iota(I32, (NSA_QB, 1), 0)
    t4 = t0 + (lax.broadcasted_iota(I32, (4 * NSA_QB, 1), 0) & (NSA_QB - 1))
    gates = jax.nn.sigmoid(gl_ref[0])

    n_slc_blocks = ks_ref.shape[2] // SLC_BLOCK
    blk = lax.broadcasted_iota(I32, (n_slc_blocks, NSA_QB), 0)
    tq = t0 + lax.broadcasted_iota(I32, (n_slc_blocks, NSA_QB), 1)
    cur = tq >> 6
    forced = (blk == 0) | (blk == cur) | (blk == cur - 1)
    causal_b = blk * SLC_BLOCK <= tq

    tiles_out = []
    for g in range(2):
        q4 = jnp.concatenate(
            [qt[:, (2 * g + jj // 2) * LANES:(2 * g + jj // 2 + 1) * LANES] * hm[jj % 2] for jj in range(4)], axis=0)

        sc = _dot_nt(q4, kc_ref[0, g])
        cend = lax.broadcasted_iota(I32, (1, sc.shape[1]), 1) * CMP_STRIDE + (CMP_BLOCK - 1)
        valid_c = cend <= t4
        pc = _softmax_rows(jnp.where(valid_c, sc, NEG_INF))
        pc = jnp.where(valid_c, pc, 0.0)
        o_c = _dot(pc.astype(BF16), vc_ref[0, g])
        psum = pc[0:NSA_QB] + pc[NSA_QB:2 * NSA_QB] + pc[2 * NSA_QB:3 * NSA_QB] + pc[3 * NSA_QB:]
        imp = _dot_hilo(psum, ov_ref[...])

        score = jnp.where(forced, FORCE_SCORE, jnp.where(causal_b, imp.T[:n_slc_blocks], -1.0))
        n_grp = n_slc_blocks // SUBLANES
        grp = [score[SUBLANES * r:SUBLANES * (r + 1)] for r in range(n_grp)]
        ranks = [jnp.zeros((SUBLANES, NSA_QB), F32) for _ in range(n_grp)]
        row = lax.broadcasted_iota(I32, (SUBLANES, NSA_QB), 0)
        for i in range(n_slc_blocks):
            ri = grp[i // SUBLANES][i % SUBLANES:i % SUBLANES + 1, :]
            for r in range(n_grp):
                if r > i // SUBLANES:
                    ahead = ri >= grp[r]
                elif r < i // SUBLANES:
                    ahead = ri > grp[r]
                else:
                    ahead = (ri > grp[r]) | ((ri == grp[r]) & (row > i % SUBLANES))
                ranks[r] = ranks[r] + jnp.where(ahead, 1.0, 0.0)
        sel64 = jnp.where(jnp.concatenate(ranks, axis=0) < float(SLC_TOPN), 1.0, 0.0)
        sel = jnp.concatenate([sel64, jnp.zeros((LANES - n_slc_blocks, NSA_QB), F32)], axis=0).T.astype(BF16)

        def sel_chunk(c, carry, last):
            m_i, l_i, acc = carry
            k0 = pl.multiple_of(c * SEL_CHUNK, SEL_CHUNK)
            kch = ks_ref[0, g, pl.ds(k0, SEL_CHUNK), :]
            vch = vs_ref[0, g, pl.ds(k0, SEL_CHUNK), :]
            picked = _dot(sel, ex_ref[:, pl.ds(k0, SEL_CHUNK)])
            if last:
                kpos = k0 + lax.broadcasted_iota(I32, (1, SEL_CHUNK), 1)
                picked = jnp.where(kpos <= t1, picked, 0.0)
            bias = (picked - 1.0) * (-NEG_INF)
            s = (_dot_nt(q4, kch).reshape(4, NSA_QB, SEL_CHUNK) + bias[None]).reshape(4 * NSA_QB, SEL_CHUNK)
            m_new = jnp.maximum(m_i, jnp.max(s, axis=-1, keepdims=True))
            alpha = jnp.exp(m_i - m_new)
            p = jnp.exp(s - m_new)
            l_new = alpha * l_i + jnp.sum(p, axis=-1, keepdims=True)
            acc_new = alpha * acc + _dot(p.astype(BF16), vch)
            return m_new, l_new, acc_new

        init = (jnp.full((4 * NSA_QB, 1), NEG_INF, F32), jnp.zeros((4 * NSA_QB, 1), F32),
                jnp.zeros((4 * NSA_QB, LANES), F32))
        n_full = t0 // SEL_CHUNK
        carry = lax.fori_loop(0, n_full, functools.partial(sel_chunk, last=False), init)
        _, l_s, acc_s = sel_chunk(n_full, carry, last=True)
        o_s = acc_s / l_s

        wlen = WIN + NSA_QB
        ws = pl.multiple_of(jnp.maximum(t0 - WIN, 0), NSA_QB)
        sw = _dot_nt(q4, kw_ref[0, g, pl.ds(ws, wlen), :])
        dist = t4 - (ws + lax.broadcasted_iota(I32, (1, wlen), 1))
        pw = _softmax_rows(jnp.where((dist >= 0) & (dist < WIN), sw, NEG_INF))
        o_w = _dot(pw.astype(BF16), vw_ref[0, g, pl.ds(ws, wlen), :])

        heads = []
        for jj in range(4):
            h = 4 * g + jj
            rows = slice(jj * NSA_QB, (jj + 1) * NSA_QB)
            heads.append(gates[:, 3 * h:3 * h + 1] * o_c[rows] + gates[:, 3 * h + 1:3 * h + 2] * o_s[rows]
                         + gates[:, 3 * h + 2:3 * h + 3] * o_w[rows])
        tiles_out.append(jnp.where(lane < HEAD_DIM, heads[0], heads[1]))
        tiles_out.append(jnp.where(lane < HEAD_DIM, heads[2], heads[3]))

    y_ref[0] = jnp.concatenate(tiles_out, axis=-1).astype(BF16)


def _nsa_attn(qn, kcd, vcd, ksd, vsd, kwd, vwd, z3, overlap):
    B, S, _ = qn.shape
    ncp = kcd.shape[2]
    assert S // SLC_BLOCK <= LANES
    expand = jnp.asarray(np.arange(LANES)[:, None] == (np.arange(S)[None, :] // SLC_BLOCK), BF16)
    full = lambda rows: pl.BlockSpec((1, 2, rows, LANES), lambda b, n: (b, 0, 0, 0))
    return pl.pallas_call(
        _nsa_kernel,
        out_shape=jax.ShapeDtypeStruct((B, S, 512), BF16),
        grid=(B, S // NSA_QB),
        in_specs=[
            pl.BlockSpec((1, NSA_QB, 512), lambda b, n: (b, n, 0)),
            full(ncp), full(ncp), full(S), full(S), full(S), full(S),
            pl.BlockSpec((1, NSA_QB, LANES), lambda b, n: (b, n, COL_GATE // LANES)),
            pl.BlockSpec((ncp, LANES), lambda b, n: (0, 0)),
            pl.BlockSpec((LANES, S), lambda b, n: (0, 0)),
        ],
        out_specs=pl.BlockSpec((1, NSA_QB, 512), lambda b, n: (b, n, 0)),
        compiler_params=pltpu.CompilerParams(dimension_semantics=("parallel", "arbitrary"),
                                             vmem_limit_bytes=48 * 1024 * 1024),
        name="nsa_attn",
    )(qn, kcd, vcd, ksd, vsd, kwd, vwd, z3, overlap, expand)


def _dil_prep_kernel(*refs):
    zs, (ra_ref, rb_ref, bd_ref, gq_ref, gk_ref), outs = refs[0:18], refs[18:23], refs[23:32]
    bd = bd_ref[...]
    scale = HEAD_DIM ** -0.5
    for g, (_, d) in enumerate(DIL_PATTERNS):
        n = DIL_PREP_ROWS // d
        for r in range(d):
            rows = pl.ds(r, n, stride=d) if d > 1 else pl.ds(0, n)
            ra, rb = ra_ref[rows, :], rb_ref[rows, :]
            for which in range(3):
                for m in range(2):
                    z = zs[2 * (3 * g + which) + m][0, rows, :]
                    if which == 0:
                        z = _head_norm_rope(z, gq_ref[g], ra, rb, bd) * scale
                    elif which == 1:
                        z = _head_norm_rope(z, gk_ref[g], ra, rb, bd)
                    outs[3 * g + which][0, r, :, m * LANES:(m + 1) * LANES] = z.astype(BF16)


def _dil_prep(z3, rope_a, rope_b, blockdiag, gq, gk):
    B, S, _ = z3.shape
    nsteps = S // DIL_PREP_ROWS
    c0 = COL_DIL // LANES
    in_specs = [pl.BlockSpec((1, DIL_PREP_ROWS, LANES), lambda b, c, k=k: (b, c, c0 + k)) for k in range(18)]
    in_specs += [
        pl.BlockSpec((DIL_PREP_ROWS, LANES), lambda b, c: (c, 0)),
        pl.BlockSpec((DIL_PREP_ROWS, LANES), lambda b, c: (c, 0)),
        pl.BlockSpec((LANES, LANES), lambda b, c: (0, 0)),
        pl.BlockSpec((3, 1, LANES), lambda b, c: (0, 0, 0)),
        pl.BlockSpec((3, 1, LANES), lambda b, c: (0, 0, 0)),
    ]
    out_shape, out_specs = [], []
    for _, d in DIL_PATTERNS:
        for _ in range(3):
            out_shape.append(jax.ShapeDtypeStruct((B, d, S // d, 256), BF16))
            out_specs.append(pl.BlockSpec((1, d, DIL_PREP_ROWS // d, 256), lambda b, c: (b, 0, c, 0)))
    return pl.pallas_call(
        _dil_prep_kernel,
        out_shape=tuple(out_shape),
        grid=(B, nsteps),
        in_specs=in_specs,
        out_specs=tuple(out_specs),
        compiler_params=pltpu.CompilerParams(dimension_semantics=("parallel", "parallel"),
                                             vmem_limit_bytes=48 * 1024 * 1024),
        name="dil_prep",
    )(*([z3] * 18), rope_a, rope_b, blockdiag, gq, gk)


def _dil_kernel(*refs, seq):
    q_refs, k_refs, v_refs, y_ref, o_scr, l_scr = refs[0:3], refs[3:6], refs[6:9], refs[9], refs[10], refs[11]
    hm = _half_masks()
    lane = lax.broadcasted_iota(I32, (DIL_BLOCK, LANES), 1)
    qi = lax.broadcasted_iota(I32, (2 * DIL_BLOCK, 2 * DIL_BLOCK), 0) & (DIL_BLOCK - 1)
    ki = lax.broadcasted_iota(I32, (2 * DIL_BLOCK, 2 * DIL_BLOCK), 1)
    causal = (ki - DIL_BLOCK) <= qi

    for g, (_, d) in enumerate(DIL_PATTERNS):
        nb = seq // d // DIL_BLOCK

        def body(u, carry, g=g, d=d, nb=nb):
            j = u % nb
            r = u // nb
            r0 = pl.multiple_of(u * DIL_BLOCK, DIL_BLOCK)
            p0 = pl.multiple_of(jnp.maximum(u - 1, 0) * DIL_BLOCK, DIL_BLOCK)
            q = q_refs[g][0, pl.ds(r0, DIL_BLOCK), :]
            kcat = jnp.concatenate([k_refs[g][0, pl.ds(p0, DIL_BLOCK), :], k_refs[g][0, pl.ds(r0, DIL_BLOCK), :]], axis=0)
            vcat = jnp.concatenate([v_refs[g][0, pl.ds(p0, DIL_BLOCK), :], v_refs[g][0, pl.ds(r0, DIL_BLOCK), :]], axis=0)
            q2 = jnp.concatenate([q * hm[0], q * hm[1]], axis=0)
            s = _dot_nt(q2, kcat)
            first_key = jnp.maximum(qi, jnp.where(j >= 1, 0, DIL_BLOCK))
            s = jnp.where(causal & (ki >= first_key), s, NEG_INF)
            m = jnp.max(s, axis=-1, keepdims=True)
            e = jnp.exp(s - m)
            den = jnp.sum(e, axis=-1, keepdims=True)
            o2 = _dot(e.astype(BF16), vcat) / den
            lse = m + jnp.log(den)
            o = jnp.where(lane < HEAD_DIM, o2[:DIL_BLOCK], o2[DIL_BLOCK:])
            lv = jnp.where(lane < HEAD_DIM, lse[:DIL_BLOCK], lse[DIL_BLOCK:])
            tok0 = j * (DIL_BLOCK * d) + r
            rows = pl.ds(tok0, DIL_BLOCK, stride=d) if d > 1 else pl.ds(pl.multiple_of(tok0, DIL_BLOCK), DIL_BLOCK)
            o_scr[g, rows, :] = o
            l_scr[g, rows, :] = lv
            return carry

        lax.fori_loop(0, seq // DIL_BLOCK, body, 0)

    def merge(c, carry):
        rows = pl.ds(pl.multiple_of(c * 512, 512), 512)
        ls = [l_scr[g, rows, :] for g in range(3)]
        mx = jnp.maximum(jnp.maximum(ls[0], ls[1]), ls[2])
        ws = [jnp.exp(l - mx) for l in ls]
        num = ws[0] * o_scr[0, rows, :] + ws[1] * o_scr[1, rows, :] + ws[2] * o_scr[2, rows, :]
        y_ref[0, rows, :] = (num / (ws[0] + ws[1] + ws[2])).astype(BF16)
        return carry

    lax.fori_loop(0, seq // 512, merge, 0)


def _dil_attn(dq, dk, dv):
    B, S, _ = dq[0].shape
    spec = pl.BlockSpec((1, S, LANES), lambda b, m: (b, 0, m))
    return pl.pallas_call(
        functools.partial(_dil_kernel, seq=S),
        out_shape=jax.ShapeDtypeStruct((B, S, 256), BF16),
        grid=(B, 2),
        in_specs=[spec] * 9,
        out_specs=spec,
        scratch_shapes=[pltpu.VMEM((3, S, LANES), F32), pltpu.VMEM((3, S, LANES), F32)],
        compiler_params=pltpu.CompilerParams(dimension_semantics=("parallel", "parallel"),
                                             vmem_limit_bytes=56 * 1024 * 1024),
        name="dil_attn",
    )(*dq, *dk, *dv)


def _merge_kernel(x_ref, yn_ref, yd_ref, mg0_ref, mg1_ref, wn_ref, wd_ref, wo_ref, g2_ref, wq_ref, x1_ref, hn_ref, pq_ref):
    u1 = _dot(yn_ref[...], wn_ref[...])
    u2 = _dot(yd_ref[...], wd_ref[...])
    merged = jax.nn.sigmoid(mg0_ref[...]) * u1 + jax.nn.sigmoid(mg1_ref[...]) * u2
    x1 = x_ref[...] + _dot(merged.astype(BF16), wo_ref[...])
    x1_ref[...] = x1
    ms = jnp.mean(x1 * x1, axis=-1, keepdims=True)
    hn = x1 * lax.rsqrt(ms + NORM_EPS) * g2_ref[...]
    hn_ref[...] = hn
    pq_ref[...] = _dot(hn.astype(BF16), wq_ref[...])


def _merge(x2, yn2, yd2, z2, wn, wd, wo, g2, wq):
    T = x2.shape[0]
    tm = 512
    row = lambda w, c=0: pl.BlockSpec((tm, w), lambda i, c=c: (i, c))
    const = lambda shape: pl.BlockSpec(shape, lambda i: (0, 0))
    shp = jax.ShapeDtypeStruct((T, D_MODEL), F32)
    return pl.pallas_call(
        _merge_kernel,
        out_shape=(shp, shp, shp),
        grid=(T // tm,),
        in_specs=[row(D_MODEL), row(512), row(256), row(D_MODEL, COL_MG // D_MODEL), row(D_MODEL, COL_MG // D_MODEL + 1),
                  const((512, D_MODEL)), const((256, D_MODEL)), const((D_MODEL, D_MODEL)), const((1, D_MODEL)),
                  const((D_MODEL, D_MODEL))],
        out_specs=(row(D_MODEL), row(D_MODEL), row(D_MODEL)),
        compiler_params=pltpu.CompilerParams(dimension_semantics=("parallel",), vmem_limit_bytes=48 * 1024 * 1024),
        name="merge",
    )(x2, yn2, yd2, z2, z2, wn, wd, wo, g2, wq)


def _top16(s, rank_id=None):
    if rank_id is None:
        rank_id = lax.broadcasted_iota(I32, (s.shape[0], 1), 0)
    big = jnp.iinfo(jnp.int32).max
    vals, ids = [], []
    for _ in range(PEER_TOPK):
        m = jnp.max(s, axis=0, keepdims=True)
        win = jnp.min(jnp.where(s == m, rank_id, big), axis=0, keepdims=True)
        vals.append(m)
        ids.append(win)
        s = jnp.where(rank_id == win, -jnp.inf, s)
    return vals, ids


def _route_kernel(q_ref, sk_ref, idx_ref, gate_ref):
    K = PEER_TOPK
    qh, ql = _split_bf16(q_ref[...])
    vals, ids = [], []
    for c in range(2):
        kh, kl = _split_bf16(sk_ref[c])
        s = _dot_nt(kh, qh) + _dot_nt(kh, ql) + _dot_nt(kl, qh)
        v, p = _top16(s)
        vals.append(v)
        ids.append(p)
    v0, p0 = jnp.concatenate(vals[0], axis=0), jnp.concatenate(ids[0], axis=0)
    v1, p1 = jnp.concatenate(vals[1], axis=0), jnp.concatenate(ids[1], axis=0)

    a8 = lax.broadcasted_iota(I32, (SUBLANES, 1), 0)
    pieces = [(v0 + vals[1][0], p0 * PEER_NKEYS + ids[1][0], lax.broadcasted_iota(I32, (K, 1), 0) * K)]
    for b in range(1, SUBLANES):
        keep = a8 < K // (b + 1)
        pieces.append((jnp.where(keep, v0[:SUBLANES] + vals[1][b], -jnp.inf), p0[:SUBLANES] * PEER_NKEYS + ids[1][b],
                       a8 * K + b))
    pieces.append((vals[0][0] + v1[SUBLANES:], ids[0][0] * PEER_NKEYS + p1[SUBLANES:], a8 + SUBLANES))
    cand = jnp.concatenate([p[0] for p in pieces], axis=0)
    eid = jnp.concatenate([p[1] for p in pieces], axis=0)
    flat = jnp.concatenate([p[2] for p in pieces], axis=0)
    v, win = _top16(cand, flat)
    sel_ids = [jnp.sum(jnp.where(flat == w, eid, 0), axis=0, keepdims=True) for w in win]
    sc = jnp.concatenate(v, axis=0)
    e = jnp.exp(sc - sc[0:1])
    gate_ref[...] = e / jnp.sum(e, axis=0, keepdims=True)
    idx_ref[...] = jnp.concatenate(sel_ids, axis=0)


def _peer_route(pq, sk_pad):
    T = pq.shape[0]
    tt = 512
    return pl.pallas_call(
        _route_kernel,
        out_shape=(jax.ShapeDtypeStruct((PEER_HEADS * PEER_TOPK, T), I32),
                   jax.ShapeDtypeStruct((PEER_HEADS * PEER_TOPK, T), F32)),
        grid=(T // tt, PEER_HEADS),
        in_specs=[pl.BlockSpec((tt, LANES), lambda i, h: (i, h)),
                  pl.BlockSpec((2, PEER_NKEYS, LANES), lambda i, h: (0, 0, 0))],
        out_specs=(pl.BlockSpec((PEER_TOPK, tt), lambda i, h: (h, i)),
                   pl.BlockSpec((PEER_TOPK, tt), lambda i, h: (h, i))),
        compiler_params=pltpu.CompilerParams(dimension_semantics=("parallel", "parallel")),
        name="peer_route",
    )(pq, sk_pad)


def _route_t_kernel(idx_ref, gate_ref, idx_o, gate_o):
    idx_o[...] = idx_ref[...].T
    gate_o[...] = gate_ref[...].T


def _route_transpose(idx_t, gate_t):
    nk, T = idx_t.shape
    tt = 512
    src = pl.BlockSpec((nk, tt), lambda i: (0, i))
    dst = pl.BlockSpec((tt, nk), lambda i: (i, 0))
    return pl.pallas_call(
        _route_t_kernel,
        out_shape=(jax.ShapeDtypeStruct((T, nk), I32), jax.ShapeDtypeStruct((T, nk), F32)),
        grid=(T // tt,),
        in_specs=[src, src],
        out_specs=(dst, dst),
        compiler_params=pltpu.CompilerParams(dimension_semantics=("parallel",)),
        name="route_transpose",
    )(idx_t, gate_t)


def _pack_words(w):
    half = w.shape[1] // 2
    bits = pltpu.bitcast(w.astype(BF16).astype(F32), jnp.uint32)
    return (bits[:, :half] >> 16) | (bits[:, half:] & jnp.uint32(0xFFFF0000))


def _pack_kernel(u_ref, v_ref, rows_ref, tiles_ref):
    n = u_ref.shape[0]
    words = jnp.concatenate([_pack_words(u_ref[...]), _pack_words(v_ref[...])], axis=-1)
    rows_ref[...] = words
    for s in range(SUBLANES):
        tiles_ref[pl.ds(s, n, stride=SUBLANES), :] = words[:, s * LANES:(s + 1) * LANES]


def _pack_tables(u, v):
    ne, d = u.shape
    tm = 256
    return pl.pallas_call(
        _pack_kernel,
        out_shape=(jax.ShapeDtypeStruct((ne, d), jnp.uint32), jax.ShapeDtypeStruct((ne * SUBLANES, LANES), jnp.uint32)),
        grid=(ne // tm,),
        in_specs=[pl.BlockSpec((tm, d), lambda i: (i, 0)), pl.BlockSpec((tm, d), lambda i: (i, 0))],
        out_specs=(pl.BlockSpec((tm, d), lambda i: (i, 0)), pl.BlockSpec((tm * SUBLANES, LANES), lambda i: (i, 0))),
        compiler_params=pltpu.CompilerParams(dimension_semantics=("parallel",)),
        name="pack_tables",
    )(u, v)


PEER_NK = PEER_HEADS * PEER_TOPK
SUBLANES = 8


def _unpack_pair(w):
    lo = pltpu.bitcast(w << 16, F32)
    hi = pltpu.bitcast(w & jnp.uint32(0xFFFF0000), F32)
    return lo, hi


def _peer_kernel(idx_cur, idx_nxt, hn_ref, gate_ref, x1_ref, uv_hbm, o_ref, buf_a, buf_b, sem):
    i = pl.program_id(0)
    last = pl.num_programs(0) - 1
    half_rows = SUBLANES // 2
    bufs = (buf_a, buf_b)

    def tile_copy(idx_ref, s, c, k):
        p = c * PEER_NK + k
        return pltpu.make_async_copy(uv_hbm.at[idx_ref[c, k]], bufs[s].at[pl.ds(p * SUBLANES, SUBLANES), :], sem.at[s, c])

    def fetch_token(idx_ref, s, c):
        for k in range(PEER_NK):
            tile_copy(idx_ref, s, c, k).start(priority=k % 2)

    def wait_token(idx_ref, s, c):
        for k in range(PEER_NK):
            tile_copy(idx_ref, s, c, k).wait()

    @pl.when(i == 0)
    def _():
        lax.fori_loop(0, PEER_TOK, lambda c, carry: (fetch_token(idx_cur, 0, c), carry)[1], 0)

    eye = lax.broadcasted_iota(I32, (PEER_NK, PEER_NK), 0) == lax.broadcasted_iota(I32, (PEER_NK, PEER_NK), 1)

    def token(slot, c):
        wait_token(idx_cur, slot, c)
        fetch_token(idx_nxt, 1 - slot, c)
        base = c * (PEER_NK * SUBLANES)
        plane = lambda s: bufs[slot][pl.ds(base + s, PEER_NK, stride=SUBLANES), :]
        x = hn_ref[pl.ds(c, 1), :]
        xs = lambda s: x[:, s * LANES:(s + 1) * LANES]
        gcol = jnp.sum(jnp.where(eye, gate_ref[pl.ds(c, 1), :], 0.0), axis=-1, keepdims=True)
        acc = jnp.zeros((PEER_NK, LANES), F32)
        for s in range(half_rows):
            lo, hi = _unpack_pair(plane(s))
            acc = acc + lo * xs(s) + hi * xs(s + half_rows)
        act = jax.nn.gelu(jnp.sum(acc, axis=-1, keepdims=True)) * gcol
        lo_cols, hi_cols = [], []
        for s in range(half_rows, SUBLANES):
            lo, hi = _unpack_pair(plane(s))
            lo_cols.append(jnp.sum(act * lo, axis=0, keepdims=True))
            hi_cols.append(jnp.sum(act * hi, axis=0, keepdims=True))
        o_ref[pl.ds(c, 1), :] = x1_ref[pl.ds(c, 1), :] + jnp.concatenate(lo_cols + hi_cols, axis=-1)

    def step(slot):
        for c in range(PEER_TOK):
            token(slot, c)

        @pl.when(i == last)
        def _():
            lax.fori_loop(0, PEER_TOK, lambda c, carry: (wait_token(idx_nxt, 1 - slot, c), carry)[1], 0)

    pl.when(i % 2 == 0)(lambda: step(0))
    pl.when(i % 2 == 1)(lambda: step(1))


def _peer_expert(n_tokens, idx_flat, hn, gates, x1, uv_tiles):
    n = n_tokens // PEER_TOK
    row = pl.BlockSpec((PEER_TOK, D_MODEL), lambda i: (i, 0))
    fetch_buf = pltpu.VMEM((PEER_TOK * PEER_NK * SUBLANES, LANES), jnp.uint32)
    return pl.pallas_call(
        _peer_kernel,
        out_shape=jax.ShapeDtypeStruct((n_tokens, D_MODEL), F32),
        grid=(n,),
        in_specs=[pl.BlockSpec((PEER_TOK, PEER_NK), lambda i: (i, 0), memory_space=pltpu.SMEM),
                  pl.BlockSpec((PEER_TOK, PEER_NK), lambda i: (jnp.minimum(i + 1, n - 1), 0), memory_space=pltpu.SMEM),
                  row, pl.BlockSpec((PEER_TOK, PEER_NK), lambda i: (i, 0)), row,
                  pl.BlockSpec(memory_space=pl.ANY)],
        out_specs=row,
        scratch_shapes=[fetch_buf, fetch_buf, pltpu.SemaphoreType.DMA((2, PEER_TOK))],
        compiler_params=pltpu.CompilerParams(dimension_semantics=("arbitrary",), vmem_limit_bytes=48 * 1024 * 1024),
        name="peer_expert",
    )(idx_flat, idx_flat, hn, gates, x1, uv_tiles)


SC_LANES = 16
SC_WORKERS = 32
SC_GATHER = 32
SC_BLOCK = 8
SC_SHARE = (13, 32)
SC_SHARE_EARLY = (24, 32)


def _sc_lane_bcast(vec, k):
    idx = jnp.full((SC_LANES, 1), k, I32)
    dn = lax.GatherDimensionNumbers(offset_dims=(), collapsed_slice_dims=(0,), start_index_map=(0,))
    return lax.gather(vec, idx, dn, slice_sizes=(1,), mode=lax.GatherScatterMode.PROMISE_IN_BOUNDS)


def _sc_unpack_pair(w):
    return plsc.bitcast(w << 16, F32), plsc.bitcast(w & jnp.uint32(0xFFFF0000), F32)


def _peer_expert_sc(t_start, n, idx, gates, hn, x1, uv_rows):
    assert n % (SC_WORKERS * SC_BLOCK) == 0 and t_start % SC_BLOCK == 0
    tokens_per_worker = n // SC_WORKERS
    n_gather = PEER_NK // SC_GATHER
    half = D_MODEL // 2
    n_chunk = half // SC_LANES
    mesh = plsc.VectorSubcoreMesh(core_axis_name="c", subcore_axis_name="s")

    def body(idx_hbm, g_hbm, hn_hbm, x1_hbm, uv_hbm, out_hbm, idx_v, g_v, x_v, o_v, rows0, rows1, sem0, sem1):
        worker = lax.axis_index("s") * 2 + lax.axis_index("c")
        lane = lax.iota(I32, SC_LANES)
        rows, sems = (rows0, rows1), (sem0, sem1)

        def gather(ti, q):
            picks = idx_v.at[ti, pl.ds(q * SC_GATHER, SC_GATHER)]
            return pltpu.make_async_copy(uv_hbm.at[picks], rows[q % 2], sems[q % 2])

        def block(bi, carry):
            o0 = pl.multiple_of(worker * tokens_per_worker + bi * SC_BLOCK, SC_BLOCK)
            t0 = pl.multiple_of(t_start + o0, SC_BLOCK)
            pltpu.sync_copy(idx_hbm.at[pl.ds(t0, SC_BLOCK)], idx_v)
            pltpu.sync_copy(g_hbm.at[pl.ds(t0, SC_BLOCK)], g_v)
            pltpu.sync_copy(hn_hbm.at[pl.ds(t0, SC_BLOCK)], x_v)
            pltpu.sync_copy(x1_hbm.at[pl.ds(t0, SC_BLOCK)], o_v)
            gather(0, 0).start()

            def token(ti, carry1):
                for q in range(n_gather):
                    if q + 1 < n_gather:
                        gather(ti, q + 1).start()
                    else:
                        @pl.when(ti + 1 < SC_BLOCK)
                        def _():
                            gather(ti + 1, 0).start()
                    gather(ti, q).wait()
                    rows_v = rows[q % 2]

                    def dot_chunk(j, accs):
                        off = pl.multiple_of(j * SC_LANES, SC_LANES)
                        xlo = x_v[ti, pl.ds(off, SC_LANES)]
                        xhi = x_v[ti, pl.ds(half + off, SC_LANES)]
                        out = []
                        for k in range(SC_GATHER):
                            lo, hi = _sc_unpack_pair(rows_v[k, pl.ds(off, SC_LANES)])
                            out.append(accs[k] + lo * xlo + hi * xhi)
                        return tuple(out)

                    accs = lax.fori_loop(0, n_chunk, dot_chunk,
                                         tuple(jnp.zeros((SC_LANES,), F32) for _ in range(SC_GATHER)))
                    acts = []
                    for h in range(SC_GATHER // SC_LANES):
                        a = jnp.zeros((SC_LANES,), F32)
                        for kk in range(SC_LANES):
                            a = jnp.where(lane == kk, jnp.sum(accs[h * SC_LANES + kk]), a)
                        y = 0.7978845608028654 * (a + 0.044715 * a * a * a)
                        th = 1.0 - 2.0 / (jnp.exp(2.0 * y) + 1.0)
                        acts.append(0.5 * a * (1.0 + th) * g_v[ti, pl.ds(q * SC_GATHER + h * SC_LANES, SC_LANES)])
                    act_b = [_sc_lane_bcast(acts[k // SC_LANES], k % SC_LANES) for k in range(SC_GATHER)]

                    def mix_chunk(j, carry2):
                        off = pl.multiple_of(j * SC_LANES, SC_LANES)
                        al = jnp.zeros((SC_LANES,), F32)
                        ah = jnp.zeros((SC_LANES,), F32)
                        for k in range(SC_GATHER):
                            lo, hi = _sc_unpack_pair(rows_v[k, pl.ds(half + off, SC_LANES)])
                            al = al + act_b[k] * lo
                            ah = ah + act_b[k] * hi
                        o_v[ti, pl.ds(off, SC_LANES)] = o_v[ti, pl.ds(off, SC_LANES)] + al
                        o_v[ti, pl.ds(half + off, SC_LANES)] = o_v[ti, pl.ds(half + off, SC_LANES)] + ah
                        return carry2

                    lax.fori_loop(0, n_chunk, mix_chunk, 0)
                return carry1

            lax.fori_loop(0, SC_BLOCK, token, 0)
            pltpu.sync_copy(o_v, out_hbm.at[pl.ds(o0, SC_BLOCK)])
            return carry

        lax.fori_loop(0, tokens_per_worker // SC_BLOCK, block, 0)

    row_buf = pltpu.VMEM((SC_GATHER, D_MODEL), jnp.uint32)
    return pl.kernel(
        body, mesh=mesh,
        out_type=jax.ShapeDtypeStruct((n, D_MODEL), F32),
        scratch_types=[pltpu.VMEM((SC_BLOCK, PEER_NK), I32), pltpu.VMEM((SC_BLOCK, PEER_NK), F32),
                       pltpu.VMEM((SC_BLOCK, D_MODEL), F32), pltpu.VMEM((SC_BLOCK, D_MODEL), F32), row_buf, row_buf,
                       pltpu.SemaphoreType.DMA, pltpu.SemaphoreType.DMA],
        compiler_params=pltpu.CompilerParams(needs_layout_passes=False),
        name="peer_expert_sc",
    )(idx, gates, hn, x1, uv_rows)


def _rope_tables(pos):
    half = ROPE_DIMS // 2
    inv = ROPE_THETA ** (-(jnp.arange(half, dtype=F32) * 2.0 / ROPE_DIMS))
    ang = pos.astype(F32)[:, None] * inv[None, :]
    cos, sin = jnp.cos(ang), jnp.sin(ang)
    n = pos.shape[0]
    a = jnp.concatenate([cos, cos, jnp.ones((n, HEAD_DIM - ROPE_DIMS), F32)], axis=-1)
    b = jnp.concatenate([-sin, sin, jnp.zeros((n, HEAD_DIM - ROPE_DIMS), F32)], axis=-1)
    return jnp.tile(a, (1, 2)), jnp.tile(b, (1, 2))


def _tile2(v):
    return jnp.tile(v.reshape(1, HEAD_DIM), (1, 2))


def kernel(x, norm1_g, w_in, nsa_q_norm, nsa_k_norm, cmp_pe_k, cmp_w1_k, cmp_w2_k, cmp_pe_v, cmp_w1_v, cmp_w2_v,
           dil_q_norm, dil_k_norm, w_up_nsa, w_up_dil, w_o, norm2_g, peer_wq, peer_subkeys, peer_u, peer_v):
    B, S, D = x.shape
    assert D == D_MODEL and S % (DIL_PATTERNS[-1][1] * DIL_BLOCK) == 0 and S >= WIN + NSA_QB and (B * S) % 1024 == 0

    n_q, n_kv, n_gate, n_dil = 512, 768, 24, 2304
    o_gate = n_q + n_kv
    o_dil = o_gate + n_gate
    o_mg = o_dil + n_dil
    w_perm = jnp.concatenate([w_in[:, o_mg:], w_in[:, :o_gate], w_in[:, o_dil:o_mg], w_in[:, o_gate:o_dil],
                              jnp.zeros((D, IN_COLS_PAD - w_in.shape[1]), w_in.dtype)], axis=1).astype(BF16)
    blockdiag = jnp.asarray(np.kron(np.eye(2), np.ones((HEAD_DIM, HEAD_DIM))), BF16)
    rope_a, rope_b = _rope_tables(jnp.arange(S))
    sub = PEER_NKEYS // 2
    sk_pad = jnp.stack([jnp.pad(peer_subkeys[0], ((0, 0), (0, sub))), jnp.pad(peer_subkeys[1], ((0, 0), (sub, 0)))])
    uv_rows, uv_tiles = _pack_tables(peer_u, peer_v)
    eye2 = jnp.eye(2, dtype=F32)
    w1_grp = lambda w: jnp.einsum('ldh,gq->lgdqh', w.reshape(CMP_BLOCK, HEAD_DIM, CMP_HIDDEN), eye2).reshape(
        CMP_BLOCK * 2 * HEAD_DIM, 2 * CMP_HIDDEN).astype(BF16)
    w2_grp = lambda w: jnp.einsum('hd,gq->ghqd', w, eye2).reshape(2 * CMP_HIDDEN, 2 * HEAD_DIM).astype(BF16)
    pe_grp = lambda pe: jnp.broadcast_to(pe[:, None, :], (CMP_BLOCK, 2, HEAD_DIM)).reshape(1, -1)
    cmp_a, cmp_b = _rope_tables(jnp.arange(S // CMP_STRIDE) * CMP_STRIDE + CMP_BLOCK - 1)
    weights = dict(w_perm=w_perm, blockdiag=blockdiag, rope_a=rope_a, rope_b=rope_b, sk_pad=sk_pad, uv_rows=uv_rows,
                   uv_tiles=uv_tiles.reshape(-1, SUBLANES, LANES),
                   wn=w_up_nsa.astype(BF16), wd=w_up_dil.astype(BF16), wo=w_o.astype(BF16), wq=peer_wq.astype(BF16),
                   cmp=(pe_grp(cmp_pe_k), pe_grp(cmp_pe_v), w1_grp(cmp_w1_k), w1_grp(cmp_w1_v), w2_grp(cmp_w2_k),
                        w2_grp(cmp_w2_v)), cmp_a=cmp_a, cmp_b=cmp_b)
    params = (norm1_g, nsa_q_norm, nsa_k_norm, dil_q_norm, dil_k_norm, norm2_g)

    n_groups = next(n for n in (4, 2, 1) if B % n == 0 and (B // n * S) % 1024 == 0)
    bg = B // n_groups
    outs = [_layer(x[g * bg:(g + 1) * bg], SC_SHARE if g == n_groups - 1 else SC_SHARE_EARLY, weights, *params)
            for g in range(n_groups)]
    return outs[0] if n_groups == 1 else jnp.concatenate(outs, axis=0)


def _layer(x, sc_share, weights, norm1_g, nsa_q_norm, nsa_k_norm, dil_q_norm, dil_k_norm, norm2_g):
    B, S, D = x.shape
    T = B * S
    x2 = x.reshape(T, D)
    blockdiag, rope_a, rope_b = weights["blockdiag"], weights["rope_a"], weights["rope_b"]
    z2 = _in_proj(x2, norm1_g.reshape(1, D), weights["w_perm"])
    z3 = z2.reshape(B, S, IN_COLS_PAD)

    qn, ksd, vsd, kwd, vwd = _nsa_prep(z3, rope_a, rope_b, blockdiag, _tile2(nsa_q_norm), _tile2(nsa_k_norm[1]),
                                       _tile2(nsa_k_norm[2]))

    n_cmp = (S - CMP_BLOCK) // CMP_STRIDE + 1
    ncp = S // CMP_STRIDE
    def flat_blocks(col):
        r = z3[:, :, col:col + LANES].reshape(B, ncp, CMP_STRIDE * LANES)
        nxt = jnp.concatenate([r[:, 1:], jnp.zeros_like(r[:, :1])], axis=1)
        return jnp.concatenate([r, nxt], axis=-1)
    kcd, vcd = _compress(flat_blocks(COL_KV), flat_blocks(COL_KV + LANES), *weights["cmp"], weights["cmp_a"],
                         weights["cmp_b"], blockdiag, _tile2(nsa_k_norm[0]))

    n_slc = S // SLC_BLOCK
    s0 = np.arange(n_cmp) * CMP_STRIDE
    b0 = np.arange(n_slc) * SLC_BLOCK
    ov = np.clip(np.minimum(s0[:, None] + CMP_BLOCK, b0[None, :] + SLC_BLOCK) - np.maximum(s0[:, None], b0[None, :]),
                 0, None) / CMP_BLOCK
    ov_pad = np.zeros((ncp, LANES), np.float32)
    ov_pad[:n_cmp, :n_slc] = ov
    y_nsa = _nsa_attn(qn, kcd, vcd, ksd, vsd, kwd, vwd, z3, jnp.asarray(ov_pad, BF16))

    gq = jnp.tile(dil_q_norm.reshape(3, 1, HEAD_DIM), (1, 1, 2))
    gk = jnp.tile(dil_k_norm.reshape(3, 1, HEAD_DIM), (1, 1, 2))
    prep = _dil_prep(z3, rope_a, rope_b, blockdiag, gq, gk)
    flat = [p.reshape(B, S, 256) for p in prep]
    y_dil = _dil_attn(flat[0::3], flat[1::3], flat[2::3])

    x1, hn, pq = _merge(x2, y_nsa.reshape(T, 512), y_dil.reshape(T, 256), z2, weights["wn"], weights["wd"], weights["wo"],
                        norm2_g.reshape(1, D), weights["wq"])

    idx, gates = _route_transpose(*_peer_route(pq, weights["sk_pad"]))
    sc_unit = SC_WORKERS * SC_BLOCK
    t_tc = T - (T * sc_share[0] // sc_share[1]) // sc_unit * sc_unit
    assert t_tc % PEER_TOK == 0 and t_tc >= 2 * PEER_TOK
    out_tc = _peer_expert(t_tc, idx, hn, gates, x1, weights["uv_tiles"])
    out_sc = _peer_expert_sc(t_tc, T - t_tc, idx, gates, hn, x1, weights["uv_rows"])
    return jnp.concatenate([out_tc, out_sc], axis=0).reshape(B, S, D)
```

```python
import functools

import numpy as np
import jax
import jax.numpy as jnp
from jax import lax
from jax.experimental import pallas as pl
from jax.experimental.pallas import tpu as pltpu
from jax.experimental.pallas import tpu_sc as plsc

F32 = jnp.float32
BF16 = jnp.bfloat16
I32 = jnp.int32

D_MODEL = 1024
HEAD_DIM = 64
ROPE_DIMS = 16
ROPE_THETA = 500000.0
NORM_EPS = 1e-6
NEG_INF = -1e30
LANES = 128

NSA_HEADS = 8
CMP_BLOCK = 32
CMP_STRIDE = 16
CMP_HIDDEN = 256
SLC_BLOCK = 64
SLC_TOPN = 16
FORCE_SCORE = 1e3
WIN = 512
NSA_QB = 128
SEL_CHUNK = 512

DIL_PATTERNS = ((128, 1), (512, 4), (2048, 16))
DIL_BLOCK = 128
DIL_PREP_ROWS = 1024

PEER_HEADS = 8
PEER_NKEYS = 128
PEER_TOPK = 16
PEER_TOK = 16

COL_MG = 0
COL_Q = 2048
COL_KV = 2560
COL_DIL = 3328
COL_GATE = 5632
IN_COLS_PAD = 5760

_NT = (((1,), (1,)), ((), ()))


def _dot(a, b):
    return jnp.dot(a, b, preferred_element_type=F32)


def _dot_nt(a, b):
    return lax.dot_general(a, b, _NT, preferred_element_type=F32)


def _split_bf16(a):
    hi = a.astype(BF16)
    lo = (a - hi.astype(F32)).astype(BF16)
    return hi, lo


def _dot_hilo(a, b_bf16):
    hi, lo = _split_bf16(a)
    return _dot(hi, b_bf16) + _dot(lo, b_bf16)


def _head_norm_rope(zt, gain, rope_a, rope_b, blockdiag):
    ss = _dot_hilo(zt * zt, blockdiag)
    zn = zt * lax.rsqrt(ss * (1.0 / HEAD_DIM) + NORM_EPS) * gain
    d = lax.broadcasted_iota(I32, zn.shape, 1) & (HEAD_DIM - 1)
    half = ROPE_DIMS // 2
    partner = jnp.where(d < half, pltpu.roll(zn, LANES - half, 1), pltpu.roll(zn, half, 1))
    return zn * rope_a + partner * rope_b


def _half_masks():
    lane = lax.broadcasted_iota(I32, (1, LANES), 1)
    lo = (lane < HEAD_DIM).astype(BF16)
    return lo, (1 - lo).astype(BF16)


def _inproj_kernel(x_ref, g_ref, w_ref, o_ref, h_scr):
    @pl.when(pl.program_id(1) == 0)
    def _():
        xf = x_ref[...]
        ms = jnp.mean(xf * xf, axis=-1, keepdims=True)
        h_scr[...] = (xf * lax.rsqrt(ms + NORM_EPS) * g_ref[...]).astype(BF16)

    o_ref[...] = _dot(h_scr[...], w_ref[...])


def _in_proj(x2, g1, w_bf16):
    T = x2.shape[0]
    tm, tn = 1024, 640
    return pl.pallas_call(
        _inproj_kernel,
        out_shape=jax.ShapeDtypeStruct((T, IN_COLS_PAD), F32),
        grid=(T // tm, IN_COLS_PAD // tn),
        in_specs=[
            pl.BlockSpec((tm, D_MODEL), lambda i, j: (i, 0)),
            pl.BlockSpec((1, D_MODEL), lambda i, j: (0, 0)),
            pl.BlockSpec((D_MODEL, tn), lambda i, j: (0, j)),
        ],
        out_specs=pl.BlockSpec((tm, tn), lambda i, j: (i, j)),
        scratch_shapes=[pltpu.VMEM((tm, D_MODEL), BF16)],
        compiler_params=pltpu.CompilerParams(dimension_semantics=("parallel", "arbitrary")),
        name="in_proj",
    )(x2, g1, w_bf16)


def _nsa_prep_kernel(zq_ref, zks_ref, zvs_ref, zkw_ref, zvw_ref, ra_ref, rb_ref, bd_ref, gq_ref, gks_ref, gkw_ref,
                     q_ref, ks_ref, vs_ref, kw_ref, vw_ref):
    ra, rb, bd = ra_ref[...], rb_ref[...], bd_ref[...]
    lane = lax.broadcasted_iota(I32, ra.shape, 1)
    scale = HEAD_DIM ** -0.5

    zq = zq_ref[0]
    tiles = [_head_norm_rope(zq[:, m * LANES:(m + 1) * LANES], gq_ref[...], ra, rb, bd) * scale for m in range(4)]
    q_ref[0] = jnp.concatenate(tiles, axis=-1).astype(BF16)

    def dup(t, out_ref):
        sw = pltpu.roll(t, HEAD_DIM, 1)
        out_ref[0, 0] = jnp.where(lane < HEAD_DIM, t, sw).astype(BF16)
        out_ref[0, 1] = jnp.where(lane < HEAD_DIM, sw, t).astype(BF16)

    dup(_head_norm_rope(zks_ref[0], gks_ref[...], ra, rb, bd), ks_ref)
    dup(_head_norm_rope(zkw_ref[0], gkw_ref[...], ra, rb, bd), kw_ref)
    dup(zvs_ref[0], vs_ref)
    dup(zvw_ref[0], vw_ref)


def _nsa_prep(z3, rope_a, rope_b, blockdiag, gq, gks, gkw):
    B, S, _ = z3.shape
    tm = 512
    kvb = COL_KV // LANES
    zcol = lambda c: pl.BlockSpec((1, tm, LANES), lambda b, i, c=c: (b, i, c))
    const = lambda shape: pl.BlockSpec(shape, lambda b, i: tuple(0 for _ in shape))
    kv_out = pl.BlockSpec((1, 2, tm, LANES), lambda b, i: (b, 0, i, 0))
    kv_shape = jax.ShapeDtypeStruct((B, 2, S, LANES), BF16)
    return pl.pallas_call(
        _nsa_prep_kernel,
        out_shape=(jax.ShapeDtypeStruct((B, S, 512), BF16), kv_shape, kv_shape, kv_shape, kv_shape),
        grid=(B, S // tm),
        in_specs=[
            pl.BlockSpec((1, tm, 512), lambda b, i: (b, i, COL_Q // 512)),
            zcol(kvb + 2), zcol(kvb + 3), zcol(kvb + 4), zcol(kvb + 5),
            pl.BlockSpec((tm, LANES), lambda b, i: (i, 0)),
            pl.BlockSpec((tm, LANES), lambda b, i: (i, 0)),
            const((LANES, LANES)), const((1, LANES)), const((1, LANES)), const((1, LANES)),
        ],
        out_specs=(pl.BlockSpec((1, tm, 512), lambda b, i: (b, i, 0)), kv_out, kv_out, kv_out, kv_out),
        compiler_params=pltpu.CompilerParams(dimension_semantics=("parallel", "parallel")),
        name="nsa_prep",
    )(z3, z3, z3, z3, z3, rope_a, rope_b, blockdiag, gq, gks, gkw)


def _compress_kernel(fk_ref, fv_ref, pek_ref, pev_ref, w1k_ref, w1v_ref, w2k_ref, w2v_ref, ra_ref, rb_ref, bd_ref, gk_ref,
                     k_ref, v_ref):
    lane = lax.broadcasted_iota(I32, (fk_ref.shape[1], LANES), 1)

    def mlp(f_ref, pe_ref, w1_ref, w2_ref):
        f = (f_ref[0] + pe_ref[...]).astype(BF16)
        h = jax.nn.gelu(_dot(f, w1_ref[...]))
        return _dot(h.astype(BF16), w2_ref[...])

    def dup(t, out_ref):
        sw = pltpu.roll(t, HEAD_DIM, 1)
        out_ref[0, 0] = jnp.where(lane < HEAD_DIM, t, sw).astype(BF16)
        out_ref[0, 1] = jnp.where(lane < HEAD_DIM, sw, t).astype(BF16)

    kc = mlp(fk_ref, pek_ref, w1k_ref, w2k_ref)
    dup(_head_norm_rope(kc, gk_ref[...], ra_ref[...], rb_ref[...], bd_ref[...]), k_ref)
    dup(mlp(fv_ref, pev_ref, w1v_ref, w2v_ref), v_ref)


def _compress(flat_k, flat_v, pek, pev, w1k, w1v, w2k, w2v, rope_a, rope_b, blockdiag, gk):
    B, nblk, width = flat_k.shape
    row = pl.BlockSpec((1, nblk, width), lambda b: (b, 0, 0))
    const = lambda shape: pl.BlockSpec(shape, lambda b: tuple(0 for _ in shape))
    out = pl.BlockSpec((1, 2, nblk, LANES), lambda b: (b, 0, 0, 0))
    shp = jax.ShapeDtypeStruct((B, 2, nblk, LANES), BF16)
    return pl.pallas_call(
        _compress_kernel,
        out_shape=(shp, shp),
        grid=(B,),
        in_specs=[row, row, const((1, width)), const((1, width)), const((width, 2 * CMP_HIDDEN)),
                  const((width, 2 * CMP_HIDDEN)), const((2 * CMP_HIDDEN, LANES)), const((2 * CMP_HIDDEN, LANES)),
                  const((nblk, LANES)), const((nblk, LANES)), const((LANES, LANES)), const((1, LANES))],
        out_specs=(out, out),
        compiler_params=pltpu.CompilerParams(dimension_semantics=("parallel",), vmem_limit_bytes=48 * 1024 * 1024),
        name="compress",
    )(flat_k, flat_v, pek, pev, w1k, w1v, w2k, w2v, rope_a, rope_b, blockdiag, gk)


def _softmax_rows(s):
    m = jnp.max(s, axis=-1, keepdims=True)
    e = jnp.exp(s - m)
    return e / jnp.sum(e, axis=-1, keepdims=True)


def _nsa_kernel(q_ref, kc_ref, vc_ref, ks_ref, vs_ref, kw_ref, vw_ref, gl_ref, ov_ref, ex_ref, y_ref):
    n = pl.program_id(1)
    t0 = n * NSA_QB
    qt = q_ref[0]
    hm = _half_masks()
    lane = lax.broadcasted_iota(I32, (NSA_QB, LANES), 1)
    t1 = t0 + lax.broadcasted_iota(I32, (NSA_QB, 1), 0)
    t4 = t0 + (lax.broadcasted_iota(I32, (4 * NSA_QB, 1), 0) & (NSA_QB - 1))
    gates = jax.nn.sigmoid(gl_ref[0])

    n_slc_blocks = ks_ref.shape[2] // SLC_BLOCK
    blk = lax.broadcasted_iota(I32, (n_slc_blocks, NSA_QB), 0)
    tq = t0 + lax.broadcasted_iota(I32, (n_slc_blocks, NSA_QB), 1)
    cur = tq >> 6
    forced = (blk == 0) | (blk == cur) | (blk == cur - 1)
    causal_b = blk * SLC_BLOCK <= tq

    tiles_out = []
    for g in range(2):
        q4 = jnp.concatenate(
            [qt[:, (2 * g + jj // 2) * LANES:(2 * g + jj // 2 + 1) * LANES] * hm[jj % 2] for jj in range(4)], axis=0)

        sc = _dot_nt(q4, kc_ref[0, g])
        cend = lax.broadcasted_iota(I32, (1, sc.shape[1]), 1) * CMP_STRIDE + (CMP_BLOCK - 1)
        valid_c = cend <= t4
        pc = _softmax_rows(jnp.where(valid_c, sc, NEG_INF))
        pc = jnp.where(valid_c, pc, 0.0)
        o_c = _dot(pc.astype(BF16), vc_ref[0, g])
        psum = pc[0:NSA_QB] + pc[NSA_QB:2 * NSA_QB] + pc[2 * NSA_QB:3 * NSA_QB] + pc[3 * NSA_QB:]
        imp = _dot_hilo(psum, ov_ref[...])

        score = jnp.where(forced, FORCE_SCORE, jnp.where(causal_b, imp.T[:n_slc_blocks], -1.0))
        n_grp = n_slc_blocks // SUBLANES
        grp = [score[SUBLANES * r:SUBLANES * (r + 1)] for r in range(n_grp)]
        ranks = [jnp.zeros((SUBLANES, NSA_QB), F32) for _ in range(n_grp)]
        row = lax.broadcasted_iota(I32, (SUBLANES, NSA_QB), 0)
        for i in range(n_slc_blocks):
            ri = grp[i // SUBLANES][i % SUBLANES:i % SUBLANES + 1, :]
            for r in range(n_grp):
                if r > i // SUBLANES:
                    ahead = ri >= grp[r]
                elif r < i // SUBLANES:
                    ahead = ri > grp[r]
                else:
                    ahead = (ri > grp[r]) | ((ri == grp[r]) & (row > i % SUBLANES))
                ranks[r] = ranks[r] + jnp.where(ahead, 1.0, 0.0)
        sel64 = jnp.where(jnp.concatenate(ranks, axis=0) < float(SLC_TOPN), 1.0, 0.0)
        sel = jnp.concatenate([sel64, jnp.zeros((LANES - n_slc_blocks, NSA_QB), F32)], axis=0).T.astype(BF16)

        def sel_chunk(c, carry, last):
            m_i, l_i, acc = carry
            k0 = pl.multiple_of(c * SEL_CHUNK, SEL_CHUNK)
            kch = ks_ref[0, g, pl.ds(k0, SEL_CHUNK), :]
            vch = vs_ref[0, g, pl.ds(k0, SEL_CHUNK), :]
            picked = _dot(sel, ex_ref[:, pl.ds(k0, SEL_CHUNK)])
            if last:
                kpos = k0 + lax.broadcasted_iota(I32, (1, SEL_CHUNK), 1)
                picked = jnp.where(kpos <= t1, picked, 0.0)
            bias = (picked - 1.0) * (-NEG_INF)
            s = (_dot_nt(q4, kch).reshape(4, NSA_QB, SEL_CHUNK) + bias[None]).reshape(4 * NSA_QB, SEL_CHUNK)
            m_new = jnp.maximum(m_i, jnp.max(s, axis=-1, keepdims=True))
            alpha = jnp.exp(m_i - m_new)
            p = jnp.exp(s - m_new)
            l_new = alpha * l_i + jnp.sum(p, axis=-1, keepdims=True)
            acc_new = alpha * acc + _dot(p.astype(BF16), vch)
            return m_new, l_new, acc_new

        init = (jnp.full((4 * NSA_QB, 1), NEG_INF, F32), jnp.zeros((4 * NSA_QB, 1), F32),
                jnp.zeros((4 * NSA_QB, LANES), F32))
        n_full = t0 // SEL_CHUNK
        carry = lax.fori_loop(0, n_full, functools.partial(sel_chunk, last=False), init)
        _, l_s, acc_s = sel_chunk(n_full, carry, last=True)
        o_s = acc_s / l_s

        wlen = WIN + NSA_QB
        ws = pl.multiple_of(jnp.maximum(t0 - WIN, 0), NSA_QB)
        sw = _dot_nt(q4, kw_ref[0, g, pl.ds(ws, wlen), :])
        dist = t4 - (ws + lax.broadcasted_iota(I32, (1, wlen), 1))
        pw = _softmax_rows(jnp.where((dist >= 0) & (dist < WIN), sw, NEG_INF))
        o_w = _dot(pw.astype(BF16), vw_ref[0, g, pl.ds(ws, wlen), :])

        heads = []
        for jj in range(4):
            h = 4 * g + jj
            rows = slice(jj * NSA_QB, (jj + 1) * NSA_QB)
            heads.append(gates[:, 3 * h:3 * h + 1] * o_c[rows] + gates[:, 3 * h + 1:3 * h + 2] * o_s[rows]
                         + gates[:, 3 * h + 2:3 * h + 3] * o_w[rows])
        tiles_out.append(jnp.where(lane < HEAD_DIM, heads[0], heads[1]))
        tiles_out.append(jnp.where(lane < HEAD_DIM, heads[2], heads[3]))

    y_ref[0] = jnp.concatenate(tiles_out, axis=-1).astype(BF16)


def _nsa_attn(qn, kcd, vcd, ksd, vsd, kwd, vwd, z3, overlap):
    B, S, _ = qn.shape
    ncp = kcd.shape[2]
    assert S // SLC_BLOCK <= LANES
    expand = jnp.asarray(np.arange(LANES)[:, None] == (np.arange(S)[None, :] // SLC_BLOCK), BF16)
    full = lambda rows: pl.BlockSpec((1, 2, rows, LANES), lambda b, n: (b, 0, 0, 0))
    return pl.pallas_call(
        _nsa_kernel,
        out_shape=jax.ShapeDtypeStruct((B, S, 512), BF16),
        grid=(B, S // NSA_QB),
        in_specs=[
            pl.BlockSpec((1, NSA_QB, 512), lambda b, n: (b, n, 0)),
            full(ncp), full(ncp), full(S), full(S), full(S), full(S),
            pl.BlockSpec((1, NSA_QB, LANES), lambda b, n: (b, n, COL_GATE // LANES)),
            pl.BlockSpec((ncp, LANES), lambda b, n: (0, 0)),
            pl.BlockSpec((LANES, S), lambda b, n: (0, 0)),
        ],
        out_specs=pl.BlockSpec((1, NSA_QB, 512), lambda b, n: (b, n, 0)),
        compiler_params=pltpu.CompilerParams(dimension_semantics=("parallel", "arbitrary"),
                                             vmem_limit_bytes=48 * 1024 * 1024),
        name="nsa_attn",
    )(qn, kcd, vcd, ksd, vsd, kwd, vwd, z3, overlap, expand)


def _dil_prep_kernel(*refs):
    zs, (ra_ref, rb_ref, bd_ref, gq_ref, gk_ref), outs = refs[0:18], refs[18:23], refs[23:32]
    bd = bd_ref[...]
    scale = HEAD_DIM ** -0.5
    for g, (_, d) in enumerate(DIL_PATTERNS):
        n = DIL_PREP_ROWS // d
        for r in range(d):
            rows = pl.ds(r, n, stride=d) if d > 1 else pl.ds(0, n)
            ra, rb = ra_ref[rows, :], rb_ref[rows, :]
            for which in range(3):
                for m in range(2):
                    z = zs[2 * (3 * g + which) + m][0, rows, :]
                    if which == 0:
                        z = _head_norm_rope(z, gq_ref[g], ra, rb, bd) * scale
                    elif which == 1:
                        z = _head_norm_rope(z, gk_ref[g], ra, rb, bd)
                    outs[3 * g + which][0, r, :, m * LANES:(m + 1) * LANES] = z.astype(BF16)


def _dil_prep(z3, rope_a, rope_b, blockdiag, gq, gk):
    B, S, _ = z3.shape
    nsteps = S // DIL_PREP_ROWS
    c0 = COL_DIL // LANES
    in_specs = [pl.BlockSpec((1, DIL_PREP_ROWS, LANES), lambda b, c, k=k: (b, c, c0 + k)) for k in range(18)]
    in_specs += [
        pl.BlockSpec((DIL_PREP_ROWS, LANES), lambda b, c: (c, 0)),
        pl.BlockSpec((DIL_PREP_ROWS, LANES), lambda b, c: (c, 0)),
        pl.BlockSpec((LANES, LANES), lambda b, c: (0, 0)),
        pl.BlockSpec((3, 1, LANES), lambda b, c: (0, 0, 0)),
        pl.BlockSpec((3, 1, LANES), lambda b, c: (0, 0, 0)),
    ]
    out_shape, out_specs = [], []
    for _, d in DIL_PATTERNS:
        for _ in range(3):
            out_shape.append(jax.ShapeDtypeStruct((B, d, S // d, 256), BF16))
            out_specs.append(pl.BlockSpec((1, d, DIL_PREP_ROWS // d, 256), lambda b, c: (b, 0, c, 0)))
    return pl.pallas_call(
        _dil_prep_kernel,
        out_shape=tuple(out_shape),
        grid=(B, nsteps),
        in_specs=in_specs,
        out_specs=tuple(out_specs),
        compiler_params=pltpu.CompilerParams(dimension_semantics=("parallel", "parallel"),
                                             vmem_limit_bytes=48 * 1024 * 1024),
        name="dil_prep",
    )(*([z3] * 18), rope_a, rope_b, blockdiag, gq, gk)


def _dil_kernel(*refs, seq):
    q_refs, k_refs, v_refs, y_ref, o_scr, l_scr = refs[0:3], refs[3:6], refs[6:9], refs[9], refs[10], refs[11]
    hm = _half_masks()
    lane = lax.broadcasted_iota(I32, (DIL_BLOCK, LANES), 1)
    qi = lax.broadcasted_iota(I32, (2 * DIL_BLOCK, 2 * DIL_BLOCK), 0) & (DIL_BLOCK - 1)
    ki = lax.broadcasted_iota(I32, (2 * DIL_BLOCK, 2 * DIL_BLOCK), 1)
    causal = (ki - DIL_BLOCK) <= qi

    for g, (_, d) in enumerate(DIL_PATTERNS):
        nb = seq // d // DIL_BLOCK

        def body(u, carry, g=g, d=d, nb=nb):
            j = u % nb
            r = u // nb
            r0 = pl.multiple_of(u * DIL_BLOCK, DIL_BLOCK)
            p0 = pl.multiple_of(jnp.maximum(u - 1, 0) * DIL_BLOCK, DIL_BLOCK)
            q = q_refs[g][0, pl.ds(r0, DIL_BLOCK), :]
            kcat = jnp.concatenate([k_refs[g][0, pl.ds(p0, DIL_BLOCK), :], k_refs[g][0, pl.ds(r0, DIL_BLOCK), :]], axis=0)
            vcat = jnp.concatenate([v_refs[g][0, pl.ds(p0, DIL_BLOCK), :], v_refs[g][0, pl.ds(r0, DIL_BLOCK), :]], axis=0)
            q2 = jnp.concatenate([q * hm[0], q * hm[1]], axis=0)
            s = _dot_nt(q2, kcat)
            first_key = jnp.maximum(qi, jnp.where(j >= 1, 0, DIL_BLOCK))
            s = jnp.where(causal & (ki >= first_key), s, NEG_INF)
            m = jnp.max(s, axis=-1, keepdims=True)
            e = jnp.exp(s - m)
            den = jnp.sum(e, axis=-1, keepdims=True)
            o2 = _dot(e.astype(BF16), vcat) / den
            lse = m + jnp.log(den)
            o = jnp.where(lane < HEAD_DIM, o2[:DIL_BLOCK], o2[DIL_BLOCK:])
            lv = jnp.where(lane < HEAD_DIM, lse[:DIL_BLOCK], lse[DIL_BLOCK:])
            tok0 = j * (DIL_BLOCK * d) + r
            rows = pl.ds(tok0, DIL_BLOCK, stride=d) if d > 1 else pl.ds(pl.multiple_of(tok0, DIL_BLOCK), DIL_BLOCK)
            o_scr[g, rows, :] = o
            l_scr[g, rows, :] = lv
            return carry

        lax.fori_loop(0, seq // DIL_BLOCK, body, 0)

    def merge(c, carry):
        rows = pl.ds(pl.multiple_of(c * 512, 512), 512)
        ls = [l_scr[g, rows, :] for g in range(3)]
        mx = jnp.maximum(jnp.maximum(ls[0], ls[1]), ls[2])
        ws = [jnp.exp(l - mx) for l in ls]
        num = ws[0] * o_scr[0, rows, :] + ws[1] * o_scr[1, rows, :] + ws[2] * o_scr[2, rows, :]
        y_ref[0, rows, :] = (num / (ws[0] + ws[1] + ws[2])).astype(BF16)
        return carry

    lax.fori_loop(0, seq // 512, merge, 0)


def _dil_attn(dq, dk, dv):
    B, S, _ = dq[0].shape
    spec = pl.BlockSpec((1, S, LANES), lambda b, m: (b, 0, m))
    return pl.pallas_call(
        functools.partial(_dil_kernel, seq=S),
        out_shape=jax.ShapeDtypeStruct((B, S, 256), BF16),
        grid=(B, 2),
        in_specs=[spec] * 9,
        out_specs=spec,
        scratch_shapes=[pltpu.VMEM((3, S, LANES), F32), pltpu.VMEM((3, S, LANES), F32)],
        compiler_params=pltpu.CompilerParams(dimension_semantics=("parallel", "parallel"),
                                             vmem_limit_bytes=56 * 1024 * 1024),
        name="dil_attn",
    )(*dq, *dk, *dv)


def _merge_kernel(x_ref, yn_ref, yd_ref, mg0_ref, mg1_ref, wn_ref, wd_ref, wo_ref, g2_ref, wq_ref, x1_ref, hn_ref, pq_ref):
    u1 = _dot(yn_ref[...], wn_ref[...])
    u2 = _dot(yd_ref[...], wd_ref[...])
    merged = jax.nn.sigmoid(mg0_ref[...]) * u1 + jax.nn.sigmoid(mg1_ref[...]) * u2
    x1 = x_ref[...] + _dot(merged.astype(BF16), wo_ref[...])
    x1_ref[...] = x1
    ms = jnp.mean(x1 * x1, axis=-1, keepdims=True)
    hn = x1 * lax.rsqrt(ms + NORM_EPS) * g2_ref[...]
    hn_ref[...] = hn
    pq_ref[...] = _dot(hn.astype(BF16), wq_ref[...])


def _merge(x2, yn2, yd2, z2, wn, wd, wo, g2, wq):
    T = x2.shape[0]
    tm = 512
    row = lambda w, c=0: pl.BlockSpec((tm, w), lambda i, c=c: (i, c))
    const = lambda shape: pl.BlockSpec(shape, lambda i: (0, 0))
    shp = jax.ShapeDtypeStruct((T, D_MODEL), F32)
    return pl.pallas_call(
        _merge_kernel,
        out_shape=(shp, shp, shp),
        grid=(T // tm,),
        in_specs=[row(D_MODEL), row(512), row(256), row(D_MODEL, COL_MG // D_MODEL), row(D_MODEL, COL_MG // D_MODEL + 1),
                  const((512, D_MODEL)), const((256, D_MODEL)), const((D_MODEL, D_MODEL)), const((1, D_MODEL)),
                  const((D_MODEL, D_MODEL))],
        out_specs=(row(D_MODEL), row(D_MODEL), row(D_MODEL)),
        compiler_params=pltpu.CompilerParams(dimension_semantics=("parallel",), vmem_limit_bytes=48 * 1024 * 1024),
        name="merge",
    )(x2, yn2, yd2, z2, z2, wn, wd, wo, g2, wq)


def _top16(s, rank_id=None):
    if rank_id is None:
        rank_id = lax.broadcasted_iota(I32, (s.shape[0], 1), 0)
    big = jnp.iinfo(jnp.int32).max
    vals, ids = [], []
    for _ in range(PEER_TOPK):
        m = jnp.max(s, axis=0, keepdims=True)
        win = jnp.min(jnp.where(s == m, rank_id, big), axis=0, keepdims=True)
        vals.append(m)
        ids.append(win)
        s = jnp.where(rank_id == win, -jnp.inf, s)
    return vals, ids


def _route_kernel(q_ref, sk_ref, idx_ref, gate_ref):
    K = PEER_TOPK
    qh, ql = _split_bf16(q_ref[...])
    vals, ids = [], []
    for c in range(2):
        kh, kl = _split_bf16(sk_ref[c])
        s = _dot_nt(kh, qh) + _dot_nt(kh, ql) + _dot_nt(kl, qh)
        v, p = _top16(s)
        vals.append(v)
        ids.append(p)
    v0, p0 = jnp.concatenate(vals[0], axis=0), jnp.concatenate(ids[0], axis=0)
    v1, p1 = jnp.concatenate(vals[1], axis=0), jnp.concatenate(ids[1], axis=0)

    a8 = lax.broadcasted_iota(I32, (SUBLANES, 1), 0)
    pieces = [(v0 + vals[1][0], p0 * PEER_NKEYS + ids[1][0], lax.broadcasted_iota(I32, (K, 1), 0) * K)]
    for b in range(1, SUBLANES):
        keep = a8 < K // (b + 1)
        pieces.append((jnp.where(keep, v0[:SUBLANES] + vals[1][b], -jnp.inf), p0[:SUBLANES] * PEER_NKEYS + ids[1][b],
                       a8 * K + b))
    pieces.append((vals[0][0] + v1[SUBLANES:], ids[0][0] * PEER_NKEYS + p1[SUBLANES:], a8 + SUBLANES))
    cand = jnp.concatenate([p[0] for p in pieces], axis=0)
    eid = jnp.concatenate([p[1] for p in pieces], axis=0)
    flat = jnp.concatenate([p[2] for p in pieces], axis=0)
    v, win = _top16(cand, flat)
    sel_ids = [jnp.sum(jnp.where(flat == w, eid, 0), axis=0, keepdims=True) for w in win]
    sc = jnp.concatenate(v, axis=0)
    e = jnp.exp(sc - sc[0:1])
    gate_ref[...] = e / jnp.sum(e, axis=0, keepdims=True)
    idx_ref[...] = jnp.concatenate(sel_ids, axis=0)


def _peer_route(pq, sk_pad):
    T = pq.shape[0]
    tt = 512
    return pl.pallas_call(
        _route_kernel,
        out_shape=(jax.ShapeDtypeStruct((PEER_HEADS * PEER_TOPK, T), I32),
                   jax.ShapeDtypeStruct((PEER_HEADS * PEER_TOPK, T), F32)),
        grid=(T // tt, PEER_HEADS),
        in_specs=[pl.BlockSpec((tt, LANES), lambda i, h: (i, h)),
                  pl.BlockSpec((2, PEER_NKEYS, LANES), lambda i, h: (0, 0, 0))],
        out_specs=(pl.BlockSpec((PEER_TOPK, tt), lambda i, h: (h, i)),
                   pl.BlockSpec((PEER_TOPK, tt), lambda i, h: (h, i))),
        compiler_params=pltpu.CompilerParams(dimension_semantics=("parallel", "parallel")),
        name="peer_route",
    )(pq, sk_pad)


def _route_t_kernel(idx_ref, gate_ref, idx_o, gate_o):
    idx_o[...] = idx_ref[...].T
    gate_o[...] = gate_ref[...].T


def _route_transpose(idx_t, gate_t):
    nk, T = idx_t.shape
    tt = 512
    src = pl.BlockSpec((nk, tt), lambda i: (0, i))
    dst = pl.BlockSpec((tt, nk), lambda i: (i, 0))
    return pl.pallas_call(
        _route_t_kernel,
        out_shape=(jax.ShapeDtypeStruct((T, nk), I32), jax.ShapeDtypeStruct((T, nk), F32)),
        grid=(T // tt,),
        in_specs=[src, src],
        out_specs=(dst, dst),
        compiler_params=pltpu.CompilerParams(dimension_semantics=("parallel",)),
        name="route_transpose",
    )(idx_t, gate_t)


def _pack_words(w):
    half = w.shape[1] // 2
    bits = pltpu.bitcast(w.astype(BF16).astype(F32), jnp.uint32)
    return (bits[:, :half] >> 16) | (bits[:, half:] & jnp.uint32(0xFFFF0000))


def _pack_kernel(u_ref, v_ref, rows_ref, tiles_ref):
    n = u_ref.shape[0]
    words = jnp.concatenate([_pack_words(u_ref[...]), _pack_words(v_ref[...])], axis=-1)
    rows_ref[...] = words
    for s in range(SUBLANES):
        tiles_ref[pl.ds(s, n, stride=SUBLANES), :] = words[:, s * LANES:(s + 1) * LANES]


def _pack_tables(u, v):
    ne, d = u.shape
    tm = 256
    return pl.pallas_call(
        _pack_kernel,
        out_shape=(jax.ShapeDtypeStruct((ne, d), jnp.uint32), jax.ShapeDtypeStruct((ne * SUBLANES, LANES), jnp.uint32)),
        grid=(ne // tm,),
        in_specs=[pl.BlockSpec((tm, d), lambda i: (i, 0)), pl.BlockSpec((tm, d), lambda i: (i, 0))],
        out_specs=(pl.BlockSpec((tm, d), lambda i: (i, 0)), pl.BlockSpec((tm * SUBLANES, LANES), lambda i: (i, 0))),
        compiler_params=pltpu.CompilerParams(dimension_semantics=("parallel",)),
        name="pack_tables",
    )(u, v)


PEER_NK = PEER_HEADS * PEER_TOPK
SUBLANES = 8


def _unpack_pair(w):
    lo = pltpu.bitcast(w << 16, F32)
    hi = pltpu.bitcast(w & jnp.uint32(0xFFFF0000), F32)
    return lo, hi


def _peer_kernel(idx_cur, idx_nxt, hn_ref, gate_ref, x1_ref, uv_hbm, o_ref, buf_a, buf_b, sem):
    i = pl.program_id(0)
    last = pl.num_programs(0) - 1
    half_rows = SUBLANES // 2
    bufs = (buf_a, buf_b)

    def tile_copy(idx_ref, s, c, k):
        p = c * PEER_NK + k
        return pltpu.make_async_copy(uv_hbm.at[idx_ref[c, k]], bufs[s].at[pl.ds(p * SUBLANES, SUBLANES), :], sem.at[s, c])

    def fetch_token(idx_ref, s, c):
        for k in range(PEER_NK):
            tile_copy(idx_ref, s, c, k).start(priority=k % 2)

    def wait_token(idx_ref, s, c):
        for k in range(PEER_NK):
            tile_copy(idx_ref, s, c, k).wait()

    @pl.when(i == 0)
    def _():
        lax.fori_loop(0, PEER_TOK, lambda c, carry: (fetch_token(idx_cur, 0, c), carry)[1], 0)

    eye = lax.broadcasted_iota(I32, (PEER_NK, PEER_NK), 0) == lax.broadcasted_iota(I32, (PEER_NK, PEER_NK), 1)

    def token(slot, c):
        wait_token(idx_cur, slot, c)
        fetch_token(idx_nxt, 1 - slot, c)
        base = c * (PEER_NK * SUBLANES)
        plane = lambda s: bufs[slot][pl.ds(base + s, PEER_NK, stride=SUBLANES), :]
        x = hn_ref[pl.ds(c, 1), :]
        xs = lambda s: x[:, s * LANES:(s + 1) * LANES]
        gcol = jnp.sum(jnp.where(eye, gate_ref[pl.ds(c, 1), :], 0.0), axis=-1, keepdims=True)
        acc = jnp.zeros((PEER_NK, LANES), F32)
        for s in range(half_rows):
            lo, hi = _unpack_pair(plane(s))
            acc = acc + lo * xs(s) + hi * xs(s + half_rows)
        act = jax.nn.gelu(jnp.sum(acc, axis=-1, keepdims=True)) * gcol
        lo_cols, hi_cols = [], []
        for s in range(half_rows, SUBLANES):
            lo, hi = _unpack_pair(plane(s))
            lo_cols.append(jnp.sum(act * lo, axis=0, keepdims=True))
            hi_cols.append(jnp.sum(act * hi, axis=0, keepdims=True))
        o_ref[pl.ds(c, 1), :] = x1_ref[pl.ds(c, 1), :] + jnp.concatenate(lo_cols + hi_cols, axis=-1)

    def step(slot):
        for c in range(PEER_TOK):
            token(slot, c)

        @pl.when(i == last)
        def _():
            lax.fori_loop(0, PEER_TOK, lambda c, carry: (wait_token(idx_nxt, 1 - slot, c), carry)[1], 0)

    pl.when(i % 2 == 0)(lambda: step(0))
    pl.when(i % 2 == 1)(lambda: step(1))


def _peer_expert(n_tokens, idx_flat, hn, gates, x1, uv_tiles):
    n = n_tokens // PEER_TOK
    row = pl.BlockSpec((PEER_TOK, D_MODEL), lambda i: (i, 0))
    fetch_buf = pltpu.VMEM((PEER_TOK * PEER_NK * SUBLANES, LANES), jnp.uint32)
    return pl.pallas_call(
        _peer_kernel,
        out_shape=jax.ShapeDtypeStruct((n_tokens, D_MODEL), F32),
        grid=(n,),
        in_specs=[pl.BlockSpec((PEER_TOK, PEER_NK), lambda i: (i, 0), memory_space=pltpu.SMEM),
                  pl.BlockSpec((PEER_TOK, PEER_NK), lambda i: (jnp.minimum(i + 1, n - 1), 0), memory_space=pltpu.SMEM),
                  row, pl.BlockSpec((PEER_TOK, PEER_NK), lambda i: (i, 0)), row,
                  pl.BlockSpec(memory_space=pl.ANY)],
        out_specs=row,
        scratch_shapes=[fetch_buf, fetch_buf, pltpu.SemaphoreType.DMA((2, PEER_TOK))],
        compiler_params=pltpu.CompilerParams(dimension_semantics=("arbitrary",), vmem_limit_bytes=48 * 1024 * 1024),
        name="peer_expert",
    )(idx_flat, idx_flat, hn, gates, x1, uv_tiles)


SC_LANES = 16
SC_WORKERS = 32
SC_GATHER = 32
SC_BLOCK = 8
SC_SHARE = (13, 32)
SC_SHARE_EARLY = (22, 32)


def _sc_lane_bcast(vec, k):
    idx = jnp.full((SC_LANES, 1), k, I32)
    dn = lax.GatherDimensionNumbers(offset_dims=(), collapsed_slice_dims=(0,), start_index_map=(0,))
    return lax.gather(vec, idx, dn, slice_sizes=(1,), mode=lax.GatherScatterMode.PROMISE_IN_BOUNDS)


def _sc_unpack_pair(w):
    return plsc.bitcast(w << 16, F32), plsc.bitcast(w & jnp.uint32(0xFFFF0000), F32)


def _peer_expert_sc(t_start, n, idx, gates, hn, x1, uv_rows):
    assert n % (SC_WORKERS * SC_BLOCK) == 0 and t_start % SC_BLOCK == 0
    tokens_per_worker = n // SC_WORKERS
    n_gather = PEER_NK // SC_GATHER
    half = D_MODEL // 2
    n_chunk = half // SC_LANES
    mesh = plsc.VectorSubcoreMesh(core_axis_name="c", subcore_axis_name="s")

    def body(idx_hbm, g_hbm, hn_hbm, x1_hbm, uv_hbm, out_hbm, idx_v, g_v, x_v, o_v, rows0, rows1, sem0, sem1):
        worker = lax.axis_index("s") * 2 + lax.axis_index("c")
        lane = lax.iota(I32, SC_LANES)
        rows, sems = (rows0, rows1), (sem0, sem1)

        def gather(ti, q):
            picks = idx_v.at[ti, pl.ds(q * SC_GATHER, SC_GATHER)]
            return pltpu.make_async_copy(uv_hbm.at[picks], rows[q % 2], sems[q % 2])

        def block(bi, carry):
            o0 = pl.multiple_of(worker * tokens_per_worker + bi * SC_BLOCK, SC_BLOCK)
            t0 = pl.multiple_of(t_start + o0, SC_BLOCK)
            pltpu.sync_copy(idx_hbm.at[pl.ds(t0, SC_BLOCK)], idx_v)
            pltpu.sync_copy(g_hbm.at[pl.ds(t0, SC_BLOCK)], g_v)
            pltpu.sync_copy(hn_hbm.at[pl.ds(t0, SC_BLOCK)], x_v)
            pltpu.sync_copy(x1_hbm.at[pl.ds(t0, SC_BLOCK)], o_v)
            gather(0, 0).start()

            def token(ti, carry1):
                for q in range(n_gather):
                    if q + 1 < n_gather:
                        gather(ti, q + 1).start()
                    else:
                        @pl.when(ti + 1 < SC_BLOCK)
                        def _():
                            gather(ti + 1, 0).start()
                    gather(ti, q).wait()
                    rows_v = rows[q % 2]

                    def dot_chunk(j, accs):
                        off = pl.multiple_of(j * SC_LANES, SC_LANES)
                        xlo = x_v[ti, pl.ds(off, SC_LANES)]
                        xhi = x_v[ti, pl.ds(half + off, SC_LANES)]
                        out = []
                        for k in range(SC_GATHER):
                            lo, hi = _sc_unpack_pair(rows_v[k, pl.ds(off, SC_LANES)])
                            out.append(accs[k] + lo * xlo + hi * xhi)
                        return tuple(out)

                    accs = lax.fori_loop(0, n_chunk, dot_chunk,
                                         tuple(jnp.zeros((SC_LANES,), F32) for _ in range(SC_GATHER)))
                    acts = []
                    for h in range(SC_GATHER // SC_LANES):
                        a = jnp.zeros((SC_LANES,), F32)
                        for kk in range(SC_LANES):
                            a = jnp.where(lane == kk, jnp.sum(accs[h * SC_LANES + kk]), a)
                        y = 0.7978845608028654 * (a + 0.044715 * a * a * a)
                        th = 1.0 - 2.0 / (jnp.exp(2.0 * y) + 1.0)
                        acts.append(0.5 * a * (1.0 + th) * g_v[ti, pl.ds(q * SC_GATHER + h * SC_LANES, SC_LANES)])
                    act_b = [_sc_lane_bcast(acts[k // SC_LANES], k % SC_LANES) for k in range(SC_GATHER)]

                    def mix_chunk(j, carry2):
                        off = pl.multiple_of(j * SC_LANES, SC_LANES)
                        al = jnp.zeros((SC_LANES,), F32)
                        ah = jnp.zeros((SC_LANES,), F32)
                        for k in range(SC_GATHER):
                            lo, hi = _sc_unpack_pair(rows_v[k, pl.ds(half + off, SC_LANES)])
                            al = al + act_b[k] * lo
                            ah = ah + act_b[k] * hi
                        o_v[ti, pl.ds(off, SC_LANES)] = o_v[ti, pl.ds(off, SC_LANES)] + al
                        o_v[ti, pl.ds(half + off, SC_LANES)] = o_v[ti, pl.ds(half + off, SC_LANES)] + ah
                        return carry2

                    lax.fori_loop(0, n_chunk, mix_chunk, 0)
                return carry1

            lax.fori_loop(0, SC_BLOCK, token, 0)
            pltpu.sync_copy(o_v, out_hbm.at[pl.ds(o0, SC_BLOCK)])
            return carry

        lax.fori_loop(0, tokens_per_worker // SC_BLOCK, block, 0)

    row_buf = pltpu.VMEM((SC_GATHER, D_MODEL), jnp.uint32)
    return pl.kernel(
        body, mesh=mesh,
        out_type=jax.ShapeDtypeStruct((n, D_MODEL), F32),
        scratch_types=[pltpu.VMEM((SC_BLOCK, PEER_NK), I32), pltpu.VMEM((SC_BLOCK, PEER_NK), F32),
                       pltpu.VMEM((SC_BLOCK, D_MODEL), F32), pltpu.VMEM((SC_BLOCK, D_MODEL), F32), row_buf, row_buf,
                       pltpu.SemaphoreType.DMA, pltpu.SemaphoreType.DMA],
        compiler_params=pltpu.CompilerParams(needs_layout_passes=False),
        name="peer_expert_sc",
    )(idx, gates, hn, x1, uv_rows)


def _rope_tables(pos):
    half = ROPE_DIMS // 2
    inv = ROPE_THETA ** (-(jnp.arange(half, dtype=F32) * 2.0 / ROPE_DIMS))
    ang = pos.astype(F32)[:, None] * inv[None, :]
    cos, sin = jnp.cos(ang), jnp.sin(ang)
    n = pos.shape[0]
    a = jnp.concatenate([cos, cos, jnp.ones((n, HEAD_DIM - ROPE_DIMS), F32)], axis=-1)
    b = jnp.concatenate([-sin, sin, jnp.zeros((n, HEAD_DIM - ROPE_DIMS), F32)], axis=-1)
    return jnp.tile(a, (1, 2)), jnp.tile(b, (1, 2))


def _tile2(v):
    return jnp.tile(v.reshape(1, HEAD_DIM), (1, 2))


def kernel(x, norm1_g, w_in, nsa_q_norm, nsa_k_norm, cmp_pe_k, cmp_w1_k, cmp_w2_k, cmp_pe_v, cmp_w1_v, cmp_w2_v,
           dil_q_norm, dil_k_norm, w_up_nsa, w_up_dil, w_o, norm2_g, peer_wq, peer_subkeys, peer_u, peer_v):
    B, S, D = x.shape
    assert D == D_MODEL and S % (DIL_PATTERNS[-1][1] * DIL_BLOCK) == 0 and S >= WIN + NSA_QB and (B * S) % 1024 == 0

    n_q, n_kv, n_gate, n_dil = 512, 768, 24, 2304
    o_gate = n_q + n_kv
    o_dil = o_gate + n_gate
    o_mg = o_dil + n_dil
    w_perm = jnp.concatenate([w_in[:, o_mg:], w_in[:, :o_gate], w_in[:, o_dil:o_mg], w_in[:, o_gate:o_dil],
                              jnp.zeros((D, IN_COLS_PAD - w_in.shape[1]), w_in.dtype)], axis=1).astype(BF16)
    blockdiag = jnp.asarray(np.kron(np.eye(2), np.ones((HEAD_DIM, HEAD_DIM))), BF16)
    rope_a, rope_b = _rope_tables(jnp.arange(S))
    sub = PEER_NKEYS // 2
    sk_pad = jnp.stack([jnp.pad(peer_subkeys[0], ((0, 0), (0, sub))), jnp.pad(peer_subkeys[1], ((0, 0), (sub, 0)))])
    uv_rows, uv_tiles = _pack_tables(peer_u, peer_v)
    eye2 = jnp.eye(2, dtype=F32)
    w1_grp = lambda w: jnp.einsum('ldh,gq->lgdqh', w.reshape(CMP_BLOCK, HEAD_DIM, CMP_HIDDEN), eye2).reshape(
        CMP_BLOCK * 2 * HEAD_DIM, 2 * CMP_HIDDEN).astype(BF16)
    w2_grp = lambda w: jnp.einsum('hd,gq->ghqd', w, eye2).reshape(2 * CMP_HIDDEN, 2 * HEAD_DIM).astype(BF16)
    pe_grp = lambda pe: jnp.broadcast_to(pe[:, None, :], (CMP_BLOCK, 2, HEAD_DIM)).reshape(1, -1)
    cmp_a, cmp_b = _rope_tables(jnp.arange(S // CMP_STRIDE) * CMP_STRIDE + CMP_BLOCK - 1)
    weights = dict(w_perm=w_perm, blockdiag=blockdiag, rope_a=rope_a, rope_b=rope_b, sk_pad=sk_pad, uv_rows=uv_rows,
                   uv_tiles=uv_tiles.reshape(-1, SUBLANES, LANES),
                   wn=w_up_nsa.astype(BF16), wd=w_up_dil.astype(BF16), wo=w_o.astype(BF16), wq=peer_wq.astype(BF16),
                   cmp=(pe_grp(cmp_pe_k), pe_grp(cmp_pe_v), w1_grp(cmp_w1_k), w1_grp(cmp_w1_v), w2_grp(cmp_w2_k),
                        w2_grp(cmp_w2_v)), cmp_a=cmp_a, cmp_b=cmp_b)
    params = (norm1_g, nsa_q_norm, nsa_k_norm, dil_q_norm, dil_k_norm, norm2_g)

    n_groups = next(n for n in (4, 2, 1) if B % n == 0 and (B // n * S) % 1024 == 0)
    bg = B // n_groups
    outs = [_layer(x[g * bg:(g + 1) * bg], SC_SHARE if g == n_groups - 1 else SC_SHARE_EARLY, weights, *params)
            for g in range(n_groups)]
    return outs[0] if n_groups == 1 else jnp.concatenate(outs, axis=0)


def _layer(x, sc_share, weights, norm1_g, nsa_q_norm, nsa_k_norm, dil_q_norm, dil_k_norm, norm2_g):
    B, S, D = x.shape
    T = B * S
    x2 = x.reshape(T, D)
    blockdiag, rope_a, rope_b = weights["blockdiag"], weights["rope_a"], weights["rope_b"]
    z2 = _in_proj(x2, norm1_g.reshape(1, D), weights["w_perm"])
    z3 = z2.reshape(B, S, IN_COLS_PAD)

    qn, ksd, vsd, kwd, vwd = _nsa_prep(z3, rope_a, rope_b, blockdiag, _tile2(nsa_q_norm), _tile2(nsa_k_norm[1]),
                                       _tile2(nsa_k_norm[2]))

    n_cmp = (S - CMP_BLOCK) // CMP_STRIDE + 1
    ncp = S // CMP_STRIDE
    def flat_blocks(col):
        r = z3[:, :, col:col + LANES].reshape(B, ncp, CMP_STRIDE * LANES)
        nxt = jnp.concatenate([r[:, 1:], jnp.zeros_like(r[:, :1])], axis=1)
        return jnp.concatenate([r, nxt], axis=-1)
    kcd, vcd = _compress(flat_blocks(COL_KV), flat_blocks(COL_KV + LANES), *weights["cmp"], weights["cmp_a"],
                         weights["cmp_b"], blockdiag, _tile2(nsa_k_norm[0]))

    n_slc = S // SLC_BLOCK
    s0 = np.arange(n_cmp) * CMP_STRIDE
    b0 = np.arange(n_slc) * SLC_BLOCK
    ov = np.clip(np.minimum(s0[:, None] + CMP_BLOCK, b0[None, :] + SLC_BLOCK) - np.maximum(s0[:, None], b0[None, :]),
                 0, None) / CMP_BLOCK
    ov_pad = np.zeros((ncp, LANES), np.float32)
    ov_pad[:n_cmp, :n_slc] = ov
    y_nsa = _nsa_attn(qn, kcd, vcd, ksd, vsd, kwd, vwd, z3, jnp.asarray(ov_pad, BF16))

    gq = jnp.tile(dil_q_norm.reshape(3, 1, HEAD_DIM), (1, 1, 2))
    gk = jnp.tile(dil_k_norm.reshape(3, 1, HEAD_DIM), (1, 1, 2))
    prep = _dil_prep(z3, rope_a, rope_b, blockdiag, gq, gk)
    flat = [p.reshape(B, S, 256) for p in prep]
    y_dil = _dil_attn(flat[0::3], flat[1::3], flat[2::3])

    x1, hn, pq = _merge(x2, y_nsa.reshape(T, 512), y_dil.reshape(T, 256), z2, weights["wn"], weights["wd"], weights["wo"],
                        norm2_g.reshape(1, D), weights["wq"])

    idx, gates = _route_transpose(*_peer_route(pq, weights["sk_pad"]))
    sc_unit = SC_WORKERS * SC_BLOCK
    t_tc = T - (T * sc_share[0] // sc_share[1]) // sc_unit * sc_unit
    assert t_tc % PEER_TOK == 0 and t_tc >= 2 * PEER_TOK
    out_tc = _peer_expert(t_tc, idx, hn, gates, x1, weights["uv_tiles"])
    out_sc = _peer_expert_sc(t_tc, T - t_tc, idx, gates, hn, x1, weights["uv_rows"])
    return jnp.concatenate([out_tc, out_sc], axis=0).reshape(B, S, D)
```

```python
import functools

import numpy as np
import jax
import jax.numpy as jnp
from jax import lax
from jax.experimental import pallas as pl
from jax.experimental.pallas import tpu as pltpu
from jax.experimental.pallas import tpu_sc as plsc

F32 = jnp.float32
BF16 = jnp.bfloat16
I32 = jnp.int32

D_MODEL = 1024
HEAD_DIM = 64
ROPE_DIMS = 16
ROPE_THETA = 500000.0
NORM_EPS = 1e-6
NEG_INF = -1e30
LANES = 128

NSA_HEADS = 8
CMP_BLOCK = 32
CMP_STRIDE = 16
CMP_HIDDEN = 256
SLC_BLOCK = 64
SLC_TOPN = 16
FORCE_SCORE = 1e3
WIN = 512
NSA_QB = 128
SEL_CHUNK = 512

DIL_PATTERNS = ((128, 1), (512, 4), (2048, 16))
DIL_BLOCK = 128
DIL_PREP_ROWS = 1024

PEER_HEADS = 8
PEER_NKEYS = 128
PEER_TOPK = 16
PEER_TOK = 16

COL_MG = 0
COL_Q = 2048
COL_KV = 2560
COL_DIL = 3328
COL_GATE = 5632
IN_COLS_PAD = 5760

_NT = (((1,), (1,)), ((), ()))


def _dot(a, b):
    return jnp.dot(a, b, preferred_element_type=F32)


def _dot_nt(a, b):
    return lax.dot_general(a, b, _NT, preferred_element_type=F32)


def _split_bf16(a):
    hi = a.astype(BF16)
    lo = (a - hi.astype(F32)).astype(BF16)
    return hi, lo


def _dot_hilo(a, b_bf16):
    hi, lo = _split_bf16(a)
    return _dot(hi, b_bf16) + _dot(lo, b_bf16)


def _head_norm_rope(zt, gain, rope_a, rope_b, blockdiag):
    ss = _dot_hilo(zt * zt, blockdiag)
    zn = zt * lax.rsqrt(ss * (1.0 / HEAD_DIM) + NORM_EPS) * gain
    d = lax.broadcasted_iota(I32, zn.shape, 1) & (HEAD_DIM - 1)
    half = ROPE_DIMS // 2
    partner = jnp.where(d < half, pltpu.roll(zn, LANES - half, 1), pltpu.roll(zn, half, 1))
    return zn * rope_a + partner * rope_b


def _half_masks():
    lane = lax.broadcasted_iota(I32, (1, LANES), 1)
    lo = (lane < HEAD_DIM).astype(BF16)
    return lo, (1 - lo).astype(BF16)


def _inproj_kernel(x_ref, g_ref, w_ref, o_ref, h_scr):
    @pl.when(pl.program_id(1) == 0)
    def _():
        xf = x_ref[...]
        ms = jnp.mean(xf * xf, axis=-1, keepdims=True)
        h_scr[...] = (xf * lax.rsqrt(ms + NORM_EPS) * g_ref[...]).astype(BF16)

    o_ref[...] = _dot(h_scr[...], w_ref[...])


def _in_proj(x2, g1, w_bf16):
    T = x2.shape[0]
    tm, tn = 1024, 640
    return pl.pallas_call(
        _inproj_kernel,
        out_shape=jax.ShapeDtypeStruct((T, IN_COLS_PAD), F32),
        grid=(T // tm, IN_COLS_PAD // tn),
        in_specs=[
            pl.BlockSpec((tm, D_MODEL), lambda i, j: (i, 0)),
            pl.BlockSpec((1, D_MODEL), lambda i, j: (0, 0)),
            pl.BlockSpec((D_MODEL, tn), lambda i, j: (0, j)),
        ],
        out_specs=pl.BlockSpec((tm, tn), lambda i, j: (i, j)),
        scratch_shapes=[pltpu.VMEM((tm, D_MODEL), BF16)],
        compiler_params=pltpu.CompilerParams(dimension_semantics=("parallel", "arbitrary")),
        name="in_proj",
    )(x2, g1, w_bf16)


def _nsa_prep_kernel(zq_ref, zks_ref, zvs_ref, zkw_ref, zvw_ref, ra_ref, rb_ref, bd_ref, gq_ref, gks_ref, gkw_ref,
                     q_ref, ks_ref, vs_ref, kw_ref, vw_ref):
    ra, rb, bd = ra_ref[...], rb_ref[...], bd_ref[...]
    lane = lax.broadcasted_iota(I32, ra.shape, 1)
    scale = HEAD_DIM ** -0.5

    zq = zq_ref[0]
    tiles = [_head_norm_rope(zq[:, m * LANES:(m + 1) * LANES], gq_ref[...], ra, rb, bd) * scale for m in range(4)]
    q_ref[0] = jnp.concatenate(tiles, axis=-1).astype(BF16)

    def dup(t, out_ref):
        sw = pltpu.roll(t, HEAD_DIM, 1)
        out_ref[0, 0] = jnp.where(lane < HEAD_DIM, t, sw).astype(BF16)
        out_ref[0, 1] = jnp.where(lane < HEAD_DIM, sw, t).astype(BF16)

    dup(_head_norm_rope(zks_ref[0], gks_ref[...], ra, rb, bd), ks_ref)
    dup(_head_norm_rope(zkw_ref[0], gkw_ref[...], ra, rb, bd), kw_ref)
    dup(zvs_ref[0], vs_ref)
    dup(zvw_ref[0], vw_ref)


def _nsa_prep(z3, rope_a, rope_b, blockdiag, gq, gks, gkw):
    B, S, _ = z3.shape
    tm = 512
    kvb = COL_KV // LANES
    zcol = lambda c: pl.BlockSpec((1, tm, LANES), lambda b, i, c=c: (b, i, c))
    const = lambda shape: pl.BlockSpec(shape, lambda b, i: tuple(0 for _ in shape))
    kv_out = pl.BlockSpec((1, 2, tm, LANES), lambda b, i: (b, 0, i, 0))
    kv_shape = jax.ShapeDtypeStruct((B, 2, S, LANES), BF16)
    return pl.pallas_call(
        _nsa_prep_kernel,
        out_shape=(jax.ShapeDtypeStruct((B, S, 512), BF16), kv_shape, kv_shape, kv_shape, kv_shape),
        grid=(B, S // tm),
        in_specs=[
            pl.BlockSpec((1, tm, 512), lambda b, i: (b, i, COL_Q // 512)),
            zcol(kvb + 2), zcol(kvb + 3), zcol(kvb + 4), zcol(kvb + 5),
            pl.BlockSpec((tm, LANES), lambda b, i: (i, 0)),
            pl.BlockSpec((tm, LANES), lambda b, i: (i, 0)),
            const((LANES, LANES)), const((1, LANES)), const((1, LANES)), const((1, LANES)),
        ],
        out_specs=(pl.BlockSpec((1, tm, 512), lambda b, i: (b, i, 0)), kv_out, kv_out, kv_out, kv_out),
        compiler_params=pltpu.CompilerParams(dimension_semantics=("parallel", "parallel")),
        name="nsa_prep",
    )(z3, z3, z3, z3, z3, rope_a, rope_b, blockdiag, gq, gks, gkw)


def _compress_kernel(fk_ref, fv_ref, pek_ref, pev_ref, w1k_ref, w1v_ref, w2k_ref, w2v_ref, ra_ref, rb_ref, bd_ref, gk_ref,
                     k_ref, v_ref):
    lane = lax.broadcasted_iota(I32, (fk_ref.shape[1], LANES), 1)

    def mlp(f_ref, pe_ref, w1_ref, w2_ref):
        f = (f_ref[0] + pe_ref[...]).astype(BF16)
        h = jax.nn.gelu(_dot(f, w1_ref[...]))
        return _dot(h.astype(BF16), w2_ref[...])

    def dup(t, out_ref):
        sw = pltpu.roll(t, HEAD_DIM, 1)
        out_ref[0, 0] = jnp.where(lane < HEAD_DIM, t, sw).astype(BF16)
        out_ref[0, 1] = jnp.where(lane < HEAD_DIM, sw, t).astype(BF16)

    kc = mlp(fk_ref, pek_ref, w1k_ref, w2k_ref)
    dup(_head_norm_rope(kc, gk_ref[...], ra_ref[...], rb_ref[...], bd_ref[...]), k_ref)
    dup(mlp(fv_ref, pev_ref, w1v_ref, w2v_ref), v_ref)


def _compress(flat_k, flat_v, pek, pev, w1k, w1v, w2k, w2v, rope_a, rope_b, blockdiag, gk):
    B, nblk, width = flat_k.shape
    row = pl.BlockSpec((1, nblk, width), lambda b: (b, 0, 0))
    const = lambda shape: pl.BlockSpec(shape, lambda b: tuple(0 for _ in shape))
    out = pl.BlockSpec((1, 2, nblk, LANES), lambda b: (b, 0, 0, 0))
    shp = jax.ShapeDtypeStruct((B, 2, nblk, LANES), BF16)
    return pl.pallas_call(
        _compress_kernel,
        out_shape=(shp, shp),
        grid=(B,),
        in_specs=[row, row, const((1, width)), const((1, width)), const((width, 2 * CMP_HIDDEN)),
                  const((width, 2 * CMP_HIDDEN)), const((2 * CMP_HIDDEN, LANES)), const((2 * CMP_HIDDEN, LANES)),
                  const((nblk, LANES)), const((nblk, LANES)), const((LANES, LANES)), const((1, LANES))],
        out_specs=(out, out),
        compiler_params=pltpu.CompilerParams(dimension_semantics=("parallel",), vmem_limit_bytes=48 * 1024 * 1024),
        name="compress",
    )(flat_k, flat_v, pek, pev, w1k, w1v, w2k, w2v, rope_a, rope_b, blockdiag, gk)


def _softmax_rows(s):
    m = jnp.max(s, axis=-1, keepdims=True)
    e = jnp.exp(s - m)
    return e / jnp.sum(e, axis=-1, keepdims=True)


def _nsa_kernel(q_ref, kc_ref, vc_ref, ks_ref, vs_ref, kw_ref, vw_ref, gl_ref, ov_ref, ex_ref, y_ref):
    n = pl.program_id(1)
    t0 = n * NSA_QB
    qt = q_ref[0]
    hm = _half_masks()
    lane = lax.broadcasted_iota(I32, (NSA_QB, LANES), 1)
    t1 = t0 + lax.broadcasted_iota(I32, (NSA_QB, 1), 0)
    t4 = t0 + (lax.broadcasted_iota(I32, (4 * NSA_QB, 1), 0) & (NSA_QB - 1))
    gates = jax.nn.sigmoid(gl_ref[0])

    n_slc_blocks = ks_ref.shape[2] // SLC_BLOCK
    blk = lax.broadcasted_iota(I32, (n_slc_blocks, NSA_QB), 0)
    tq = t0 + lax.broadcasted_iota(I32, (n_slc_blocks, NSA_QB), 1)
    cur = tq >> 6
    forced = (blk == 0) | (blk == cur) | (blk == cur - 1)
    causal_b = blk * SLC_BLOCK <= tq

    tiles_out = []
    for g in range(2):
        q4 = jnp.concatenate(
            [qt[:, (2 * g + jj // 2) * LANES:(2 * g + jj // 2 + 1) * LANES] * hm[jj % 2] for jj in range(4)], axis=0)

        sc = _dot_nt(q4, kc_ref[0, g])
        cend = lax.broadcasted_iota(I32, (1, sc.shape[1]), 1) * CMP_STRIDE + (CMP_BLOCK - 1)
        valid_c = cend <= t4
        pc = _softmax_rows(jnp.where(valid_c, sc, NEG_INF))
        pc = jnp.where(valid_c, pc, 0.0)
        o_c = _dot(pc.astype(BF16), vc_ref[0, g])
        psum = pc[0:NSA_QB] + pc[NSA_QB:2 * NSA_QB] + pc[2 * NSA_QB:3 * NSA_QB] + pc[3 * NSA_QB:]
        imp = _dot_hilo(psum, ov_ref[...])

        score = jnp.where(forced, FORCE_SCORE, jnp.where(causal_b, imp.T[:n_slc_blocks], -1.0))
        n_grp = n_slc_blocks // SUBLANES
        grp = [score[SUBLANES * r:SUBLANES * (r + 1)] for r in range(n_grp)]
        ranks = [jnp.zeros((SUBLANES, NSA_QB), F32) for _ in range(n_grp)]
        row = lax.broadcasted_iota(I32, (SUBLANES, NSA_QB), 0)
        for i in range(n_slc_blocks):
            ri = grp[i // SUBLANES][i % SUBLANES:i % SUBLANES + 1, :]
            for r in range(n_grp):
                if r > i // SUBLANES:
                    ahead = ri >= grp[r]
                elif r < i // SUBLANES:
                    ahead = ri > grp[r]
                else:
                    ahead = (ri > grp[r]) | ((ri == grp[r]) & (row > i % SUBLANES))
                ranks[r] = ranks[r] + jnp.where(ahead, 1.0, 0.0)
        sel64 = jnp.where(jnp.concatenate(ranks, axis=0) < float(SLC_TOPN), 1.0, 0.0)
        sel = jnp.concatenate([sel64, jnp.zeros((LANES - n_slc_blocks, NSA_QB), F32)], axis=0).T.astype(BF16)

        def sel_chunk(c, carry, last):
            m_i, l_i, acc = carry
            k0 = pl.multiple_of(c * SEL_CHUNK, SEL_CHUNK)
            kch = ks_ref[0, g, pl.ds(k0, SEL_CHUNK), :]
            vch = vs_ref[0, g, pl.ds(k0, SEL_CHUNK), :]
            picked = _dot(sel, ex_ref[:, pl.ds(k0, SEL_CHUNK)])
            if last:
                kpos = k0 + lax.broadcasted_iota(I32, (1, SEL_CHUNK), 1)
                picked = jnp.where(kpos <= t1, picked, 0.0)
            bias = (picked - 1.0) * (-NEG_INF)
            s = (_dot_nt(q4, kch).reshape(4, NSA_QB, SEL_CHUNK) + bias[None]).reshape(4 * NSA_QB, SEL_CHUNK)
            m_new = jnp.maximum(m_i, jnp.max(s, axis=-1, keepdims=True))
            alpha = jnp.exp(m_i - m_new)
            p = jnp.exp(s - m_new)
            l_new = alpha * l_i + jnp.sum(p, axis=-1, keepdims=True)
            acc_new = alpha * acc + _dot(p.astype(BF16), vch)
            return m_new, l_new, acc_new

        init = (jnp.full((4 * NSA_QB, 1), NEG_INF, F32), jnp.zeros((4 * NSA_QB, 1), F32),
                jnp.zeros((4 * NSA_QB, LANES), F32))
        n_full = t0 // SEL_CHUNK
        carry = lax.fori_loop(0, n_full, functools.partial(sel_chunk, last=False), init)
        _, l_s, acc_s = sel_chunk(n_full, carry, last=True)
        o_s = acc_s / l_s

        wlen = WIN + NSA_QB
        ws = pl.multiple_of(jnp.maximum(t0 - WIN, 0), NSA_QB)
        sw = _dot_nt(q4, kw_ref[0, g, pl.ds(ws, wlen), :])
        dist = t4 - (ws + lax.broadcasted_iota(I32, (1, wlen), 1))
        pw = _softmax_rows(jnp.where((dist >= 0) & (dist < WIN), sw, NEG_INF))
        o_w = _dot(pw.astype(BF16), vw_ref[0, g, pl.ds(ws, wlen), :])

        heads = []
        for jj in range(4):
            h = 4 * g + jj
            rows = slice(jj * NSA_QB, (jj + 1) * NSA_QB)
            heads.append(gates[:, 3 * h:3 * h + 1] * o_c[rows] + gates[:, 3 * h + 1:3 * h + 2] * o_s[rows]
                         + gates[:, 3 * h + 2:3 * h + 3] * o_w[rows])
        tiles_out.append(jnp.where(lane < HEAD_DIM, heads[0], heads[1]))
        tiles_out.append(jnp.where(lane < HEAD_DIM, heads[2], heads[3]))

    y_ref[0] = jnp.concatenate(tiles_out, axis=-1).astype(BF16)


def _nsa_attn(qn, kcd, vcd, ksd, vsd, kwd, vwd, z3, overlap):
    B, S, _ = qn.shape
    ncp = kcd.shape[2]
    assert S // SLC_BLOCK <= LANES
    expand = jnp.asarray(np.arange(LANES)[:, None] == (np.arange(S)[None, :] // SLC_BLOCK), BF16)
    full = lambda rows: pl.BlockSpec((1, 2, rows, LANES), lambda b, n: (b, 0, 0, 0))
    return pl.pallas_call(
        _nsa_kernel,
        out_shape=jax.ShapeDtypeStruct((B, S, 512), BF16),
        grid=(B, S // NSA_QB),
        in_specs=[
            pl.BlockSpec((1, NSA_QB, 512), lambda b, n: (b, n, 0)),
            full(ncp), full(ncp), full(S), full(S), full(S), full(S),
            pl.BlockSpec((1, NSA_QB, LANES), lambda b, n: (b, n, COL_GATE // LANES)),
            pl.BlockSpec((ncp, LANES), lambda b, n: (0, 0)),
            pl.BlockSpec((LANES, S), lambda b, n: (0, 0)),
        ],
        out_specs=pl.BlockSpec((1, NSA_QB, 512), lambda b, n: (b, n, 0)),
        compiler_params=pltpu.CompilerParams(dimension_semantics=("parallel", "arbitrary"),
                                             vmem_limit_bytes=48 * 1024 * 1024),
        name="nsa_attn",
    )(qn, kcd, vcd, ksd, vsd, kwd, vwd, z3, overlap, expand)


def _dil_prep_kernel(*refs):
    zs, (ra_ref, rb_ref, bd_ref, gq_ref, gk_ref), outs = refs[0:18], refs[18:23], refs[23:32]
    bd = bd_ref[...]
    scale = HEAD_DIM ** -0.5
    for g, (_, d) in enumerate(DIL_PATTERNS):
        n = DIL_PREP_ROWS // d
        for r in range(d):
            rows = pl.ds(r, n, stride=d) if d > 1 else pl.ds(0, n)
            ra, rb = ra_ref[rows, :], rb_ref[rows, :]
            for which in range(3):
                for m in range(2):
                    z = zs[2 * (3 * g + which) + m][0, rows, :]
                    if which == 0:
                        z = _head_norm_rope(z, gq_ref[g], ra, rb, bd) * scale
                    elif which == 1:
                        z = _head_norm_rope(z, gk_ref[g], ra, rb, bd)
                    outs[3 * g + which][0, r, :, m * LANES:(m + 1) * LANES] = z.astype(BF16)


def _dil_prep(z3, rope_a, rope_b, blockdiag, gq, gk):
    B, S, _ = z3.shape
    nsteps = S // DIL_PREP_ROWS
    c0 = COL_DIL // LANES
    in_specs = [pl.BlockSpec((1, DIL_PREP_ROWS, LANES), lambda b, c, k=k: (b, c, c0 + k)) for k in range(18)]
    in_specs += [
        pl.BlockSpec((DIL_PREP_ROWS, LANES), lambda b, c: (c, 0)),
        pl.BlockSpec((DIL_PREP_ROWS, LANES), lambda b, c: (c, 0)),
        pl.BlockSpec((LANES, LANES), lambda b, c: (0, 0)),
        pl.BlockSpec((3, 1, LANES), lambda b, c: (0, 0, 0)),
        pl.BlockSpec((3, 1, LANES), lambda b, c: (0, 0, 0)),
    ]
    out_shape, out_specs = [], []
    for _, d in DIL_PATTERNS:
        for _ in range(3):
            out_shape.append(jax.ShapeDtypeStruct((B, d, S // d, 256), BF16))
            out_specs.append(pl.BlockSpec((1, d, DIL_PREP_ROWS // d, 256), lambda b, c: (b, 0, c, 0)))
    return pl.pallas_call(
        _dil_prep_kernel,
        out_shape=tuple(out_shape),
        grid=(B, nsteps),
        in_specs=in_specs,
        out_specs=tuple(out_specs),
        compiler_params=pltpu.CompilerParams(dimension_semantics=("parallel", "parallel"),
                                             vmem_limit_bytes=48 * 1024 * 1024),
        name="dil_prep",
    )(*([z3] * 18), rope_a, rope_b, blockdiag, gq, gk)


def _dil_kernel(*refs, seq):
    q_refs, k_refs, v_refs, y_ref, o_scr, l_scr = refs[0:3], refs[3:6], refs[6:9], refs[9], refs[10], refs[11]
    hm = _half_masks()
    lane = lax.broadcasted_iota(I32, (DIL_BLOCK, LANES), 1)
    qi = lax.broadcasted_iota(I32, (2 * DIL_BLOCK, 2 * DIL_BLOCK), 0) & (DIL_BLOCK - 1)
    ki = lax.broadcasted_iota(I32, (2 * DIL_BLOCK, 2 * DIL_BLOCK), 1)
    causal = (ki - DIL_BLOCK) <= qi

    for g, (_, d) in enumerate(DIL_PATTERNS):
        nb = seq // d // DIL_BLOCK

        def body(u, carry, g=g, d=d, nb=nb):
            j = u % nb
            r = u // nb
            r0 = pl.multiple_of(u * DIL_BLOCK, DIL_BLOCK)
            p0 = pl.multiple_of(jnp.maximum(u - 1, 0) * DIL_BLOCK, DIL_BLOCK)
            q = q_refs[g][0, pl.ds(r0, DIL_BLOCK), :]
            kcat = jnp.concatenate([k_refs[g][0, pl.ds(p0, DIL_BLOCK), :], k_refs[g][0, pl.ds(r0, DIL_BLOCK), :]], axis=0)
            vcat = jnp.concatenate([v_refs[g][0, pl.ds(p0, DIL_BLOCK), :], v_refs[g][0, pl.ds(r0, DIL_BLOCK), :]], axis=0)
            q2 = jnp.concatenate([q * hm[0], q * hm[1]], axis=0)
            s = _dot_nt(q2, kcat)
            first_key = jnp.maximum(qi, jnp.where(j >= 1, 0, DIL_BLOCK))
            s = jnp.where(causal & (ki >= first_key), s, NEG_INF)
            m = jnp.max(s, axis=-1, keepdims=True)
            e = jnp.exp(s - m)
            den = jnp.sum(e, axis=-1, keepdims=True)
            o2 = _dot(e.astype(BF16), vcat) / den
            lse = m + jnp.log(den)
            o = jnp.where(lane < HEAD_DIM, o2[:DIL_BLOCK], o2[DIL_BLOCK:])
            lv = jnp.where(lane < HEAD_DIM, lse[:DIL_BLOCK], lse[DIL_BLOCK:])
            tok0 = j * (DIL_BLOCK * d) + r
            rows = pl.ds(tok0, DIL_BLOCK, stride=d) if d > 1 else pl.ds(pl.multiple_of(tok0, DIL_BLOCK), DIL_BLOCK)
            o_scr[g, rows, :] = o
            l_scr[g, rows, :] = lv
            return carry

        lax.fori_loop(0, seq // DIL_BLOCK, body, 0)

    def merge(c, carry):
        rows = pl.ds(pl.multiple_of(c * 512, 512), 512)
        ls = [l_scr[g, rows, :] for g in range(3)]
        mx = jnp.maximum(jnp.maximum(ls[0], ls[1]), ls[2])
        ws = [jnp.exp(l - mx) for l in ls]
        num = ws[0] * o_scr[0, rows, :] + ws[1] * o_scr[1, rows, :] + ws[2] * o_scr[2, rows, :]
        y_ref[0, rows, :] = (num / (ws[0] + ws[1] + ws[2])).astype(BF16)
        return carry

    lax.fori_loop(0, seq // 512, merge, 0)


def _dil_attn(dq, dk, dv):
    B, S, _ = dq[0].shape
    spec = pl.BlockSpec((1, S, LANES), lambda b, m: (b, 0, m))
    return pl.pallas_call(
        functools.partial(_dil_kernel, seq=S),
        out_shape=jax.ShapeDtypeStruct((B, S, 256), BF16),
        grid=(B, 2),
        in_specs=[spec] * 9,
        out_specs=spec,
        scratch_shapes=[pltpu.VMEM((3, S, LANES), F32), pltpu.VMEM((3, S, LANES), F32)],
        compiler_params=pltpu.CompilerParams(dimension_semantics=("parallel", "parallel"),
                                             vmem_limit_bytes=56 * 1024 * 1024),
        name="dil_attn",
    )(*dq, *dk, *dv)


def _merge_kernel(x_ref, yn_ref, yd_ref, mg0_ref, mg1_ref, wn_ref, wd_ref, wo_ref, g2_ref, wq_ref, x1_ref, hn_ref, pq_ref):
    u1 = _dot(yn_ref[...], wn_ref[...])
    u2 = _dot(yd_ref[...], wd_ref[...])
    merged = jax.nn.sigmoid(mg0_ref[...]) * u1 + jax.nn.sigmoid(mg1_ref[...]) * u2
    x1 = x_ref[...] + _dot(merged.astype(BF16), wo_ref[...])
    x1_ref[...] = x1
    ms = jnp.mean(x1 * x1, axis=-1, keepdims=True)
    hn = x1 * lax.rsqrt(ms + NORM_EPS) * g2_ref[...]
    hn_ref[...] = hn
    pq_ref[...] = _dot(hn.astype(BF16), wq_ref[...])


def _merge(x2, yn2, yd2, z2, wn, wd, wo, g2, wq):
    T = x2.shape[0]
    tm = 512
    row = lambda w, c=0: pl.BlockSpec((tm, w), lambda i, c=c: (i, c))
    const = lambda shape: pl.BlockSpec(shape, lambda i: (0, 0))
    shp = jax.ShapeDtypeStruct((T, D_MODEL), F32)
    return pl.pallas_call(
        _merge_kernel,
        out_shape=(shp, shp, shp),
        grid=(T // tm,),
        in_specs=[row(D_MODEL), row(512), row(256), row(D_MODEL, COL_MG // D_MODEL), row(D_MODEL, COL_MG // D_MODEL + 1),
                  const((512, D_MODEL)), const((256, D_MODEL)), const((D_MODEL, D_MODEL)), const((1, D_MODEL)),
                  const((D_MODEL, D_MODEL))],
        out_specs=(row(D_MODEL), row(D_MODEL), row(D_MODEL)),
        compiler_params=pltpu.CompilerParams(dimension_semantics=("parallel",), vmem_limit_bytes=48 * 1024 * 1024),
        name="merge",
    )(x2, yn2, yd2, z2, z2, wn, wd, wo, g2, wq)


def _top16(s, rank_id=None):
    if rank_id is None:
        rank_id = lax.broadcasted_iota(I32, (s.shape[0], 1), 0)
    big = jnp.iinfo(jnp.int32).max
    vals, ids = [], []
    for _ in range(PEER_TOPK):
        m = jnp.max(s, axis=0, keepdims=True)
        win = jnp.min(jnp.where(s == m, rank_id, big), axis=0, keepdims=True)
        vals.append(m)
        ids.append(win)
        s = jnp.where(rank_id == win, -jnp.inf, s)
    return vals, ids


def _route_kernel(q_ref, sk_ref, idx_ref, gate_ref):
    K = PEER_TOPK
    qh, ql = _split_bf16(q_ref[...])
    vals, ids = [], []
    for c in range(2):
        kh, kl = _split_bf16(sk_ref[c])
        s = _dot_nt(kh, qh) + _dot_nt(kh, ql) + _dot_nt(kl, qh)
        v, p = _top16(s)
        vals.append(v)
        ids.append(p)
    v0, p0 = jnp.concatenate(vals[0], axis=0), jnp.concatenate(ids[0], axis=0)
    v1, p1 = jnp.concatenate(vals[1], axis=0), jnp.concatenate(ids[1], axis=0)

    a8 = lax.broadcasted_iota(I32, (SUBLANES, 1), 0)
    pieces = [(v0 + vals[1][0], p0 * PEER_NKEYS + ids[1][0], lax.broadcasted_iota(I32, (K, 1), 0) * K)]
    for b in range(1, SUBLANES):
        keep = a8 < K // (b + 1)
        pieces.append((jnp.where(keep, v0[:SUBLANES] + vals[1][b], -jnp.inf), p0[:SUBLANES] * PEER_NKEYS + ids[1][b],
                       a8 * K + b))
    pieces.append((vals[0][0] + v1[SUBLANES:], ids[0][0] * PEER_NKEYS + p1[SUBLANES:], a8 + SUBLANES))
    cand = jnp.concatenate([p[0] for p in pieces], axis=0)
    eid = jnp.concatenate([p[1] for p in pieces], axis=0)
    flat = jnp.concatenate([p[2] for p in pieces], axis=0)
    v, win = _top16(cand, flat)
    sel_ids = [jnp.sum(jnp.where(flat == w, eid, 0), axis=0, keepdims=True) for w in win]
    sc = jnp.concatenate(v, axis=0)
    e = jnp.exp(sc - sc[0:1])
    gate_ref[...] = e / jnp.sum(e, axis=0, keepdims=True)
    idx_ref[...] = jnp.concatenate(sel_ids, axis=0)


def _peer_route(pq, sk_pad):
    T = pq.shape[0]
    tt = 512
    return pl.pallas_call(
        _route_kernel,
        out_shape=(jax.ShapeDtypeStruct((PEER_HEADS * PEER_TOPK, T), I32),
                   jax.ShapeDtypeStruct((PEER_HEADS * PEER_TOPK, T), F32)),
        grid=(T // tt, PEER_HEADS),
        in_specs=[pl.BlockSpec((tt, LANES), lambda i, h: (i, h)),
                  pl.BlockSpec((2, PEER_NKEYS, LANES), lambda i, h: (0, 0, 0))],
        out_specs=(pl.BlockSpec((PEER_TOPK, tt), lambda i, h: (h, i)),
                   pl.BlockSpec((PEER_TOPK, tt), lambda i, h: (h, i))),
        compiler_params=pltpu.CompilerParams(dimension_semantics=("parallel", "parallel")),
        name="peer_route",
    )(pq, sk_pad)


def _route_t_kernel(idx_ref, gate_ref, idx_o, gate_o):
    idx_o[...] = idx_ref[...].T
    gate_o[...] = gate_ref[...].T


def _route_transpose(idx_t, gate_t):
    nk, T = idx_t.shape
    tt = 512
    src = pl.BlockSpec((nk, tt), lambda i: (0, i))
    dst = pl.BlockSpec((tt, nk), lambda i: (i, 0))
    return pl.pallas_call(
        _route_t_kernel,
        out_shape=(jax.ShapeDtypeStruct((T, nk), I32), jax.ShapeDtypeStruct((T, nk), F32)),
        grid=(T // tt,),
        in_specs=[src, src],
        out_specs=(dst, dst),
        compiler_params=pltpu.CompilerParams(dimension_semantics=("parallel",)),
        name="route_transpose",
    )(idx_t, gate_t)


def _pack_words(w):
    half = w.shape[1] // 2
    bits = pltpu.bitcast(w.astype(BF16).astype(F32), jnp.uint32)
    return (bits[:, :half] >> 16) | (bits[:, half:] & jnp.uint32(0xFFFF0000))


def _pack_kernel(u_ref, v_ref, rows_ref, tiles_ref):
    n = u_ref.shape[0]
    words = jnp.concatenate([_pack_words(u_ref[...]), _pack_words(v_ref[...])], axis=-1)
    rows_ref[...] = words
    for s in range(SUBLANES):
        tiles_ref[pl.ds(s, n, stride=SUBLANES), :] = words[:, s * LANES:(s + 1) * LANES]


def _pack_tables(u, v):
    ne, d = u.shape
    tm = 256
    return pl.pallas_call(
        _pack_kernel,
        out_shape=(jax.ShapeDtypeStruct((ne, d), jnp.uint32), jax.ShapeDtypeStruct((ne * SUBLANES, LANES), jnp.uint32)),
        grid=(ne // tm,),
        in_specs=[pl.BlockSpec((tm, d), lambda i: (i, 0)), pl.BlockSpec((tm, d), lambda i: (i, 0))],
        out_specs=(pl.BlockSpec((tm, d), lambda i: (i, 0)), pl.BlockSpec((tm * SUBLANES, LANES), lambda i: (i, 0))),
        compiler_params=pltpu.CompilerParams(dimension_semantics=("parallel",)),
        name="pack_tables",
    )(u, v)


PEER_NK = PEER_HEADS * PEER_TOPK
SUBLANES = 8


def _unpack_pair(w):
    lo = pltpu.bitcast(w << 16, F32)
    hi = pltpu.bitcast(w & jnp.uint32(0xFFFF0000), F32)
    return lo, hi


def _peer_kernel(idx_cur, idx_nxt, hn_ref, gate_ref, x1_ref, uv_hbm, o_ref, buf_a, buf_b, sem):
    i = pl.program_id(0)
    last = pl.num_programs(0) - 1
    half_rows = SUBLANES // 2
    bufs = (buf_a, buf_b)

    def tile_copy(idx_ref, s, c, k):
        p = c * PEER_NK + k
        return pltpu.make_async_copy(uv_hbm.at[idx_ref[c, k]], bufs[s].at[pl.ds(p * SUBLANES, SUBLANES), :], sem.at[s, c])

    def fetch_token(idx_ref, s, c):
        for k in range(PEER_NK):
            tile_copy(idx_ref, s, c, k).start(priority=k % 2)

    def wait_token(idx_ref, s, c):
        for k in range(PEER_NK):
            tile_copy(idx_ref, s, c, k).wait()

    @pl.when(i == 0)
    def _():
        lax.fori_loop(0, PEER_TOK, lambda c, carry: (fetch_token(idx_cur, 0, c), carry)[1], 0)

    eye = lax.broadcasted_iota(I32, (PEER_NK, PEER_NK), 0) == lax.broadcasted_iota(I32, (PEER_NK, PEER_NK), 1)

    def token(slot, c):
        wait_token(idx_cur, slot, c)
        fetch_token(idx_nxt, 1 - slot, c)
        base = c * (PEER_NK * SUBLANES)
        plane = lambda s: bufs[slot][pl.ds(base + s, PEER_NK, stride=SUBLANES), :]
        x = hn_ref[pl.ds(c, 1), :]
        xs = lambda s: x[:, s * LANES:(s + 1) * LANES]
        gcol = jnp.sum(jnp.where(eye, gate_ref[pl.ds(c, 1), :], 0.0), axis=-1, keepdims=True)
        acc = jnp.zeros((PEER_NK, LANES), F32)
        for s in range(half_rows):
            lo, hi = _unpack_pair(plane(s))
            acc = acc + lo * xs(s) + hi * xs(s + half_rows)
        act = jax.nn.gelu(jnp.sum(acc, axis=-1, keepdims=True)) * gcol
        lo_cols, hi_cols = [], []
        for s in range(half_rows, SUBLANES):
            lo, hi = _unpack_pair(plane(s))
            lo_cols.append(jnp.sum(act * lo, axis=0, keepdims=True))
            hi_cols.append(jnp.sum(act * hi, axis=0, keepdims=True))
        o_ref[pl.ds(c, 1), :] = x1_ref[pl.ds(c, 1), :] + jnp.concatenate(lo_cols + hi_cols, axis=-1)

    def step(slot):
        for c in range(PEER_TOK):
            token(slot, c)

        @pl.when(i == last)
        def _():
            lax.fori_loop(0, PEER_TOK, lambda c, carry: (wait_token(idx_nxt, 1 - slot, c), carry)[1], 0)

    pl.when(i % 2 == 0)(lambda: step(0))
    pl.when(i % 2 == 1)(lambda: step(1))


def _peer_expert(n_tokens, idx_flat, hn, gates, x1, uv_tiles):
    n = n_tokens // PEER_TOK
    row = pl.BlockSpec((PEER_TOK, D_MODEL), lambda i: (i, 0))
    fetch_buf = pltpu.VMEM((PEER_TOK * PEER_NK * SUBLANES, LANES), jnp.uint32)
    return pl.pallas_call(
        _peer_kernel,
        out_shape=jax.ShapeDtypeStruct((n_tokens, D_MODEL), F32),
        grid=(n,),
        in_specs=[pl.BlockSpec((PEER_TOK, PEER_NK), lambda i: (i, 0), memory_space=pltpu.SMEM),
                  pl.BlockSpec((PEER_TOK, PEER_NK), lambda i: (jnp.minimum(i + 1, n - 1), 0), memory_space=pltpu.SMEM),
                  row, pl.BlockSpec((PEER_TOK, PEER_NK), lambda i: (i, 0)), row,
                  pl.BlockSpec(memory_space=pl.ANY)],
        out_specs=row,
        scratch_shapes=[fetch_buf, fetch_buf, pltpu.SemaphoreType.DMA((2, PEER_TOK))],
        compiler_params=pltpu.CompilerParams(dimension_semantics=("arbitrary",), vmem_limit_bytes=48 * 1024 * 1024),
        name="peer_expert",
    )(idx_flat, idx_flat, hn, gates, x1, uv_tiles)


SC_LANES = 16
SC_WORKERS = 32
SC_GATHER = 32
SC_BLOCK = 8
SC_SHARE = (13, 32)
SC_SHARE_EARLY = (22, 32)


def _sc_lane_bcast(vec, k):
    idx = jnp.full((SC_LANES, 1), k, I32)
    dn = lax.GatherDimensionNumbers(offset_dims=(), collapsed_slice_dims=(0,), start_index_map=(0,))
    return lax.gather(vec, idx, dn, slice_sizes=(1,), mode=lax.GatherScatterMode.PROMISE_IN_BOUNDS)


def _sc_unpack_pair(w):
    return plsc.bitcast(w << 16, F32), plsc.bitcast(w & jnp.uint32(0xFFFF0000), F32)


def _peer_expert_sc(t_start, n, idx, gates, hn, x1, uv_rows):
    assert n % (SC_WORKERS * SC_BLOCK) == 0 and t_start % SC_BLOCK == 0
    tokens_per_worker = n // SC_WORKERS
    n_gather = PEER_NK // SC_GATHER
    half = D_MODEL // 2
    n_chunk = half // SC_LANES
    mesh = plsc.VectorSubcoreMesh(core_axis_name="c", subcore_axis_name="s")

    def body(idx_hbm, g_hbm, hn_hbm, x1_hbm, uv_hbm, out_hbm, idx_v, g_v, x_v, o_v, rows0, rows1, sem0, sem1):
        worker = lax.axis_index("s") * 2 + lax.axis_index("c")
        lane = lax.iota(I32, SC_LANES)
        rows, sems = (rows0, rows1), (sem0, sem1)

        def gather(ti, q):
            picks = idx_v.at[ti, pl.ds(q * SC_GATHER, SC_GATHER)]
            return pltpu.make_async_copy(uv_hbm.at[picks], rows[q % 2], sems[q % 2])

        def block(bi, carry):
            o0 = pl.multiple_of(worker * tokens_per_worker + bi * SC_BLOCK, SC_BLOCK)
            t0 = pl.multiple_of(t_start + o0, SC_BLOCK)
            pltpu.sync_copy(idx_hbm.at[pl.ds(t0, SC_BLOCK)], idx_v)
            pltpu.sync_copy(g_hbm.at[pl.ds(t0, SC_BLOCK)], g_v)
            pltpu.sync_copy(hn_hbm.at[pl.ds(t0, SC_BLOCK)], x_v)
            pltpu.sync_copy(x1_hbm.at[pl.ds(t0, SC_BLOCK)], o_v)
            gather(0, 0).start()

            def token(ti, carry1):
                for q in range(n_gather):
                    if q + 1 < n_gather:
                        gather(ti, q + 1).start()
                    else:
                        @pl.when(ti + 1 < SC_BLOCK)
                        def _():
                            gather(ti + 1, 0).start()
                    gather(ti, q).wait()
                    rows_v = rows[q % 2]

                    def dot_chunk(j, accs):
                        off = pl.multiple_of(j * SC_LANES, SC_LANES)
                        xlo = x_v[ti, pl.ds(off, SC_LANES)]
                        xhi = x_v[ti, pl.ds(half + off, SC_LANES)]
                        out = []
                        for k in range(SC_GATHER):
                            lo, hi = _sc_unpack_pair(rows_v[k, pl.ds(off, SC_LANES)])
                            out.append(accs[k] + lo * xlo + hi * xhi)
                        return tuple(out)

                    accs = lax.fori_loop(0, n_chunk, dot_chunk,
                                         tuple(jnp.zeros((SC_LANES,), F32) for _ in range(SC_GATHER)))
                    acts = []
                    for h in range(SC_GATHER // SC_LANES):
                        a = jnp.zeros((SC_LANES,), F32)
                        for kk in range(SC_LANES):
                            a = jnp.where(lane == kk, jnp.sum(accs[h * SC_LANES + kk]), a)
                        y = 0.7978845608028654 * (a + 0.044715 * a * a * a)
                        th = 1.0 - 2.0 / (jnp.exp(2.0 * y) + 1.0)
                        acts.append(0.5 * a * (1.0 + th) * g_v[ti, pl.ds(q * SC_GATHER + h * SC_LANES, SC_LANES)])
                    act_b = [_sc_lane_bcast(acts[k // SC_LANES], k % SC_LANES) for k in range(SC_GATHER)]

                    def mix_chunk(j, carry2):
                        off = pl.multiple_of(j * SC_LANES, SC_LANES)
                        al = jnp.zeros((SC_LANES,), F32)
                        ah = jnp.zeros((SC_LANES,), F32)
                        for k in range(SC_GATHER):
                            lo, hi = _sc_unpack_pair(rows_v[k, pl.ds(half + off, SC_LANES)])
                            al = al + act_b[k] * lo
                            ah = ah + act_b[k] * hi
                        o_v[ti, pl.ds(off, SC_LANES)] = o_v[ti, pl.ds(off, SC_LANES)] + al
                        o_v[ti, pl.ds(half + off, SC_LANES)] = o_v[ti, pl.ds(half + off, SC_LANES)] + ah
                        return carry2

                    lax.fori_loop(0, n_chunk, mix_chunk, 0)
                return carry1

            lax.fori_loop(0, SC_BLOCK, token, 0)
            pltpu.sync_copy(o_v, out_hbm.at[pl.ds(o0, SC_BLOCK)])
            return carry

        lax.fori_loop(0, tokens_per_worker // SC_BLOCK, block, 0)

    row_buf = pltpu.VMEM((SC_GATHER, D_MODEL), jnp.uint32)
    return pl.kernel(
        body, mesh=mesh,
        out_type=jax.ShapeDtypeStruct((n, D_MODEL), F32),
        scratch_types=[pltpu.VMEM((SC_BLOCK, PEER_NK), I32), pltpu.VMEM((SC_BLOCK, PEER_NK), F32),
                       pltpu.VMEM((SC_BLOCK, D_MODEL), F32), pltpu.VMEM((SC_BLOCK, D_MODEL), F32), row_buf, row_buf,
                       pltpu.SemaphoreType.DMA, pltpu.SemaphoreType.DMA],
        compiler_params=pltpu.CompilerParams(needs_layout_passes=False),
        name="peer_expert_sc",
    )(idx, gates, hn, x1, uv_rows)


def _rope_tables(pos):
    half = ROPE_DIMS // 2
    inv = ROPE_THETA ** (-(jnp.arange(half, dtype=F32) * 2.0 / ROPE_DIMS))
    ang = pos.astype(F32)[:, None] * inv[None, :]
    cos, sin = jnp.cos(ang), jnp.sin(ang)
    n = pos.shape[0]
    a = jnp.concatenate([cos, cos, jnp.ones((n, HEAD_DIM - ROPE_DIMS), F32)], axis=-1)
    b = jnp.concatenate([-sin, sin, jnp.zeros((n, HEAD_DIM - ROPE_DIMS), F32)], axis=-1)
    return jnp.tile(a, (1, 2)), jnp.tile(b, (1, 2))


def _tile2(v):
    return jnp.tile(v.reshape(1, HEAD_DIM), (1, 2))


def kernel(x, norm1_g, w_in, nsa_q_norm, nsa_k_norm, cmp_pe_k, cmp_w1_k, cmp_w2_k, cmp_pe_v, cmp_w1_v, cmp_w2_v,
           dil_q_norm, dil_k_norm, w_up_nsa, w_up_dil, w_o, norm2_g, peer_wq, peer_subkeys, peer_u, peer_v):
    B, S, D = x.shape
    assert D == D_MODEL and S % (DIL_PATTERNS[-1][1] * DIL_BLOCK) == 0 and S >= WIN + NSA_QB and (B * S) % 1024 == 0

    n_q, n_kv, n_gate, n_dil = 512, 768, 24, 2304
    o_gate = n_q + n_kv
    o_dil = o_gate + n_gate
    o_mg = o_dil + n_dil
    w_perm = jnp.concatenate([w_in[:, o_mg:], w_in[:, :o_gate], w_in[:, o_dil:o_mg], w_in[:, o_gate:o_dil],
                              jnp.zeros((D, IN_COLS_PAD - w_in.shape[1]), w_in.dtype)], axis=1).astype(BF16)
    blockdiag = jnp.asarray(np.kron(np.eye(2), np.ones((HEAD_DIM, HEAD_DIM))), BF16)
    rope_a, rope_b = _rope_tables(jnp.arange(S))
    sub = PEER_NKEYS // 2
    sk_pad = jnp.stack([jnp.pad(peer_subkeys[0], ((0, 0), (0, sub))), jnp.pad(peer_subkeys[1], ((0, 0), (sub, 0)))])
    uv_rows, uv_tiles = _pack_tables(peer_u, peer_v)
    eye2 = jnp.eye(2, dtype=F32)
    w1_grp = lambda w: jnp.einsum('ldh,gq->lgdqh', w.reshape(CMP_BLOCK, HEAD_DIM, CMP_HIDDEN), eye2).reshape(
        CMP_BLOCK * 2 * HEAD_DIM, 2 * CMP_HIDDEN).astype(BF16)
    w2_grp = lambda w: jnp.einsum('hd,gq->ghqd', w, eye2).reshape(2 * CMP_HIDDEN, 2 * HEAD_DIM).astype(BF16)
    pe_grp = lambda pe: jnp.broadcast_to(pe[:, None, :], (CMP_BLOCK, 2, HEAD_DIM)).reshape(1, -1)
    cmp_a, cmp_b = _rope_tables(jnp.arange(S // CMP_STRIDE) * CMP_STRIDE + CMP_BLOCK - 1)
    weights = dict(w_perm=w_perm, blockdiag=blockdiag, rope_a=rope_a, rope_b=rope_b, sk_pad=sk_pad, uv_rows=uv_rows,
                   uv_tiles=uv_tiles.reshape(-1, SUBLANES, LANES),
                   wn=w_up_nsa.astype(BF16), wd=w_up_dil.astype(BF16), wo=w_o.astype(BF16), wq=peer_wq.astype(BF16),
                   cmp=(pe_grp(cmp_pe_k), pe_grp(cmp_pe_v), w1_grp(cmp_w1_k), w1_grp(cmp_w1_v), w2_grp(cmp_w2_k),
                        w2_grp(cmp_w2_v)), cmp_a=cmp_a, cmp_b=cmp_b)
    params = (norm1_g, nsa_q_norm, nsa_k_norm, dil_q_norm, dil_k_norm, norm2_g)

    n_groups = next(n for n in (8, 4, 2, 1) if B % n == 0 and (B // n * S) % 1024 == 0)
    bg = B // n_groups
    outs = [_layer(x[g * bg:(g + 1) * bg], SC_SHARE if g == n_groups - 1 else SC_SHARE_EARLY, weights, *params)
            for g in range(n_groups)]
    return outs[0] if n_groups == 1 else jnp.concatenate(outs, axis=0)


def _layer(x, sc_share, weights, norm1_g, nsa_q_norm, nsa_k_norm, dil_q_norm, dil_k_norm, norm2_g):
    B, S, D = x.shape
    T = B * S
    x2 = x.reshape(T, D)
    blockdiag, rope_a, rope_b = weights["blockdiag"], weights["rope_a"], weights["rope_b"]
    z2 = _in_proj(x2, norm1_g.reshape(1, D), weights["w_perm"])
    z3 = z2.reshape(B, S, IN_COLS_PAD)

    qn, ksd, vsd, kwd, vwd = _nsa_prep(z3, rope_a, rope_b, blockdiag, _tile2(nsa_q_norm), _tile2(nsa_k_norm[1]),
                                       _tile2(nsa_k_norm[2]))

    n_cmp = (S - CMP_BLOCK) // CMP_STRIDE + 1
    ncp = S // CMP_STRIDE
    def flat_blocks(col):
        r = z3[:, :, col:col + LANES].reshape(B, ncp, CMP_STRIDE * LANES)
        nxt = jnp.concatenate([r[:, 1:], jnp.zeros_like(r[:, :1])], axis=1)
        return jnp.concatenate([r, nxt], axis=-1)
    kcd, vcd = _compress(flat_blocks(COL_KV), flat_blocks(COL_KV + LANES), *weights["cmp"], weights["cmp_a"],
                         weights["cmp_b"], blockdiag, _tile2(nsa_k_norm[0]))

    n_slc = S // SLC_BLOCK
    s0 = np.arange(n_cmp) * CMP_STRIDE
    b0 = np.arange(n_slc) * SLC_BLOCK
    ov = np.clip(np.minimum(s0[:, None] + CMP_BLOCK, b0[None, :] + SLC_BLOCK) - np.maximum(s0[:, None], b0[None, :]),
                 0, None) / CMP_BLOCK
    ov_pad = np.zeros((ncp, LANES), np.float32)
    ov_pad[:n_cmp, :n_slc] = ov
    y_nsa = _nsa_attn(qn, kcd, vcd, ksd, vsd, kwd, vwd, z3, jnp.asarray(ov_pad, BF16))

    gq = jnp.tile(dil_q_norm.reshape(3, 1, HEAD_DIM), (1, 1, 2))
    gk = jnp.tile(dil_k_norm.reshape(3, 1, HEAD_DIM), (1, 1, 2))
    prep = _dil_prep(z3, rope_a, rope_b, blockdiag, gq, gk)
    flat = [p.reshape(B, S, 256) for p in prep]
    y_dil = _dil_attn(flat[0::3], flat[1::3], flat[2::3])

    x1, hn, pq = _merge(x2, y_nsa.reshape(T, 512), y_dil.reshape(T, 256), z2, weights["wn"], weights["wd"], weights["wo"],
                        norm2_g.reshape(1, D), weights["wq"])

    idx, gates = _route_transpose(*_peer_route(pq, weights["sk_pad"]))
    sc_unit = SC_WORKERS * SC_BLOCK
    t_tc = T - (T * sc_share[0] // sc_share[1]) // sc_unit * sc_unit
    assert t_tc % PEER_TOK == 0 and t_tc >= 2 * PEER_TOK
    out_tc = _peer_expert(t_tc, idx, hn, gates, x1, weights["uv_tiles"])
    out_sc = _peer_expert_sc(t_tc, T - t_tc, idx, gates, hn, x1, weights["uv_rows"])
    return jnp.concatenate([out_tc, out_sc], axis=0).reshape(B, S, D)
```
